```python
import math
import jax
import jax.numpy as jnp
from jax import lax
import numpy as np

D_MODEL = 1024
BATCH = 8
SEQ = 4096
DEPTH = 2

MEM_LEN = 256
D_RG = 1024
RG_BLOCKS = 4
RG_BLOCK = D_RG // RG_BLOCKS
RG_C = 8.0
CONV_W = 4
D_ML = 1024
ML_HEADS = 4
ML_HEAD_DIM = D_ML // ML_HEADS
ML_CHUNK = 128
D_XA = 1024
XA_HEADS = 4
XA_HEAD_DIM = D_XA // XA_HEADS
N_BRANCH = 3
N_IN = 2 * D_RG + 2 * D_ML + 2 * ML_HEADS + D_XA + N_BRANCH * D_MODEL
D_FF = 3584
N_EXPERTS = 8
TOP_K = 2
MOE_BLOCK = 128
N_DENSE = (DEPTH + 1) // 2
N_MOE = DEPTH // 2
EPS = 1e-6

kernel_name = "hybrid_rglru_mlstm_memxattn_moe"


def rms_norm(x, g):
    xf = x.astype(jnp.float32)
    y = xf * lax.rsqrt(jnp.mean(xf * xf, axis=-1, keepdims=True) + EPS)
    return (y * g.astype(jnp.float32)).astype(x.dtype)


def causal_dwconv(x, w, b):
    C = x.shape[-1]
    y = lax.conv_general_dilated(
        x, w[:, None, :].astype(x.dtype), window_strides=(1,),
        padding=[(w.shape[0] - 1, 0)], dimension_numbers=("NWC", "WIO", "NWC"),
        feature_group_count=C)
    return y + b


def block_diag(x, w):
    G, bi, bo = w.shape
    xs = x.reshape(*x.shape[:-1], G, bi)
    return jnp.einsum("...gi,gio->...go", xs, w).reshape(*x.shape[:-1], G * bo)


def split_columns(z):
    sizes = [D_RG, D_RG, D_ML, D_ML, ML_HEADS, ML_HEADS, D_XA, N_BRANCH * D_MODEL]
    out = []
    off = 0
    for s in sizes:
        out.append(z[..., off:off + s])
        off += s
    return out


def rg_lru(x, w_a, b_a, w_x, b_x, lam):
    xf = x.astype(jnp.float32)
    r = jax.nn.sigmoid(block_diag(x, w_a).astype(jnp.float32) + b_a.astype(jnp.float32))
    i = jax.nn.sigmoid(block_diag(x, w_x).astype(jnp.float32) + b_x.astype(jnp.float32))
    log_a = -RG_C * r * jax.nn.softplus(-lam.astype(jnp.float32))
    a = jnp.exp(log_a)
    b_term = jnp.sqrt(-jnp.expm1(2.0 * log_a)) * (i * xf)

    def combine(left, right):
        a1, b1 = left
        a2, b2 = right
        return a1 * a2, a2 * b1 + b2

    _, h = lax.associative_scan(combine, (a, b_term), axis=1)
    return h.astype(x.dtype)


def mlstm_chunkwise(q, k, v, i_pre, f_pre):
    B, S, H, d = q.shape
    L = ML_CHUNK
    NC = S // L

    def to_chunks(t):
        t = t.reshape(B, NC, L, H, *t.shape[3:])
        return jnp.moveaxis(t, (1, 3), (0, 2))

    qc, kc, vc = to_chunks(q), to_chunks(k), to_chunks(v)
    ic = to_chunks(i_pre)
    lfc = to_chunks(jax.nn.log_sigmoid(f_pre))
    causal = jnp.tril(jnp.ones((L, L), dtype=bool))

    def step(carry, inp):
        C, n, m = carry
        qb, kb, vb, ib, lf = inp
        b = jnp.cumsum(lf, axis=-1)
        g = b + m[..., None]
        D = b[..., :, None] - b[..., None, :] + ib[..., None, :]
        D = jnp.where(causal, D, -jnp.inf)
        m_row = jnp.maximum(g, jnp.max(D, axis=-1))
        W = jnp.exp(D - m_row[..., None]) * jnp.einsum("bhjd,bhsd->bhjs", qb, kb)
        inter = jnp.exp(g - m_row)
        num = inter[..., None] * jnp.einsum("bhjk,bhkv->bhjv", qb, C) + jnp.einsum("bhjs,bhsv->bhjv", W, vb)
        den = inter * jnp.einsum("bhjk,bhk->bhj", qb, n) + jnp.sum(W, axis=-1)
        h = num / jnp.maximum(jnp.abs(den), jnp.exp(-m_row))[..., None]
        bL = b[..., -1]
        dL = bL[..., None] - b + ib
        m_new = jnp.maximum(bL + m, jnp.max(dL, axis=-1))
        decay = jnp.exp(bL + m - m_new)
        wL = jnp.exp(dL - m_new[..., None])
        C_new = decay[..., None, None] * C + jnp.einsum("bhs,bhsk,bhsv->bhkv", wL, kb, vb)
        n_new = decay[..., None] * n + jnp.einsum("bhs,bhsk->bhk", wL, kb)
        return (C_new, n_new, m_new), h

    init = (jnp.zeros((B, H, d, d), jnp.float32), jnp.zeros((B, H, d), jnp.float32),
            jnp.zeros((B, H), jnp.float32))
    _, hc = lax.scan(step, init, (qc, kc, vc, ic, lfc))
    return jnp.moveaxis(hc, (0, 2), (1, 3)).reshape(B, S, H, d)


def mlstm_branch(u, o_pre, i_pre, f_pre, conv_w, conv_b, w_q, w_k, w_v, b_i, b_f, norm_g):
    B, S, _ = u.shape
    c = jax.nn.silu(causal_dwconv(u, conv_w, conv_b))
    q = block_diag(c, w_q).reshape(B, S, ML_HEADS, ML_HEAD_DIM).astype(jnp.float32)
    k = block_diag(c, w_k).reshape(B, S, ML_HEADS, ML_HEAD_DIM).astype(jnp.float32) * (ML_HEAD_DIM ** -0.5)
    v = block_diag(u, w_v).reshape(B, S, ML_HEADS, ML_HEAD_DIM).astype(jnp.float32)
    ig = i_pre.astype(jnp.float32) + b_i.astype(jnp.float32)
    fg = f_pre.astype(jnp.float32) + b_f.astype(jnp.float32)
    h = mlstm_chunkwise(q, k, v, ig, fg)
    o = jax.nn.sigmoid(o_pre.astype(jnp.float32)).reshape(B, S, ML_HEADS, ML_HEAD_DIM)
    h = o * h
    h = h * lax.rsqrt(jnp.mean(h * h, axis=-1, keepdims=True) + EPS)
    return (h.reshape(B, S, D_ML) * norm_g.astype(jnp.float32)).astype(u.dtype)


def memory_attention(q, mem_n, w_kv):
    B, S, _ = q.shape
    M = mem_n.shape[1]
    kv = (mem_n @ w_kv).reshape(B, M, 2, XA_HEADS, XA_HEAD_DIM)
    k, v = kv[:, :, 0], kv[:, :, 1]
    qh = q.reshape(B, S, XA_HEADS, XA_HEAD_DIM)
    s = jnp.einsum("bshd,bmhd->bhsm", qh, k).astype(jnp.float32) * (XA_HEAD_DIM ** -0.5)
    p = jax.nn.softmax(s, axis=-1).astype(q.dtype)
    o = jnp.einsum("bhsm,bmhd->bshd", p, v)
    return o.reshape(B, S, D_XA)


def swiglu(x, w1, w3, w2):
    return (jax.nn.silu(x @ w1) * (x @ w3)) @ w2


def moe_swiglu(x2, router_w, router_b, w1, w3, w2):
    T, D = x2.shape
    logits = (x2 @ router_w).astype(jnp.float32) + router_b.astype(jnp.float32)
    top_val, top_idx = lax.top_k(logits, TOP_K)
    gates = jax.nn.softmax(top_val, axis=-1).astype(x2.dtype)
    A = T * TOP_K
    e_flat = top_idx.reshape(A).astype(jnp.int32)
    g_flat = gates.reshape(A)
    tok_flat = jnp.arange(A, dtype=jnp.int32) // TOP_K
    order = jnp.argsort(e_flat)
    e_sorted = e_flat[order]
    tok_sorted = tok_flat[order]
    g_sorted = g_flat[order]
    counts = jnp.bincount(e_flat, length=N_EXPERTS).astype(jnp.int32)
    padded = ((counts + MOE_BLOCK - 1) // MOE_BLOCK) * MOE_BLOCK
    start = jnp.cumsum(counts) - counts
    pend = jnp.cumsum(padded)
    pstart = pend - padded
    dest = pstart[e_sorted] + (jnp.arange(A, dtype=jnp.int32) - start[e_sorted])
    P = A + N_EXPERTS * MOE_BLOCK
    NB = P // MOE_BLOCK
    buf_tok = jnp.zeros((P,), jnp.int32).at[dest].set(tok_sorted)
    buf_g = jnp.zeros((P,), x2.dtype).at[dest].set(g_sorted)
    blk_start = jnp.arange(NB, dtype=jnp.int32) * MOE_BLOCK
    blk_e = jnp.minimum(jnp.searchsorted(pend, blk_start, side="right"), N_EXPERTS - 1).astype(jnp.int32)

    def expert_block(args):
        tok_b, e_b = args
        xb = x2[tok_b]
        return swiglu(xb, w1[e_b], w3[e_b], w2[e_b])

    y_blk = lax.map(expert_block, (buf_tok.reshape(NB, MOE_BLOCK), blk_e))
    y = y_blk.reshape(P, D) * buf_g[:, None]
    return jnp.zeros((T, D), x2.dtype).at[buf_tok].add(y)


def setup_inputs(seed: int = 0) -> dict:
    key = jax.random.key(seed)
    ks = iter(jax.random.split(key, 48))
    f32 = jnp.float32

    def nrm(shape, scale):
        return jax.random.normal(next(ks), shape, f32) * scale

    def gain(shape):
        return 1.0 + nrm(shape, 0.05)

    res_scale = (2.0 * DEPTH) ** -0.5
    u = jax.random.uniform(next(ks), (DEPTH, D_RG), f32, minval=0.9, maxval=0.999)
    p = u ** (1.0 / RG_C)
    rg_lambda = jnp.log(p) - jnp.log1p(-p)
    ml_b_f = jnp.linspace(3.0, 6.0, ML_HEADS, dtype=f32)[None, :] + nrm((DEPTH, ML_HEADS), 0.1)
    return {
        "x": nrm((BATCH, SEQ, D_MODEL), 1.0),
        "mem": nrm((BATCH, MEM_LEN, D_MODEL), 1.0),
        "norm_mix_g": gain((DEPTH, D_MODEL)),
        "w_in": nrm((DEPTH, D_MODEL, N_IN), D_MODEL ** -0.5),
        "conv_rg_w": nrm((DEPTH, CONV_W, D_RG), CONV_W ** -0.5),
        "conv_rg_b": nrm((DEPTH, D_RG), 0.02),
        "rg_w_a": nrm((DEPTH, RG_BLOCKS, RG_BLOCK, RG_BLOCK), RG_BLOCK ** -0.5),
        "rg_b_a": nrm((DEPTH, D_RG), 0.1),
        "rg_w_x": nrm((DEPTH, RG_BLOCKS, RG_BLOCK, RG_BLOCK), RG_BLOCK ** -0.5),
        "rg_b_x": nrm((DEPTH, D_RG), 0.1),
        "rg_lambda": rg_lambda,
        "conv_ml_w": nrm((DEPTH, CONV_W, D_ML), CONV_W ** -0.5),
        "conv_ml_b": nrm((DEPTH, D_ML), 0.02),
        "ml_w_q": nrm((DEPTH, ML_HEADS, ML_HEAD_DIM, ML_HEAD_DIM), ML_HEAD_DIM ** -0.5),
        "ml_w_k": nrm((DEPTH, ML_HEADS, ML_HEAD_DIM, ML_HEAD_DIM), ML_HEAD_DIM ** -0.5),
        "ml_w_v": nrm((DEPTH, ML_HEADS, ML_HEAD_DIM, ML_HEAD_DIM), ML_HEAD_DIM ** -0.5),
        "ml_b_i": nrm((DEPTH, ML_HEADS), 0.1),
        "ml_b_f": ml_b_f,
        "ml_norm_g": gain((DEPTH, D_ML)),
        "mem_norm_g": gain((DEPTH, D_MODEL)),
        "w_kv": nrm((DEPTH, D_MODEL, 2 * D_XA), D_MODEL ** -0.5),
        "w_br_rg": nrm((DEPTH, D_RG, D_MODEL), D_RG ** -0.5),
        "w_br_ml": nrm((DEPTH, D_ML, D_MODEL), D_ML ** -0.5),
        "w_br_xa": nrm((DEPTH, D_XA, D_MODEL), D_XA ** -0.5),
        "b_merge": nrm((DEPTH, N_BRANCH * D_MODEL), 0.1),
        "w_out": nrm((DEPTH, D_MODEL, D_MODEL), D_MODEL ** -0.5 * res_scale),
        "norm_ffn_g": gain((DEPTH, D_MODEL)),
        "ffn_w1": nrm((N_DENSE, D_MODEL, D_FF), D_MODEL ** -0.5),
        "ffn_w3": nrm((N_DENSE, D_MODEL, D_FF), D_MODEL ** -0.5),
        "ffn_w2": nrm((N_DENSE, D_FF, D_MODEL), D_FF ** -0.5 * res_scale),
        "router_w": nrm((N_MOE, D_MODEL, N_EXPERTS), D_MODEL ** -0.5),
        "router_b": nrm((N_MOE, N_EXPERTS), 0.01),
        "moe_w1": nrm((N_MOE, N_EXPERTS, D_MODEL, D_FF), D_MODEL ** -0.5),
        "moe_w3": nrm((N_MOE, N_EXPERTS, D_MODEL, D_FF), D_MODEL ** -0.5),
        "moe_w2": nrm((N_MOE, N_EXPERTS, D_FF, D_MODEL), D_FF ** -0.5 * res_scale),
        "final_norm_g": gain((D_MODEL,)),
    }


def reference(x, mem, norm_mix_g, w_in, conv_rg_w, conv_rg_b, rg_w_a, rg_b_a, rg_w_x, rg_b_x,
              rg_lambda, conv_ml_w, conv_ml_b, ml_w_q, ml_w_k, ml_w_v, ml_b_i, ml_b_f, ml_norm_g,
              mem_norm_g, w_kv, w_br_rg, w_br_ml, w_br_xa, b_merge, w_out, norm_ffn_g,
              ffn_w1, ffn_w3, ffn_w2, router_w, router_b, moe_w1, moe_w3, moe_w2, final_norm_g):
    B, S, D = x.shape
    for l in range(DEPTH):
        h = rms_norm(x, norm_mix_g[l])
        z = h @ w_in[l]
        a_x, a_y, m_u, m_o, m_i, m_f, xa_q, gate_pre = split_columns(z)
        xa_c = causal_dwconv(a_x, conv_rg_w[l], conv_rg_b[l])
        y_rg = rg_lru(xa_c, rg_w_a[l], rg_b_a[l], rg_w_x[l], rg_b_x[l], rg_lambda[l]) * jax.nn.gelu(a_y)
        y_ml = mlstm_branch(m_u, m_o, m_i, m_f, conv_ml_w[l], conv_ml_b[l], ml_w_q[l], ml_w_k[l],
                            ml_w_v[l], ml_b_i[l], ml_b_f[l], ml_norm_g[l])
        mem_n = rms_norm(mem, mem_norm_g[l])
        y_xa = memory_attention(xa_q, mem_n, w_kv[l])
        gates = jax.nn.sigmoid((gate_pre + b_merge[l]).astype(jnp.float32)).astype(x.dtype)
        gates = gates.reshape(B, S, N_BRANCH, D)
        merged = (gates[:, :, 0] * (y_rg @ w_br_rg[l])
                  + gates[:, :, 1] * (y_ml @ w_br_ml[l])
                  + gates[:, :, 2] * (y_xa @ w_br_xa[l]))
        x = x + merged @ w_out[l]
        h2 = rms_norm(x, norm_ffn_g[l])
        j = l // 2
        if l % 2 == 0:
            f = swiglu(h2, ffn_w1[j], ffn_w3[j], ffn_w2[j])
        else:
            f = moe_swiglu(h2.reshape(B * S, D), router_w[j], router_b[j],
                           moe_w1[j], moe_w3[j], moe_w2[j]).reshape(B, S, D)
        x = x + f
    return rms_norm(x, final_norm_g)
```

```python
import functools

import jax
import jax.numpy as jnp
from jax import lax
from jax.experimental import pallas as pl
from jax.experimental.pallas import tpu as pltpu

EPS = 1e-6
RG_C = 8.0
CONV_W = 4
ML_CHUNK = 128
XA_HEADS = 4
TOP_K = 2
N_BRANCH = 3

V7X_VMEM_BYTES = 64 * 1024 * 1024
LANES = 128
SUBLANES = 8

F32 = jnp.float32
BF16 = jnp.bfloat16


def _params(semantics, vmem_mib):
    assert vmem_mib * 1024 * 1024 <= V7X_VMEM_BYTES
    return pltpu.CompilerParams(dimension_semantics=semantics,
                                vmem_limit_bytes=vmem_mib * 1024 * 1024)


def _const_spec(shape):
    nd = len(shape)
    return pl.BlockSpec(shape, lambda *_: (0,) * nd)


def _rms(x, g):
    ms = jnp.mean(x * x, axis=-1, keepdims=True)
    return x * lax.rsqrt(ms + EPS) * g


def _dot(a, b):
    return jnp.dot(a, b, preferred_element_type=F32)


def _dot_nt(a, b):
    return lax.dot_general(a, b, (((1,), (1,)), ((), ())), preferred_element_type=F32)


def _dot_tn(a, b):
    return lax.dot_general(a, b, (((0,), (0,)), ((), ())), preferred_element_type=F32)


def _block_diag(xb, w_ref):
    G, bi, _ = w_ref.shape
    return jnp.concatenate([_dot(xb[:, g * bi:(g + 1) * bi], w_ref[g]) for g in range(G)], axis=1)


def _split3(x):
    h1 = x.astype(BF16)
    r1 = x - h1.astype(F32)
    h2 = r1.astype(BF16)
    h3 = (r1 - h2.astype(F32)).astype(BF16)
    return h1, h2, h3


def _rg_kernel(x_ref, g_ref, wax_ref, way_ref, cw_ref, cb_ref, wa_ref, ba_ref, wx_ref, bx_ref,
               lam_ref, o_ref, axbuf, a_s, b_s, h_s, carry, *, B, TT):
    R = TT * B
    halo = (CONV_W - 1) * B

    @pl.when(pl.program_id(0) == 0)
    def _():
        axbuf[0:halo, :] = jnp.zeros((halo, axbuf.shape[1]), F32)
        carry[...] = jnp.zeros_like(carry)

    h = _rms(x_ref[...], g_ref[...]).astype(BF16)
    axbuf[halo:halo + R, :] = _dot(h, wax_ref[...])
    cw = cw_ref[...]
    xc = cb_ref[...] + cw[0:1, :] * axbuf[0:R, :]
    for k in range(1, CONV_W):
        xc = xc + cw[k:k + 1, :] * axbuf[k * B:k * B + R, :]
    axbuf[0:halo, :] = axbuf[R:R + halo, :]

    xcb = xc.astype(BF16)
    r = jax.nn.sigmoid(_block_diag(xcb, wa_ref) + ba_ref[...])
    ig = jax.nn.sigmoid(_block_diag(xcb, wx_ref) + bx_ref[...])
    log_a = -RG_C * r * jax.nn.softplus(-lam_ref[...])
    a = jnp.exp(log_a)
    a_s[...] = a
    b_s[...] = jnp.sqrt(-jnp.tanh(log_a) * (a * a + 1.0)) * (ig * xc)

    def step(t, hc):
        off = pl.multiple_of(t * B, B)
        hn = a_s[pl.ds(off, B), :] * hc + b_s[pl.ds(off, B), :]
        h_s[pl.ds(off, B), :] = hn
        return hn

    carry[...] = lax.fori_loop(0, TT, step, carry[...], unroll=8)
    ay = _dot(h, way_ref[...])
    o_ref[...] = (h_s[...] * jax.nn.gelu(ay)).astype(BF16)


def _rg_branch(x_tm, B, norm_g, w_ax, w_ay, conv_w, conv_b, w_a, b_a, w_x, b_x, lam):
    SB, D = x_tm.shape
    S = SB // B
    C = w_ax.shape[1]
    TT = min(64, S)
    assert S % TT == 0 and B % SUBLANES == 0
    R = TT * B
    halo = (CONV_W - 1) * B
    row = lambda i: (i, 0)
    return pl.pallas_call(
        functools.partial(_rg_kernel, B=B, TT=TT),
        out_shape=jax.ShapeDtypeStruct((SB, C), BF16),
        grid=(S // TT,),
        in_specs=[pl.BlockSpec((R, D), row), _const_spec((1, D)), _const_spec((D, C)),
                  _const_spec((D, C)), _const_spec((CONV_W, C)), _const_spec((1, C)),
                  _const_spec(w_a.shape), _const_spec((1, C)), _const_spec(w_x.shape),
                  _const_spec((1, C)), _const_spec((1, C))],
        out_specs=pl.BlockSpec((R, C), row),
        scratch_shapes=[pltpu.VMEM((halo + R, C), F32), pltpu.VMEM((R, C), F32),
                        pltpu.VMEM((R, C), F32), pltpu.VMEM((R, C), F32), pltpu.VMEM((B, C), F32)],
        compiler_params=_params(("arbitrary",), 48),
        name="rg_branch",
    )(x_tm, norm_g, w_ax, w_ay, conv_w, conv_b, w_a, b_a, w_x, b_x, lam)


def _mlstm_kernel(x_ref, g_ref, wmu_ref, wmo_ref, wif_ref, wift_ref, bifc_ref, bifr_ref, cw_ref,
                  cb_ref, wq_ref, wk_ref, wv_ref, ng_ref, o_ref, ubuf, c_st, n_st, m_st, *, TS):
    H, d, _ = wq_ref.shape
    L = ML_CHUNK
    pad = SUBLANES
    tail = CONV_W - 1

    @pl.when(pl.program_id(1) == 0)
    def _():
        ubuf[0:pad, :] = jnp.zeros((pad, ubuf.shape[1]), F32)
        c_st[...] = jnp.zeros_like(c_st)
        n_st[...] = jnp.zeros_like(n_st)
        m_st[...] = jnp.zeros_like(m_st)

    h = _rms(x_ref[...], g_ref[...]).astype(BF16)
    u = _dot(h, wmu_ref[...])
    ubuf[pad:pad + TS, :] = u
    cw = cw_ref[...]
    c = cb_ref[...] + cw[0:1, :] * ubuf[pad - tail:pad - tail + TS, :]
    for k in range(1, CONV_W):
        c = c + cw[k:k + 1, :] * ubuf[pad - tail + k:pad - tail + k + TS, :]
    ubuf[0:pad, :] = ubuf[TS:TS + pad, :]
    c = jax.nn.silu(c)
    cb16 = c.astype(BF16)
    q = _block_diag(cb16, wq_ref)
    k_ = _block_diag(cb16, wk_ref) * (d ** -0.5)
    v = _block_diag(u.astype(BF16), wv_ref)
    og = jax.nn.sigmoid(_dot(h, wmo_ref[...]))
    if_c = _dot(h, wif_ref[...]) + bifc_ref[...]
    if_r = _dot_nt(wift_ref[...], h) + bifr_ref[...]
    lf_c = jax.nn.log_sigmoid(if_c)
    lf_r = jax.nn.log_sigmoid(if_r)
    ri = lax.broadcasted_iota(jnp.int32, (L, L), 0)
    ci = lax.broadcasted_iota(jnp.int32, (L, L), 1)
    causal = ci <= ri
    tri_l = jnp.where(causal, 1.0, 0.0).astype(BF16)
    tri_u = jnp.where(ri <= ci, 1.0, 0.0).astype(BF16)
    ng = ng_ref[...]

    for ck in range(TS // L):
        r0 = ck * L
        bc = sum(_dot(tri_l, p) for p in _split3(lf_c[r0:r0 + L, :]))
        br = sum(_dot(p, tri_u) for p in _split3(lf_r[:, r0:r0 + L]))
        for hd in range(H):
            b_col = bc[:, H + hd:H + hd + 1]
            b_row = br[H + hd:H + hd + 1, :]
            i_col = if_c[r0:r0 + L, hd:hd + 1]
            i_row = if_r[hd:hd + 1, r0:r0 + L]
            m = m_st[hd][0:1, 0:1]
            cs = slice(hd * d, (hd + 1) * d)
            qf = q[r0:r0 + L, cs]
            kf = k_[r0:r0 + L, cs]
            vf = v[r0:r0 + L, cs]
            qb, kb, vb = qf.astype(BF16), kf.astype(BF16), vf.astype(BF16)
            g = b_col + m
            dm = jnp.where(causal, b_col - b_row + i_row, -jnp.inf)
            m_row = jnp.maximum(g, jnp.max(dm, axis=-1, keepdims=True))
            w = jnp.exp(dm - m_row) * _dot_nt(qb, kb)
            inter = jnp.exp(g - m_row)
            cmat = c_st[hd]
            nvec = n_st[hd]
            num = inter * _dot(qb, cmat.astype(BF16)) + _dot(w.astype(BF16), vb)
            den = inter * jnp.sum(qf * nvec, axis=-1, keepdims=True) + jnp.sum(w, axis=-1, keepdims=True)
            hh = num / jnp.maximum(jnp.abs(den), jnp.exp(-m_row))
            b_last = b_col[L - 1:L, :]
            dl = b_last - b_col + i_col
            m_new = jnp.maximum(b_last + m, jnp.max(dl, axis=0, keepdims=True))
            decay = jnp.exp(b_last + m - m_new)
            wl = jnp.exp(dl - m_new)
            c_st[hd] = decay * cmat + _dot_tn(kb, (wl * vf).astype(BF16))
            n_st[hd] = decay * nvec + jnp.sum(wl * kf, axis=0, keepdims=True)
            m_st[hd] = jnp.broadcast_to(m_new, m_st.shape[1:])
            y = og[r0:r0 + L, cs] * hh
            y = y * lax.rsqrt(jnp.mean(y * y, axis=-1, keepdims=True) + EPS)
            o_ref[r0:r0 + L, cs] = (y * ng[:, cs]).astype(BF16)


def _mlstm_branch(x2d, B, norm_g, w_mu, w_mo, w_if, w_ift, b_if_c, b_if_r, conv_w, conv_b,
                  w_q, w_k, w_v, ml_norm_g):
    T, D = x2d.shape
    S = T // B
    C = w_mu.shape[1]
    H, d, _ = w_q.shape
    TS = min(256, S)
    assert S % TS == 0 and TS % ML_CHUNK == 0
    nS = S // TS
    row = lambda b, s: (b * nS + s, 0)
    return pl.pallas_call(
        functools.partial(_mlstm_kernel, TS=TS),
        out_shape=jax.ShapeDtypeStruct((T, C), BF16),
        grid=(B, nS),
        in_specs=[pl.BlockSpec((TS, D), row), _const_spec((1, D)), _const_spec((D, C)),
                  _const_spec((D, C)), _const_spec((D, LANES)), _const_spec((SUBLANES, D)),
                  _const_spec((1, LANES)), _const_spec((SUBLANES, 1)), _const_spec((CONV_W, C)),
                  _const_spec((1, C)), _const_spec(w_q.shape), _const_spec(w_k.shape),
                  _const_spec(w_v.shape), _const_spec((1, C))],
        out_specs=pl.BlockSpec((TS, C), row),
        scratch_shapes=[pltpu.VMEM((SUBLANES + TS, C), F32), pltpu.VMEM((H, d, d), F32),
                        pltpu.VMEM((H, 1, d), F32), pltpu.VMEM((H, SUBLANES, LANES), F32)],
        compiler_params=_params(("arbitrary", "arbitrary"), 48),
        name="mlstm_branch",
    )(x2d, norm_g, w_mu, w_mo, w_if, w_ift, b_if_c, b_if_r, conv_w, conv_b, w_q, w_k, w_v, ml_norm_g)


def _kv_kernel(mem_ref, g_ref, w_ref, o_ref):
    o_ref[...] = _dot(_rms(mem_ref[...], g_ref[...]).astype(BF16), w_ref[...]).astype(BF16)


def _mem_kv(mem, g, w_kv):
    B, M, D = mem.shape
    N = w_kv.shape[1]
    return pl.pallas_call(
        _kv_kernel,
        out_shape=jax.ShapeDtypeStruct((B, M, N), BF16),
        grid=(B,),
        in_specs=[pl.BlockSpec((None, M, D), lambda b: (b, 0, 0)), _const_spec((1, D)),
                  _const_spec((D, N))],
        out_specs=pl.BlockSpec((None, M, N), lambda b: (b, 0, 0)),
        compiler_params=_params(("arbitrary",), 32),
        name="mem_kv",
    )(mem, g, w_kv)


def _merge_kernel(x_ref, g_ref, wq_ref, wg_ref, bm_ref, yrg_ref, yml_ref, kv_ref, wrg_ref, wml_ref,
                  wxa_ref, wo_ref, o_ref):
    x = x_ref[...]
    D = x.shape[1]
    h = _rms(x, g_ref[...]).astype(BF16)
    q = _dot(h, wq_ref[...]).astype(BF16)
    dxa = q.shape[1]
    dh = dxa // XA_HEADS
    heads = []
    for hd in range(XA_HEADS):
        kh = kv_ref[:, hd * dh:(hd + 1) * dh]
        vh = kv_ref[:, dxa + hd * dh:dxa + (hd + 1) * dh]
        s = _dot_nt(q[:, hd * dh:(hd + 1) * dh], kh) * (dh ** -0.5)
        e = jnp.exp(s - jnp.max(s, axis=-1, keepdims=True))
        p = e / jnp.sum(e, axis=-1, keepdims=True)
        heads.append(_dot(p.astype(BF16), vh))
    y_xa = jnp.concatenate(heads, axis=1).astype(BF16)

    def gate(k):
        return jax.nn.sigmoid(_dot(h, wg_ref[:, k * D:(k + 1) * D]) + bm_ref[:, k * D:(k + 1) * D])

    merged = gate(0) * _dot(yrg_ref[...], wrg_ref[...])
    merged = merged + gate(1) * _dot(yml_ref[...], wml_ref[...])
    merged = merged + gate(2) * _dot(y_xa, wxa_ref[...])
    o_ref[...] = x + _dot(merged.astype(BF16), wo_ref[...])


def _merge(x2d, B, norm_g, w_q, w_g, b_merge, y_rg_tm, y_ml, kv, w_br_rg, w_br_ml, w_br_xa, w_out):
    T, D = x2d.shape
    S = T // B
    C = y_ml.shape[1]
    M, N = kv.shape[1:]
    TM = min(512, S)
    assert S % TM == 0
    nS = S // TM
    row = lambda b, s: (b * nS + s, 0)
    one = pl.Buffered(1)
    cspec = lambda shape: pl.BlockSpec(shape, lambda *_: (0,) * len(shape), pipeline_mode=one)
    return pl.pallas_call(
        _merge_kernel,
        out_shape=jax.ShapeDtypeStruct((T, D), F32),
        grid=(B, nS),
        in_specs=[pl.BlockSpec((TM, D), row), cspec((1, D)), cspec(w_q.shape), cspec(w_g.shape),
                  cspec(b_merge.shape), pl.BlockSpec((TM, C), lambda b, s: (s, b)),
                  pl.BlockSpec((TM, C), row), pl.BlockSpec((None, M, N), lambda b, s: (b, 0, 0)),
                  cspec(w_br_rg.shape), cspec(w_br_ml.shape), cspec(w_br_xa.shape),
                  cspec(w_out.shape)],
        out_specs=pl.BlockSpec((TM, D), row),
        compiler_params=_params(("arbitrary", "arbitrary"), 56),
        name="merge",
    )(x2d, norm_g, w_q, w_g, b_merge, y_rg_tm, y_ml, kv, w_br_rg, w_br_ml, w_br_xa, w_out)


def _ffn_kernel(x_ref, g_ref, w1_ref, w3_ref, w2_ref, o_ref, hs, acc):
    f = pl.program_id(1)

    @pl.when(f == 0)
    def _():
        hs[...] = _rms(x_ref[...], g_ref[...]).astype(BF16)
        acc[...] = jnp.zeros_like(acc)

    hb = hs[...]
    mid = jax.nn.silu(_dot(hb, w1_ref[...])) * _dot(hb, w3_ref[...])
    acc[...] += _dot(mid.astype(BF16), w2_ref[...])

    @pl.when(f == pl.num_programs(1) - 1)
    def _():
        o_ref[...] = x_ref[...] + acc[...]


def _ffn(x2d, norm_g, w1, w3, w2):
    T, D = x2d.shape
    F = w1.shape[1]
    TM = min(1024, T)
    TF = 512 if F % 512 == 0 else F
    assert T % TM == 0
    return pl.pallas_call(
        _ffn_kernel,
        out_shape=jax.ShapeDtypeStruct((T, D), F32),
        grid=(T // TM, F // TF),
        in_specs=[pl.BlockSpec((TM, D), lambda i, f: (i, 0)), _const_spec((1, D)),
                  pl.BlockSpec((D, TF), lambda i, f: (0, f)), pl.BlockSpec((D, TF), lambda i, f: (0, f)),
                  pl.BlockSpec((TF, D), lambda i, f: (f, 0))],
        out_specs=pl.BlockSpec((TM, D), lambda i, f: (i, 0)),
        scratch_shapes=[pltpu.VMEM((TM, D), BF16), pltpu.VMEM((TM, D), F32)],
        compiler_params=_params(("arbitrary", "arbitrary"), 48),
        name="ffn_dense",
    )(x2d, norm_g, w1, w3, w2)


def _route_kernel(x_ref, g_ref, whi_ref, wlo_ref, rb_ref, idx_ref, gate_ref, cnt_ref, cnt_s, *, E):
    @pl.when(pl.program_id(0) == 0)
    def _():
        cnt_s[...] = jnp.zeros_like(cnt_s)

    h2 = _rms(x_ref[...], g_ref[...])
    TM = h2.shape[0]
    hi = h2.astype(BF16)
    lo = (h2 - hi.astype(F32)).astype(BF16)
    logits = _dot(hi, whi_ref[...]) + (_dot(lo, whi_ref[...]) + _dot(hi, wlo_ref[...]))
    lt = logits.T[0:E, :] + rb_ref[...]
    ie = lax.broadcasted_iota(jnp.int32, (E, TM), 0)
    m1 = jnp.max(lt, axis=0, keepdims=True)
    i1 = jnp.min(jnp.where(lt == m1, ie, E), axis=0, keepdims=True)
    l2 = jnp.where(ie == i1, -jnp.inf, lt)
    m2 = jnp.max(l2, axis=0, keepdims=True)
    i2 = jnp.min(jnp.where(l2 == m2, ie, E), axis=0, keepdims=True)
    ex = jnp.exp(m2 - m1)
    g1 = 1.0 / (1.0 + ex)
    g2 = ex / (1.0 + ex)
    oh1 = jnp.where(ie == i1, 1.0, 0.0)
    oh2 = jnp.where(ie == i2, 1.0, 0.0)
    oh = oh1 + oh2
    ri = lax.broadcasted_iota(jnp.int32, (TM, TM), 0)
    ci = lax.broadcasted_iota(jnp.int32, (TM, TM), 1)
    upper = jnp.where(ri < ci, 1.0, 0.0).astype(BF16)
    excl = _dot(oh.astype(BF16), upper) + cnt_s[:, 0:1]
    r1 = jnp.sum(oh1 * excl, axis=0, keepdims=True)
    r2 = jnp.sum(oh2 * excl, axis=0, keepdims=True)
    cnt_new = cnt_s[...] + jnp.sum(oh, axis=1, keepdims=True)
    cnt_s[...] = cnt_new
    cnt_ref[...] = cnt_new.astype(jnp.int32)
    row = lax.broadcasted_iota(jnp.int32, (SUBLANES, TM), 0)
    r1i = r1.astype(jnp.int32)
    r2i = r2.astype(jnp.int32)
    idx_ref[...] = jnp.where(row == 0, i1, jnp.where(row == 1, i2, jnp.where(row == 2, r1i,
                             jnp.where(row == 3, r2i, 0))))
    gate_ref[...] = jnp.where(row == 0, g1, jnp.where(row == 1, g2, 0.0))


def _route(x2d, norm_g, w_hi, w_lo, rb, E):
    T, D = x2d.shape
    TM = min(512, T)
    assert T % TM == 0 and E <= SUBLANES
    return pl.pallas_call(
        functools.partial(_route_kernel, E=E),
        out_shape=(jax.ShapeDtypeStruct((SUBLANES, T), jnp.int32),
                   jax.ShapeDtypeStruct((SUBLANES, T), F32),
                   jax.ShapeDtypeStruct((E, LANES), jnp.int32)),
        grid=(T // TM,),
        in_specs=[pl.BlockSpec((TM, D), lambda i: (i, 0)), _const_spec((1, D)),
                  _const_spec((D, LANES)), _const_spec((D, LANES)), _const_spec((E, 1))],
        out_specs=(pl.BlockSpec((SUBLANES, TM), lambda i: (0, i)),
                   pl.BlockSpec((SUBLANES, TM), lambda i: (0, i)),
                   _const_spec((E, LANES))),
        scratch_shapes=[pltpu.VMEM((E, LANES), F32)],
        compiler_params=_params(("arbitrary",), 32),
        name="moe_route",
    )(x2d, norm_g, w_hi, w_lo, rb)


def _dispatch_kernel(pos_ref, x_ref, g_ref, init_ref, xs_ref, h2_s, sem):
    del init_ref
    TM = x_ref.shape[0]
    h2_s[...] = _rms(x_ref[...], g_ref[...])

    def copy(t, k):
        return pltpu.make_async_copy(h2_s.at[pl.ds(t, 1), :],
                                     xs_ref.at[pl.ds(pos_ref[k, t], 1), :], sem)

    def start(t, c):
        for k in range(TOP_K):
            copy(t, k).start()
        return c

    def wait(t, c):
        for k in range(TOP_K):
            copy(t, k).wait()
        return c

    lax.fori_loop(0, TM, start, 0)
    lax.fori_loop(0, TM, wait, 0)


def _dispatch(pos, x2d, norm_g, P):
    T, D = x2d.shape
    TM = min(256, T)
    assert T % TM == 0
    init = jnp.zeros((P, D), F32)
    return pl.pallas_call(
        _dispatch_kernel,
        out_shape=jax.ShapeDtypeStruct((P, D), F32),
        grid=(T // TM,),
        in_specs=[pl.BlockSpec((TOP_K, TM), lambda i: (0, i), memory_space=pltpu.SMEM),
                  pl.BlockSpec((TM, D), lambda i: (i, 0)), _const_spec((1, D)),
                  pl.BlockSpec(memory_space=pl.ANY)],
        out_specs=pl.BlockSpec(memory_space=pl.ANY),
        scratch_shapes=[pltpu.VMEM((TM, D), F32), pltpu.SemaphoreType.DMA(())],
        input_output_aliases={3: 0},
        compiler_params=_params(("arbitrary",), 32),
        name="moe_dispatch",
    )(pos, x2d, norm_g, init)


def _group_ffn_kernel(te_ref, tn_ref, x_ref, w1_ref, w3_ref, w2_ref, o_ref, hs, acc):
    i = pl.program_id(0)
    f = pl.program_id(1)

    @pl.when(f == 0)
    def _():
        hs[...] = x_ref[...].astype(BF16)
        acc[...] = jnp.zeros_like(acc)

    @pl.when(tn_ref[i] > 0)
    def _():
        hb = hs[...]
        mid = jax.nn.silu(_dot(hb, w1_ref[...])) * _dot(hb, w3_ref[...])
        acc[...] += _dot(mid.astype(BF16), w2_ref[...])

    @pl.when(f == pl.num_programs(1) - 1)
    def _():
        o_ref[...] = acc[...]


def _group_ffn(tile_e, tile_n, xs, w1, w3, w2, TMG):
    P, D = xs.shape
    F = w1.shape[2]
    TF = 512 if F % 512 == 0 else F
    grid_spec = pltpu.PrefetchScalarGridSpec(
        num_scalar_prefetch=2,
        grid=(P // TMG, F // TF),
        in_specs=[pl.BlockSpec((TMG, D), lambda i, f, te, tn: (i, 0)),
                  pl.BlockSpec((None, D, TF), lambda i, f, te, tn: (te[i], 0, f)),
                  pl.BlockSpec((None, D, TF), lambda i, f, te, tn: (te[i], 0, f)),
                  pl.BlockSpec((None, TF, D), lambda i, f, te, tn: (te[i], f, 0))],
        out_specs=pl.BlockSpec((TMG, D), lambda i, f, te, tn: (i, 0)),
        scratch_shapes=[pltpu.VMEM((TMG, D), BF16), pltpu.VMEM((TMG, D), F32)])
    return pl.pallas_call(
        _group_ffn_kernel,
        out_shape=jax.ShapeDtypeStruct((P, D), F32),
        grid_spec=grid_spec,
        compiler_params=_params(("arbitrary", "arbitrary"), 48),
        name="moe_group_ffn",
    )(tile_e, tile_n, xs, w1, w3, w2)


def _combine_kernel(pos_ref, x_ref, gt_ref, fg_ref, ys_ref, o_ref, buf, sem, *, final_norm):
    TM = x_ref.shape[0]

    def copy(t, k):
        return pltpu.make_async_copy(ys_ref.at[pl.ds(pos_ref[k, t], 1), :],
                                     buf.at[k, pl.ds(t, 1), :], sem)

    def start(t, c):
        for k in range(TOP_K):
            copy(t, k).start()
        return c

    def wait(t, c):
        for k in range(TOP_K):
            copy(t, k).wait()
        return c

    lax.fori_loop(0, TM, start, 0)
    lax.fori_loop(0, TM, wait, 0)
    gt = gt_ref[...]
    f = gt[:, 0:1] * buf[0] + gt[:, 1:2] * buf[1]
    xo = x_ref[...] + f
    o_ref[...] = _rms(xo, fg_ref[...]) if final_norm else xo


def _combine(pos, x2d, gates_t, final_g, ys, final_norm):
    T, D = x2d.shape
    TM = min(256, T)
    assert T % TM == 0
    return pl.pallas_call(
        functools.partial(_combine_kernel, final_norm=final_norm),
        out_shape=jax.ShapeDtypeStruct((T, D), F32),
        grid=(T // TM,),
        in_specs=[pl.BlockSpec((TOP_K, TM), lambda i: (0, i), memory_space=pltpu.SMEM),
                  pl.BlockSpec((TM, D), lambda i: (i, 0)),
                  pl.BlockSpec((TM, SUBLANES), lambda i: (i, 0)), _const_spec((1, D)),
                  pl.BlockSpec(memory_space=pl.ANY)],
        out_specs=pl.BlockSpec((TM, D), lambda i: (i, 0)),
        scratch_shapes=[pltpu.VMEM((TOP_K, TM, D), F32), pltpu.SemaphoreType.DMA(())],
        compiler_params=_params(("arbitrary",), 32),
        name="moe_combine",
    )(pos, x2d, gates_t, final_g, ys)


def _moe(x2d, norm_g, router_w, router_b, w1, w3, w2, final_g, final_norm):
    T, D = x2d.shape
    E = router_w.shape[1]
    TMG = 512
    rw = jnp.pad(router_w, ((0, 0), (0, LANES - E)))
    rw_hi = rw.astype(BF16)
    rw_lo = (rw - rw_hi.astype(F32)).astype(BF16)
    idx, gates, cnt = _route(x2d, norm_g, rw_hi, rw_lo, router_b.reshape(E, 1), E)
    counts = cnt[:, 0]
    padded = ((counts + TMG - 1) // TMG) * TMG
    pend = jnp.cumsum(padded)
    pstart = pend - padded
    pos = pstart[idx[0:TOP_K]] + idx[TOP_K:2 * TOP_K]
    P = T * TOP_K + E * TMG
    tile_start = jnp.arange(P // TMG, dtype=jnp.int32) * TMG
    tile_e = jnp.minimum(jnp.searchsorted(pend, tile_start, side="right"), E - 1).astype(jnp.int32)
    tile_n = jnp.clip(pstart[tile_e] + counts[tile_e] - tile_start, 0, TMG).astype(jnp.int32)
    xs = _dispatch(pos, x2d, norm_g, P)
    ys = _group_ffn(tile_e, tile_n, xs, w1, w3, w2, TMG)
    return _combine(pos, x2d, gates.T, final_g, ys, final_norm)


def _final_norm_kernel(x_ref, g_ref, o_ref):
    o_ref[...] = _rms(x_ref[...], g_ref[...])


def _final_norm(x2d, g):
    T, D = x2d.shape
    TM = min(1024, T)
    return pl.pallas_call(
        _final_norm_kernel,
        out_shape=jax.ShapeDtypeStruct((T, D), F32),
        grid=(T // TM,),
        in_specs=[pl.BlockSpec((TM, D), lambda i: (i, 0)), _const_spec((1, D))],
        out_specs=pl.BlockSpec((TM, D), lambda i: (i, 0)),
        compiler_params=_params(("arbitrary",), 32),
        name="final_norm",
    )(x2d, g)


def kernel(x, mem, norm_mix_g, w_in, conv_rg_w, conv_rg_b, rg_w_a, rg_b_a, rg_w_x, rg_b_x, rg_lambda,
           conv_ml_w, conv_ml_b, ml_w_q, ml_w_k, ml_w_v, ml_b_i, ml_b_f, ml_norm_g, mem_norm_g, w_kv,
           w_br_rg, w_br_ml, w_br_xa, b_merge, w_out, norm_ffn_g, ffn_w1, ffn_w3, ffn_w2, router_w,
           router_b, moe_w1, moe_w3, moe_w2, final_norm_g):
    B, S, D = x.shape
    depth = w_in.shape[0]
    d_rg = conv_rg_w.shape[2]
    d_ml = conv_ml_w.shape[2]
    H = ml_w_q.shape[1]
    d_xa = w_kv.shape[2] // 2
    o_ax, o_ay = 0, d_rg
    o_mu, o_mo = 2 * d_rg, 2 * d_rg + d_ml
    o_mi = 2 * d_rg + 2 * d_ml
    o_mf = o_mi + H
    o_q = o_mf + H
    o_g = o_q + d_xa
    assert w_in.shape[2] == o_g + N_BRANCH * D and 2 * H <= SUBLANES

    bf = lambda a: a.astype(BF16)
    row = lambda a: a.reshape(1, -1)
    x2d = x.reshape(B * S, D)
    fg = row(final_norm_g)
    for l in range(depth):
        wl = w_in[l]
        x_tm = x2d.reshape(B, S, D).transpose(1, 0, 2).reshape(S * B, D)
        y_rg = _rg_branch(x_tm, B, row(norm_mix_g[l]), bf(wl[:, o_ax:o_ax + d_rg]),
                          bf(wl[:, o_ay:o_ay + d_rg]), conv_rg_w[l], row(conv_rg_b[l]), bf(rg_w_a[l]),
                          row(rg_b_a[l]), bf(rg_w_x[l]), row(rg_b_x[l]), row(rg_lambda[l]))
        w_if = jnp.pad(wl[:, o_mi:o_mi + 2 * H], ((0, 0), (0, LANES - 2 * H)))
        w_ift = jnp.pad(wl[:, o_mi:o_mi + 2 * H].T, ((0, SUBLANES - 2 * H), (0, 0)))
        b_if = jnp.concatenate([ml_b_i[l], ml_b_f[l]])
        y_ml = _mlstm_branch(x2d, B, row(norm_mix_g[l]), bf(wl[:, o_mu:o_mu + d_ml]),
                             bf(wl[:, o_mo:o_mo + d_ml]), bf(w_if), bf(w_ift),
                             jnp.pad(b_if, (0, LANES - 2 * H)).reshape(1, LANES),
                             jnp.pad(b_if, (0, SUBLANES - 2 * H)).reshape(SUBLANES, 1),
                             conv_ml_w[l], row(conv_ml_b[l]), bf(ml_w_q[l]), bf(ml_w_k[l]),
                             bf(ml_w_v[l]), row(ml_norm_g[l]))
        kv = _mem_kv(mem, row(mem_norm_g[l]), bf(w_kv[l]))
        x2d = _merge(x2d, B, row(norm_mix_g[l]), bf(wl[:, o_q:o_q + d_xa]), bf(wl[:, o_g:]),
                     row(b_merge[l]), y_rg.reshape(S, B * d_rg), y_ml, kv, bf(w_br_rg[l]),
                     bf(w_br_ml[l]), bf(w_br_xa[l]), bf(w_out[l]))
        j = l // 2
        if l % 2 == 0:
            x2d = _ffn(x2d, row(norm_ffn_g[l]), bf(ffn_w1[j]), bf(ffn_w3[j]), bf(ffn_w2[j]))
        else:
            x2d = _moe(x2d, row(norm_ffn_g[l]), router_w[j], router_b[j], bf(moe_w1[j]),
                       bf(moe_w3[j]), bf(moe_w2[j]), fg, l == depth - 1)
    if depth % 2 == 1:
        x2d = _final_norm(x2d, fg)
    return x2d.reshape(B, S, D)
```

```python
import functools

import jax
import jax.numpy as jnp
from jax import lax
from jax.experimental import pallas as pl
from jax.experimental.pallas import tpu as pltpu

EPS = 1e-6
RG_C = 8.0
CONV_W = 4
ML_CHUNK = 128
XA_HEADS = 4
TOP_K = 2
N_BRANCH = 3

V7X_VMEM_BYTES = 64 * 1024 * 1024
LANES = 128
SUBLANES = 8

F32 = jnp.float32
BF16 = jnp.bfloat16


def _params(semantics, vmem_mib):
    assert vmem_mib * 1024 * 1024 <= V7X_VMEM_BYTES
    return pltpu.CompilerParams(dimension_semantics=semantics,
                                vmem_limit_bytes=vmem_mib * 1024 * 1024)


def _const_spec(shape):
    nd = len(shape)
    return pl.BlockSpec(shape, lambda *_: (0,) * nd)


def _rms(x, g):
    ms = jnp.mean(x * x, axis=-1, keepdims=True)
    return x * lax.rsqrt(ms + EPS) * g


def _dot(a, b):
    return jnp.dot(a, b, preferred_element_type=F32)


def _dot_nt(a, b):
    return lax.dot_general(a, b, (((1,), (1,)), ((), ())), preferred_element_type=F32)


def _dot_tn(a, b):
    return lax.dot_general(a, b, (((0,), (0,)), ((), ())), preferred_element_type=F32)


def _block_diag(xb, w_ref):
    G, bi, _ = w_ref.shape
    return jnp.concatenate([_dot(xb[:, g * bi:(g + 1) * bi], w_ref[g]) for g in range(G)], axis=1)


def _split3(x):
    h1 = x.astype(BF16)
    r1 = x - h1.astype(F32)
    h2 = r1.astype(BF16)
    h3 = (r1 - h2.astype(F32)).astype(BF16)
    return h1, h2, h3


def _rg_kernel(x_ref, g_ref, wax_ref, way_ref, cw_ref, cb_ref, wa_ref, ba_ref, wx_ref, bx_ref,
               lam_ref, o_ref, axbuf, a_s, b_s, h_s, carry, *, B, TT):
    R = TT * B
    halo = (CONV_W - 1) * B

    @pl.when(pl.program_id(0) == 0)
    def _():
        axbuf[0:halo, :] = jnp.zeros((halo, axbuf.shape[1]), F32)
        carry[...] = jnp.zeros_like(carry)

    h = _rms(x_ref[...], g_ref[...]).astype(BF16)
    axbuf[halo:halo + R, :] = _dot(h, wax_ref[...])
    cw = cw_ref[...]
    xc = cb_ref[...] + cw[0:1, :] * axbuf[0:R, :]
    for k in range(1, CONV_W):
        xc = xc + cw[k:k + 1, :] * axbuf[k * B:k * B + R, :]
    axbuf[0:halo, :] = axbuf[R:R + halo, :]

    xcb = xc.astype(BF16)
    r = jax.nn.sigmoid(_block_diag(xcb, wa_ref) + ba_ref[...])
    ig = jax.nn.sigmoid(_block_diag(xcb, wx_ref) + bx_ref[...])
    log_a = -RG_C * r * jax.nn.softplus(-lam_ref[...])
    a = jnp.exp(log_a)
    a_s[...] = a
    b_s[...] = jnp.sqrt(-jnp.tanh(log_a) * (a * a + 1.0)) * (ig * xc)

    def step(t, hc):
        off = pl.multiple_of(t * B, B)
        hn = a_s[pl.ds(off, B), :] * hc + b_s[pl.ds(off, B), :]
        h_s[pl.ds(off, B), :] = hn
        return hn

    carry[...] = lax.fori_loop(0, TT, step, carry[...], unroll=8)
    ay = _dot(h, way_ref[...])
    o_ref[...] = (h_s[...] * jax.nn.gelu(ay)).astype(BF16)


def _rg_branch(x_tm, B, norm_g, w_ax, w_ay, conv_w, conv_b, w_a, b_a, w_x, b_x, lam):
    SB, D = x_tm.shape
    S = SB // B
    C = w_ax.shape[1]
    TT = min(64, S)
    assert S % TT == 0 and B % SUBLANES == 0
    R = TT * B
    halo = (CONV_W - 1) * B
    row = lambda i: (i, 0)
    return pl.pallas_call(
        functools.partial(_rg_kernel, B=B, TT=TT),
        out_shape=jax.ShapeDtypeStruct((SB, C), BF16),
        grid=(S // TT,),
        in_specs=[pl.BlockSpec((R, D), row), _const_spec((1, D)), _const_spec((D, C)),
                  _const_spec((D, C)), _const_spec((CONV_W, C)), _const_spec((1, C)),
                  _const_spec(w_a.shape), _const_spec((1, C)), _const_spec(w_x.shape),
                  _const_spec((1, C)), _const_spec((1, C))],
        out_specs=pl.BlockSpec((R, C), row),
        scratch_shapes=[pltpu.VMEM((halo + R, C), F32), pltpu.VMEM((R, C), F32),
                        pltpu.VMEM((R, C), F32), pltpu.VMEM((R, C), F32), pltpu.VMEM((B, C), F32)],
        compiler_params=_params(("arbitrary",), 48),
        name="rg_branch",
    )(x_tm, norm_g, w_ax, w_ay, conv_w, conv_b, w_a, b_a, w_x, b_x, lam)


def _mlstm_kernel(x_ref, g_ref, wmu_ref, wmo_ref, wif_ref, wift_ref, bifc_ref, bifr_ref, cw_ref,
                  cb_ref, wq_ref, wk_ref, wv_ref, ng_ref, o_ref, ubuf, c_st, n_st, m_st, *, TS):
    H, d, _ = wq_ref.shape
    L = ML_CHUNK
    pad = SUBLANES
    tail = CONV_W - 1

    @pl.when(pl.program_id(1) == 0)
    def _():
        ubuf[0:pad, :] = jnp.zeros((pad, ubuf.shape[1]), F32)
        c_st[...] = jnp.zeros_like(c_st)
        n_st[...] = jnp.zeros_like(n_st)
        m_st[...] = jnp.zeros_like(m_st)

    h = _rms(x_ref[...], g_ref[...]).astype(BF16)
    u = _dot(h, wmu_ref[...])
    ubuf[pad:pad + TS, :] = u
    cw = cw_ref[...]
    c = cb_ref[...] + cw[0:1, :] * ubuf[pad - tail:pad - tail + TS, :]
    for k in range(1, CONV_W):
        c = c + cw[k:k + 1, :] * ubuf[pad - tail + k:pad - tail + k + TS, :]
    ubuf[0:pad, :] = ubuf[TS:TS + pad, :]
    c = jax.nn.silu(c)
    cb16 = c.astype(BF16)
    q = _block_diag(cb16, wq_ref)
    k_ = _block_diag(cb16, wk_ref) * (d ** -0.5)
    v = _block_diag(u.astype(BF16), wv_ref)
    og = jax.nn.sigmoid(_dot(h, wmo_ref[...]))
    if_c = _dot(h, wif_ref[...]) + bifc_ref[...]
    if_r = _dot_nt(wift_ref[...], h) + bifr_ref[...]
    lf_c = jax.nn.log_sigmoid(if_c)
    lf_r = jax.nn.log_sigmoid(if_r)
    ri = lax.broadcasted_iota(jnp.int32, (L, L), 0)
    ci = lax.broadcasted_iota(jnp.int32, (L, L), 1)
    causal = ci <= ri
    tri_l = jnp.where(causal, 1.0, 0.0).astype(BF16)
    tri_u = jnp.where(ri <= ci, 1.0, 0.0).astype(BF16)
    ng = ng_ref[...]

    for ck in range(TS // L):
        r0 = ck * L
        bc = sum(_dot(tri_l, p) for p in _split3(lf_c[r0:r0 + L, :]))
        br = sum(_dot(p, tri_u) for p in _split3(lf_r[:, r0:r0 + L]))
        for hd in range(H):
            b_col = bc[:, H + hd:H + hd + 1]
            b_row = br[H + hd:H + hd + 1, :]
            i_col = if_c[r0:r0 + L, hd:hd + 1]
            i_row = if_r[hd:hd + 1, r0:r0 + L]
            m = m_st[hd][0:1, 0:1]
            cs = slice(hd * d, (hd + 1) * d)
            qf = q[r0:r0 + L, cs]
            kf = k_[r0:r0 + L, cs]
            vf = v[r0:r0 + L, cs]
            qb, kb, vb = qf.astype(BF16), kf.astype(BF16), vf.astype(BF16)
            g = b_col + m
            dm = jnp.where(causal, b_col - b_row + i_row, -jnp.inf)
            m_row = jnp.maximum(g, jnp.max(dm, axis=-1, keepdims=True))
            w = jnp.exp(dm - m_row) * _dot_nt(qb, kb)
            inter = jnp.exp(g - m_row)
            cmat = c_st[hd]
            nvec = n_st[hd]
            num = inter * _dot(qb, cmat.astype(BF16)) + _dot(w.astype(BF16), vb)
            den = inter * jnp.sum(qf * nvec, axis=-1, keepdims=True) + jnp.sum(w, axis=-1, keepdims=True)
            hh = num / jnp.maximum(jnp.abs(den), jnp.exp(-m_row))
            b_last = b_col[L - 1:L, :]
            dl = b_last - b_col + i_col
            m_new = jnp.maximum(b_last + m, jnp.max(dl, axis=0, keepdims=True))
            decay = jnp.exp(b_last + m - m_new)
            wl = jnp.exp(dl - m_new)
            c_st[hd] = decay * cmat + _dot_tn(kb, (wl * vf).astype(BF16))
            n_st[hd] = decay * nvec + jnp.sum(wl * kf, axis=0, keepdims=True)
            m_st[hd] = jnp.broadcast_to(m_new, m_st.shape[1:])
            y = og[r0:r0 + L, cs] * hh
            y = y * lax.rsqrt(jnp.mean(y * y, axis=-1, keepdims=True) + EPS)
            o_ref[r0:r0 + L, cs] = (y * ng[:, cs]).astype(BF16)


def _mlstm_branch(x2d, B, norm_g, w_mu, w_mo, w_if, w_ift, b_if_c, b_if_r, conv_w, conv_b,
                  w_q, w_k, w_v, ml_norm_g):
    T, D = x2d.shape
    S = T // B
    C = w_mu.shape[1]
    H, d, _ = w_q.shape
    TS = min(256, S)
    assert S % TS == 0 and TS % ML_CHUNK == 0
    nS = S // TS
    row = lambda b, s: (b * nS + s, 0)
    return pl.pallas_call(
        functools.partial(_mlstm_kernel, TS=TS),
        out_shape=jax.ShapeDtypeStruct((T, C), BF16),
        grid=(B, nS),
        in_specs=[pl.BlockSpec((TS, D), row), _const_spec((1, D)), _const_spec((D, C)),
                  _const_spec((D, C)), _const_spec((D, LANES)), _const_spec((SUBLANES, D)),
                  _const_spec((1, LANES)), _const_spec((SUBLANES, 1)), _const_spec((CONV_W, C)),
                  _const_spec((1, C)), _const_spec(w_q.shape), _const_spec(w_k.shape),
                  _const_spec(w_v.shape), _const_spec((1, C))],
        out_specs=pl.BlockSpec((TS, C), row),
        scratch_shapes=[pltpu.VMEM((SUBLANES + TS, C), F32), pltpu.VMEM((H, d, d), F32),
                        pltpu.VMEM((H, 1, d), F32), pltpu.VMEM((H, SUBLANES, LANES), F32)],
        compiler_params=_params(("arbitrary", "arbitrary"), 48),
        name="mlstm_branch",
    )(x2d, norm_g, w_mu, w_mo, w_if, w_ift, b_if_c, b_if_r, conv_w, conv_b, w_q, w_k, w_v, ml_norm_g)


def _kv_kernel(mem_ref, g_ref, w_ref, o_ref):
    o_ref[...] = _dot(_rms(mem_ref[...], g_ref[...]).astype(BF16), w_ref[...]).astype(BF16)


def _mem_kv(mem, g, w_kv):
    B, M, D = mem.shape
    N = w_kv.shape[1]
    return pl.pallas_call(
        _kv_kernel,
        out_shape=jax.ShapeDtypeStruct((B, M, N), BF16),
        grid=(B,),
        in_specs=[pl.BlockSpec((None, M, D), lambda b: (b, 0, 0)), _const_spec((1, D)),
                  _const_spec((D, N))],
        out_specs=pl.BlockSpec((None, M, N), lambda b: (b, 0, 0)),
        compiler_params=_params(("arbitrary",), 32),
        name="mem_kv",
    )(mem, g, w_kv)


def _merge_kernel(x_ref, g_ref, wq_ref, wg_ref, bm_ref, yrg_ref, yml_ref, kv_ref, wrg_ref, wml_ref,
                  wxa_ref, wo_ref, o_ref):
    x = x_ref[...]
    D = x.shape[1]
    h = _rms(x, g_ref[...]).astype(BF16)
    q = _dot(h, wq_ref[...]).astype(BF16)
    dxa = q.shape[1]
    dh = dxa // XA_HEADS
    heads = []
    for hd in range(XA_HEADS):
        kh = kv_ref[:, hd * dh:(hd + 1) * dh]
        vh = kv_ref[:, dxa + hd * dh:dxa + (hd + 1) * dh]
        s = _dot_nt(q[:, hd * dh:(hd + 1) * dh], kh) * (dh ** -0.5)
        e = jnp.exp(s - jnp.max(s, axis=-1, keepdims=True))
        p = e / jnp.sum(e, axis=-1, keepdims=True)
        heads.append(_dot(p.astype(BF16), vh))
    y_xa = jnp.concatenate(heads, axis=1).astype(BF16)

    def gate(k):
        return jax.nn.sigmoid(_dot(h, wg_ref[:, k * D:(k + 1) * D]) + bm_ref[:, k * D:(k + 1) * D])

    merged = gate(0) * _dot(yrg_ref[...], wrg_ref[...])
    merged = merged + gate(1) * _dot(yml_ref[...], wml_ref[...])
    merged = merged + gate(2) * _dot(y_xa, wxa_ref[...])
    o_ref[...] = x + _dot(merged.astype(BF16), wo_ref[...])


def _merge(x2d, B, norm_g, w_q, w_g, b_merge, y_rg_tm, y_ml, kv, w_br_rg, w_br_ml, w_br_xa, w_out):
    T, D = x2d.shape
    S = T // B
    C = y_ml.shape[1]
    M, N = kv.shape[1:]
    TM = min(512, S)
    assert S % TM == 0
    nS = S // TM
    row = lambda b, s: (b * nS + s, 0)
    one = pl.Buffered(1)
    cspec = lambda shape: pl.BlockSpec(shape, lambda *_: (0,) * len(shape), pipeline_mode=one)
    return pl.pallas_call(
        _merge_kernel,
        out_shape=jax.ShapeDtypeStruct((T, D), F32),
        grid=(B, nS),
        in_specs=[pl.BlockSpec((TM, D), row), cspec((1, D)), cspec(w_q.shape), cspec(w_g.shape),
                  cspec(b_merge.shape), pl.BlockSpec((TM, C), lambda b, s: (s, b)),
                  pl.BlockSpec((TM, C), row), pl.BlockSpec((None, M, N), lambda b, s: (b, 0, 0)),
                  cspec(w_br_rg.shape), cspec(w_br_ml.shape), cspec(w_br_xa.shape),
                  cspec(w_out.shape)],
        out_specs=pl.BlockSpec((TM, D), row),
        compiler_params=_params(("arbitrary", "arbitrary"), 56),
        name="merge",
    )(x2d, norm_g, w_q, w_g, b_merge, y_rg_tm, y_ml, kv, w_br_rg, w_br_ml, w_br_xa, w_out)


def _ffn_kernel(x_ref, g_ref, w1_ref, w3_ref, w2_ref, o_ref, hs, acc):
    f = pl.program_id(1)

    @pl.when(f == 0)
    def _():
        hs[...] = _rms(x_ref[...], g_ref[...]).astype(BF16)
        acc[...] = jnp.zeros_like(acc)

    hb = hs[...]
    mid = jax.nn.silu(_dot(hb, w1_ref[...])) * _dot(hb, w3_ref[...])
    acc[...] += _dot(mid.astype(BF16), w2_ref[...])

    @pl.when(f == pl.num_programs(1) - 1)
    def _():
        o_ref[...] = x_ref[...] + acc[...]


def _ffn(x2d, norm_g, w1, w3, w2):
    T, D = x2d.shape
    F = w1.shape[1]
    TM = min(1024, T)
    TF = 512 if F % 512 == 0 else F
    assert T % TM == 0
    return pl.pallas_call(
        _ffn_kernel,
        out_shape=jax.ShapeDtypeStruct((T, D), F32),
        grid=(T // TM, F // TF),
        in_specs=[pl.BlockSpec((TM, D), lambda i, f: (i, 0)), _const_spec((1, D)),
                  pl.BlockSpec((D, TF), lambda i, f: (0, f)), pl.BlockSpec((D, TF), lambda i, f: (0, f)),
                  pl.BlockSpec((TF, D), lambda i, f: (f, 0))],
        out_specs=pl.BlockSpec((TM, D), lambda i, f: (i, 0)),
        scratch_shapes=[pltpu.VMEM((TM, D), BF16), pltpu.VMEM((TM, D), F32)],
        compiler_params=_params(("arbitrary", "arbitrary"), 48),
        name="ffn_dense",
    )(x2d, norm_g, w1, w3, w2)


ROUTE_TM = 512
RUN_ALIGN = 16


def _route_kernel(x_ref, g_ref, whi_ref, wlo_ref, rb_ref, idx_ref, gate_ref, len_ref, off_ref,
                  tot_ref, run_s, *, E):
    @pl.when(pl.program_id(0) == 0)
    def _():
        run_s[...] = jnp.zeros_like(run_s)

    h2 = _rms(x_ref[...], g_ref[...])
    TM = h2.shape[0]
    hi = h2.astype(BF16)
    lo = (h2 - hi.astype(F32)).astype(BF16)
    logits = _dot(hi, whi_ref[...]) + (_dot(lo, whi_ref[...]) + _dot(hi, wlo_ref[...]))
    lt = logits.T[0:E, :] + rb_ref[...]
    ie = lax.broadcasted_iota(jnp.int32, (E, TM), 0)
    m1 = jnp.max(lt, axis=0, keepdims=True)
    i1 = jnp.min(jnp.where(lt == m1, ie, E), axis=0, keepdims=True)
    l2 = jnp.where(ie == i1, -jnp.inf, lt)
    m2 = jnp.max(l2, axis=0, keepdims=True)
    i2 = jnp.min(jnp.where(l2 == m2, ie, E), axis=0, keepdims=True)
    ex = jnp.exp(m2 - m1)
    g1 = 1.0 / (1.0 + ex)
    g2 = ex / (1.0 + ex)
    oh1 = jnp.where(ie == i1, 1.0, 0.0)
    oh2 = jnp.where(ie == i2, 1.0, 0.0)
    oh = oh1 + oh2
    ri = lax.broadcasted_iota(jnp.int32, (TM, TM), 0)
    ci = lax.broadcasted_iota(jnp.int32, (TM, TM), 1)
    upper = jnp.where(ri < ci, 1.0, 0.0).astype(BF16)
    excl = _dot(oh.astype(BF16), upper)
    cnt = jnp.sum(oh, axis=1, keepdims=True).astype(jnp.int32)
    run_len = jnp.broadcast_to(((cnt + (RUN_ALIGN - 1)) // RUN_ALIGN) * RUN_ALIGN, (E, LANES))
    iec = lax.broadcasted_iota(jnp.int32, (E, LANES), 0)
    run_start = jnp.zeros((E, LANES), jnp.int32)
    for e in range(E - 1):
        run_start = run_start + jnp.where(iec > e, run_len[e:e + 1, :], 0)
    slot = run_start[:, 0:1].astype(F32) + excl
    s1 = jnp.sum(oh1 * slot, axis=0, keepdims=True).astype(jnp.int32)
    s2 = jnp.sum(oh2 * slot, axis=0, keepdims=True).astype(jnp.int32)
    len_ref[...] = run_len
    off_ref[...] = run_s[...]
    run_s[...] = run_s[...] + run_len
    tot_ref[...] = run_s[...]
    row = lax.broadcasted_iota(jnp.int32, (SUBLANES, TM), 0)
    idx_ref[...] = jnp.where(row == 0, i1, jnp.where(row == 1, i2, jnp.where(row == 2, s1,
                             jnp.where(row == 3, s2, 0))))
    gate_ref[...] = jnp.where(row == 0, g1, jnp.where(row == 1, g2, 0.0))


def _route(x2d, norm_g, w_hi, w_lo, rb, E):
    T, D = x2d.shape
    TM = min(ROUTE_TM, T)
    nT = T // TM
    assert T % TM == 0 and E == SUBLANES
    return pl.pallas_call(
        functools.partial(_route_kernel, E=E),
        out_shape=(jax.ShapeDtypeStruct((SUBLANES, T), jnp.int32),
                   jax.ShapeDtypeStruct((SUBLANES, T), F32),
                   jax.ShapeDtypeStruct((nT * E, LANES), jnp.int32),
                   jax.ShapeDtypeStruct((nT * E, LANES), jnp.int32),
                   jax.ShapeDtypeStruct((E, LANES), jnp.int32)),
        grid=(nT,),
        in_specs=[pl.BlockSpec((TM, D), lambda i: (i, 0)), _const_spec((1, D)),
                  _const_spec((D, LANES)), _const_spec((D, LANES)), _const_spec((E, 1))],
        out_specs=(pl.BlockSpec((SUBLANES, TM), lambda i: (0, i)),
                   pl.BlockSpec((SUBLANES, TM), lambda i: (0, i)),
                   pl.BlockSpec((E, LANES), lambda i: (i, 0)),
                   pl.BlockSpec((E, LANES), lambda i: (i, 0)),
                   _const_spec((E, LANES))),
        scratch_shapes=[pltpu.VMEM((E, LANES), jnp.int32)],
        compiler_params=_params(("arbitrary",), 32),
        name="moe_route",
    )(x2d, norm_g, w_hi, w_lo, rb)


def _run_dmas(len_ref, off_ref, i, E, max_len, tile_ref, sorted_ref, sem, to_sorted, wait):
    local = 0
    for e in range(E):
        n = len_ref[i * E + e]
        base = off_ref[i * E + e]
        done = 0
        sz = max_len
        while sz >= RUN_ALIGN:
            @pl.when((n & sz) != 0)
            def _(sz=sz, local=local, base=base, done=done):
                t_rows = tile_ref.at[pl.ds(pl.multiple_of(local + done, RUN_ALIGN), sz), :]
                s_rows = sorted_ref.at[pl.ds(pl.multiple_of(base + done, RUN_ALIGN), sz), :]
                cp = (pltpu.make_async_copy(t_rows, s_rows, sem) if to_sorted
                      else pltpu.make_async_copy(s_rows, t_rows, sem))
                if wait:
                    cp.wait()
                else:
                    cp.start()
            done = done + (n & sz)
            sz //= 2
        local = local + n


def _dispatch_kernel(len_ref, off_ref, x_ref, g_ref, ls_ref, init_ref, xs_ref, xs_t, sem, *, E):
    del init_ref
    i = pl.program_id(0)
    TM = x_ref.shape[0]
    LS = xs_t.shape[0]
    h2 = _rms(x_ref[...], g_ref[...]).astype(BF16)
    ls = ls_ref[...]
    j = lax.broadcasted_iota(jnp.int32, (LS, TM), 0)
    onehot = jnp.where(j == ls[2:3, :], 1.0, jnp.where(j == ls[3:4, :], 1.0, 0.0)).astype(BF16)
    xs_t[...] = _dot(onehot, h2).astype(BF16)
    _run_dmas(len_ref, off_ref, i, E, TM, xs_t, xs_ref, sem, True, False)
    _run_dmas(len_ref, off_ref, i, E, TM, xs_t, xs_ref, sem, True, True)


def _dispatch(run_len, run_off, idx, x2d, norm_g, P, E):
    T, D = x2d.shape
    TM = min(ROUTE_TM, T)
    LS = TOP_K * TM + E * RUN_ALIGN
    init = jnp.zeros((P, D), BF16)
    grid_spec = pltpu.PrefetchScalarGridSpec(
        num_scalar_prefetch=2,
        grid=(T // TM,),
        in_specs=[pl.BlockSpec((TM, D), lambda i, *_: (i, 0)),
                  pl.BlockSpec((1, D), lambda i, *_: (0, 0)),
                  pl.BlockSpec((SUBLANES, TM), lambda i, *_: (0, i)),
                  pl.BlockSpec(memory_space=pl.ANY)],
        out_specs=pl.BlockSpec(memory_space=pl.ANY),
        scratch_shapes=[pltpu.VMEM((LS, D), BF16), pltpu.SemaphoreType.DMA(())])
    return pl.pallas_call(
        functools.partial(_dispatch_kernel, E=E),
        out_shape=jax.ShapeDtypeStruct((P, D), BF16),
        grid_spec=grid_spec,
        input_output_aliases={5: 0},
        compiler_params=_params(("arbitrary",), 32),
        name="moe_dispatch",
    )(run_len, run_off, x2d, norm_g, idx, init)


def _group_ffn_kernel(te_ref, tn_ref, x_ref, w1_ref, w3_ref, w2_ref, o_ref, acc):
    i = pl.program_id(0)
    f = pl.program_id(1)

    @pl.when(f == 0)
    def _():
        acc[...] = jnp.zeros_like(acc)

    @pl.when(tn_ref[i] > 0)
    def _():
        hb = x_ref[...]
        mid = jax.nn.silu(_dot(hb, w1_ref[...])) * _dot(hb, w3_ref[...])
        acc[...] += _dot(mid.astype(BF16), w2_ref[...])

    @pl.when(f == pl.num_programs(1) - 1)
    def _():
        o_ref[...] = acc[...]


def _group_ffn(tile_e, tile_n, xs, w1, w3, w2, TMG):
    P, D = xs.shape
    F = w1.shape[2]
    TF = 512 if F % 512 == 0 else F
    nF = F // TF
    fblk = lambda i, f, tn: jnp.where(tn[i] > 0, f, nF - 1)
    grid_spec = pltpu.PrefetchScalarGridSpec(
        num_scalar_prefetch=2,
        grid=(P // TMG, nF),
        in_specs=[pl.BlockSpec((TMG, D), lambda i, f, te, tn: (i, 0)),
                  pl.BlockSpec((None, D, TF), lambda i, f, te, tn: (te[i], 0, fblk(i, f, tn))),
                  pl.BlockSpec((None, D, TF), lambda i, f, te, tn: (te[i], 0, fblk(i, f, tn))),
                  pl.BlockSpec((None, TF, D), lambda i, f, te, tn: (te[i], fblk(i, f, tn), 0))],
        out_specs=pl.BlockSpec((TMG, D), lambda i, f, te, tn: (i, 0)),
        scratch_shapes=[pltpu.VMEM((TMG, D), F32)])
    return pl.pallas_call(
        _group_ffn_kernel,
        out_shape=jax.ShapeDtypeStruct((P, D), F32),
        grid_spec=grid_spec,
        compiler_params=_params(("arbitrary", "arbitrary"), 48),
        name="moe_group_ffn",
    )(tile_e, tile_n, xs, w1, w3, w2)


def _combine_kernel(len_ref, off_ref, x_ref, lsc_ref, gt_ref, fg_ref, ys_ref, o_ref, yt, sem, *, E,
                    final_norm):
    i = pl.program_id(0)
    TM = x_ref.shape[0]
    LS, D = yt.shape
    yt[TOP_K * TM:LS, :] = jnp.zeros((LS - TOP_K * TM, D), F32)
    _run_dmas(len_ref, off_ref, i, E, TM, yt, ys_ref, sem, False, False)
    _run_dmas(len_ref, off_ref, i, E, TM, yt, ys_ref, sem, False, True)
    y = yt[...]
    yh = y.astype(BF16)
    yl = (y - yh.astype(F32)).astype(BF16)
    lsc = lsc_ref[...]
    gt = gt_ref[...]
    j = lax.broadcasted_iota(jnp.int32, (TM, LS), 1)
    f = None
    for k in range(TOP_K):
        pk = jnp.where(j == lsc[:, TOP_K + k:TOP_K + k + 1], 1.0, 0.0).astype(BF16)
        fk = gt[:, k:k + 1] * (_dot(pk, yh) + _dot(pk, yl))
        f = fk if f is None else f + fk
    xo = x_ref[...] + f
    o_ref[...] = _rms(xo, fg_ref[...]) if final_norm else xo


def _combine(run_len, run_off, x2d, idx_t, gates_t, final_g, ys, E, final_norm):
    T, D = x2d.shape
    TM = min(ROUTE_TM, T)
    LS = TOP_K * TM + E * RUN_ALIGN
    grid_spec = pltpu.PrefetchScalarGridSpec(
        num_scalar_prefetch=2,
        grid=(T // TM,),
        in_specs=[pl.BlockSpec((TM, D), lambda i, *_: (i, 0)),
                  pl.BlockSpec((TM, SUBLANES), lambda i, *_: (i, 0)),
                  pl.BlockSpec((TM, SUBLANES), lambda i, *_: (i, 0)),
                  pl.BlockSpec((1, D), lambda i, *_: (0, 0)),
                  pl.BlockSpec(memory_space=pl.ANY)],
        out_specs=pl.BlockSpec((TM, D), lambda i, *_: (i, 0)),
        scratch_shapes=[pltpu.VMEM((LS, D), F32), pltpu.SemaphoreType.DMA(())])
    return pl.pallas_call(
        functools.partial(_combine_kernel, E=E, final_norm=final_norm),
        out_shape=jax.ShapeDtypeStruct((T, D), F32),
        grid_spec=grid_spec,
        compiler_params=_params(("arbitrary",), 48),
        name="moe_combine",
    )(run_len, run_off, x2d, idx_t, gates_t, final_g, ys)


def _moe(x2d, norm_g, router_w, router_b, w1, w3, w2, final_g, final_norm):
    T, D = x2d.shape
    E = router_w.shape[1]
    TMG = 512
    nT = T // min(ROUTE_TM, T)
    rw = jnp.pad(router_w, ((0, 0), (0, LANES - E)))
    rw_hi = rw.astype(BF16)
    rw_lo = (rw - rw_hi.astype(F32)).astype(BF16)
    idx, gates, run_len, run_off, tot = _route(x2d, norm_g, rw_hi, rw_lo, router_b.reshape(E, 1), E)
    tot = tot[:, 0]
    padded = ((tot + TMG - 1) // TMG) * TMG
    pend = jnp.cumsum(padded)
    pstart = pend - padded
    run_len = run_len[:, 0]
    run_off = (run_off[:, 0].reshape(nT, E) + pstart[None, :]).reshape(nT * E)
    P = -(-(T * TOP_K + nT * E * (RUN_ALIGN - 1) + E * (TMG - 1)) // TMG) * TMG
    tile_start = jnp.arange(P // TMG, dtype=jnp.int32) * TMG
    tile_e = jnp.minimum(jnp.sum(tile_start[:, None] >= pend[None, :], axis=1), E - 1).astype(jnp.int32)
    sel = tile_e[:, None] == jnp.arange(E, dtype=jnp.int32)[None, :]
    tile_end = jnp.sum(jnp.where(sel, (pstart + tot)[None, :], 0), axis=1)
    tile_n = jnp.clip(tile_end - tile_start, 0, TMG).astype(jnp.int32)
    xs = _dispatch(run_len, run_off, idx, x2d, norm_g, P, E)
    ys = _group_ffn(tile_e, tile_n, xs, w1, w3, w2, TMG)
    return _combine(run_len, run_off, x2d, idx.T, gates.T, final_g, ys, E, final_norm)


def _final_norm_kernel(x_ref, g_ref, o_ref):
    o_ref[...] = _rms(x_ref[...], g_ref[...])


def _final_norm(x2d, g):
    T, D = x2d.shape
    TM = min(1024, T)
    return pl.pallas_call(
        _final_norm_kernel,
        out_shape=jax.ShapeDtypeStruct((T, D), F32),
        grid=(T // TM,),
        in_specs=[pl.BlockSpec((TM, D), lambda i: (i, 0)), _const_spec((1, D))],
        out_specs=pl.BlockSpec((TM, D), lambda i: (i, 0)),
        compiler_params=_params(("arbitrary",), 32),
        name="final_norm",
    )(x2d, g)


def kernel(x, mem, norm_mix_g, w_in, conv_rg_w, conv_rg_b, rg_w_a, rg_b_a, rg_w_x, rg_b_x, rg_lambda,
           conv_ml_w, conv_ml_b, ml_w_q, ml_w_k, ml_w_v, ml_b_i, ml_b_f, ml_norm_g, mem_norm_g, w_kv,
           w_br_rg, w_br_ml, w_br_xa, b_merge, w_out, norm_ffn_g, ffn_w1, ffn_w3, ffn_w2, router_w,
           router_b, moe_w1, moe_w3, moe_w2, final_norm_g):
    B, S, D = x.shape
    depth = w_in.shape[0]
    d_rg = conv_rg_w.shape[2]
    d_ml = conv_ml_w.shape[2]
    H = ml_w_q.shape[1]
    d_xa = w_kv.shape[2] // 2
    o_ax, o_ay = 0, d_rg
    o_mu, o_mo = 2 * d_rg, 2 * d_rg + d_ml
    o_mi = 2 * d_rg + 2 * d_ml
    o_mf = o_mi + H
    o_q = o_mf + H
    o_g = o_q + d_xa
    assert w_in.shape[2] == o_g + N_BRANCH * D and 2 * H <= SUBLANES

    bf = lambda a: a.astype(BF16)
    row = lambda a: a.reshape(1, -1)
    x2d = x.reshape(B * S, D)
    fg = row(final_norm_g)
    for l in range(depth):
        wl = w_in[l]
        x_tm = x2d.reshape(B, S, D).transpose(1, 0, 2).reshape(S * B, D)
        y_rg = _rg_branch(x_tm, B, row(norm_mix_g[l]), bf(wl[:, o_ax:o_ax + d_rg]),
                          bf(wl[:, o_ay:o_ay + d_rg]), conv_rg_w[l], row(conv_rg_b[l]), bf(rg_w_a[l]),
                          row(rg_b_a[l]), bf(rg_w_x[l]), row(rg_b_x[l]), row(rg_lambda[l]))
        w_if = jnp.pad(wl[:, o_mi:o_mi + 2 * H], ((0, 0), (0, LANES - 2 * H)))
        w_ift = jnp.pad(wl[:, o_mi:o_mi + 2 * H].T, ((0, SUBLANES - 2 * H), (0, 0)))
        b_if = jnp.concatenate([ml_b_i[l], ml_b_f[l]])
        y_ml = _mlstm_branch(x2d, B, row(norm_mix_g[l]), bf(wl[:, o_mu:o_mu + d_ml]),
                             bf(wl[:, o_mo:o_mo + d_ml]), bf(w_if), bf(w_ift),
                             jnp.pad(b_if, (0, LANES - 2 * H)).reshape(1, LANES),
                             jnp.pad(b_if, (0, SUBLANES - 2 * H)).reshape(SUBLANES, 1),
                             conv_ml_w[l], row(conv_ml_b[l]), bf(ml_w_q[l]), bf(ml_w_k[l]),
                             bf(ml_w_v[l]), row(ml_norm_g[l]))
        kv = _mem_kv(mem, row(mem_norm_g[l]), bf(w_kv[l]))
        x2d = _merge(x2d, B, row(norm_mix_g[l]), bf(wl[:, o_q:o_q + d_xa]), bf(wl[:, o_g:]),
                     row(b_merge[l]), y_rg.reshape(S, B * d_rg), y_ml, kv, bf(w_br_rg[l]),
                     bf(w_br_ml[l]), bf(w_br_xa[l]), bf(w_out[l]))
        j = l // 2
        if l % 2 == 0:
            x2d = _ffn(x2d, row(norm_ffn_g[l]), bf(ffn_w1[j]), bf(ffn_w3[j]), bf(ffn_w2[j]))
        else:
            x2d = _moe(x2d, row(norm_ffn_g[l]), router_w[j], router_b[j], bf(moe_w1[j]),
                       bf(moe_w3[j]), bf(moe_w2[j]), fg, l == depth - 1)
    if depth % 2 == 1:
        x2d = _final_norm(x2d, fg)
    return x2d.reshape(B, S, D)
```

```python
import functools

import jax
import jax.numpy as jnp
from jax import lax
from jax.experimental import pallas as pl
from jax.experimental.pallas import tpu as pltpu

EPS = 1e-6
RG_C = 8.0
CONV_W = 4
ML_CHUNK = 128
XA_HEADS = 4
TOP_K = 2
N_BRANCH = 3

V7X_VMEM_BYTES = 64 * 1024 * 1024
LANES = 128
SUBLANES = 8

F32 = jnp.float32
BF16 = jnp.bfloat16


def _params(semantics, vmem_mib):
    assert vmem_mib * 1024 * 1024 <= V7X_VMEM_BYTES
    return pltpu.CompilerParams(dimension_semantics=semantics,
                                vmem_limit_bytes=vmem_mib * 1024 * 1024)


def _const_spec(shape):
    nd = len(shape)
    return pl.BlockSpec(shape, lambda *_: (0,) * nd)


def _rms(x, g):
    ms = jnp.mean(x * x, axis=-1, keepdims=True)
    return x * lax.rsqrt(ms + EPS) * g


def _dot(a, b):
    return jnp.dot(a, b, preferred_element_type=F32)


def _dot_nt(a, b):
    return lax.dot_general(a, b, (((1,), (1,)), ((), ())), preferred_element_type=F32)


def _dot_tn(a, b):
    return lax.dot_general(a, b, (((0,), (0,)), ((), ())), preferred_element_type=F32)


def _block_diag(xb, w_ref):
    G, bi, _ = w_ref.shape
    return jnp.concatenate([_dot(xb[:, g * bi:(g + 1) * bi], w_ref[g]) for g in range(G)], axis=1)


def _split3(x):
    h1 = x.astype(BF16)
    r1 = x - h1.astype(F32)
    h2 = r1.astype(BF16)
    h3 = (r1 - h2.astype(F32)).astype(BF16)
    return h1, h2, h3


def _rg_kernel(x_ref, g_ref, wax_ref, way_ref, cw_ref, cb_ref, wa_ref, ba_ref, wx_ref, bx_ref,
               lam_ref, o_ref, axbuf, a_s, b_s, h_s, carry, *, B, TT):
    R = TT * B
    halo = (CONV_W - 1) * B

    @pl.when(pl.program_id(0) == 0)
    def _():
        axbuf[0:halo, :] = jnp.zeros((halo, axbuf.shape[1]), F32)
        carry[...] = jnp.zeros_like(carry)

    h = _rms(x_ref[...], g_ref[...]).astype(BF16)
    G, bi, _ = wa_ref.shape
    rate = -RG_C * jax.nn.softplus(-lam_ref[...])
    cols = [slice(g * bi, (g + 1) * bi) for g in range(G)]
    axbuf[halo:halo + R, cols[0]] = _dot(h, wax_ref[:, cols[0]])
    for g in range(G):
        cs = cols[g]
        if g + 1 < G:
            axbuf[halo:halo + R, cols[g + 1]] = _dot(h, wax_ref[:, cols[g + 1]])
        ay = _dot(h, way_ref[:, cs])
        cw = cw_ref[:, cs]
        xc = cb_ref[:, cs] + cw[0:1, :] * axbuf[0:R, cs]
        for k in range(1, CONV_W):
            xc = xc + cw[k:k + 1, :] * axbuf[k * B:k * B + R, cs]
        axbuf[0:halo, cs] = axbuf[R:R + halo, cs]
        xcb = xc.astype(BF16)
        r = jax.nn.sigmoid(_dot(xcb, wa_ref[g]) + ba_ref[:, cs])
        ig = jax.nn.sigmoid(_dot(xcb, wx_ref[g]) + bx_ref[:, cs])
        log_a = r * rate[:, cs]
        a = jnp.exp(log_a)
        a_s[:, cs] = a
        b_s[:, cs] = jnp.sqrt(-jnp.tanh(log_a) * (a * a + 1.0)) * (ig * xc)

        def step(t, hc, cs=cs):
            off = pl.multiple_of(t * B, B)
            hn = a_s[pl.ds(off, B), cs] * hc + b_s[pl.ds(off, B), cs]
            h_s[pl.ds(off, B), cs] = hn
            return hn

        carry[:, cs] = lax.fori_loop(0, TT, step, carry[:, cs], unroll=True)
        o_ref[:, cs] = (h_s[:, cs] * jax.nn.gelu(ay)).astype(BF16)


def _rg_branch(x_tm, B, norm_g, w_ax, w_ay, conv_w, conv_b, w_a, b_a, w_x, b_x, lam):
    SB, D = x_tm.shape
    S = SB // B
    C = w_ax.shape[1]
    TT = min(64, S)
    assert S % TT == 0 and B % SUBLANES == 0
    R = TT * B
    halo = (CONV_W - 1) * B
    row = lambda i: (i, 0)
    return pl.pallas_call(
        functools.partial(_rg_kernel, B=B, TT=TT),
        out_shape=jax.ShapeDtypeStruct((SB, C), BF16),
        grid=(S // TT,),
        in_specs=[pl.BlockSpec((R, D), row), _const_spec((1, D)), _const_spec((D, C)),
                  _const_spec((D, C)), _const_spec((CONV_W, C)), _const_spec((1, C)),
                  _const_spec(w_a.shape), _const_spec((1, C)), _const_spec(w_x.shape),
                  _const_spec((1, C)), _const_spec((1, C))],
        out_specs=pl.BlockSpec((R, C), row),
        scratch_shapes=[pltpu.VMEM((halo + R, C), F32), pltpu.VMEM((R, C), F32),
                        pltpu.VMEM((R, C), F32), pltpu.VMEM((R, C), F32), pltpu.VMEM((B, C), F32)],
        compiler_params=_params(("arbitrary",), 48),
        name="rg_branch",
    )(x_tm, norm_g, w_ax, w_ay, conv_w, conv_b, w_a, b_a, w_x, b_x, lam)


def _mlstm_kernel(x_ref, g_ref, wmu_ref, wmo_ref, wif_ref, wift_ref, bifc_ref, bifr_ref, cw_ref,
                  cb_ref, wq_ref, wk_ref, wv_ref, ng_ref, o_ref, xtm_ref, ubuf, c_st, n_st, m_st, *, TS):
    H, d, _ = wq_ref.shape
    xtm_ref[...] = x_ref[...]
    L = ML_CHUNK
    pad = SUBLANES
    tail = CONV_W - 1

    @pl.when(pl.program_id(1) == 0)
    def _():
        ubuf[0:pad, :] = jnp.zeros((pad, ubuf.shape[1]), F32)
        c_st[...] = jnp.zeros_like(c_st)
        n_st[...] = jnp.zeros_like(n_st)
        m_st[...] = jnp.zeros_like(m_st)

    h = _rms(x_ref[...], g_ref[...]).astype(BF16)
    u = _dot(h, wmu_ref[...])
    ubuf[pad:pad + TS, :] = u
    cw = cw_ref[...]
    c = cb_ref[...] + cw[0:1, :] * ubuf[pad - tail:pad - tail + TS, :]
    for k in range(1, CONV_W):
        c = c + cw[k:k + 1, :] * ubuf[pad - tail + k:pad - tail + k + TS, :]
    ubuf[0:pad, :] = ubuf[TS:TS + pad, :]
    c = jax.nn.silu(c)
    cb16 = c.astype(BF16)
    q = _block_diag(cb16, wq_ref)
    k_ = _block_diag(cb16, wk_ref) * (d ** -0.5)
    v = _block_diag(u.astype(BF16), wv_ref)
    og = jax.nn.sigmoid(_dot(h, wmo_ref[...]))
    if_c = _dot(h, wif_ref[...]) + bifc_ref[...]
    if_r = _dot_nt(wift_ref[...], h) + bifr_ref[...]
    lf_c = jax.nn.log_sigmoid(if_c)
    lf_r = jax.nn.log_sigmoid(if_r)
    ri = lax.broadcasted_iota(jnp.int32, (L, L), 0)
    ci = lax.broadcasted_iota(jnp.int32, (L, L), 1)
    causal = ci <= ri
    tri_l = jnp.where(causal, 1.0, 0.0).astype(BF16)
    tri_u = jnp.where(ri <= ci, 1.0, 0.0).astype(BF16)
    ng = ng_ref[...]

    for ck in range(TS // L):
        r0 = ck * L
        bc = sum(_dot(tri_l, p) for p in _split3(lf_c[r0:r0 + L, :]))
        br = sum(_dot(p, tri_u) for p in _split3(lf_r[:, r0:r0 + L]))
        for hd in range(H):
            b_col = bc[:, H + hd:H + hd + 1]
            b_row = br[H + hd:H + hd + 1, :]
            i_col = if_c[r0:r0 + L, hd:hd + 1]
            i_row = if_r[hd:hd + 1, r0:r0 + L]
            m = m_st[hd][0:1, 0:1]
            cs = slice(hd * d, (hd + 1) * d)
            qf = q[r0:r0 + L, cs]
            kf = k_[r0:r0 + L, cs]
            vf = v[r0:r0 + L, cs]
            qb, kb, vb = qf.astype(BF16), kf.astype(BF16), vf.astype(BF16)
            g = b_col + m
            dm = jnp.where(causal, b_col - b_row + i_row, -jnp.inf)
            m_row = jnp.maximum(g, jnp.max(dm, axis=-1, keepdims=True))
            w = jnp.exp(dm - m_row) * _dot_nt(qb, kb)
            inter = jnp.exp(g - m_row)
            cmat = c_st[hd]
            nvec = n_st[hd]
            num = inter * _dot(qb, cmat.astype(BF16)) + _dot(w.astype(BF16), vb)
            den = inter * jnp.sum(qf * nvec, axis=-1, keepdims=True) + jnp.sum(w, axis=-1, keepdims=True)
            hh = num / jnp.maximum(jnp.abs(den), jnp.exp(-m_row))
            b_last = b_col[L - 1:L, :]
            dl = b_last - b_col + i_col
            m_new = jnp.maximum(b_last + m, jnp.max(dl, axis=0, keepdims=True))
            decay = jnp.exp(b_last + m - m_new)
            wl = jnp.exp(dl - m_new)
            c_st[hd] = decay * cmat + _dot_tn(kb, (wl * vf).astype(BF16))
            n_st[hd] = decay * nvec + jnp.sum(wl * kf, axis=0, keepdims=True)
            m_st[hd] = jnp.broadcast_to(m_new, m_st.shape[1:])
            y = og[r0:r0 + L, cs] * hh
            y = y * lax.rsqrt(jnp.mean(y * y, axis=-1, keepdims=True) + EPS)
            o_ref[r0:r0 + L, cs] = (y * ng[:, cs]).astype(BF16)


def _mlstm_branch(x2d, B, norm_g, w_mu, w_mo, w_if, w_ift, b_if_c, b_if_r, conv_w, conv_b,
                  w_q, w_k, w_v, ml_norm_g):
    T, D = x2d.shape
    S = T // B
    C = w_mu.shape[1]
    H, d, _ = w_q.shape
    TS = min(256, S)
    assert S % TS == 0 and TS % ML_CHUNK == 0
    nS = S // TS
    row = lambda b, s: (b * nS + s, 0)
    return pl.pallas_call(
        functools.partial(_mlstm_kernel, TS=TS),
        out_shape=(jax.ShapeDtypeStruct((T, C), BF16), jax.ShapeDtypeStruct((S, B * D), F32)),
        grid=(B, nS),
        in_specs=[pl.BlockSpec((TS, D), row), _const_spec((1, D)), _const_spec((D, C)),
                  _const_spec((D, C)), _const_spec((D, LANES)), _const_spec((SUBLANES, D)),
                  _const_spec((1, LANES)), _const_spec((SUBLANES, 1)), _const_spec((CONV_W, C)),
                  _const_spec((1, C)), _const_spec(w_q.shape), _const_spec(w_k.shape),
                  _const_spec(w_v.shape), _const_spec((1, C))],
        out_specs=(pl.BlockSpec((TS, C), row), pl.BlockSpec((TS, D), lambda b, s: (s, b))),
        scratch_shapes=[pltpu.VMEM((SUBLANES + TS, C), F32), pltpu.VMEM((H, d, d), F32),
                        pltpu.VMEM((H, 1, d), F32), pltpu.VMEM((H, SUBLANES, LANES), F32)],
        compiler_params=_params(("arbitrary", "arbitrary"), 48),
        name="mlstm_branch",
    )(x2d, norm_g, w_mu, w_mo, w_if, w_ift, b_if_c, b_if_r, conv_w, conv_b, w_q, w_k, w_v, ml_norm_g)


def _kv_kernel(mem_ref, g_ref, w_ref, o_ref):
    o_ref[...] = _dot(_rms(mem_ref[...], g_ref[...]).astype(BF16), w_ref[...]).astype(BF16)


def _mem_kv(mem, g, w_kv):
    B, M, D = mem.shape
    N = w_kv.shape[1]
    return pl.pallas_call(
        _kv_kernel,
        out_shape=jax.ShapeDtypeStruct((B, M, N), BF16),
        grid=(B,),
        in_specs=[pl.BlockSpec((None, M, D), lambda b: (b, 0, 0)), _const_spec((1, D)),
                  _const_spec((D, N))],
        out_specs=pl.BlockSpec((None, M, N), lambda b: (b, 0, 0)),
        compiler_params=_params(("arbitrary",), 32),
        name="mem_kv",
    )(mem, g, w_kv)


def _merge_kernel(x_ref, g_ref, wq_ref, wg_ref, bm_ref, yrg_ref, yml_ref, kv_ref, wrg_ref, wml_ref,
                  wxa_ref, wo_ref, o_ref):
    x = x_ref[...]
    D = x.shape[1]
    h = _rms(x, g_ref[...]).astype(BF16)
    q = _dot(h, wq_ref[...]).astype(BF16)
    dxa = q.shape[1]
    dh = dxa // XA_HEADS
    heads = []
    for hd in range(XA_HEADS):
        kh = kv_ref[:, hd * dh:(hd + 1) * dh]
        vh = kv_ref[:, dxa + hd * dh:dxa + (hd + 1) * dh]
        s = _dot_nt(q[:, hd * dh:(hd + 1) * dh], kh) * (dh ** -0.5)
        e = jnp.exp(s - jnp.max(s, axis=-1, keepdims=True))
        p = e / jnp.sum(e, axis=-1, keepdims=True)
        heads.append(_dot(p.astype(BF16), vh))
    y_xa = jnp.concatenate(heads, axis=1).astype(BF16)

    def gate(k):
        return jax.nn.sigmoid(_dot(h, wg_ref[:, k * D:(k + 1) * D]) + bm_ref[:, k * D:(k + 1) * D])

    merged = gate(0) * _dot(yrg_ref[...], wrg_ref[...])
    merged = merged + gate(1) * _dot(yml_ref[...], wml_ref[...])
    merged = merged + gate(2) * _dot(y_xa, wxa_ref[...])
    o_ref[...] = x + _dot(merged.astype(BF16), wo_ref[...])


def _merge(x2d, B, norm_g, w_q, w_g, b_merge, y_rg_tm, y_ml, kv, w_br_rg, w_br_ml, w_br_xa, w_out):
    T, D = x2d.shape
    S = T // B
    C = y_ml.shape[1]
    M, N = kv.shape[1:]
    TM = min(512, S)
    assert S % TM == 0
    nS = S // TM
    row = lambda b, s: (b * nS + s, 0)
    one = pl.Buffered(1)
    cspec = lambda shape: pl.BlockSpec(shape, lambda *_: (0,) * len(shape), pipeline_mode=one)
    return pl.pallas_call(
        _merge_kernel,
        out_shape=jax.ShapeDtypeStruct((T, D), F32),
        grid=(B, nS),
        in_specs=[pl.BlockSpec((TM, D), row), cspec((1, D)), cspec(w_q.shape), cspec(w_g.shape),
                  cspec(b_merge.shape), pl.BlockSpec((TM, C), lambda b, s: (s, b)),
                  pl.BlockSpec((TM, C), row), pl.BlockSpec((None, M, N), lambda b, s: (b, 0, 0)),
                  cspec(w_br_rg.shape), cspec(w_br_ml.shape), cspec(w_br_xa.shape),
                  cspec(w_out.shape)],
        out_specs=pl.BlockSpec((TM, D), row),
        compiler_params=_params(("arbitrary", "arbitrary"), 56),
        name="merge",
    )(x2d, norm_g, w_q, w_g, b_merge, y_rg_tm, y_ml, kv, w_br_rg, w_br_ml, w_br_xa, w_out)


def _ffn_kernel(x_ref, g_ref, w1_ref, w3_ref, w2_ref, o_ref, hs, acc):
    f = pl.program_id(1)

    @pl.when(f == 0)
    def _():
        hs[...] = _rms(x_ref[...], g_ref[...]).astype(BF16)
        acc[...] = jnp.zeros_like(acc)

    hb = hs[...]
    mid = jax.nn.silu(_dot(hb, w1_ref[...])) * _dot(hb, w3_ref[...])
    acc[...] += _dot(mid.astype(BF16), w2_ref[...])

    @pl.when(f == pl.num_programs(1) - 1)
    def _():
        o_ref[...] = x_ref[...] + acc[...]


FFN_ROW_TILE = 512


def _ffn_hidden_tile(F):
    for tf in (1792, 1024, 512, 256):
        if F % tf == 0:
            return tf
    return F


def _ffn(x2d, norm_g, w1, w3, w2):
    T, D = x2d.shape
    F = w1.shape[1]
    TM = min(FFN_ROW_TILE, T)
    TF = _ffn_hidden_tile(F)
    assert T % TM == 0
    return pl.pallas_call(
        _ffn_kernel,
        out_shape=jax.ShapeDtypeStruct((T, D), F32),
        grid=(T // TM, F // TF),
        in_specs=[pl.BlockSpec((TM, D), lambda i, f: (i, 0)), _const_spec((1, D)),
                  pl.BlockSpec((D, TF), lambda i, f: (0, f)), pl.BlockSpec((D, TF), lambda i, f: (0, f)),
                  pl.BlockSpec((TF, D), lambda i, f: (f, 0))],
        out_specs=pl.BlockSpec((TM, D), lambda i, f: (i, 0)),
        scratch_shapes=[pltpu.VMEM((TM, D), BF16), pltpu.VMEM((TM, D), F32)],
        compiler_params=_params(("arbitrary", "arbitrary"), 56),
        name="ffn_dense",
    )(x2d, norm_g, w1, w3, w2)


ROUTE_TM = 512
RUN_ALIGN = 16


def _route_kernel(x_ref, g_ref, whi_ref, wlo_ref, rb_ref, idx_ref, gate_ref, len_ref, off_ref,
                  tot_ref, run_s, *, E):
    @pl.when(pl.program_id(0) == 0)
    def _():
        run_s[...] = jnp.zeros_like(run_s)

    h2 = _rms(x_ref[...], g_ref[...])
    TM = h2.shape[0]
    hi = h2.astype(BF16)
    lo = (h2 - hi.astype(F32)).astype(BF16)
    logits = _dot(hi, whi_ref[...]) + (_dot(lo, whi_ref[...]) + _dot(hi, wlo_ref[...]))
    lt = logits.T[0:E, :] + rb_ref[...]
    ie = lax.broadcasted_iota(jnp.int32, (E, TM), 0)
    m1 = jnp.max(lt, axis=0, keepdims=True)
    i1 = jnp.min(jnp.where(lt == m1, ie, E), axis=0, keepdims=True)
    l2 = jnp.where(ie == i1, -jnp.inf, lt)
    m2 = jnp.max(l2, axis=0, keepdims=True)
    i2 = jnp.min(jnp.where(l2 == m2, ie, E), axis=0, keepdims=True)
    ex = jnp.exp(m2 - m1)
    g1 = 1.0 / (1.0 + ex)
    g2 = ex / (1.0 + ex)
    oh1 = jnp.where(ie == i1, 1.0, 0.0)
    oh2 = jnp.where(ie == i2, 1.0, 0.0)
    oh = oh1 + oh2
    ri = lax.broadcasted_iota(jnp.int32, (TM, TM), 0)
    ci = lax.broadcasted_iota(jnp.int32, (TM, TM), 1)
    upper = jnp.where(ri < ci, 1.0, 0.0).astype(BF16)
    excl = _dot(oh.astype(BF16), upper)
    cnt = jnp.sum(oh, axis=1, keepdims=True).astype(jnp.int32)
    run_len = jnp.broadcast_to(((cnt + (RUN_ALIGN - 1)) // RUN_ALIGN) * RUN_ALIGN, (E, LANES))
    iec = lax.broadcasted_iota(jnp.int32, (E, LANES), 0)
    run_start = jnp.zeros((E, LANES), jnp.int32)
    for e in range(E - 1):
        run_start = run_start + jnp.where(iec > e, run_len[e:e + 1, :], 0)
    slot = run_start[:, 0:1].astype(F32) + excl
    s1 = jnp.sum(oh1 * slot, axis=0, keepdims=True).astype(jnp.int32)
    s2 = jnp.sum(oh2 * slot, axis=0, keepdims=True).astype(jnp.int32)
    len_ref[...] = run_len
    off_ref[...] = run_s[...]
    run_s[...] = run_s[...] + run_len
    tot_ref[...] = run_s[...]
    row = lax.broadcasted_iota(jnp.int32, (SUBLANES, TM), 0)
    idx_ref[...] = jnp.where(row == 0, i1, jnp.where(row == 1, i2, jnp.where(row == 2, s1,
                             jnp.where(row == 3, s2, 0))))
    gate_ref[...] = jnp.where(row == 0, g1, jnp.where(row == 1, g2, 0.0))


def _route(x2d, norm_g, w_hi, w_lo, rb, E):
    T, D = x2d.shape
    TM = min(ROUTE_TM, T)
    nT = T // TM
    assert T % TM == 0 and E == SUBLANES
    return pl.pallas_call(
        functools.partial(_route_kernel, E=E),
        out_shape=(jax.ShapeDtypeStruct((SUBLANES, T), jnp.int32),
                   jax.ShapeDtypeStruct((SUBLANES, T), F32),
                   jax.ShapeDtypeStruct((nT * E, LANES), jnp.int32),
                   jax.ShapeDtypeStruct((nT * E, LANES), jnp.int32),
                   jax.ShapeDtypeStruct((E, LANES), jnp.int32)),
        grid=(nT,),
        in_specs=[pl.BlockSpec((TM, D), lambda i: (i, 0)), _const_spec((1, D)),
                  _const_spec((D, LANES)), _const_spec((D, LANES)), _const_spec((E, 1))],
        out_specs=(pl.BlockSpec((SUBLANES, TM), lambda i: (0, i)),
                   pl.BlockSpec((SUBLANES, TM), lambda i: (0, i)),
                   pl.BlockSpec((E, LANES), lambda i: (i, 0)),
                   pl.BlockSpec((E, LANES), lambda i: (i, 0)),
                   _const_spec((E, LANES))),
        scratch_shapes=[pltpu.VMEM((E, LANES), jnp.int32)],
        compiler_params=_params(("arbitrary",), 32),
        name="moe_route",
    )(x2d, norm_g, w_hi, w_lo, rb)


def _run_dmas(len_ref, off_ref, i, E, max_len, tile_ref, sorted_ref, sem, to_sorted, wait):
    local = 0
    for e in range(E):
        n = len_ref[i * E + e]
        base = off_ref[i * E + e]
        done = 0
        sz = max_len
        while sz >= RUN_ALIGN:
            @pl.when((n & sz) != 0)
            def _(sz=sz, local=local, base=base, done=done):
                t_rows = tile_ref.at[pl.ds(pl.multiple_of(local + done, RUN_ALIGN), sz), :]
                s_rows = sorted_ref.at[pl.ds(pl.multiple_of(base + done, RUN_ALIGN), sz), :]
                cp = (pltpu.make_async_copy(t_rows, s_rows, sem) if to_sorted
                      else pltpu.make_async_copy(s_rows, t_rows, sem))
                if wait:
                    cp.wait()
                else:
                    cp.start()
            done = done + (n & sz)
            sz //= 2
        local = local + n


def _dispatch_kernel(len_ref, off_ref, x_ref, g_ref, ls_ref, init_ref, xs_ref, xs_t, sem, *, E):
    del init_ref
    i = pl.program_id(0)
    TM = x_ref.shape[0]
    LS = xs_t.shape[0]
    h2 = _rms(x_ref[...], g_ref[...]).astype(BF16)
    ls = ls_ref[...]
    j = lax.broadcasted_iota(jnp.int32, (LS, TM), 0)
    onehot = jnp.where(j == ls[2:3, :], 1.0, jnp.where(j == ls[3:4, :], 1.0, 0.0)).astype(BF16)
    xs_t[...] = _dot(onehot, h2).astype(BF16)
    _run_dmas(len_ref, off_ref, i, E, TM, xs_t, xs_ref, sem, True, False)
    _run_dmas(len_ref, off_ref, i, E, TM, xs_t, xs_ref, sem, True, True)


def _dispatch(run_len, run_off, idx, x2d, norm_g, P, E):
    T, D = x2d.shape
    TM = min(ROUTE_TM, T)
    LS = TOP_K * TM + E * RUN_ALIGN
    init = jnp.zeros((P, D), BF16)
    grid_spec = pltpu.PrefetchScalarGridSpec(
        num_scalar_prefetch=2,
        grid=(T // TM,),
        in_specs=[pl.BlockSpec((TM, D), lambda i, *_: (i, 0)),
                  pl.BlockSpec((1, D), lambda i, *_: (0, 0)),
                  pl.BlockSpec((SUBLANES, TM), lambda i, *_: (0, i)),
                  pl.BlockSpec(memory_space=pl.ANY)],
        out_specs=pl.BlockSpec(memory_space=pl.ANY),
        scratch_shapes=[pltpu.VMEM((LS, D), BF16), pltpu.SemaphoreType.DMA(())])
    return pl.pallas_call(
        functools.partial(_dispatch_kernel, E=E),
        out_shape=jax.ShapeDtypeStruct((P, D), BF16),
        grid_spec=grid_spec,
        input_output_aliases={5: 0},
        compiler_params=_params(("arbitrary",), 32),
        name="moe_dispatch",
    )(run_len, run_off, x2d, norm_g, idx, init)


def _group_ffn_kernel(te_ref, tn_ref, x_ref, w1_ref, w3_ref, w2_ref, o_ref, acc):
    i = pl.program_id(0)
    f = pl.program_id(1)

    @pl.when(f == 0)
    def _():
        acc[...] = jnp.zeros_like(acc)

    @pl.when(tn_ref[i] > 0)
    def _():
        hb = x_ref[...]
        mid = jax.nn.silu(_dot(hb, w1_ref[...])) * _dot(hb, w3_ref[...])
        acc[...] += _dot(mid.astype(BF16), w2_ref[...])

    @pl.when(f == pl.num_programs(1) - 1)
    def _():
        o_ref[...] = acc[...]


def _group_ffn(tile_e, tile_n, xs, w1, w3, w2, TMG):
    P, D = xs.shape
    F = w1.shape[2]
    TF = _ffn_hidden_tile(F)
    nF = F // TF
    fblk = lambda i, f, tn: jnp.where(tn[i] > 0, f, nF - 1)
    grid_spec = pltpu.PrefetchScalarGridSpec(
        num_scalar_prefetch=2,
        grid=(P // TMG, nF),
        in_specs=[pl.BlockSpec((TMG, D), lambda i, f, te, tn: (i, 0)),
                  pl.BlockSpec((None, D, TF), lambda i, f, te, tn: (te[i], 0, fblk(i, f, tn))),
                  pl.BlockSpec((None, D, TF), lambda i, f, te, tn: (te[i], 0, fblk(i, f, tn))),
                  pl.BlockSpec((None, TF, D), lambda i, f, te, tn: (te[i], fblk(i, f, tn), 0))],
        out_specs=pl.BlockSpec((TMG, D), lambda i, f, te, tn: (i, 0)),
        scratch_shapes=[pltpu.VMEM((TMG, D), F32)])
    return pl.pallas_call(
        _group_ffn_kernel,
        out_shape=jax.ShapeDtypeStruct((P, D), F32),
        grid_spec=grid_spec,
        compiler_params=_params(("arbitrary", "arbitrary"), 56),
        name="moe_group_ffn",
    )(tile_e, tile_n, xs, w1, w3, w2)


def _combine_kernel(len_ref, off_ref, x_ref, lsr_ref, lsc_ref, gt_ref, fg_ref, ys_ref, o_ref, yt, sem,
                    *, E, final_norm):
    i = pl.program_id(0)
    TM = x_ref.shape[0]
    LS, D = yt.shape
    yt[TOP_K * TM:LS, :] = jnp.zeros((LS - TOP_K * TM, D), F32)
    _run_dmas(len_ref, off_ref, i, E, TM, yt, ys_ref, sem, False, False)
    lsr = lsr_ref[...]
    gt = gt_ref[...]
    js = lax.broadcasted_iota(jnp.int32, (LS, TM), 0)
    lane = lax.broadcasted_iota(jnp.int32, (TM, LANES), 1)
    gs = None
    for k in range(TOP_K):
        t1, t2, t3 = (t.astype(F32) for t in _split3(gt[:, k:k + 1]))
        terms = jnp.where(lane == 0, t1, jnp.where(lane == 1, t2, jnp.where(lane == 2, t3, 0.0)))
        pk = jnp.where(js == lsr[TOP_K + k:TOP_K + k + 1, :], 1.0, 0.0).astype(BF16)
        gk = jnp.sum(_dot(pk, terms.astype(BF16)), axis=1, keepdims=True)
        gs = gk if gs is None else gs + gk
    _run_dmas(len_ref, off_ref, i, E, TM, yt, ys_ref, sem, False, True)
    z = yt[...] * gs
    zh = z.astype(BF16)
    zl = (z - zh.astype(F32)).astype(BF16)
    lsc = lsc_ref[...]
    jt = lax.broadcasted_iota(jnp.int32, (TM, LS), 1)
    pt = None
    for k in range(TOP_K):
        ok = jnp.where(jt == lsc[:, TOP_K + k:TOP_K + k + 1], 1.0, 0.0)
        pt = ok if pt is None else pt + ok
    pt = pt.astype(BF16)
    xo = x_ref[...] + (_dot(pt, zh) + _dot(pt, zl))
    o_ref[...] = _rms(xo, fg_ref[...]) if final_norm else xo


def _combine(run_len, run_off, x2d, idx, idx_t, gates_t, final_g, ys, E, final_norm):
    T, D = x2d.shape
    TM = min(ROUTE_TM, T)
    LS = TOP_K * TM + E * RUN_ALIGN
    grid_spec = pltpu.PrefetchScalarGridSpec(
        num_scalar_prefetch=2,
        grid=(T // TM,),
        in_specs=[pl.BlockSpec((TM, D), lambda i, *_: (i, 0)),
                  pl.BlockSpec((SUBLANES, TM), lambda i, *_: (0, i)),
                  pl.BlockSpec((TM, SUBLANES), lambda i, *_: (i, 0)),
                  pl.BlockSpec((TM, SUBLANES), lambda i, *_: (i, 0)),
                  pl.BlockSpec((1, D), lambda i, *_: (0, 0)),
                  pl.BlockSpec(memory_space=pl.ANY)],
        out_specs=pl.BlockSpec((TM, D), lambda i, *_: (i, 0)),
        scratch_shapes=[pltpu.VMEM((LS, D), F32), pltpu.SemaphoreType.DMA(())])
    return pl.pallas_call(
        functools.partial(_combine_kernel, E=E, final_norm=final_norm),
        out_shape=jax.ShapeDtypeStruct((T, D), F32),
        grid_spec=grid_spec,
        compiler_params=_params(("arbitrary",), 48),
        name="moe_combine",
    )(run_len, run_off, x2d, idx, idx_t, gates_t, final_g, ys)


def _moe(x2d, norm_g, router_w, router_b, w1, w3, w2, final_g, final_norm):
    T, D = x2d.shape
    E = router_w.shape[1]
    TMG = 512
    nT = T // min(ROUTE_TM, T)
    rw = jnp.pad(router_w, ((0, 0), (0, LANES - E)))
    rw_hi = rw.astype(BF16)
    rw_lo = (rw - rw_hi.astype(F32)).astype(BF16)
    idx, gates, run_len, run_off, tot = _route(x2d, norm_g, rw_hi, rw_lo, router_b.reshape(E, 1), E)
    tot = tot[:, 0]
    padded = ((tot + TMG - 1) // TMG) * TMG
    pend = jnp.cumsum(padded)
    pstart = pend - padded
    run_len = run_len[:, 0]
    run_off = (run_off[:, 0].reshape(nT, E) + pstart[None, :]).reshape(nT * E)
    P = -(-(T * TOP_K + nT * E * (RUN_ALIGN - 1) + E * (TMG - 1)) // TMG) * TMG
    tile_start = jnp.arange(P // TMG, dtype=jnp.int32) * TMG
    tile_e = jnp.minimum(jnp.sum(tile_start[:, None] >= pend[None, :], axis=1), E - 1).astype(jnp.int32)
    sel = tile_e[:, None] == jnp.arange(E, dtype=jnp.int32)[None, :]
    tile_end = jnp.sum(jnp.where(sel, (pstart + tot)[None, :], 0), axis=1)
    tile_n = jnp.clip(tile_end - tile_start, 0, TMG).astype(jnp.int32)
    xs = _dispatch(run_len, run_off, idx, x2d, norm_g, P, E)
    ys = _group_ffn(tile_e, tile_n, xs, w1, w3, w2, TMG)
    return _combine(run_len, run_off, x2d, idx, idx.T, gates.T, final_g, ys, E, final_norm)


def _final_norm_kernel(x_ref, g_ref, o_ref):
    o_ref[...] = _rms(x_ref[...], g_ref[...])


def _final_norm(x2d, g):
    T, D = x2d.shape
    TM = min(1024, T)
    return pl.pallas_call(
        _final_norm_kernel,
        out_shape=jax.ShapeDtypeStruct((T, D), F32),
        grid=(T // TM,),
        in_specs=[pl.BlockSpec((TM, D), lambda i: (i, 0)), _const_spec((1, D))],
        out_specs=pl.BlockSpec((TM, D), lambda i: (i, 0)),
        compiler_params=_params(("arbitrary",), 32),
        name="final_norm",
    )(x2d, g)


def kernel(x, mem, norm_mix_g, w_in, conv_rg_w, conv_rg_b, rg_w_a, rg_b_a, rg_w_x, rg_b_x, rg_lambda,
           conv_ml_w, conv_ml_b, ml_w_q, ml_w_k, ml_w_v, ml_b_i, ml_b_f, ml_norm_g, mem_norm_g, w_kv,
           w_br_rg, w_br_ml, w_br_xa, b_merge, w_out, norm_ffn_g, ffn_w1, ffn_w3, ffn_w2, router_w,
           router_b, moe_w1, moe_w3, moe_w2, final_norm_g):
    B, S, D = x.shape
    depth = w_in.shape[0]
    d_rg = conv_rg_w.shape[2]
    d_ml = conv_ml_w.shape[2]
    H = ml_w_q.shape[1]
    d_xa = w_kv.shape[2] // 2
    o_ax, o_ay = 0, d_rg
    o_mu, o_mo = 2 * d_rg, 2 * d_rg + d_ml
    o_mi = 2 * d_rg + 2 * d_ml
    o_mf = o_mi + H
    o_q = o_mf + H
    o_g = o_q + d_xa
    assert w_in.shape[2] == o_g + N_BRANCH * D and 2 * H <= SUBLANES

    bf = lambda a: a.astype(BF16)
    row = lambda a: a.reshape(1, -1)
    x2d = x.reshape(B * S, D)
    fg = row(final_norm_g)
    for l in range(depth):
        wl = w_in[l]
        w_if = jnp.pad(wl[:, o_mi:o_mi + 2 * H], ((0, 0), (0, LANES - 2 * H)))
        w_ift = jnp.pad(wl[:, o_mi:o_mi + 2 * H].T, ((0, SUBLANES - 2 * H), (0, 0)))
        b_if = jnp.concatenate([ml_b_i[l], ml_b_f[l]])
        y_ml, x_tm = _mlstm_branch(x2d, B, row(norm_mix_g[l]), bf(wl[:, o_mu:o_mu + d_ml]),
                                   bf(wl[:, o_mo:o_mo + d_ml]), bf(w_if), bf(w_ift),
                                   jnp.pad(b_if, (0, LANES - 2 * H)).reshape(1, LANES),
                                   jnp.pad(b_if, (0, SUBLANES - 2 * H)).reshape(SUBLANES, 1),
                                   conv_ml_w[l], row(conv_ml_b[l]), bf(ml_w_q[l]), bf(ml_w_k[l]),
                                   bf(ml_w_v[l]), row(ml_norm_g[l]))
        y_rg = _rg_branch(x_tm.reshape(S * B, D), B, row(norm_mix_g[l]), bf(wl[:, o_ax:o_ax + d_rg]),
                          bf(wl[:, o_ay:o_ay + d_rg]), conv_rg_w[l], row(conv_rg_b[l]), bf(rg_w_a[l]),
                          row(rg_b_a[l]), bf(rg_w_x[l]), row(rg_b_x[l]), row(rg_lambda[l]))
        kv = _mem_kv(mem, row(mem_norm_g[l]), bf(w_kv[l]))
        x2d = _merge(x2d, B, row(norm_mix_g[l]), bf(wl[:, o_q:o_q + d_xa]), bf(wl[:, o_g:]),
                     row(b_merge[l]), y_rg.reshape(S, B * d_rg), y_ml, kv, bf(w_br_rg[l]),
                     bf(w_br_ml[l]), bf(w_br_xa[l]), bf(w_out[l]))
        j = l // 2
        if l % 2 == 0:
            x2d = _ffn(x2d, row(norm_ffn_g[l]), bf(ffn_w1[j]), bf(ffn_w3[j]), bf(ffn_w2[j]))
        else:
            x2d = _moe(x2d, row(norm_ffn_g[l]), router_w[j], router_b[j], bf(moe_w1[j]),
                       bf(moe_w3[j]), bf(moe_w2[j]), fg, l == depth - 1)
    if depth % 2 == 1:
        x2d = _final_norm(x2d, fg)
    return x2d.reshape(B, S, D)
```

```python
import functools

import jax
import jax.numpy as jnp
from jax import lax
from jax.experimental import pallas as pl
from jax.experimental.pallas import tpu as pltpu

EPS = 1e-6
RG_C = 8.0
CONV_W = 4
ML_CHUNK = 128
XA_HEADS = 4
TOP_K = 2
N_BRANCH = 3

V7X_VMEM_BYTES = 64 * 1024 * 1024
LANES = 128
SUBLANES = 8

F32 = jnp.float32
BF16 = jnp.bfloat16


def _params(semantics, vmem_mib):
    assert vmem_mib * 1024 * 1024 <= V7X_VMEM_BYTES
    return pltpu.CompilerParams(dimension_semantics=semantics,
                                vmem_limit_bytes=vmem_mib * 1024 * 1024)


def _const_spec(shape):
    nd = len(shape)
    return pl.BlockSpec(shape, lambda *_: (0,) * nd)


def _rms(x, g):
    ms = jnp.mean(x * x, axis=-1, keepdims=True)
    return x * lax.rsqrt(ms + EPS) * g


def _dot(a, b):
    return jnp.dot(a, b, preferred_element_type=F32)


def _dot_nt(a, b):
    return lax.dot_general(a, b, (((1,), (1,)), ((), ())), preferred_element_type=F32)


def _dot_tn(a, b):
    return lax.dot_general(a, b, (((0,), (0,)), ((), ())), preferred_element_type=F32)


def _block_diag(xb, w_ref):
    G, bi, _ = w_ref.shape
    return jnp.concatenate([_dot(xb[:, g * bi:(g + 1) * bi], w_ref[g]) for g in range(G)], axis=1)


def _split3(x):
    h1 = x.astype(BF16)
    r1 = x - h1.astype(F32)
    h2 = r1.astype(BF16)
    h3 = (r1 - h2.astype(F32)).astype(BF16)
    return h1, h2, h3


def _rg_kernel(x_ref, g_ref, wax_ref, way_ref, cw_ref, cb_ref, wa_ref, ba_ref, wx_ref, bx_ref,
               lam_ref, o_ref, axbuf, a_s, b_s, h_s, carry, *, B, TT):
    R = TT * B
    halo = (CONV_W - 1) * B

    @pl.when(pl.program_id(0) == 0)
    def _():
        axbuf[0:halo, :] = jnp.zeros((halo, axbuf.shape[1]), F32)
        carry[...] = jnp.zeros_like(carry)

    x = pltpu.einshape("btd->tbd", x_ref[...]).reshape(R, x_ref.shape[2])
    h = _rms(x, g_ref[...]).astype(BF16)
    G, bi, _ = wa_ref.shape
    rate = -RG_C * jax.nn.softplus(-lam_ref[...])
    cols = [slice(g * bi, (g + 1) * bi) for g in range(G)]
    axbuf[halo:halo + R, cols[0]] = _dot(h, wax_ref[:, cols[0]])
    for g in range(G):
        cs = cols[g]
        if g + 1 < G:
            axbuf[halo:halo + R, cols[g + 1]] = _dot(h, wax_ref[:, cols[g + 1]])
        ay = _dot(h, way_ref[:, cs])
        cw = cw_ref[:, cs]
        xc = cb_ref[:, cs] + cw[0:1, :] * axbuf[0:R, cs]
        for k in range(1, CONV_W):
            xc = xc + cw[k:k + 1, :] * axbuf[k * B:k * B + R, cs]
        axbuf[0:halo, cs] = axbuf[R:R + halo, cs]
        xcb = xc.astype(BF16)
        r = jax.nn.sigmoid(_dot(xcb, wa_ref[g]) + ba_ref[:, cs])
        ig = jax.nn.sigmoid(_dot(xcb, wx_ref[g]) + bx_ref[:, cs])
        log_a = r * rate[:, cs]
        a = jnp.exp(log_a)
        a_s[:, cs] = a
        b_s[:, cs] = jnp.sqrt(-jnp.tanh(log_a) * (a * a + 1.0)) * (ig * xc)

        def step(t, hc, cs=cs):
            off = pl.multiple_of(t * B, B)
            hn = a_s[pl.ds(off, B), cs] * hc + b_s[pl.ds(off, B), cs]
            h_s[pl.ds(off, B), cs] = hn
            return hn

        carry[:, cs] = lax.fori_loop(0, TT, step, carry[:, cs], unroll=True)
        y = (h_s[:, cs] * jax.nn.gelu(ay)).reshape(TT, B, bi)
        o_ref[:, :, cs] = pltpu.einshape("tbc->btc", y).astype(BF16)


def _rg_branch(x3d, norm_g, w_ax, w_ay, conv_w, conv_b, w_a, b_a, w_x, b_x, lam):
    B, S, D = x3d.shape
    C = w_ax.shape[1]
    TT = min(64, S)
    assert S % TT == 0 and B % SUBLANES == 0
    R = TT * B
    halo = (CONV_W - 1) * B
    tile = lambda i: (0, i, 0)
    return pl.pallas_call(
        functools.partial(_rg_kernel, B=B, TT=TT),
        out_shape=jax.ShapeDtypeStruct((B, S, C), BF16),
        grid=(S // TT,),
        in_specs=[pl.BlockSpec((B, TT, D), tile), _const_spec((1, D)), _const_spec((D, C)),
                  _const_spec((D, C)), _const_spec((CONV_W, C)), _const_spec((1, C)),
                  _const_spec(w_a.shape), _const_spec((1, C)), _const_spec(w_x.shape),
                  _const_spec((1, C)), _const_spec((1, C))],
        out_specs=pl.BlockSpec((B, TT, C), tile),
        scratch_shapes=[pltpu.VMEM((halo + R, C), F32), pltpu.VMEM((R, C), F32),
                        pltpu.VMEM((R, C), F32), pltpu.VMEM((R, C), F32), pltpu.VMEM((B, C), F32)],
        compiler_params=_params(("arbitrary",), 48),
        name="rg_branch",
    )(x3d, norm_g, w_ax, w_ay, conv_w, conv_b, w_a, b_a, w_x, b_x, lam)


def _mlstm_kernel(x_ref, g_ref, wmu_ref, wmo_ref, wif_ref, wift_ref, bifc_ref, bifr_ref, cw_ref,
                  cb_ref, wq_ref, wk_ref, wv_ref, ng_ref, o_ref, ubuf, c_st, n_st, m_st, *, TS):
    H, d, _ = wq_ref.shape
    L = ML_CHUNK
    pad = SUBLANES
    tail = CONV_W - 1

    @pl.when(pl.program_id(1) == 0)
    def _():
        ubuf[0:pad, :] = jnp.zeros((pad, ubuf.shape[1]), F32)
        c_st[...] = jnp.zeros_like(c_st)
        n_st[...] = jnp.zeros_like(n_st)
        m_st[...] = jnp.zeros_like(m_st)

    h = _rms(x_ref[...], g_ref[...]).astype(BF16)
    u = _dot(h, wmu_ref[...])
    ubuf[pad:pad + TS, :] = u
    cw = cw_ref[...]
    c = cb_ref[...] + cw[0:1, :] * ubuf[pad - tail:pad - tail + TS, :]
    for k in range(1, CONV_W):
        c = c + cw[k:k + 1, :] * ubuf[pad - tail + k:pad - tail + k + TS, :]
    ubuf[0:pad, :] = ubuf[TS:TS + pad, :]
    c = jax.nn.silu(c)
    cb16 = c.astype(BF16)
    q = _block_diag(cb16, wq_ref)
    k_ = _block_diag(cb16, wk_ref) * (d ** -0.5)
    v = _block_diag(u.astype(BF16), wv_ref)
    og = jax.nn.sigmoid(_dot(h, wmo_ref[...]))
    if_c = _dot(h, wif_ref[...]) + bifc_ref[...]
    if_r = _dot_nt(wift_ref[...], h) + bifr_ref[...]
    lf_c = jax.nn.log_sigmoid(if_c)
    lf_r = jax.nn.log_sigmoid(if_r)
    ri = lax.broadcasted_iota(jnp.int32, (L, L), 0)
    ci = lax.broadcasted_iota(jnp.int32, (L, L), 1)
    causal = ci <= ri
    tri_l = jnp.where(causal, 1.0, 0.0).astype(BF16)
    tri_u = jnp.where(ri <= ci, 1.0, 0.0).astype(BF16)
    ng = ng_ref[...]

    for ck in range(TS // L):
        r0 = ck * L
        bc = sum(_dot(tri_l, p) for p in _split3(lf_c[r0:r0 + L, :]))
        br = sum(_dot(p, tri_u) for p in _split3(lf_r[:, r0:r0 + L]))
        for hd in range(H):
            b_col = bc[:, H + hd:H + hd + 1]
            b_row = br[H + hd:H + hd + 1, :]
            i_col = if_c[r0:r0 + L, hd:hd + 1]
            i_row = if_r[hd:hd + 1, r0:r0 + L]
            m = m_st[hd][0:1, 0:1]
            cs = slice(hd * d, (hd + 1) * d)
            qf = q[r0:r0 + L, cs]
            kf = k_[r0:r0 + L, cs]
            vf = v[r0:r0 + L, cs]
            qb, kb, vb = qf.astype(BF16), kf.astype(BF16), vf.astype(BF16)
            g = b_col + m
            dm = jnp.where(causal, b_col - b_row + i_row, -jnp.inf)
            m_row = jnp.maximum(g, jnp.max(dm, axis=-1, keepdims=True))
            w = jnp.exp(dm - m_row) * _dot_nt(qb, kb)
            inter = jnp.exp(g - m_row)
            cmat = c_st[hd]
            nvec = n_st[hd]
            num = inter * _dot(qb, cmat.astype(BF16)) + _dot(w.astype(BF16), vb)
            den = inter * jnp.sum(qf * nvec, axis=-1, keepdims=True) + jnp.sum(w, axis=-1, keepdims=True)
            hh = num / jnp.maximum(jnp.abs(den), jnp.exp(-m_row))
            b_last = b_col[L - 1:L, :]
            dl = b_last - b_col + i_col
            m_new = jnp.maximum(b_last + m, jnp.max(dl, axis=0, keepdims=True))
            decay = jnp.exp(b_last + m - m_new)
            wl = jnp.exp(dl - m_new)
            c_st[hd] = decay * cmat + _dot_tn(kb, (wl * vf).astype(BF16))
            n_st[hd] = decay * nvec + jnp.sum(wl * kf, axis=0, keepdims=True)
            m_st[hd] = jnp.broadcast_to(m_new, m_st.shape[1:])
            y = og[r0:r0 + L, cs] * hh
            y = y * lax.rsqrt(jnp.mean(y * y, axis=-1, keepdims=True) + EPS)
            o_ref[r0:r0 + L, cs] = (y * ng[:, cs]).astype(BF16)


def _mlstm_branch(x2d, B, norm_g, w_mu, w_mo, w_if, w_ift, b_if_c, b_if_r, conv_w, conv_b,
                  w_q, w_k, w_v, ml_norm_g):
    T, D = x2d.shape
    S = T // B
    C = w_mu.shape[1]
    H, d, _ = w_q.shape
    TS = min(256, S)
    assert S % TS == 0 and TS % ML_CHUNK == 0
    nS = S // TS
    row = lambda b, s: (b * nS + s, 0)
    return pl.pallas_call(
        functools.partial(_mlstm_kernel, TS=TS),
        out_shape=jax.ShapeDtypeStruct((T, C), BF16),
        grid=(B, nS),
        in_specs=[pl.BlockSpec((TS, D), row), _const_spec((1, D)), _const_spec((D, C)),
                  _const_spec((D, C)), _const_spec((D, LANES)), _const_spec((SUBLANES, D)),
                  _const_spec((1, LANES)), _const_spec((SUBLANES, 1)), _const_spec((CONV_W, C)),
                  _const_spec((1, C)), _const_spec(w_q.shape), _const_spec(w_k.shape),
                  _const_spec(w_v.shape), _const_spec((1, C))],
        out_specs=pl.BlockSpec((TS, C), row),
        scratch_shapes=[pltpu.VMEM((SUBLANES + TS, C), F32), pltpu.VMEM((H, d, d), F32),
                        pltpu.VMEM((H, 1, d), F32), pltpu.VMEM((H, SUBLANES, LANES), F32)],
        compiler_params=_params(("arbitrary", "arbitrary"), 48),
        name="mlstm_branch",
    )(x2d, norm_g, w_mu, w_mo, w_if, w_ift, b_if_c, b_if_r, conv_w, conv_b, w_q, w_k, w_v, ml_norm_g)


def _kv_kernel(mem_ref, g_ref, w_ref, o_ref):
    o_ref[...] = _dot(_rms(mem_ref[...], g_ref[...]).astype(BF16), w_ref[...]).astype(BF16)


def _mem_kv(mem, g, w_kv):
    B, M, D = mem.shape
    N = w_kv.shape[1]
    return pl.pallas_call(
        _kv_kernel,
        out_shape=jax.ShapeDtypeStruct((B, M, N), BF16),
        grid=(B,),
        in_specs=[pl.BlockSpec((None, M, D), lambda b: (b, 0, 0)), _const_spec((1, D)),
                  _const_spec((D, N))],
        out_specs=pl.BlockSpec((None, M, N), lambda b: (b, 0, 0)),
        compiler_params=_params(("arbitrary",), 32),
        name="mem_kv",
    )(mem, g, w_kv)


def _merge_kernel(x_ref, g_ref, wq_ref, wg_ref, bm_ref, yrg_ref, yml_ref, kv_ref, wrg_ref, wml_ref,
                  wxa_ref, wo_ref, o_ref):
    x = x_ref[...]
    D = x.shape[1]
    h = _rms(x, g_ref[...]).astype(BF16)
    q = _dot(h, wq_ref[...]).astype(BF16)
    dxa = q.shape[1]
    dh = dxa // XA_HEADS
    heads = []
    for hd in range(XA_HEADS):
        kh = kv_ref[:, hd * dh:(hd + 1) * dh]
        vh = kv_ref[:, dxa + hd * dh:dxa + (hd + 1) * dh]
        s = _dot_nt(q[:, hd * dh:(hd + 1) * dh], kh) * (dh ** -0.5)
        e = jnp.exp(s - jnp.max(s, axis=-1, keepdims=True))
        p = e / jnp.sum(e, axis=-1, keepdims=True)
        heads.append(_dot(p.astype(BF16), vh))
    y_xa = jnp.concatenate(heads, axis=1).astype(BF16)

    def gate(k):
        return jax.nn.sigmoid(_dot(h, wg_ref[:, k * D:(k + 1) * D]) + bm_ref[:, k * D:(k + 1) * D])

    merged = gate(0) * _dot(yrg_ref[...], wrg_ref[...])
    merged = merged + gate(1) * _dot(yml_ref[...], wml_ref[...])
    merged = merged + gate(2) * _dot(y_xa, wxa_ref[...])
    o_ref[...] = x + _dot(merged.astype(BF16), wo_ref[...])


def _merge(x2d, B, norm_g, w_q, w_g, b_merge, y_rg, y_ml, kv, w_br_rg, w_br_ml, w_br_xa, w_out):
    T, D = x2d.shape
    S = T // B
    C = y_ml.shape[1]
    M, N = kv.shape[1:]
    TM = min(512, S)
    assert S % TM == 0
    nS = S // TM
    row = lambda b, s: (b * nS + s, 0)
    one = pl.Buffered(1)
    cspec = lambda shape: pl.BlockSpec(shape, lambda *_: (0,) * len(shape), pipeline_mode=one)
    return pl.pallas_call(
        _merge_kernel,
        out_shape=jax.ShapeDtypeStruct((T, D), F32),
        grid=(B, nS),
        in_specs=[pl.BlockSpec((TM, D), row), cspec((1, D)), cspec(w_q.shape), cspec(w_g.shape),
                  cspec(b_merge.shape), pl.BlockSpec((TM, C), row),
                  pl.BlockSpec((TM, C), row), pl.BlockSpec((None, M, N), lambda b, s: (b, 0, 0)),
                  cspec(w_br_rg.shape), cspec(w_br_ml.shape), cspec(w_br_xa.shape),
                  cspec(w_out.shape)],
        out_specs=pl.BlockSpec((TM, D), row),
        compiler_params=_params(("arbitrary", "arbitrary"), 56),
        name="merge",
    )(x2d, norm_g, w_q, w_g, b_merge, y_rg, y_ml, kv, w_br_rg, w_br_ml, w_br_xa, w_out)


def _ffn_kernel(x_ref, g_ref, w1_ref, w3_ref, w2_ref, o_ref, hs, acc):
    f = pl.program_id(1)

    @pl.when(f == 0)
    def _():
        hs[...] = _rms(x_ref[...], g_ref[...]).astype(BF16)
        acc[...] = jnp.zeros_like(acc)

    hb = hs[...]
    mid = jax.nn.silu(_dot(hb, w1_ref[...])) * _dot(hb, w3_ref[...])
    acc[...] += _dot(mid.astype(BF16), w2_ref[...])

    @pl.when(f == pl.num_programs(1) - 1)
    def _():
        o_ref[...] = x_ref[...] + acc[...]


FFN_ROW_TILE = 512


def _ffn_hidden_tile(F):
    for tf in (1792, 1024, 512, 256):
        if F % tf == 0:
            return tf
    return F


def _ffn(x2d, norm_g, w1, w3, w2):
    T, D = x2d.shape
    F = w1.shape[1]
    TM = min(FFN_ROW_TILE, T)
    TF = _ffn_hidden_tile(F)
    assert T % TM == 0
    return pl.pallas_call(
        _ffn_kernel,
        out_shape=jax.ShapeDtypeStruct((T, D), F32),
        grid=(T // TM, F // TF),
        in_specs=[pl.BlockSpec((TM, D), lambda i, f: (i, 0)), _const_spec((1, D)),
                  pl.BlockSpec((D, TF), lambda i, f: (0, f)), pl.BlockSpec((D, TF), lambda i, f: (0, f)),
                  pl.BlockSpec((TF, D), lambda i, f: (f, 0))],
        out_specs=pl.BlockSpec((TM, D), lambda i, f: (i, 0)),
        scratch_shapes=[pltpu.VMEM((TM, D), BF16), pltpu.VMEM((TM, D), F32)],
        compiler_params=_params(("arbitrary", "arbitrary"), 56),
        name="ffn_dense",
    )(x2d, norm_g, w1, w3, w2)


ROUTE_TM = 512
RUN_ALIGN = 16


def _route_kernel(x_ref, g_ref, whi_ref, wlo_ref, rb_ref, idx_ref, gate_ref, len_ref, off_ref,
                  tot_ref, run_s, *, E):
    @pl.when(pl.program_id(0) == 0)
    def _():
        run_s[...] = jnp.zeros_like(run_s)

    h2 = _rms(x_ref[...], g_ref[...])
    TM = h2.shape[0]
    hi = h2.astype(BF16)
    lo = (h2 - hi.astype(F32)).astype(BF16)
    logits = _dot(hi, whi_ref[...]) + (_dot(lo, whi_ref[...]) + _dot(hi, wlo_ref[...]))
    lt = logits.T[0:E, :] + rb_ref[...]
    ie = lax.broadcasted_iota(jnp.int32, (E, TM), 0)
    m1 = jnp.max(lt, axis=0, keepdims=True)
    i1 = jnp.min(jnp.where(lt == m1, ie, E), axis=0, keepdims=True)
    l2 = jnp.where(ie == i1, -jnp.inf, lt)
    m2 = jnp.max(l2, axis=0, keepdims=True)
    i2 = jnp.min(jnp.where(l2 == m2, ie, E), axis=0, keepdims=True)
    ex = jnp.exp(m2 - m1)
    g1 = 1.0 / (1.0 + ex)
    g2 = ex / (1.0 + ex)
    oh1 = jnp.where(ie == i1, 1.0, 0.0)
    oh2 = jnp.where(ie == i2, 1.0, 0.0)
    oh = oh1 + oh2
    ri = lax.broadcasted_iota(jnp.int32, (TM, TM), 0)
    ci = lax.broadcasted_iota(jnp.int32, (TM, TM), 1)
    upper = jnp.where(ri < ci, 1.0, 0.0).astype(BF16)
    excl = _dot(oh.astype(BF16), upper)
    cnt = jnp.sum(oh, axis=1, keepdims=True).astype(jnp.int32)
    run_len = jnp.broadcast_to(((cnt + (RUN_ALIGN - 1)) // RUN_ALIGN) * RUN_ALIGN, (E, LANES))
    iec = lax.broadcasted_iota(jnp.int32, (E, LANES), 0)
    run_start = jnp.zeros((E, LANES), jnp.int32)
    for e in range(E - 1):
        run_start = run_start + jnp.where(iec > e, run_len[e:e + 1, :], 0)
    slot = run_start[:, 0:1].astype(F32) + excl
    s1 = jnp.sum(oh1 * slot, axis=0, keepdims=True).astype(jnp.int32)
    s2 = jnp.sum(oh2 * slot, axis=0, keepdims=True).astype(jnp.int32)
    len_ref[...] = run_len
    off_ref[...] = run_s[...]
    run_s[...] = run_s[...] + run_len
    tot_ref[...] = run_s[...]
    row = lax.broadcasted_iota(jnp.int32, (SUBLANES, TM), 0)
    idx_ref[...] = jnp.where(row == 0, i1, jnp.where(row == 1, i2, jnp.where(row == 2, s1,
                             jnp.where(row == 3, s2, 0))))
    gate_ref[...] = jnp.where(row == 0, g1, jnp.where(row == 1, g2, 0.0))


def _route(x2d, norm_g, w_hi, w_lo, rb, E):
    T, D = x2d.shape
    TM = min(ROUTE_TM, T)
    nT = T // TM
    assert T % TM == 0 and E == SUBLANES
    return pl.pallas_call(
        functools.partial(_route_kernel, E=E),
        out_shape=(jax.ShapeDtypeStruct((SUBLANES, T), jnp.int32),
                   jax.ShapeDtypeStruct((SUBLANES, T), F32),
                   jax.ShapeDtypeStruct((nT * E, LANES), jnp.int32),
                   jax.ShapeDtypeStruct((nT * E, LANES), jnp.int32),
                   jax.ShapeDtypeStruct((E, LANES), jnp.int32)),
        grid=(nT,),
        in_specs=[pl.BlockSpec((TM, D), lambda i: (i, 0)), _const_spec((1, D)),
                  _const_spec((D, LANES)), _const_spec((D, LANES)), _const_spec((E, 1))],
        out_specs=(pl.BlockSpec((SUBLANES, TM), lambda i: (0, i)),
                   pl.BlockSpec((SUBLANES, TM), lambda i: (0, i)),
                   pl.BlockSpec((E, LANES), lambda i: (i, 0)),
                   pl.BlockSpec((E, LANES), lambda i: (i, 0)),
                   _const_spec((E, LANES))),
        scratch_shapes=[pltpu.VMEM((E, LANES), jnp.int32)],
        compiler_params=_params(("arbitrary",), 32),
        name="moe_route",
    )(x2d, norm_g, w_hi, w_lo, rb)


def _run_dmas(len_ref, off_ref, i, E, max_len, tile_ref, sorted_ref, sem, to_sorted, wait):
    local = 0
    for e in range(E):
        n = len_ref[i * E + e]
        base = off_ref[i * E + e]
        done = 0
        sz = max_len
        while sz >= RUN_ALIGN:
            @pl.when((n & sz) != 0)
            def _(sz=sz, local=local, base=base, done=done):
                t_rows = tile_ref.at[pl.ds(pl.multiple_of(local + done, RUN_ALIGN), sz), :]
                s_rows = sorted_ref.at[pl.ds(pl.multiple_of(base + done, RUN_ALIGN), sz), :]
                cp = (pltpu.make_async_copy(t_rows, s_rows, sem) if to_sorted
                      else pltpu.make_async_copy(s_rows, t_rows, sem))
                if wait:
                    cp.wait()
                else:
                    cp.start()
            done = done + (n & sz)
            sz //= 2
        local = local + n


def _dispatch_kernel(len_ref, off_ref, x_ref, g_ref, ls_ref, init_ref, xs_ref, xs_t, sem, *, E):
    del init_ref
    i = pl.program_id(0)
    TM = x_ref.shape[0]
    LS = xs_t.shape[0]
    h2 = _rms(x_ref[...], g_ref[...]).astype(BF16)
    ls = ls_ref[...]
    j = lax.broadcasted_iota(jnp.int32, (LS, TM), 0)
    onehot = jnp.where(j == ls[2:3, :], 1.0, jnp.where(j == ls[3:4, :], 1.0, 0.0)).astype(BF16)
    xs_t[...] = _dot(onehot, h2).astype(BF16)
    _run_dmas(len_ref, off_ref, i, E, TM, xs_t, xs_ref, sem, True, False)
    _run_dmas(len_ref, off_ref, i, E, TM, xs_t, xs_ref, sem, True, True)


def _dispatch(run_len, run_off, idx, x2d, norm_g, P, E):
    T, D = x2d.shape
    TM = min(ROUTE_TM, T)
    LS = TOP_K * TM + E * RUN_ALIGN
    init = jnp.zeros((P, D), BF16)
    grid_spec = pltpu.PrefetchScalarGridSpec(
        num_scalar_prefetch=2,
        grid=(T // TM,),
        in_specs=[pl.BlockSpec((TM, D), lambda i, *_: (i, 0)),
                  pl.BlockSpec((1, D), lambda i, *_: (0, 0)),
                  pl.BlockSpec((SUBLANES, TM), lambda i, *_: (0, i)),
                  pl.BlockSpec(memory_space=pl.ANY)],
        out_specs=pl.BlockSpec(memory_space=pl.ANY),
        scratch_shapes=[pltpu.VMEM((LS, D), BF16), pltpu.SemaphoreType.DMA(())])
    return pl.pallas_call(
        functools.partial(_dispatch_kernel, E=E),
        out_shape=jax.ShapeDtypeStruct((P, D), BF16),
        grid_spec=grid_spec,
        input_output_aliases={5: 0},
        compiler_params=_params(("arbitrary",), 32),
        name="moe_dispatch",
    )(run_len, run_off, x2d, norm_g, idx, init)


def _group_ffn_kernel(te_ref, tn_ref, x_ref, w1_ref, w3_ref, w2_ref, o_ref, acc):
    i = pl.program_id(0)
    f = pl.program_id(1)

    @pl.when(f == 0)
    def _():
        acc[...] = jnp.zeros_like(acc)

    @pl.when(tn_ref[i] > 0)
    def _():
        hb = x_ref[...]
        mid = jax.nn.silu(_dot(hb, w1_ref[...])) * _dot(hb, w3_ref[...])
        acc[...] += _dot(mid.astype(BF16), w2_ref[...])

    @pl.when(f == pl.num_programs(1) - 1)
    def _():
        o_ref[...] = acc[...]


def _group_ffn(tile_e, tile_n, xs, w1, w3, w2, TMG):
    P, D = xs.shape
    F = w1.shape[2]
    TF = _ffn_hidden_tile(F)
    nF = F // TF
    fblk = lambda i, f, tn: jnp.where(tn[i] > 0, f, nF - 1)
    grid_spec = pltpu.PrefetchScalarGridSpec(
        num_scalar_prefetch=2,
        grid=(P // TMG, nF),
        in_specs=[pl.BlockSpec((TMG, D), lambda i, f, te, tn: (i, 0)),
                  pl.BlockSpec((None, D, TF), lambda i, f, te, tn: (te[i], 0, fblk(i, f, tn))),
                  pl.BlockSpec((None, D, TF), lambda i, f, te, tn: (te[i], 0, fblk(i, f, tn))),
                  pl.BlockSpec((None, TF, D), lambda i, f, te, tn: (te[i], fblk(i, f, tn), 0))],
        out_specs=pl.BlockSpec((TMG, D), lambda i, f, te, tn: (i, 0)),
        scratch_shapes=[pltpu.VMEM((TMG, D), F32)])
    return pl.pallas_call(
        _group_ffn_kernel,
        out_shape=jax.ShapeDtypeStruct((P, D), F32),
        grid_spec=grid_spec,
        compiler_params=_params(("arbitrary", "arbitrary"), 56),
        name="moe_group_ffn",
    )(tile_e, tile_n, xs, w1, w3, w2)


def _combine_kernel(len_ref, off_ref, x_ref, lsr_ref, lsc_ref, gt_ref, fg_ref, ys_ref, o_ref, yt, sem,
                    *, E, final_norm):
    i = pl.program_id(0)
    TM = x_ref.shape[0]
    LS, D = yt.shape
    yt[TOP_K * TM:LS, :] = jnp.zeros((LS - TOP_K * TM, D), F32)
    _run_dmas(len_ref, off_ref, i, E, TM, yt, ys_ref, sem, False, False)
    lsr = lsr_ref[...]
    gt = gt_ref[...]
    js = lax.broadcasted_iota(jnp.int32, (LS, TM), 0)
    lane = lax.broadcasted_iota(jnp.int32, (TM, LANES), 1)
    gs = None
    for k in range(TOP_K):
        t1, t2, t3 = (t.astype(F32) for t in _split3(gt[:, k:k + 1]))
        terms = jnp.where(lane == 0, t1, jnp.where(lane == 1, t2, jnp.where(lane == 2, t3, 0.0)))
        pk = jnp.where(js == lsr[TOP_K + k:TOP_K + k + 1, :], 1.0, 0.0).astype(BF16)
        gk = jnp.sum(_dot(pk, terms.astype(BF16)), axis=1, keepdims=True)
        gs = gk if gs is None else gs + gk
    _run_dmas(len_ref, off_ref, i, E, TM, yt, ys_ref, sem, False, True)
    z = yt[...] * gs
    zh = z.astype(BF16)
    zl = (z - zh.astype(F32)).astype(BF16)
    lsc = lsc_ref[...]
    jt = lax.broadcasted_iota(jnp.int32, (TM, LS), 1)
    pt = None
    for k in range(TOP_K):
        ok = jnp.where(jt == lsc[:, TOP_K + k:TOP_K + k + 1], 1.0, 0.0)
        pt = ok if pt is None else pt + ok
    pt = pt.astype(BF16)
    xo = x_ref[...] + (_dot(pt, zh) + _dot(pt, zl))
    o_ref[...] = _rms(xo, fg_ref[...]) if final_norm else xo


def _combine(run_len, run_off, x2d, idx, idx_t, gates_t, final_g, ys, E, final_norm):
    T, D = x2d.shape
    TM = min(ROUTE_TM, T)
    LS = TOP_K * TM + E * RUN_ALIGN
    grid_spec = pltpu.PrefetchScalarGridSpec(
        num_scalar_prefetch=2,
        grid=(T // TM,),
        in_specs=[pl.BlockSpec((TM, D), lambda i, *_: (i, 0)),
                  pl.BlockSpec((SUBLANES, TM), lambda i, *_: (0, i)),
                  pl.BlockSpec((TM, SUBLANES), lambda i, *_: (i, 0)),
                  pl.BlockSpec((TM, SUBLANES), lambda i, *_: (i, 0)),
                  pl.BlockSpec((1, D), lambda i, *_: (0, 0)),
                  pl.BlockSpec(memory_space=pl.ANY)],
        out_specs=pl.BlockSpec((TM, D), lambda i, *_: (i, 0)),
        scratch_shapes=[pltpu.VMEM((LS, D), F32), pltpu.SemaphoreType.DMA(())])
    return pl.pallas_call(
        functools.partial(_combine_kernel, E=E, final_norm=final_norm),
        out_shape=jax.ShapeDtypeStruct((T, D), F32),
        grid_spec=grid_spec,
        compiler_params=_params(("arbitrary",), 48),
        name="moe_combine",
    )(run_len, run_off, x2d, idx, idx_t, gates_t, final_g, ys)


def _moe(x2d, norm_g, router_w, router_b, w1, w3, w2, final_g, final_norm):
    T, D = x2d.shape
    E = router_w.shape[1]
    TMG = 512
    nT = T // min(ROUTE_TM, T)
    rw = jnp.pad(router_w, ((0, 0), (0, LANES - E)))
    rw_hi = rw.astype(BF16)
    rw_lo = (rw - rw_hi.astype(F32)).astype(BF16)
    idx, gates, run_len, run_off, tot = _route(x2d, norm_g, rw_hi, rw_lo, router_b.reshape(E, 1), E)
    tot = tot[:, 0]
    padded = ((tot + TMG - 1) // TMG) * TMG
    pend = jnp.cumsum(padded)
    pstart = pend - padded
    run_len = run_len[:, 0]
    run_off = (run_off[:, 0].reshape(nT, E) + pstart[None, :]).reshape(nT * E)
    P = -(-(T * TOP_K + nT * E * (RUN_ALIGN - 1) + E * (TMG - 1)) // TMG) * TMG
    tile_start = jnp.arange(P // TMG, dtype=jnp.int32) * TMG
    tile_e = jnp.minimum(jnp.sum(tile_start[:, None] >= pend[None, :], axis=1), E - 1).astype(jnp.int32)
    sel = tile_e[:, None] == jnp.arange(E, dtype=jnp.int32)[None, :]
    tile_end = jnp.sum(jnp.where(sel, (pstart + tot)[None, :], 0), axis=1)
    tile_n = jnp.clip(tile_end - tile_start, 0, TMG).astype(jnp.int32)
    xs = _dispatch(run_len, run_off, idx, x2d, norm_g, P, E)
    ys = _group_ffn(tile_e, tile_n, xs, w1, w3, w2, TMG)
    return _combine(run_len, run_off, x2d, idx, idx.T, gates.T, final_g, ys, E, final_norm)


def _final_norm_kernel(x_ref, g_ref, o_ref):
    o_ref[...] = _rms(x_ref[...], g_ref[...])


def _final_norm(x2d, g):
    T, D = x2d.shape
    TM = min(1024, T)
    return pl.pallas_call(
        _final_norm_kernel,
        out_shape=jax.ShapeDtypeStruct((T, D), F32),
        grid=(T // TM,),
        in_specs=[pl.BlockSpec((TM, D), lambda i: (i, 0)), _const_spec((1, D))],
        out_specs=pl.BlockSpec((TM, D), lambda i: (i, 0)),
        compiler_params=_params(("arbitrary",), 32),
        name="final_norm",
    )(x2d, g)


def kernel(x, mem, norm_mix_g, w_in, conv_rg_w, conv_rg_b, rg_w_a, rg_b_a, rg_w_x, rg_b_x, rg_lambda,
           conv_ml_w, conv_ml_b, ml_w_q, ml_w_k, ml_w_v, ml_b_i, ml_b_f, ml_norm_g, mem_norm_g, w_kv,
           w_br_rg, w_br_ml, w_br_xa, b_merge, w_out, norm_ffn_g, ffn_w1, ffn_w3, ffn_w2, router_w,
           router_b, moe_w1, moe_w3, moe_w2, final_norm_g):
    B, S, D = x.shape
    depth = w_in.shape[0]
    d_rg = conv_rg_w.shape[2]
    d_ml = conv_ml_w.shape[2]
    H = ml_w_q.shape[1]
    d_xa = w_kv.shape[2] // 2
    o_ax, o_ay = 0, d_rg
    o_mu, o_mo = 2 * d_rg, 2 * d_rg + d_ml
    o_mi = 2 * d_rg + 2 * d_ml
    o_mf = o_mi + H
    o_q = o_mf + H
    o_g = o_q + d_xa
    assert w_in.shape[2] == o_g + N_BRANCH * D and 2 * H <= SUBLANES

    bf = lambda a: a.astype(BF16)
    row = lambda a: a.reshape(1, -1)
    x2d = x.reshape(B * S, D)
    fg = row(final_norm_g)
    for l in range(depth):
        wl = w_in[l]
        w_if = jnp.pad(wl[:, o_mi:o_mi + 2 * H], ((0, 0), (0, LANES - 2 * H)))
        w_ift = jnp.pad(wl[:, o_mi:o_mi + 2 * H].T, ((0, SUBLANES - 2 * H), (0, 0)))
        b_if = jnp.concatenate([ml_b_i[l], ml_b_f[l]])
        y_ml = _mlstm_branch(x2d, B, row(norm_mix_g[l]), bf(wl[:, o_mu:o_mu + d_ml]),
                                   bf(wl[:, o_mo:o_mo + d_ml]), bf(w_if), bf(w_ift),
                                   jnp.pad(b_if, (0, LANES - 2 * H)).reshape(1, LANES),
                                   jnp.pad(b_if, (0, SUBLANES - 2 * H)).reshape(SUBLANES, 1),
                                   conv_ml_w[l], row(conv_ml_b[l]), bf(ml_w_q[l]), bf(ml_w_k[l]),
                                   bf(ml_w_v[l]), row(ml_norm_g[l]))
        y_rg = _rg_branch(x2d.reshape(B, S, D), row(norm_mix_g[l]), bf(wl[:, o_ax:o_ax + d_rg]),
                          bf(wl[:, o_ay:o_ay + d_rg]), conv_rg_w[l], row(conv_rg_b[l]), bf(rg_w_a[l]),
                          row(rg_b_a[l]), bf(rg_w_x[l]), row(rg_b_x[l]), row(rg_lambda[l]))
        kv = _mem_kv(mem, row(mem_norm_g[l]), bf(w_kv[l]))
        x2d = _merge(x2d, B, row(norm_mix_g[l]), bf(wl[:, o_q:o_q + d_xa]), bf(wl[:, o_g:]),
                     row(b_merge[l]), y_rg.reshape(B * S, d_rg), y_ml, kv, bf(w_br_rg[l]),
                     bf(w_br_ml[l]), bf(w_br_xa[l]), bf(w_out[l]))
        j = l // 2
        if l % 2 == 0:
            x2d = _ffn(x2d, row(norm_ffn_g[l]), bf(ffn_w1[j]), bf(ffn_w3[j]), bf(ffn_w2[j]))
        else:
            x2d = _moe(x2d, row(norm_ffn_g[l]), router_w[j], router_b[j], bf(moe_w1[j]),
                       bf(moe_w3[j]), bf(moe_w2[j]), fg, l == depth - 1)
    if depth % 2 == 1:
        x2d = _final_norm(x2d, fg)
    return x2d.reshape(B, S, D)
```

```python
import functools

import jax
import jax.numpy as jnp
from jax import lax
from jax.experimental import pallas as pl
from jax.experimental.pallas import tpu as pltpu

EPS = 1e-6
RG_C = 8.0
CONV_W = 4
ML_CHUNK = 128
XA_HEADS = 4
TOP_K = 2
N_BRANCH = 3

V7X_VMEM_BYTES = 64 * 1024 * 1024
LANES = 128
SUBLANES = 8

F32 = jnp.float32
BF16 = jnp.bfloat16


def _params(semantics, vmem_mib):
    assert vmem_mib * 1024 * 1024 <= V7X_VMEM_BYTES
    return pltpu.CompilerParams(dimension_semantics=semantics,
                                vmem_limit_bytes=vmem_mib * 1024 * 1024)


def _const_spec(shape):
    nd = len(shape)
    return pl.BlockSpec(shape, lambda *_: (0,) * nd)


def _rms(x, g):
    ms = jnp.mean(x * x, axis=-1, keepdims=True)
    return x * lax.rsqrt(ms + EPS) * g


def _dot(a, b):
    return jnp.dot(a, b, preferred_element_type=F32)


def _dot_nt(a, b):
    return lax.dot_general(a, b, (((1,), (1,)), ((), ())), preferred_element_type=F32)


def _dot_tn(a, b):
    return lax.dot_general(a, b, (((0,), (0,)), ((), ())), preferred_element_type=F32)


def _split3(x):
    h1 = x.astype(BF16)
    r1 = x - h1.astype(F32)
    h2 = r1.astype(BF16)
    h3 = (r1 - h2.astype(F32)).astype(BF16)
    return h1, h2, h3


def _rg_kernel(x_ref, g_ref, wax_ref, way_ref, cw_ref, cb_ref, wa_ref, ba_ref, wx_ref, bx_ref,
               lam_ref, o_ref, axbuf, a_s, b_s, h_s, carry, *, B, TT):
    R = TT * B
    halo = (CONV_W - 1) * B

    @pl.when(pl.program_id(0) == 0)
    def _():
        axbuf[0:halo, :] = jnp.zeros((halo, axbuf.shape[1]), F32)
        carry[...] = jnp.zeros_like(carry)

    x = pltpu.einshape("btd->tbd", x_ref[...]).reshape(R, x_ref.shape[2])
    h = _rms(x, g_ref[...]).astype(BF16)
    G, bi, _ = wa_ref.shape
    rate = -RG_C * jax.nn.softplus(-lam_ref[...])
    cols = [slice(g * bi, (g + 1) * bi) for g in range(G)]
    axbuf[halo:halo + R, cols[0]] = _dot(h, wax_ref[:, cols[0]])
    for g in range(G):
        cs = cols[g]
        if g + 1 < G:
            axbuf[halo:halo + R, cols[g + 1]] = _dot(h, wax_ref[:, cols[g + 1]])
        ay = _dot(h, way_ref[:, cs])
        cw = cw_ref[:, cs]
        xc = cb_ref[:, cs] + cw[0:1, :] * axbuf[0:R, cs]
        for k in range(1, CONV_W):
            xc = xc + cw[k:k + 1, :] * axbuf[k * B:k * B + R, cs]
        axbuf[0:halo, cs] = axbuf[R:R + halo, cs]
        xcb = xc.astype(BF16)
        r = jax.nn.sigmoid(_dot(xcb, wa_ref[g]) + ba_ref[:, cs])
        ig = jax.nn.sigmoid(_dot(xcb, wx_ref[g]) + bx_ref[:, cs])
        log_a = r * rate[:, cs]
        a = jnp.exp(log_a)
        a_s[:, cs] = a
        b_s[:, cs] = jnp.sqrt(-jnp.tanh(log_a) * (a * a + 1.0)) * (ig * xc)

        def step(t, hc, cs=cs):
            off = pl.multiple_of(t * B, B)
            hn = a_s[pl.ds(off, B), cs] * hc + b_s[pl.ds(off, B), cs]
            h_s[pl.ds(off, B), cs] = hn
            return hn

        carry[:, cs] = lax.fori_loop(0, TT, step, carry[:, cs], unroll=True)
        y = (h_s[:, cs] * jax.nn.gelu(ay)).reshape(TT, B, bi)
        o_ref[:, :, cs] = pltpu.einshape("tbc->btc", y).astype(BF16)


def _rg_branch(x3d, norm_g, w_ax, w_ay, conv_w, conv_b, w_a, b_a, w_x, b_x, lam):
    B, S, D = x3d.shape
    C = w_ax.shape[1]
    TT = min(64, S)
    assert S % TT == 0 and B % SUBLANES == 0
    R = TT * B
    halo = (CONV_W - 1) * B
    tile = lambda i: (0, i, 0)
    return pl.pallas_call(
        functools.partial(_rg_kernel, B=B, TT=TT),
        out_shape=jax.ShapeDtypeStruct((B, S, C), BF16),
        grid=(S // TT,),
        in_specs=[pl.BlockSpec((B, TT, D), tile), _const_spec((1, D)), _const_spec((D, C)),
                  _const_spec((D, C)), _const_spec((CONV_W, C)), _const_spec((1, C)),
                  _const_spec(w_a.shape), _const_spec((1, C)), _const_spec(w_x.shape),
                  _const_spec((1, C)), _const_spec((1, C))],
        out_specs=pl.BlockSpec((B, TT, C), tile),
        scratch_shapes=[pltpu.VMEM((halo + R, C), F32), pltpu.VMEM((R, C), F32),
                        pltpu.VMEM((R, C), F32), pltpu.VMEM((R, C), F32), pltpu.VMEM((B, C), F32)],
        compiler_params=_params(("arbitrary",), 48),
        name="rg_branch",
    )(x3d, norm_g, w_ax, w_ay, conv_w, conv_b, w_a, b_a, w_x, b_x, lam)


def _alternate(first, second):
    result = None
    live = [True, True]
    while any(live):
        for n, gen in enumerate((first, second)):
            if live[n]:
                try:
                    next(gen)
                except StopIteration as stop:
                    live[n] = False
                    if n == 1:
                        result = stop.value
    return result


def _mlstm_kernel(x_ref, g_ref, wmu_ref, wmo_ref, wif_ref, wift_ref, bifc_ref, bifr_ref, cw_ref,
                  cb_ref, wq_ref, wk_ref, wv_ref, ng_ref, o_ref, ubuf, q_s, k_s, v_s, og_s, ifc_s, ifr_s,
                  lfc_s, lfr_s, c_st, n_st, m_st, *, TS, nS):
    H, d, _ = wq_ref.shape
    L = ML_CHUNK
    pad = SUBLANES
    tail = CONV_W - 1
    g = pl.program_id(0)

    @pl.when(g == 0)
    def _():
        for r in (q_s, k_s, v_s, og_s, ifc_s, ifr_s, lfc_s, lfr_s):
            r[...] = jnp.zeros_like(r)

    @pl.when(lax.rem(g, nS) == 0)
    def _():
        ubuf[0:pad, :] = jnp.zeros((pad, ubuf.shape[1]), F32)

    @pl.when((g == 0) | (lax.rem(g + (nS - 1), nS) == 0))
    def _():
        c_st[...] = jnp.zeros_like(c_st)
        n_st[...] = jnp.zeros_like(n_st)
        m_st[...] = jnp.zeros_like(m_st)

    if_c = ifc_s[...]
    if_r = ifr_s[...]
    lf_c = lfc_s[...]
    lf_r = lfr_s[...]
    ri = lax.broadcasted_iota(jnp.int32, (L, L), 0)
    ci = lax.broadcasted_iota(jnp.int32, (L, L), 1)
    causal = ci <= ri
    tri_l = jnp.where(causal, 1.0, 0.0).astype(BF16)
    tri_u = jnp.where(ri <= ci, 1.0, 0.0).astype(BF16)
    ng = ng_ref[...]
    nck = TS // L
    bcs = [sum(_dot(tri_l, p) for p in _split3(lf_c[ck * L:(ck + 1) * L, :])) for ck in range(nck)]
    brs = [sum(_dot(p, tri_u) for p in _split3(lf_r[:, ck * L:(ck + 1) * L])) for ck in range(nck)]
    h = _rms(x_ref[...], g_ref[...]).astype(BF16)

    def recurrence(hd):
        cs = slice(hd * d, (hd + 1) * d)
        for ck in range(nck):
            r0 = ck * L
            bc, br = bcs[ck], brs[ck]
            b_col = bc[:, H + hd:H + hd + 1]
            b_row = br[H + hd:H + hd + 1, :]
            i_col = if_c[r0:r0 + L, hd:hd + 1]
            i_row = if_r[hd:hd + 1, r0:r0 + L]
            m = m_st[hd][0:1, 0:1]
            qb = q_s[r0:r0 + L, cs]
            kb = k_s[r0:r0 + L, cs]
            vb = v_s[r0:r0 + L, cs]
            gg = b_col + m
            dm = jnp.where(causal, b_col - b_row + i_row, -jnp.inf)
            yield
            m_row = jnp.maximum(gg, jnp.max(dm, axis=-1, keepdims=True))
            s_qk = _dot_nt(qb, kb)
            yield
            w = jnp.exp(dm - m_row) * s_qk
            inter = jnp.exp(gg - m_row)
            cmat = c_st[hd]
            nvec = n_st[hd]
            yield
            num = inter * _dot(qb, cmat.astype(BF16)) + _dot(w.astype(BF16), vb)
            yield
            qf, kf, vf = qb.astype(F32), kb.astype(F32), vb.astype(F32)
            den = inter * jnp.sum(qf * nvec, axis=-1, keepdims=True) + jnp.sum(w, axis=-1, keepdims=True)
            hh = num / jnp.maximum(jnp.abs(den), jnp.exp(-m_row))
            yield
            b_last = b_col[L - 1:L, :]
            dl = b_last - b_col + i_col
            m_new = jnp.maximum(b_last + m, jnp.max(dl, axis=0, keepdims=True))
            decay = jnp.exp(b_last + m - m_new)
            wl = jnp.exp(dl - m_new)
            yield
            c_st[hd] = decay * cmat + _dot_tn(kb, (wl * vf).astype(BF16))
            n_st[hd] = decay * nvec + jnp.sum(wl * kf, axis=0, keepdims=True)
            m_st[hd] = jnp.broadcast_to(m_new, m_st.shape[1:])
            yield
            y = og_s[r0:r0 + L, cs] * hh
            y = y * lax.rsqrt(jnp.mean(y * y, axis=-1, keepdims=True) + EPS)
            o_ref[r0:r0 + L, cs] = (y * ng[:, cs]).astype(BF16)
            yield

    def projection(hd):
        cs = slice(hd * d, (hd + 1) * d)
        u = _dot(h, wmu_ref[:, cs])
        ubuf[pad:pad + TS, cs] = u
        yield
        mo = _dot(h, wmo_ref[:, cs])
        yield
        cw = cw_ref[:, cs]
        c = cb_ref[:, cs] + cw[0:1, :] * ubuf[pad - tail:pad - tail + TS, cs]
        for k in range(1, CONV_W):
            c = c + cw[k:k + 1, :] * ubuf[pad - tail + k:pad - tail + k + TS, cs]
            yield
        ubuf[0:pad, cs] = ubuf[TS:TS + pad, cs]
        cb16 = jax.nn.silu(c).astype(BF16)
        yield
        q = _dot(cb16, wq_ref[hd]).astype(BF16)
        yield
        k_ = (_dot(cb16, wk_ref[hd]) * (d ** -0.5)).astype(BF16)
        yield
        v = _dot(u.astype(BF16), wv_ref[hd]).astype(BF16)
        yield
        og = jax.nn.sigmoid(mo)
        yield
        return q, k_, v, og

    for hd in range(H):
        cs = slice(hd * d, (hd + 1) * d)
        q, k_, v, og = _alternate(recurrence(hd), projection(hd))
        q_s[:, cs] = q
        k_s[:, cs] = k_
        v_s[:, cs] = v
        og_s[:, cs] = og
    new_if_c = _dot(h, wif_ref[...]) + bifc_ref[...]
    new_if_r = _dot_nt(wift_ref[...], h) + bifr_ref[...]
    ifc_s[...] = new_if_c
    ifr_s[...] = new_if_r
    lfc_s[...] = jax.nn.log_sigmoid(new_if_c)
    lfr_s[...] = jax.nn.log_sigmoid(new_if_r)


def _mlstm_branch(x2d, B, norm_g, w_mu, w_mo, w_if, w_ift, b_if_c, b_if_r, conv_w, conv_b,
                  w_q, w_k, w_v, ml_norm_g):
    T, D = x2d.shape
    S = T // B
    C = w_mu.shape[1]
    H, d, _ = w_q.shape
    TS = min(256, S)
    assert S % TS == 0 and TS % ML_CHUNK == 0
    nS = S // TS
    G = B * nS
    return pl.pallas_call(
        functools.partial(_mlstm_kernel, TS=TS, nS=nS),
        out_shape=jax.ShapeDtypeStruct((T, C), BF16),
        grid=(G + 1,),
        in_specs=[pl.BlockSpec((TS, D), lambda g: (jnp.minimum(g, G - 1), 0)), _const_spec((1, D)),
                  _const_spec((D, C)), _const_spec((D, C)), _const_spec((D, LANES)),
                  _const_spec((SUBLANES, D)), _const_spec((1, LANES)), _const_spec((SUBLANES, 1)),
                  _const_spec((CONV_W, C)), _const_spec((1, C)), _const_spec(w_q.shape),
                  _const_spec(w_k.shape), _const_spec(w_v.shape), _const_spec((1, C))],
        out_specs=pl.BlockSpec((TS, C), lambda g: (jnp.maximum(g - 1, 0), 0)),
        scratch_shapes=[pltpu.VMEM((SUBLANES + TS, C), F32), pltpu.VMEM((TS, C), BF16),
                        pltpu.VMEM((TS, C), BF16), pltpu.VMEM((TS, C), BF16), pltpu.VMEM((TS, C), F32),
                        pltpu.VMEM((TS, LANES), F32), pltpu.VMEM((SUBLANES, TS), F32),
                        pltpu.VMEM((TS, LANES), F32), pltpu.VMEM((SUBLANES, TS), F32),
                        pltpu.VMEM((H, d, d), F32), pltpu.VMEM((H, 1, d), F32),
                        pltpu.VMEM((H, SUBLANES, LANES), F32)],
        compiler_params=_params(("arbitrary",), 48),
        name="mlstm_branch",
    )(x2d, norm_g, w_mu, w_mo, w_if, w_ift, b_if_c, b_if_r, conv_w, conv_b, w_q, w_k, w_v, ml_norm_g)


def _kv_kernel(mem_ref, g_ref, w_ref, o_ref):
    o_ref[...] = _dot(_rms(mem_ref[...], g_ref[...]).astype(BF16), w_ref[...]).astype(BF16)


def _mem_kv(mem, g, w_kv):
    B, M, D = mem.shape
    N = w_kv.shape[1]
    return pl.pallas_call(
        _kv_kernel,
        out_shape=jax.ShapeDtypeStruct((B, M, N), BF16),
        grid=(B,),
        in_specs=[pl.BlockSpec((None, M, D), lambda b: (b, 0, 0)), _const_spec((1, D)),
                  _const_spec((D, N))],
        out_specs=pl.BlockSpec((None, M, N), lambda b: (b, 0, 0)),
        compiler_params=_params(("arbitrary",), 32),
        name="mem_kv",
    )(mem, g, w_kv)


def _merge_kernel(x_ref, g_ref, wq_ref, wg_ref, bm_ref, yrg_ref, yml_ref, kv_ref, wrg_ref, wml_ref,
                  wxa_ref, wo_ref, o_ref):
    x = x_ref[...]
    D = x.shape[1]
    h = _rms(x, g_ref[...]).astype(BF16)
    q = _dot(h, wq_ref[...]).astype(BF16)
    dxa = q.shape[1]
    dh = dxa // XA_HEADS
    heads = []
    for hd in range(XA_HEADS):
        kh = kv_ref[:, hd * dh:(hd + 1) * dh]
        vh = kv_ref[:, dxa + hd * dh:dxa + (hd + 1) * dh]
        s = _dot_nt(q[:, hd * dh:(hd + 1) * dh], kh) * (dh ** -0.5)
        e = jnp.exp(s - jnp.max(s, axis=-1, keepdims=True))
        p = e / jnp.sum(e, axis=-1, keepdims=True)
        heads.append(_dot(p.astype(BF16), vh))
    y_xa = jnp.concatenate(heads, axis=1).astype(BF16)

    def gate(k):
        return jax.nn.sigmoid(_dot(h, wg_ref[:, k * D:(k + 1) * D]) + bm_ref[:, k * D:(k + 1) * D])

    merged = gate(0) * _dot(yrg_ref[...], wrg_ref[...])
    merged = merged + gate(1) * _dot(yml_ref[...], wml_ref[...])
    merged = merged + gate(2) * _dot(y_xa, wxa_ref[...])
    o_ref[...] = x + _dot(merged.astype(BF16), wo_ref[...])


def _merge(x2d, B, norm_g, w_q, w_g, b_merge, y_rg, y_ml, kv, w_br_rg, w_br_ml, w_br_xa, w_out):
    T, D = x2d.shape
    S = T // B
    C = y_ml.shape[1]
    M, N = kv.shape[1:]
    TM = min(512, S)
    assert S % TM == 0
    nS = S // TM
    row = lambda b, s: (b * nS + s, 0)
    one = pl.Buffered(1)
    cspec = lambda shape: pl.BlockSpec(shape, lambda *_: (0,) * len(shape), pipeline_mode=one)
    return pl.pallas_call(
        _merge_kernel,
        out_shape=jax.ShapeDtypeStruct((T, D), F32),
        grid=(B, nS),
        in_specs=[pl.BlockSpec((TM, D), row), cspec((1, D)), cspec(w_q.shape), cspec(w_g.shape),
                  cspec(b_merge.shape), pl.BlockSpec((TM, C), row),
                  pl.BlockSpec((TM, C), row), pl.BlockSpec((None, M, N), lambda b, s: (b, 0, 0)),
                  cspec(w_br_rg.shape), cspec(w_br_ml.shape), cspec(w_br_xa.shape),
                  cspec(w_out.shape)],
        out_specs=pl.BlockSpec((TM, D), row),
        compiler_params=_params(("arbitrary", "arbitrary"), 56),
        name="merge",
    )(x2d, norm_g, w_q, w_g, b_merge, y_rg, y_ml, kv, w_br_rg, w_br_ml, w_br_xa, w_out)


def _ffn_kernel(x_ref, g_ref, w1_ref, w3_ref, w2_ref, o_ref, hs, acc):
    f = pl.program_id(1)

    @pl.when(f == 0)
    def _():
        hs[...] = _rms(x_ref[...], g_ref[...]).astype(BF16)
        acc[...] = jnp.zeros_like(acc)

    hb = hs[...]
    mid = jax.nn.silu(_dot(hb, w1_ref[...])) * _dot(hb, w3_ref[...])
    acc[...] += _dot(mid.astype(BF16), w2_ref[...])

    @pl.when(f == pl.num_programs(1) - 1)
    def _():
        o_ref[...] = x_ref[...] + acc[...]


FFN_ROW_TILE = 512


def _ffn_hidden_tile(F):
    for tf in (1792, 1024, 512, 256):
        if F % tf == 0:
            return tf
    return F


def _ffn(x2d, norm_g, w1, w3, w2):
    T, D = x2d.shape
    F = w1.shape[1]
    TM = min(FFN_ROW_TILE, T)
    TF = _ffn_hidden_tile(F)
    assert T % TM == 0
    return pl.pallas_call(
        _ffn_kernel,
        out_shape=jax.ShapeDtypeStruct((T, D), F32),
        grid=(T // TM, F // TF),
        in_specs=[pl.BlockSpec((TM, D), lambda i, f: (i, 0)), _const_spec((1, D)),
                  pl.BlockSpec((D, TF), lambda i, f: (0, f)), pl.BlockSpec((D, TF), lambda i, f: (0, f)),
                  pl.BlockSpec((TF, D), lambda i, f: (f, 0))],
        out_specs=pl.BlockSpec((TM, D), lambda i, f: (i, 0)),
        scratch_shapes=[pltpu.VMEM((TM, D), BF16), pltpu.VMEM((TM, D), F32)],
        compiler_params=_params(("arbitrary", "arbitrary"), 56),
        name="ffn_dense",
    )(x2d, norm_g, w1, w3, w2)


ROUTE_TM = 512
RUN_ALIGN = 16


def _route_kernel(x_ref, g_ref, whi_ref, wlo_ref, rb_ref, idx_ref, gate_ref, len_ref, off_ref,
                  tot_ref, run_s, *, E):
    @pl.when(pl.program_id(0) == 0)
    def _():
        run_s[...] = jnp.zeros_like(run_s)

    h2 = _rms(x_ref[...], g_ref[...])
    TM = h2.shape[0]
    hi = h2.astype(BF16)
    lo = (h2 - hi.astype(F32)).astype(BF16)
    logits = _dot(hi, whi_ref[...]) + (_dot(lo, whi_ref[...]) + _dot(hi, wlo_ref[...]))
    lt = logits.T[0:E, :] + rb_ref[...]
    ie = lax.broadcasted_iota(jnp.int32, (E, TM), 0)
    m1 = jnp.max(lt, axis=0, keepdims=True)
    i1 = jnp.min(jnp.where(lt == m1, ie, E), axis=0, keepdims=True)
    l2 = jnp.where(ie == i1, -jnp.inf, lt)
    m2 = jnp.max(l2, axis=0, keepdims=True)
    i2 = jnp.min(jnp.where(l2 == m2, ie, E), axis=0, keepdims=True)
    ex = jnp.exp(m2 - m1)
    g1 = 1.0 / (1.0 + ex)
    g2 = ex / (1.0 + ex)
    oh1 = jnp.where(ie == i1, 1.0, 0.0)
    oh2 = jnp.where(ie == i2, 1.0, 0.0)
    oh = oh1 + oh2
    ri = lax.broadcasted_iota(jnp.int32, (TM, TM), 0)
    ci = lax.broadcasted_iota(jnp.int32, (TM, TM), 1)
    upper = jnp.where(ri < ci, 1.0, 0.0).astype(BF16)
    excl = _dot(oh.astype(BF16), upper)
    cnt = jnp.sum(oh, axis=1, keepdims=True).astype(jnp.int32)
    run_len = jnp.broadcast_to(((cnt + (RUN_ALIGN - 1)) // RUN_ALIGN) * RUN_ALIGN, (E, LANES))
    iec = lax.broadcasted_iota(jnp.int32, (E, LANES), 0)
    run_start = jnp.zeros((E, LANES), jnp.int32)
    for e in range(E - 1):
        run_start = run_start + jnp.where(iec > e, run_len[e:e + 1, :], 0)
    slot = run_start[:, 0:1].astype(F32) + excl
    s1 = jnp.sum(oh1 * slot, axis=0, keepdims=True).astype(jnp.int32)
    s2 = jnp.sum(oh2 * slot, axis=0, keepdims=True).astype(jnp.int32)
    len_ref[...] = run_len
    off_ref[...] = run_s[...]
    run_s[...] = run_s[...] + run_len
    tot_ref[...] = run_s[...]
    row = lax.broadcasted_iota(jnp.int32, (SUBLANES, TM), 0)
    idx_ref[...] = jnp.where(row == 0, i1, jnp.where(row == 1, i2, jnp.where(row == 2, s1,
                             jnp.where(row == 3, s2, 0))))
    gate_ref[...] = jnp.where(row == 0, g1, jnp.where(row == 1, g2, 0.0))


def _route(x2d, norm_g, w_hi, w_lo, rb, E):
    T, D = x2d.shape
    TM = min(ROUTE_TM, T)
    nT = T // TM
    assert T % TM == 0 and E == SUBLANES
    return pl.pallas_call(
        functools.partial(_route_kernel, E=E),
        out_shape=(jax.ShapeDtypeStruct((SUBLANES, T), jnp.int32),
                   jax.ShapeDtypeStruct((SUBLANES, T), F32),
                   jax.ShapeDtypeStruct((nT * E, LANES), jnp.int32),
                   jax.ShapeDtypeStruct((nT * E, LANES), jnp.int32),
                   jax.ShapeDtypeStruct((E, LANES), jnp.int32)),
        grid=(nT,),
        in_specs=[pl.BlockSpec((TM, D), lambda i: (i, 0)), _const_spec((1, D)),
                  _const_spec((D, LANES)), _const_spec((D, LANES)), _const_spec((E, 1))],
        out_specs=(pl.BlockSpec((SUBLANES, TM), lambda i: (0, i)),
                   pl.BlockSpec((SUBLANES, TM), lambda i: (0, i)),
                   pl.BlockSpec((E, LANES), lambda i: (i, 0)),
                   pl.BlockSpec((E, LANES), lambda i: (i, 0)),
                   _const_spec((E, LANES))),
        scratch_shapes=[pltpu.VMEM((E, LANES), jnp.int32)],
        compiler_params=_params(("arbitrary",), 32),
        name="moe_route",
    )(x2d, norm_g, w_hi, w_lo, rb)


def _run_dmas(len_ref, off_ref, i, E, max_len, tile_ref, sorted_ref, sem, to_sorted, wait):
    local = 0
    for e in range(E):
        n = len_ref[i * E + e]
        base = off_ref[i * E + e]
        done = 0
        sz = max_len
        while sz >= RUN_ALIGN:
            @pl.when((n & sz) != 0)
            def _(sz=sz, local=local, base=base, done=done):
                t_rows = tile_ref.at[pl.ds(pl.multiple_of(local + done, RUN_ALIGN), sz), :]
                s_rows = sorted_ref.at[pl.ds(pl.multiple_of(base + done, RUN_ALIGN), sz), :]
                cp = (pltpu.make_async_copy(t_rows, s_rows, sem) if to_sorted
                      else pltpu.make_async_copy(s_rows, t_rows, sem))
                if wait:
                    cp.wait()
                else:
                    cp.start()
            done = done + (n & sz)
            sz //= 2
        local = local + n


def _dispatch_kernel(len_ref, off_ref, x_ref, g_ref, ls_ref, init_ref, xs_ref, xs_t, sem, *, E):
    del init_ref
    i = pl.program_id(0)
    TM = x_ref.shape[0]
    LS = xs_t.shape[0]
    h2 = _rms(x_ref[...], g_ref[...]).astype(BF16)
    ls = ls_ref[...]
    j = lax.broadcasted_iota(jnp.int32, (LS, TM), 0)
    onehot = jnp.where(j == ls[2:3, :], 1.0, jnp.where(j == ls[3:4, :], 1.0, 0.0)).astype(BF16)
    xs_t[...] = _dot(onehot, h2).astype(BF16)
    _run_dmas(len_ref, off_ref, i, E, TM, xs_t, xs_ref, sem, True, False)
    _run_dmas(len_ref, off_ref, i, E, TM, xs_t, xs_ref, sem, True, True)


def _dispatch(run_len, run_off, idx, x2d, norm_g, P, E):
    T, D = x2d.shape
    TM = min(ROUTE_TM, T)
    LS = TOP_K * TM + E * RUN_ALIGN
    init = jnp.zeros((P, D), BF16)
    grid_spec = pltpu.PrefetchScalarGridSpec(
        num_scalar_prefetch=2,
        grid=(T // TM,),
        in_specs=[pl.BlockSpec((TM, D), lambda i, *_: (i, 0)),
                  pl.BlockSpec((1, D), lambda i, *_: (0, 0)),
                  pl.BlockSpec((SUBLANES, TM), lambda i, *_: (0, i)),
                  pl.BlockSpec(memory_space=pl.ANY)],
        out_specs=pl.BlockSpec(memory_space=pl.ANY),
        scratch_shapes=[pltpu.VMEM((LS, D), BF16), pltpu.SemaphoreType.DMA(())])
    return pl.pallas_call(
        functools.partial(_dispatch_kernel, E=E),
        out_shape=jax.ShapeDtypeStruct((P, D), BF16),
        grid_spec=grid_spec,
        input_output_aliases={5: 0},
        compiler_params=_params(("arbitrary",), 32),
        name="moe_dispatch",
    )(run_len, run_off, x2d, norm_g, idx, init)


def _group_ffn_kernel(te_ref, tn_ref, x_ref, w1_ref, w3_ref, w2_ref, o_ref, acc):
    i = pl.program_id(0)
    f = pl.program_id(1)

    @pl.when(f == 0)
    def _():
        acc[...] = jnp.zeros_like(acc)

    @pl.when(tn_ref[i] > 0)
    def _():
        hb = x_ref[...]
        mid = jax.nn.silu(_dot(hb, w1_ref[...])) * _dot(hb, w3_ref[...])
        acc[...] += _dot(mid.astype(BF16), w2_ref[...])

    @pl.when(f == pl.num_programs(1) - 1)
    def _():
        o_ref[...] = acc[...]


def _group_ffn(tile_e, tile_n, xs, w1, w3, w2, TMG):
    P, D = xs.shape
    F = w1.shape[2]
    TF = _ffn_hidden_tile(F)
    nF = F // TF
    fblk = lambda i, f, tn: jnp.where(tn[i] > 0, f, nF - 1)
    grid_spec = pltpu.PrefetchScalarGridSpec(
        num_scalar_prefetch=2,
        grid=(P // TMG, nF),
        in_specs=[pl.BlockSpec((TMG, D), lambda i, f, te, tn: (i, 0)),
                  pl.BlockSpec((None, D, TF), lambda i, f, te, tn: (te[i], 0, fblk(i, f, tn))),
                  pl.BlockSpec((None, D, TF), lambda i, f, te, tn: (te[i], 0, fblk(i, f, tn))),
                  pl.BlockSpec((None, TF, D), lambda i, f, te, tn: (te[i], fblk(i, f, tn), 0))],
        out_specs=pl.BlockSpec((TMG, D), lambda i, f, te, tn: (i, 0)),
        scratch_shapes=[pltpu.VMEM((TMG, D), F32)])
    return pl.pallas_call(
        _group_ffn_kernel,
        out_shape=jax.ShapeDtypeStruct((P, D), F32),
        grid_spec=grid_spec,
        compiler_params=_params(("arbitrary", "arbitrary"), 56),
        name="moe_group_ffn",
    )(tile_e, tile_n, xs, w1, w3, w2)


def _combine_kernel(len_ref, off_ref, x_ref, lsr_ref, lsc_ref, gt_ref, fg_ref, ys_ref, o_ref, yt, sem,
                    *, E, final_norm):
    i = pl.program_id(0)
    TM = x_ref.shape[0]
    _, LS, D = yt.shape
    slot = lax.rem(i, 2)

    def fetch(tile, s):
        yt[s, TOP_K * TM:LS, :] = jnp.zeros((LS - TOP_K * TM, D), F32)
        _run_dmas(len_ref, off_ref, tile, E, TM, yt.at[s], ys_ref, sem.at[s], False, False)

    @pl.when(i == 0)
    def _():
        fetch(0, 0)

    @pl.when(i + 1 < pl.num_programs(0))
    def _():
        fetch(i + 1, 1 - slot)

    lsr = lsr_ref[...]
    gt = gt_ref[...]
    js = lax.broadcasted_iota(jnp.int32, (LS, TM), 0)
    lane = lax.broadcasted_iota(jnp.int32, (TM, LANES), 1)
    gs = None
    for k in range(TOP_K):
        t1, t2, t3 = (t.astype(F32) for t in _split3(gt[:, k:k + 1]))
        terms = jnp.where(lane == 0, t1, jnp.where(lane == 1, t2, jnp.where(lane == 2, t3, 0.0)))
        pk = jnp.where(js == lsr[TOP_K + k:TOP_K + k + 1, :], 1.0, 0.0).astype(BF16)
        gk = jnp.sum(_dot(pk, terms.astype(BF16)), axis=1, keepdims=True)
        gs = gk if gs is None else gs + gk
    _run_dmas(len_ref, off_ref, i, E, TM, yt.at[slot], ys_ref, sem.at[slot], False, True)
    z = yt[slot] * gs
    zh = z.astype(BF16)
    zl = (z - zh.astype(F32)).astype(BF16)
    lsc = lsc_ref[...]
    jt = lax.broadcasted_iota(jnp.int32, (TM, LS), 1)
    pt = None
    for k in range(TOP_K):
        ok = jnp.where(jt == lsc[:, TOP_K + k:TOP_K + k + 1], 1.0, 0.0)
        pt = ok if pt is None else pt + ok
    pt = pt.astype(BF16)
    xo = x_ref[...] + (_dot(pt, zh) + _dot(pt, zl))
    o_ref[...] = _rms(xo, fg_ref[...]) if final_norm else xo


def _combine(run_len, run_off, x2d, idx, idx_t, gates_t, final_g, ys, E, final_norm):
    T, D = x2d.shape
    TM = min(ROUTE_TM, T)
    LS = TOP_K * TM + E * RUN_ALIGN
    grid_spec = pltpu.PrefetchScalarGridSpec(
        num_scalar_prefetch=2,
        grid=(T // TM,),
        in_specs=[pl.BlockSpec((TM, D), lambda i, *_: (i, 0)),
                  pl.BlockSpec((SUBLANES, TM), lambda i, *_: (0, i)),
                  pl.BlockSpec((TM, SUBLANES), lambda i, *_: (i, 0)),
                  pl.BlockSpec((TM, SUBLANES), lambda i, *_: (i, 0)),
                  pl.BlockSpec((1, D), lambda i, *_: (0, 0)),
                  pl.BlockSpec(memory_space=pl.ANY)],
        out_specs=pl.BlockSpec((TM, D), lambda i, *_: (i, 0)),
        scratch_shapes=[pltpu.VMEM((2, LS, D), F32), pltpu.SemaphoreType.DMA((2,))])
    return pl.pallas_call(
        functools.partial(_combine_kernel, E=E, final_norm=final_norm),
        out_shape=jax.ShapeDtypeStruct((T, D), F32),
        grid_spec=grid_spec,
        compiler_params=_params(("arbitrary",), 48),
        name="moe_combine",
    )(run_len, run_off, x2d, idx, idx_t, gates_t, final_g, ys)


def _moe(x2d, norm_g, router_w, router_b, w1, w3, w2, final_g, final_norm):
    T, D = x2d.shape
    E = router_w.shape[1]
    TMG = 512
    nT = T // min(ROUTE_TM, T)
    rw = jnp.pad(router_w, ((0, 0), (0, LANES - E)))
    rw_hi = rw.astype(BF16)
    rw_lo = (rw - rw_hi.astype(F32)).astype(BF16)
    idx, gates, run_len, run_off, tot = _route(x2d, norm_g, rw_hi, rw_lo, router_b.reshape(E, 1), E)
    tot = tot[:, 0]
    padded = ((tot + TMG - 1) // TMG) * TMG
    pend = jnp.cumsum(padded)
    pstart = pend - padded
    run_len = run_len[:, 0]
    run_off = (run_off[:, 0].reshape(nT, E) + pstart[None, :]).reshape(nT * E)
    P = -(-(T * TOP_K + nT * E * (RUN_ALIGN - 1) + E * (TMG - 1)) // TMG) * TMG
    tile_start = jnp.arange(P // TMG, dtype=jnp.int32) * TMG
    tile_e = jnp.minimum(jnp.sum(tile_start[:, None] >= pend[None, :], axis=1), E - 1).astype(jnp.int32)
    sel = tile_e[:, None] == jnp.arange(E, dtype=jnp.int32)[None, :]
    tile_end = jnp.sum(jnp.where(sel, (pstart + tot)[None, :], 0), axis=1)
    tile_n = jnp.clip(tile_end - tile_start, 0, TMG).astype(jnp.int32)
    xs = _dispatch(run_len, run_off, idx, x2d, norm_g, P, E)
    ys = _group_ffn(tile_e, tile_n, xs, w1, w3, w2, TMG)
    return _combine(run_len, run_off, x2d, idx, idx.T, gates.T, final_g, ys, E, final_norm)


def _final_norm_kernel(x_ref, g_ref, o_ref):
    o_ref[...] = _rms(x_ref[...], g_ref[...])


def _final_norm(x2d, g):
    T, D = x2d.shape
    TM = min(1024, T)
    return pl.pallas_call(
        _final_norm_kernel,
        out_shape=jax.ShapeDtypeStruct((T, D), F32),
        grid=(T // TM,),
        in_specs=[pl.BlockSpec((TM, D), lambda i: (i, 0)), _const_spec((1, D))],
        out_specs=pl.BlockSpec((TM, D), lambda i: (i, 0)),
        compiler_params=_params(("arbitrary",), 32),
        name="final_norm",
    )(x2d, g)


def kernel(x, mem, norm_mix_g, w_in, conv_rg_w, conv_rg_b, rg_w_a, rg_b_a, rg_w_x, rg_b_x, rg_lambda,
           conv_ml_w, conv_ml_b, ml_w_q, ml_w_k, ml_w_v, ml_b_i, ml_b_f, ml_norm_g, mem_norm_g, w_kv,
           w_br_rg, w_br_ml, w_br_xa, b_merge, w_out, norm_ffn_g, ffn_w1, ffn_w3, ffn_w2, router_w,
           router_b, moe_w1, moe_w3, moe_w2, final_norm_g):
    B, S, D = x.shape
    depth = w_in.shape[0]
    d_rg = conv_rg_w.shape[2]
    d_ml = conv_ml_w.shape[2]
    H = ml_w_q.shape[1]
    d_xa = w_kv.shape[2] // 2
    o_ax, o_ay = 0, d_rg
    o_mu, o_mo = 2 * d_rg, 2 * d_rg + d_ml
    o_mi = 2 * d_rg + 2 * d_ml
    o_mf = o_mi + H
    o_q = o_mf + H
    o_g = o_q + d_xa
    assert w_in.shape[2] == o_g + N_BRANCH * D and 2 * H <= SUBLANES

    bf = lambda a: a.astype(BF16)
    row = lambda a: a.reshape(1, -1)
    x2d = x.reshape(B * S, D)
    fg = row(final_norm_g)
    for l in range(depth):
        wl = w_in[l]
        w_if = jnp.pad(wl[:, o_mi:o_mi + 2 * H], ((0, 0), (0, LANES - 2 * H)))
        w_ift = jnp.pad(wl[:, o_mi:o_mi + 2 * H].T, ((0, SUBLANES - 2 * H), (0, 0)))
        b_if = jnp.concatenate([ml_b_i[l], ml_b_f[l]])
        y_ml = _mlstm_branch(x2d, B, row(norm_mix_g[l]), bf(wl[:, o_mu:o_mu + d_ml]),
                                   bf(wl[:, o_mo:o_mo + d_ml]), bf(w_if), bf(w_ift),
                                   jnp.pad(b_if, (0, LANES - 2 * H)).reshape(1, LANES),
                                   jnp.pad(b_if, (0, SUBLANES - 2 * H)).reshape(SUBLANES, 1),
                                   conv_ml_w[l], row(conv_ml_b[l]), bf(ml_w_q[l]), bf(ml_w_k[l]),
                                   bf(ml_w_v[l]), row(ml_norm_g[l]))
        y_rg = _rg_branch(x2d.reshape(B, S, D), row(norm_mix_g[l]), bf(wl[:, o_ax:o_ax + d_rg]),
                          bf(wl[:, o_ay:o_ay + d_rg]), conv_rg_w[l], row(conv_rg_b[l]), bf(rg_w_a[l]),
                          row(rg_b_a[l]), bf(rg_w_x[l]), row(rg_b_x[l]), row(rg_lambda[l]))
        kv = _mem_kv(mem, row(mem_norm_g[l]), bf(w_kv[l]))
        x2d = _merge(x2d, B, row(norm_mix_g[l]), bf(wl[:, o_q:o_q + d_xa]), bf(wl[:, o_g:]),
                     row(b_merge[l]), y_rg.reshape(B * S, d_rg), y_ml, kv, bf(w_br_rg[l]),
                     bf(w_br_ml[l]), bf(w_br_xa[l]), bf(w_out[l]))
        j = l // 2
        if l % 2 == 0:
            x2d = _ffn(x2d, row(norm_ffn_g[l]), bf(ffn_w1[j]), bf(ffn_w3[j]), bf(ffn_w2[j]))
        else:
            x2d = _moe(x2d, row(norm_ffn_g[l]), router_w[j], router_b[j], bf(moe_w1[j]),
                       bf(moe_w3[j]), bf(moe_w2[j]), fg, l == depth - 1)
    if depth % 2 == 1:
        x2d = _final_norm(x2d, fg)
    return x2d.reshape(B, S, D)
```

```python
import functools

import jax
import jax.numpy as jnp
from jax import lax
from jax.experimental import pallas as pl
from jax.experimental.pallas import tpu as pltpu

EPS = 1e-6
RG_C = 8.0
CONV_W = 4
ML_CHUNK = 128
XA_HEADS = 4
TOP_K = 2
N_BRANCH = 3

V7X_VMEM_BYTES = 64 * 1024 * 1024
LANES = 128
SUBLANES = 8

F32 = jnp.float32
BF16 = jnp.bfloat16


def _params(semantics, vmem_mib):
    assert vmem_mib * 1024 * 1024 <= V7X_VMEM_BYTES
    return pltpu.CompilerParams(dimension_semantics=semantics,
                                vmem_limit_bytes=vmem_mib * 1024 * 1024)


def _const_spec(shape):
    nd = len(shape)
    return pl.BlockSpec(shape, lambda *_: (0,) * nd)


def _rms(x, g):
    ms = jnp.mean(x * x, axis=-1, keepdims=True)
    return x * lax.rsqrt(ms + EPS) * g


def _dot(a, b):
    return jnp.dot(a, b, preferred_element_type=F32)


def _dot_nt(a, b):
    return lax.dot_general(a, b, (((1,), (1,)), ((), ())), preferred_element_type=F32)


def _dot_tn(a, b):
    return lax.dot_general(a, b, (((0,), (0,)), ((), ())), preferred_element_type=F32)


def _split3(x):
    h1 = x.astype(BF16)
    r1 = x - h1.astype(F32)
    h2 = r1.astype(BF16)
    h3 = (r1 - h2.astype(F32)).astype(BF16)
    return h1, h2, h3


def _rg_kernel(x_ref, g_ref, wax_ref, way_ref, cw_ref, cb_ref, wa_ref, ba_ref, wx_ref, bx_ref,
               lam_ref, o_ref, axbuf, a_s, b_s, h_s, carry, *, B, TT):
    R = TT * B
    halo = (CONV_W - 1) * B

    @pl.when(pl.program_id(0) == 0)
    def _():
        axbuf[0:halo, :] = jnp.zeros((halo, axbuf.shape[1]), F32)
        carry[...] = jnp.zeros_like(carry)

    x = pltpu.einshape("btd->tbd", x_ref[...]).reshape(R, x_ref.shape[2])
    h = _rms(x, g_ref[...]).astype(BF16)
    G, bi, _ = wa_ref.shape
    rate = -RG_C * jax.nn.softplus(-lam_ref[...])
    cols = [slice(g * bi, (g + 1) * bi) for g in range(G)]
    axbuf[halo:halo + R, cols[0]] = _dot(h, wax_ref[:, cols[0]])
    for g in range(G):
        cs = cols[g]
        if g + 1 < G:
            axbuf[halo:halo + R, cols[g + 1]] = _dot(h, wax_ref[:, cols[g + 1]])
        ay = _dot(h, way_ref[:, cs])
        cw = cw_ref[:, cs]
        xc = cb_ref[:, cs] + cw[0:1, :] * axbuf[0:R, cs]
        for k in range(1, CONV_W):
            xc = xc + cw[k:k + 1, :] * axbuf[k * B:k * B + R, cs]
        axbuf[0:halo, cs] = axbuf[R:R + halo, cs]
        xcb = xc.astype(BF16)
        r = jax.nn.sigmoid(_dot(xcb, wa_ref[g]) + ba_ref[:, cs])
        ig = jax.nn.sigmoid(_dot(xcb, wx_ref[g]) + bx_ref[:, cs])
        log_a = r * rate[:, cs]
        a = jnp.exp(log_a)
        a_s[:, cs] = a
        b_s[:, cs] = jnp.sqrt(-jnp.tanh(log_a) * (a * a + 1.0)) * (ig * xc)

        def step(t, hc, cs=cs):
            off = pl.multiple_of(t * B, B)
            hn = a_s[pl.ds(off, B), cs] * hc + b_s[pl.ds(off, B), cs]
            h_s[pl.ds(off, B), cs] = hn
            return hn

        carry[:, cs] = lax.fori_loop(0, TT, step, carry[:, cs], unroll=True)
        y = (h_s[:, cs] * jax.nn.gelu(ay)).reshape(TT, B, bi)
        o_ref[:, :, cs] = pltpu.einshape("tbc->btc", y).astype(BF16)


def _rg_branch(x3d, norm_g, w_ax, w_ay, conv_w, conv_b, w_a, b_a, w_x, b_x, lam):
    B, S, D = x3d.shape
    C = w_ax.shape[1]
    TT = min(64, S)
    assert S % TT == 0 and B % SUBLANES == 0
    R = TT * B
    halo = (CONV_W - 1) * B
    tile = lambda i: (0, i, 0)
    return pl.pallas_call(
        functools.partial(_rg_kernel, B=B, TT=TT),
        out_shape=jax.ShapeDtypeStruct((B, S, C), BF16),
        grid=(S // TT,),
        in_specs=[pl.BlockSpec((B, TT, D), tile), _const_spec((1, D)), _const_spec((D, C)),
                  _const_spec((D, C)), _const_spec((CONV_W, C)), _const_spec((1, C)),
                  _const_spec(w_a.shape), _const_spec((1, C)), _const_spec(w_x.shape),
                  _const_spec((1, C)), _const_spec((1, C))],
        out_specs=pl.BlockSpec((B, TT, C), tile),
        scratch_shapes=[pltpu.VMEM((halo + R, C), F32), pltpu.VMEM((R, C), F32),
                        pltpu.VMEM((R, C), F32), pltpu.VMEM((R, C), F32), pltpu.VMEM((B, C), F32)],
        compiler_params=_params(("arbitrary",), 48),
        name="rg_branch",
    )(x3d, norm_g, w_ax, w_ay, conv_w, conv_b, w_a, b_a, w_x, b_x, lam)


def _alternate(first, second):
    result = None
    live = [True, True]
    while any(live):
        for n, gen in enumerate((first, second)):
            if live[n]:
                try:
                    next(gen)
                except StopIteration as stop:
                    live[n] = False
                    if n == 1:
                        result = stop.value
    return result


def _mlstm_kernel(x_ref, g_ref, wmu_ref, wmo_ref, wif_ref, wift_ref, bifc_ref, bifr_ref, cw_ref,
                  cb_ref, wq_ref, wk_ref, wv_ref, ng_ref, o_ref, ubuf, q_s, k_s, v_s, og_s, ifc_s, ifr_s,
                  lfc_s, lfr_s, c_st, n_st, m_st, *, TS, nS):
    H, d, _ = wq_ref.shape
    L = ML_CHUNK
    pad = SUBLANES
    tail = CONV_W - 1
    g = pl.program_id(0)

    @pl.when(g == 0)
    def _():
        for r in (q_s, k_s, v_s, og_s, ifc_s, ifr_s, lfc_s, lfr_s):
            r[...] = jnp.zeros_like(r)

    @pl.when(lax.rem(g, nS) == 0)
    def _():
        ubuf[0:pad, :] = jnp.zeros((pad, ubuf.shape[1]), F32)

    @pl.when((g == 0) | (lax.rem(g + (nS - 1), nS) == 0))
    def _():
        c_st[...] = jnp.zeros_like(c_st)
        n_st[...] = jnp.zeros_like(n_st)
        m_st[...] = jnp.zeros_like(m_st)

    if_c = ifc_s[...]
    if_r = ifr_s[...]
    lf_c = lfc_s[...]
    lf_r = lfr_s[...]
    ri = lax.broadcasted_iota(jnp.int32, (L, L), 0)
    ci = lax.broadcasted_iota(jnp.int32, (L, L), 1)
    causal = ci <= ri
    tri_l = jnp.where(causal, 1.0, 0.0).astype(BF16)
    tri_u = jnp.where(ri <= ci, 1.0, 0.0).astype(BF16)
    ng = ng_ref[...]
    nck = TS // L
    bcs = [sum(_dot(tri_l, p) for p in _split3(lf_c[ck * L:(ck + 1) * L, :])) for ck in range(nck)]
    brs = [sum(_dot(p, tri_u) for p in _split3(lf_r[:, ck * L:(ck + 1) * L])) for ck in range(nck)]
    h = _rms(x_ref[...], g_ref[...]).astype(BF16)

    def recurrence(hd):
        cs = slice(hd * d, (hd + 1) * d)
        for ck in range(nck):
            r0 = ck * L
            bc, br = bcs[ck], brs[ck]
            b_col = bc[:, H + hd:H + hd + 1]
            b_row = br[H + hd:H + hd + 1, :]
            i_col = if_c[r0:r0 + L, hd:hd + 1]
            i_row = if_r[hd:hd + 1, r0:r0 + L]
            m = m_st[hd][0:1, 0:1]
            qb = q_s[r0:r0 + L, cs]
            kb = k_s[r0:r0 + L, cs]
            vb = v_s[r0:r0 + L, cs]
            gg = b_col + m
            dm = jnp.where(causal, b_col - b_row + i_row, -jnp.inf)
            yield
            m_row = jnp.maximum(gg, jnp.max(dm, axis=-1, keepdims=True))
            s_qk = _dot_nt(qb, kb)
            yield
            w = jnp.exp(dm - m_row) * s_qk
            inter = jnp.exp(gg - m_row)
            cmat = c_st[hd]
            nvec = n_st[hd]
            yield
            num = inter * _dot(qb, cmat.astype(BF16)) + _dot(w.astype(BF16), vb)
            yield
            qf, kf, vf = qb.astype(F32), kb.astype(F32), vb.astype(F32)
            den = inter * jnp.sum(qf * nvec, axis=-1, keepdims=True) + jnp.sum(w, axis=-1, keepdims=True)
            hh = num / jnp.maximum(jnp.abs(den), jnp.exp(-m_row))
            yield
            b_last = b_col[L - 1:L, :]
            dl = b_last - b_col + i_col
            m_new = jnp.maximum(b_last + m, jnp.max(dl, axis=0, keepdims=True))
            decay = jnp.exp(b_last + m - m_new)
            wl = jnp.exp(dl - m_new)
            yield
            c_st[hd] = decay * cmat + _dot_tn(kb, (wl * vf).astype(BF16))
            n_st[hd] = decay * nvec + jnp.sum(wl * kf, axis=0, keepdims=True)
            m_st[hd] = jnp.broadcast_to(m_new, m_st.shape[1:])
            yield
            y = og_s[r0:r0 + L, cs] * hh
            y = y * lax.rsqrt(jnp.mean(y * y, axis=-1, keepdims=True) + EPS)
            o_ref[r0:r0 + L, cs] = (y * ng[:, cs]).astype(BF16)
            yield

    def projection(hd):
        cs = slice(hd * d, (hd + 1) * d)
        u = _dot(h, wmu_ref[:, cs])
        ubuf[pad:pad + TS, cs] = u
        yield
        mo = _dot(h, wmo_ref[:, cs])
        yield
        cw = cw_ref[:, cs]
        c = cb_ref[:, cs] + cw[0:1, :] * ubuf[pad - tail:pad - tail + TS, cs]
        for k in range(1, CONV_W):
            c = c + cw[k:k + 1, :] * ubuf[pad - tail + k:pad - tail + k + TS, cs]
            yield
        ubuf[0:pad, cs] = ubuf[TS:TS + pad, cs]
        cb16 = jax.nn.silu(c).astype(BF16)
        yield
        q = _dot(cb16, wq_ref[hd]).astype(BF16)
        yield
        k_ = (_dot(cb16, wk_ref[hd]) * (d ** -0.5)).astype(BF16)
        yield
        v = _dot(u.astype(BF16), wv_ref[hd]).astype(BF16)
        yield
        og = jax.nn.sigmoid(mo)
        yield
        return q, k_, v, og

    for hd in range(H):
        cs = slice(hd * d, (hd + 1) * d)
        q, k_, v, og = _alternate(recurrence(hd), projection(hd))
        q_s[:, cs] = q
        k_s[:, cs] = k_
        v_s[:, cs] = v
        og_s[:, cs] = og
    new_if_c = _dot(h, wif_ref[...]) + bifc_ref[...]
    new_if_r = _dot_nt(wift_ref[...], h) + bifr_ref[...]
    ifc_s[...] = new_if_c
    ifr_s[...] = new_if_r
    lfc_s[...] = jax.nn.log_sigmoid(new_if_c)
    lfr_s[...] = jax.nn.log_sigmoid(new_if_r)


def _mlstm_branch(x2d, B, norm_g, w_mu, w_mo, w_if, w_ift, b_if_c, b_if_r, conv_w, conv_b,
                  w_q, w_k, w_v, ml_norm_g):
    T, D = x2d.shape
    S = T // B
    C = w_mu.shape[1]
    H, d, _ = w_q.shape
    TS = min(256, S)
    assert S % TS == 0 and TS % ML_CHUNK == 0
    nS = S // TS
    G = B * nS
    return pl.pallas_call(
        functools.partial(_mlstm_kernel, TS=TS, nS=nS),
        out_shape=jax.ShapeDtypeStruct((T, C), BF16),
        grid=(G + 1,),
        in_specs=[pl.BlockSpec((TS, D), lambda g: (jnp.minimum(g, G - 1), 0)), _const_spec((1, D)),
                  _const_spec((D, C)), _const_spec((D, C)), _const_spec((D, LANES)),
                  _const_spec((SUBLANES, D)), _const_spec((1, LANES)), _const_spec((SUBLANES, 1)),
                  _const_spec((CONV_W, C)), _const_spec((1, C)), _const_spec(w_q.shape),
                  _const_spec(w_k.shape), _const_spec(w_v.shape), _const_spec((1, C))],
        out_specs=pl.BlockSpec((TS, C), lambda g: (jnp.maximum(g - 1, 0), 0)),
        scratch_shapes=[pltpu.VMEM((SUBLANES + TS, C), F32), pltpu.VMEM((TS, C), BF16),
                        pltpu.VMEM((TS, C), BF16), pltpu.VMEM((TS, C), BF16), pltpu.VMEM((TS, C), F32),
                        pltpu.VMEM((TS, LANES), F32), pltpu.VMEM((SUBLANES, TS), F32),
                        pltpu.VMEM((TS, LANES), F32), pltpu.VMEM((SUBLANES, TS), F32),
                        pltpu.VMEM((H, d, d), F32), pltpu.VMEM((H, 1, d), F32),
                        pltpu.VMEM((H, SUBLANES, LANES), F32)],
        compiler_params=_params(("arbitrary",), 48),
        name="mlstm_branch",
    )(x2d, norm_g, w_mu, w_mo, w_if, w_ift, b_if_c, b_if_r, conv_w, conv_b, w_q, w_k, w_v, ml_norm_g)


def _kv_kernel(mem_ref, g_ref, w_ref, o_ref):
    o_ref[...] = _dot(_rms(mem_ref[...], g_ref[...]).astype(BF16), w_ref[...]).astype(BF16)


def _mem_kv(mem, g, w_kv):
    B, M, D = mem.shape
    N = w_kv.shape[1]
    return pl.pallas_call(
        _kv_kernel,
        out_shape=jax.ShapeDtypeStruct((B, M, N), BF16),
        grid=(B,),
        in_specs=[pl.BlockSpec((None, M, D), lambda b: (b, 0, 0)), _const_spec((1, D)),
                  _const_spec((D, N))],
        out_specs=pl.BlockSpec((None, M, N), lambda b: (b, 0, 0)),
        compiler_params=_params(("arbitrary",), 32),
        name="mem_kv",
    )(mem, g, w_kv)


def _merge_kernel(x_ref, g_ref, wq_ref, wg_ref, bm_ref, yrg_ref, yml_ref, kv_ref, wrg_ref, wml_ref,
                  wxa_ref, wo_ref, o_ref):
    x = x_ref[...]
    D = x.shape[1]
    h = _rms(x, g_ref[...]).astype(BF16)
    q = _dot(h, wq_ref[...]).astype(BF16)
    dxa = q.shape[1]
    dh = dxa // XA_HEADS
    heads = []
    for hd in range(XA_HEADS):
        kh = kv_ref[:, hd * dh:(hd + 1) * dh]
        vh = kv_ref[:, dxa + hd * dh:dxa + (hd + 1) * dh]
        s = _dot_nt(q[:, hd * dh:(hd + 1) * dh], kh) * (dh ** -0.5)
        e = jnp.exp(s - jnp.max(s, axis=-1, keepdims=True))
        p = e / jnp.sum(e, axis=-1, keepdims=True)
        heads.append(_dot(p.astype(BF16), vh))
    y_xa = jnp.concatenate(heads, axis=1).astype(BF16)

    def gate(k):
        return jax.nn.sigmoid(_dot(h, wg_ref[:, k * D:(k + 1) * D]) + bm_ref[:, k * D:(k + 1) * D])

    merged = gate(0) * _dot(yrg_ref[...], wrg_ref[...])
    merged = merged + gate(1) * _dot(yml_ref[...], wml_ref[...])
    merged = merged + gate(2) * _dot(y_xa, wxa_ref[...])
    o_ref[...] = x + _dot(merged.astype(BF16), wo_ref[...])


def _merge(x2d, B, norm_g, w_q, w_g, b_merge, y_rg, y_ml, kv, w_br_rg, w_br_ml, w_br_xa, w_out):
    T, D = x2d.shape
    S = T // B
    C = y_ml.shape[1]
    M, N = kv.shape[1:]
    TM = min(512, S)
    assert S % TM == 0
    nS = S // TM
    row = lambda b, s: (b * nS + s, 0)
    one = pl.Buffered(1)
    cspec = lambda shape: pl.BlockSpec(shape, lambda *_: (0,) * len(shape), pipeline_mode=one)
    return pl.pallas_call(
        _merge_kernel,
        out_shape=jax.ShapeDtypeStruct((T, D), F32),
        grid=(B, nS),
        in_specs=[pl.BlockSpec((TM, D), row), cspec((1, D)), cspec(w_q.shape), cspec(w_g.shape),
                  cspec(b_merge.shape), pl.BlockSpec((TM, C), row),
                  pl.BlockSpec((TM, C), row), pl.BlockSpec((None, M, N), lambda b, s: (b, 0, 0)),
                  cspec(w_br_rg.shape), cspec(w_br_ml.shape), cspec(w_br_xa.shape),
                  cspec(w_out.shape)],
        out_specs=pl.BlockSpec((TM, D), row),
        compiler_params=_params(("arbitrary", "arbitrary"), 56),
        name="merge",
    )(x2d, norm_g, w_q, w_g, b_merge, y_rg, y_ml, kv, w_br_rg, w_br_ml, w_br_xa, w_out)


def _ffn_kernel(x_ref, g_ref, w1_ref, w3_ref, w2_ref, *rest, n_cast):
    cast_in, o_ref, cast_out = rest[:n_cast], rest[n_cast], rest[n_cast + 1:2 * n_cast + 1]
    hs, acc = rest[2 * n_cast + 1:]
    f = pl.program_id(1)
    for src, dst in zip(cast_in, cast_out):
        dst[...] = src[...].astype(BF16)

    @pl.when(f == 0)
    def _():
        hs[...] = _rms(x_ref[...], g_ref[...]).astype(BF16)
        acc[...] = jnp.zeros_like(acc)

    hb = hs[...]
    mid = jax.nn.silu(_dot(hb, w1_ref[...])) * _dot(hb, w3_ref[...])
    acc[...] += _dot(mid.astype(BF16), w2_ref[...])

    @pl.when(f == pl.num_programs(1) - 1)
    def _():
        o_ref[...] = x_ref[...] + acc[...]


FFN_ROW_TILE = 512


def _ffn_hidden_tile(F):
    for tf in (1792, 1024, 512, 256):
        if F % tf == 0:
            return tf
    return F


def _ffn_can_cast(x2d, w1, t):
    steps = (x2d.shape[0] // min(FFN_ROW_TILE, x2d.shape[0])) * (w1.shape[1] // _ffn_hidden_tile(w1.shape[1]))
    rows = t.size // t.shape[-1]
    return rows % steps == 0 and (rows // steps) % (2 * SUBLANES) == 0


def _ffn(x2d, norm_g, w1, w3, w2, cast=()):
    T, D = x2d.shape
    F = w1.shape[1]
    TM = min(FFN_ROW_TILE, T)
    TF = _ffn_hidden_tile(F)
    assert T % TM == 0
    nF = F // TF
    steps = (T // TM) * nF
    cast2d = [t.reshape(-1, t.shape[-1]) for t in cast]
    cast_specs = [pl.BlockSpec((t.shape[0] // steps, t.shape[1]), lambda i, f: (i * nF + f, 0))
                  for t in cast2d]
    out = pl.pallas_call(
        functools.partial(_ffn_kernel, n_cast=len(cast)),
        out_shape=(jax.ShapeDtypeStruct((T, D), F32),
                   *[jax.ShapeDtypeStruct(t.shape, BF16) for t in cast2d]),
        grid=(T // TM, nF),
        in_specs=[pl.BlockSpec((TM, D), lambda i, f: (i, 0)), _const_spec((1, D)),
                  pl.BlockSpec((D, TF), lambda i, f: (0, f)), pl.BlockSpec((D, TF), lambda i, f: (0, f)),
                  pl.BlockSpec((TF, D), lambda i, f: (f, 0)), *cast_specs],
        out_specs=(pl.BlockSpec((TM, D), lambda i, f: (i, 0)), *cast_specs),
        scratch_shapes=[pltpu.VMEM((TM, D), BF16), pltpu.VMEM((TM, D), F32)],
        compiler_params=_params(("arbitrary", "arbitrary"), 56),
        name="ffn_dense",
    )(x2d, norm_g, w1, w3, w2, *cast2d)
    return out[0], [o.reshape(t.shape) for o, t in zip(out[1:], cast)]


ROUTE_TM = 512
RUN_ALIGN = 16


def _route_kernel(x_ref, g_ref, whi_ref, wlo_ref, rb_ref, idx_ref, gate_ref, len_ref, off_ref,
                  tot_ref, run_s, *, E):
    @pl.when(pl.program_id(0) == 0)
    def _():
        run_s[...] = jnp.zeros_like(run_s)

    h2 = _rms(x_ref[...], g_ref[...])
    TM = h2.shape[0]
    hi = h2.astype(BF16)
    lo = (h2 - hi.astype(F32)).astype(BF16)
    logits = _dot(hi, whi_ref[...]) + (_dot(lo, whi_ref[...]) + _dot(hi, wlo_ref[...]))
    lt = logits.T[0:E, :] + rb_ref[...]
    ie = lax.broadcasted_iota(jnp.int32, (E, TM), 0)
    m1 = jnp.max(lt, axis=0, keepdims=True)
    i1 = jnp.min(jnp.where(lt == m1, ie, E), axis=0, keepdims=True)
    l2 = jnp.where(ie == i1, -jnp.inf, lt)
    m2 = jnp.max(l2, axis=0, keepdims=True)
    i2 = jnp.min(jnp.where(l2 == m2, ie, E), axis=0, keepdims=True)
    ex = jnp.exp(m2 - m1)
    g1 = 1.0 / (1.0 + ex)
    g2 = ex / (1.0 + ex)
    oh1 = jnp.where(ie == i1, 1.0, 0.0)
    oh2 = jnp.where(ie == i2, 1.0, 0.0)
    oh = oh1 + oh2
    ri = lax.broadcasted_iota(jnp.int32, (TM, TM), 0)
    ci = lax.broadcasted_iota(jnp.int32, (TM, TM), 1)
    upper = jnp.where(ri < ci, 1.0, 0.0).astype(BF16)
    excl = _dot(oh.astype(BF16), upper)
    cnt = jnp.sum(oh, axis=1, keepdims=True).astype(jnp.int32)
    run_len = jnp.broadcast_to(((cnt + (RUN_ALIGN - 1)) // RUN_ALIGN) * RUN_ALIGN, (E, LANES))
    iec = lax.broadcasted_iota(jnp.int32, (E, LANES), 0)
    run_start = jnp.zeros((E, LANES), jnp.int32)
    for e in range(E - 1):
        run_start = run_start + jnp.where(iec > e, run_len[e:e + 1, :], 0)
    slot = run_start[:, 0:1].astype(F32) + excl
    s1 = jnp.sum(oh1 * slot, axis=0, keepdims=True).astype(jnp.int32)
    s2 = jnp.sum(oh2 * slot, axis=0, keepdims=True).astype(jnp.int32)
    len_ref[...] = run_len
    off_ref[...] = run_s[...]
    run_s[...] = run_s[...] + run_len
    tot_ref[...] = run_s[...]
    row = lax.broadcasted_iota(jnp.int32, (SUBLANES, TM), 0)
    idx_ref[...] = jnp.where(row == 0, i1, jnp.where(row == 1, i2, jnp.where(row == 2, s1,
                             jnp.where(row == 3, s2, 0))))
    gate_ref[...] = jnp.where(row == 0, g1, jnp.where(row == 1, g2, 0.0))


def _route(x2d, norm_g, w_hi, w_lo, rb, E):
    T, D = x2d.shape
    TM = min(ROUTE_TM, T)
    nT = T // TM
    assert T % TM == 0 and E == SUBLANES
    return pl.pallas_call(
        functools.partial(_route_kernel, E=E),
        out_shape=(jax.ShapeDtypeStruct((SUBLANES, T), jnp.int32),
                   jax.ShapeDtypeStruct((SUBLANES, T), F32),
                   jax.ShapeDtypeStruct((nT * E, LANES), jnp.int32),
                   jax.ShapeDtypeStruct((nT * E, LANES), jnp.int32),
                   jax.ShapeDtypeStruct((E, LANES), jnp.int32)),
        grid=(nT,),
        in_specs=[pl.BlockSpec((TM, D), lambda i: (i, 0)), _const_spec((1, D)),
                  _const_spec((D, LANES)), _const_spec((D, LANES)), _const_spec((E, 1))],
        out_specs=(pl.BlockSpec((SUBLANES, TM), lambda i: (0, i)),
                   pl.BlockSpec((SUBLANES, TM), lambda i: (0, i)),
                   pl.BlockSpec((E, LANES), lambda i: (i, 0)),
                   pl.BlockSpec((E, LANES), lambda i: (i, 0)),
                   _const_spec((E, LANES))),
        scratch_shapes=[pltpu.VMEM((E, LANES), jnp.int32)],
        compiler_params=_params(("arbitrary",), 32),
        name="moe_route",
    )(x2d, norm_g, w_hi, w_lo, rb)


def _run_dmas(len_ref, off_ref, i, E, max_len, tile_ref, sorted_ref, sem, to_sorted, wait):
    local = 0
    for e in range(E):
        n = len_ref[i * E + e]
        base = off_ref[i * E + e]
        done = 0
        sz = max_len
        while sz >= RUN_ALIGN:
            @pl.when((n & sz) != 0)
            def _(sz=sz, local=local, base=base, done=done):
                t_rows = tile_ref.at[pl.ds(pl.multiple_of(local + done, RUN_ALIGN), sz), :]
                s_rows = sorted_ref.at[pl.ds(pl.multiple_of(base + done, RUN_ALIGN), sz), :]
                cp = (pltpu.make_async_copy(t_rows, s_rows, sem) if to_sorted
                      else pltpu.make_async_copy(s_rows, t_rows, sem))
                if wait:
                    cp.wait()
                else:
                    cp.start()
            done = done + (n & sz)
            sz //= 2
        local = local + n


def _dispatch_kernel(len_ref, off_ref, x_ref, g_ref, ls_ref, init_ref, xs_ref, xs_t, sem, *, E):
    del init_ref
    i = pl.program_id(0)
    TM = x_ref.shape[0]
    LS = xs_t.shape[1]
    slot = lax.rem(i, 2)
    h2 = _rms(x_ref[...], g_ref[...]).astype(BF16)
    ls = ls_ref[...]
    j = lax.broadcasted_iota(jnp.int32, (LS, TM), 0)
    onehot = jnp.where(j == ls[2:3, :], 1.0, jnp.where(j == ls[3:4, :], 1.0, 0.0)).astype(BF16)
    xs_t[slot] = _dot(onehot, h2).astype(BF16)

    @pl.when(i > 0)
    def _():
        _run_dmas(len_ref, off_ref, i - 1, E, TM, xs_t.at[1 - slot], xs_ref, sem.at[1 - slot], True, True)

    _run_dmas(len_ref, off_ref, i, E, TM, xs_t.at[slot], xs_ref, sem.at[slot], True, False)

    @pl.when(i == pl.num_programs(0) - 1)
    def _():
        _run_dmas(len_ref, off_ref, i, E, TM, xs_t.at[slot], xs_ref, sem.at[slot], True, True)


def _dispatch(run_len, run_off, idx, x2d, norm_g, P, E):
    T, D = x2d.shape
    TM = min(ROUTE_TM, T)
    LS = TOP_K * TM + E * RUN_ALIGN
    init = jnp.zeros((P, D), BF16)
    grid_spec = pltpu.PrefetchScalarGridSpec(
        num_scalar_prefetch=2,
        grid=(T // TM,),
        in_specs=[pl.BlockSpec((TM, D), lambda i, *_: (i, 0)),
                  pl.BlockSpec((1, D), lambda i, *_: (0, 0)),
                  pl.BlockSpec((SUBLANES, TM), lambda i, *_: (0, i)),
                  pl.BlockSpec(memory_space=pl.ANY)],
        out_specs=pl.BlockSpec(memory_space=pl.ANY),
        scratch_shapes=[pltpu.VMEM((2, LS, D), BF16), pltpu.SemaphoreType.DMA((2,))])
    return pl.pallas_call(
        functools.partial(_dispatch_kernel, E=E),
        out_shape=jax.ShapeDtypeStruct((P, D), BF16),
        grid_spec=grid_spec,
        input_output_aliases={5: 0},
        compiler_params=_params(("arbitrary",), 32),
        name="moe_dispatch",
    )(run_len, run_off, x2d, norm_g, idx, init)


def _group_ffn_kernel(te_ref, tn_ref, x_ref, w1_ref, w3_ref, w2_ref, o_ref, acc):
    i = pl.program_id(0)
    f = pl.program_id(1)

    @pl.when(f == 0)
    def _():
        acc[...] = jnp.zeros_like(acc)

    @pl.when(tn_ref[i] > 0)
    def _():
        hb = x_ref[...]
        mid = jax.nn.silu(_dot(hb, w1_ref[...])) * _dot(hb, w3_ref[...])
        acc[...] += _dot(mid.astype(BF16), w2_ref[...])

    @pl.when(f == pl.num_programs(1) - 1)
    def _():
        o_ref[...] = acc[...]


def _group_ffn(tile_e, tile_n, xs, w1, w3, w2, TMG):
    P, D = xs.shape
    F = w1.shape[2]
    TF = _ffn_hidden_tile(F)
    nF = F // TF
    fblk = lambda i, f, tn: jnp.where(tn[i] > 0, f, nF - 1)
    grid_spec = pltpu.PrefetchScalarGridSpec(
        num_scalar_prefetch=2,
        grid=(P // TMG, nF),
        in_specs=[pl.BlockSpec((TMG, D), lambda i, f, te, tn: (i, 0)),
                  pl.BlockSpec((None, D, TF), lambda i, f, te, tn: (te[i], 0, fblk(i, f, tn))),
                  pl.BlockSpec((None, D, TF), lambda i, f, te, tn: (te[i], 0, fblk(i, f, tn))),
                  pl.BlockSpec((None, TF, D), lambda i, f, te, tn: (te[i], fblk(i, f, tn), 0))],
        out_specs=pl.BlockSpec((TMG, D), lambda i, f, te, tn: (i, 0)),
        scratch_shapes=[pltpu.VMEM((TMG, D), F32)])
    return pl.pallas_call(
        _group_ffn_kernel,
        out_shape=jax.ShapeDtypeStruct((P, D), F32),
        grid_spec=grid_spec,
        compiler_params=_params(("arbitrary", "arbitrary"), 56),
        name="moe_group_ffn",
    )(tile_e, tile_n, xs, w1, w3, w2)


def _combine_kernel(len_ref, off_ref, x_ref, lsr_ref, lsc_ref, gt_ref, fg_ref, ys_ref, o_ref, yt, sem,
                    *, E, final_norm):
    i = pl.program_id(0)
    TM = x_ref.shape[0]
    _, LS, D = yt.shape
    slot = lax.rem(i, 2)

    def fetch(tile, s):
        yt[s, TOP_K * TM:LS, :] = jnp.zeros((LS - TOP_K * TM, D), F32)
        _run_dmas(len_ref, off_ref, tile, E, TM, yt.at[s], ys_ref, sem.at[s], False, False)

    @pl.when(i == 0)
    def _():
        fetch(0, 0)

    @pl.when(i + 1 < pl.num_programs(0))
    def _():
        fetch(i + 1, 1 - slot)

    lsr = lsr_ref[...]
    gt = gt_ref[...]
    js = lax.broadcasted_iota(jnp.int32, (LS, TM), 0)
    lane = lax.broadcasted_iota(jnp.int32, (TM, LANES), 1)
    gs = None
    for k in range(TOP_K):
        t1, t2, t3 = (t.astype(F32) for t in _split3(gt[:, k:k + 1]))
        terms = jnp.where(lane == 0, t1, jnp.where(lane == 1, t2, jnp.where(lane == 2, t3, 0.0)))
        pk = jnp.where(js == lsr[TOP_K + k:TOP_K + k + 1, :], 1.0, 0.0).astype(BF16)
        gk = jnp.sum(_dot(pk, terms.astype(BF16)), axis=1, keepdims=True)
        gs = gk if gs is None else gs + gk
    _run_dmas(len_ref, off_ref, i, E, TM, yt.at[slot], ys_ref, sem.at[slot], False, True)
    z = yt[slot] * gs
    zh = z.astype(BF16)
    zl = (z - zh.astype(F32)).astype(BF16)
    lsc = lsc_ref[...]
    jt = lax.broadcasted_iota(jnp.int32, (TM, LS), 1)
    pt = None
    for k in range(TOP_K):
        ok = jnp.where(jt == lsc[:, TOP_K + k:TOP_K + k + 1], 1.0, 0.0)
        pt = ok if pt is None else pt + ok
    pt = pt.astype(BF16)
    xo = x_ref[...] + (_dot(pt, zh) + _dot(pt, zl))
    o_ref[...] = _rms(xo, fg_ref[...]) if final_norm else xo


def _combine(run_len, run_off, x2d, idx, idx_t, gates_t, final_g, ys, E, final_norm):
    T, D = x2d.shape
    TM = min(ROUTE_TM, T)
    LS = TOP_K * TM + E * RUN_ALIGN
    grid_spec = pltpu.PrefetchScalarGridSpec(
        num_scalar_prefetch=2,
        grid=(T // TM,),
        in_specs=[pl.BlockSpec((TM, D), lambda i, *_: (i, 0)),
                  pl.BlockSpec((SUBLANES, TM), lambda i, *_: (0, i)),
                  pl.BlockSpec((TM, SUBLANES), lambda i, *_: (i, 0)),
                  pl.BlockSpec((TM, SUBLANES), lambda i, *_: (i, 0)),
                  pl.BlockSpec((1, D), lambda i, *_: (0, 0)),
                  pl.BlockSpec(memory_space=pl.ANY)],
        out_specs=pl.BlockSpec((TM, D), lambda i, *_: (i, 0)),
        scratch_shapes=[pltpu.VMEM((2, LS, D), F32), pltpu.SemaphoreType.DMA((2,))])
    return pl.pallas_call(
        functools.partial(_combine_kernel, E=E, final_norm=final_norm),
        out_shape=jax.ShapeDtypeStruct((T, D), F32),
        grid_spec=grid_spec,
        compiler_params=_params(("arbitrary",), 48),
        name="moe_combine",
    )(run_len, run_off, x2d, idx, idx_t, gates_t, final_g, ys)


def _moe(x2d, norm_g, router_w, router_b, w1, w3, w2, final_g, final_norm):
    T, D = x2d.shape
    E = router_w.shape[1]
    TMG = 512
    nT = T // min(ROUTE_TM, T)
    rw = jnp.pad(router_w, ((0, 0), (0, LANES - E)))
    rw_hi = rw.astype(BF16)
    rw_lo = (rw - rw_hi.astype(F32)).astype(BF16)
    idx, gates, run_len, run_off, tot = _route(x2d, norm_g, rw_hi, rw_lo, router_b.reshape(E, 1), E)
    tot = tot[:, 0]
    padded = ((tot + TMG - 1) // TMG) * TMG
    pend = jnp.cumsum(padded)
    pstart = pend - padded
    run_len = run_len[:, 0]
    run_off = (run_off[:, 0].reshape(nT, E) + pstart[None, :]).reshape(nT * E)
    P = -(-(T * TOP_K + nT * E * (RUN_ALIGN - 1) + E * (TMG - 1)) // TMG) * TMG
    tile_start = jnp.arange(P // TMG, dtype=jnp.int32) * TMG
    tile_e = jnp.minimum(jnp.sum(tile_start[:, None] >= pend[None, :], axis=1), E - 1).astype(jnp.int32)
    sel = tile_e[:, None] == jnp.arange(E, dtype=jnp.int32)[None, :]
    tile_end = jnp.sum(jnp.where(sel, (pstart + tot)[None, :], 0), axis=1)
    tile_n = jnp.clip(tile_end - tile_start, 0, TMG).astype(jnp.int32)
    xs = _dispatch(run_len, run_off, idx, x2d, norm_g, P, E)
    ys = _group_ffn(tile_e, tile_n, xs, w1, w3, w2, TMG)
    return _combine(run_len, run_off, x2d, idx, idx.T, gates.T, final_g, ys, E, final_norm)


def _final_norm_kernel(x_ref, g_ref, o_ref):
    o_ref[...] = _rms(x_ref[...], g_ref[...])


def _final_norm(x2d, g):
    T, D = x2d.shape
    TM = min(1024, T)
    return pl.pallas_call(
        _final_norm_kernel,
        out_shape=jax.ShapeDtypeStruct((T, D), F32),
        grid=(T // TM,),
        in_specs=[pl.BlockSpec((TM, D), lambda i: (i, 0)), _const_spec((1, D))],
        out_specs=pl.BlockSpec((TM, D), lambda i: (i, 0)),
        compiler_params=_params(("arbitrary",), 32),
        name="final_norm",
    )(x2d, g)


def kernel(x, mem, norm_mix_g, w_in, conv_rg_w, conv_rg_b, rg_w_a, rg_b_a, rg_w_x, rg_b_x, rg_lambda,
           conv_ml_w, conv_ml_b, ml_w_q, ml_w_k, ml_w_v, ml_b_i, ml_b_f, ml_norm_g, mem_norm_g, w_kv,
           w_br_rg, w_br_ml, w_br_xa, b_merge, w_out, norm_ffn_g, ffn_w1, ffn_w3, ffn_w2, router_w,
           router_b, moe_w1, moe_w3, moe_w2, final_norm_g):
    B, S, D = x.shape
    depth = w_in.shape[0]
    d_rg = conv_rg_w.shape[2]
    d_ml = conv_ml_w.shape[2]
    H = ml_w_q.shape[1]
    d_xa = w_kv.shape[2] // 2
    o_ax, o_ay = 0, d_rg
    o_mu, o_mo = 2 * d_rg, 2 * d_rg + d_ml
    o_mi = 2 * d_rg + 2 * d_ml
    o_mf = o_mi + H
    o_q = o_mf + H
    o_g = o_q + d_xa
    assert w_in.shape[2] == o_g + N_BRANCH * D and 2 * H <= SUBLANES

    bf = lambda a: a.astype(BF16)
    row = lambda a: a.reshape(1, -1)
    x2d = x.reshape(B * S, D)
    fg = row(final_norm_g)
    moe_bf16 = []
    for l in range(depth):
        wl = w_in[l]
        w_if = jnp.pad(wl[:, o_mi:o_mi + 2 * H], ((0, 0), (0, LANES - 2 * H)))
        w_ift = jnp.pad(wl[:, o_mi:o_mi + 2 * H].T, ((0, SUBLANES - 2 * H), (0, 0)))
        b_if = jnp.concatenate([ml_b_i[l], ml_b_f[l]])
        y_ml = _mlstm_branch(x2d, B, row(norm_mix_g[l]), bf(wl[:, o_mu:o_mu + d_ml]),
                                   bf(wl[:, o_mo:o_mo + d_ml]), bf(w_if), bf(w_ift),
                                   jnp.pad(b_if, (0, LANES - 2 * H)).reshape(1, LANES),
                                   jnp.pad(b_if, (0, SUBLANES - 2 * H)).reshape(SUBLANES, 1),
                                   conv_ml_w[l], row(conv_ml_b[l]), bf(ml_w_q[l]), bf(ml_w_k[l]),
                                   bf(ml_w_v[l]), row(ml_norm_g[l]))
        y_rg = _rg_branch(x2d.reshape(B, S, D), row(norm_mix_g[l]), bf(wl[:, o_ax:o_ax + d_rg]),
                          bf(wl[:, o_ay:o_ay + d_rg]), conv_rg_w[l], row(conv_rg_b[l]), bf(rg_w_a[l]),
                          row(rg_b_a[l]), bf(rg_w_x[l]), row(rg_b_x[l]), row(rg_lambda[l]))
        kv = _mem_kv(mem, row(mem_norm_g[l]), bf(w_kv[l]))
        x2d = _merge(x2d, B, row(norm_mix_g[l]), bf(wl[:, o_q:o_q + d_xa]), bf(wl[:, o_g:]),
                     row(b_merge[l]), y_rg.reshape(B * S, d_rg), y_ml, kv, bf(w_br_rg[l]),
                     bf(w_br_ml[l]), bf(w_br_xa[l]), bf(w_out[l]))
        j = l // 2
        if l % 2 == 0:
            w1 = bf(ffn_w1[j])
            nxt = [w[j] for w in (moe_w1, moe_w3, moe_w2)] if l + 1 < depth else []
            nxt = nxt if all(_ffn_can_cast(x2d, w1, w) for w in nxt) else []
            x2d, moe_bf16 = _ffn(x2d, row(norm_ffn_g[l]), w1, bf(ffn_w3[j]), bf(ffn_w2[j]), nxt)
        else:
            e1, e3, e2 = moe_bf16 if moe_bf16 else (bf(moe_w1[j]), bf(moe_w3[j]), bf(moe_w2[j]))
            moe_bf16 = []
            x2d = _moe(x2d, row(norm_ffn_g[l]), router_w[j], router_b[j], e1, e3, e2, fg, l == depth - 1)
    if depth % 2 == 1:
        x2d = _final_norm(x2d, fg)
    return x2d.reshape(B, S, D)
```

```python
import functools

import jax
import jax.numpy as jnp
from jax import lax
from jax.experimental import pallas as pl
from jax.experimental.pallas import tpu as pltpu

EPS = 1e-6
RG_C = 8.0
CONV_W = 4
ML_CHUNK = 128
XA_HEADS = 4
TOP_K = 2
N_BRANCH = 3

V7X_VMEM_BYTES = 64 * 1024 * 1024
LANES = 128
SUBLANES = 8

F32 = jnp.float32
BF16 = jnp.bfloat16


def _params(semantics, vmem_mib):
    assert vmem_mib * 1024 * 1024 <= V7X_VMEM_BYTES
    return pltpu.CompilerParams(dimension_semantics=semantics,
                                vmem_limit_bytes=vmem_mib * 1024 * 1024)


def _const_spec(shape):
    nd = len(shape)
    return pl.BlockSpec(shape, lambda *_: (0,) * nd)


def _rms(x, g):
    ms = jnp.mean(x * x, axis=-1, keepdims=True)
    return x * lax.rsqrt(ms + EPS) * g


def _dot(a, b):
    return jnp.dot(a, b, preferred_element_type=F32)


def _dot_nt(a, b):
    return lax.dot_general(a, b, (((1,), (1,)), ((), ())), preferred_element_type=F32)


def _dot_tn(a, b):
    return lax.dot_general(a, b, (((0,), (0,)), ((), ())), preferred_element_type=F32)


def _split3(x):
    h1 = x.astype(BF16)
    r1 = x - h1.astype(F32)
    h2 = r1.astype(BF16)
    h3 = (r1 - h2.astype(F32)).astype(BF16)
    return h1, h2, h3


def _rg_kernel(x_ref, g_ref, wax_ref, way_ref, cw_ref, cb_ref, wa_ref, ba_ref, wx_ref, bx_ref,
               lam_ref, o_ref, axbuf, a_s, b_s, h_s, carry, *, B, TT):
    R = TT * B
    halo = (CONV_W - 1) * B

    @pl.when(pl.program_id(0) == 0)
    def _():
        axbuf[0:halo, :] = jnp.zeros((halo, axbuf.shape[1]), F32)
        carry[...] = jnp.zeros_like(carry)

    x = pltpu.einshape("btd->tbd", x_ref[...]).reshape(R, x_ref.shape[2])
    h = _rms(x, g_ref[...]).astype(BF16)
    G, bi, _ = wa_ref.shape
    rate = -RG_C * jax.nn.softplus(-lam_ref[...])
    cols = [slice(g * bi, (g + 1) * bi) for g in range(G)]
    axbuf[halo:halo + R, cols[0]] = _dot(h, wax_ref[:, cols[0]])
    for g in range(G):
        cs = cols[g]
        if g + 1 < G:
            axbuf[halo:halo + R, cols[g + 1]] = _dot(h, wax_ref[:, cols[g + 1]])
        ay = _dot(h, way_ref[:, cs])
        cw = cw_ref[:, cs]
        xc = cb_ref[:, cs] + cw[0:1, :] * axbuf[0:R, cs]
        for k in range(1, CONV_W):
            xc = xc + cw[k:k + 1, :] * axbuf[k * B:k * B + R, cs]
        axbuf[0:halo, cs] = axbuf[R:R + halo, cs]
        xcb = xc.astype(BF16)
        r = jax.nn.sigmoid(_dot(xcb, wa_ref[g]) + ba_ref[:, cs])
        ig = jax.nn.sigmoid(_dot(xcb, wx_ref[g]) + bx_ref[:, cs])
        log_a = r * rate[:, cs]
        a = jnp.exp(log_a)
        a_s[:, cs] = a
        b_s[:, cs] = jnp.sqrt(-jnp.tanh(log_a) * (a * a + 1.0)) * (ig * xc)

        def step(t, hc, cs=cs):
            off = pl.multiple_of(t * B, B)
            hn = a_s[pl.ds(off, B), cs] * hc + b_s[pl.ds(off, B), cs]
            h_s[pl.ds(off, B), cs] = hn
            return hn

        carry[:, cs] = lax.fori_loop(0, TT, step, carry[:, cs], unroll=True)
        y = (h_s[:, cs] * jax.nn.gelu(ay)).reshape(TT, B, bi)
        o_ref[:, :, cs] = pltpu.einshape("tbc->btc", y).astype(BF16)


def _rg_branch(x3d, norm_g, w_ax, w_ay, conv_w, conv_b, w_a, b_a, w_x, b_x, lam):
    B, S, D = x3d.shape
    C = w_ax.shape[1]
    TT = min(64, S)
    assert S % TT == 0 and B % SUBLANES == 0
    R = TT * B
    halo = (CONV_W - 1) * B
    tile = lambda i: (0, i, 0)
    return pl.pallas_call(
        functools.partial(_rg_kernel, B=B, TT=TT),
        out_shape=jax.ShapeDtypeStruct((B, S, C), BF16),
        grid=(S // TT,),
        in_specs=[pl.BlockSpec((B, TT, D), tile), _const_spec((1, D)), _const_spec((D, C)),
                  _const_spec((D, C)), _const_spec((CONV_W, C)), _const_spec((1, C)),
                  _const_spec(w_a.shape), _const_spec((1, C)), _const_spec(w_x.shape),
                  _const_spec((1, C)), _const_spec((1, C))],
        out_specs=pl.BlockSpec((B, TT, C), tile),
        scratch_shapes=[pltpu.VMEM((halo + R, C), F32), pltpu.VMEM((R, C), F32),
                        pltpu.VMEM((R, C), F32), pltpu.VMEM((R, C), F32), pltpu.VMEM((B, C), F32)],
        compiler_params=_params(("arbitrary",), 48),
        name="rg_branch",
    )(x3d, norm_g, w_ax, w_ay, conv_w, conv_b, w_a, b_a, w_x, b_x, lam)


def _alternate(first, second):
    result = None
    live = [True, True]
    while any(live):
        for n, gen in enumerate((first, second)):
            if live[n]:
                try:
                    next(gen)
                except StopIteration as stop:
                    live[n] = False
                    if n == 1:
                        result = stop.value
    return result


def _mlstm_kernel(x_ref, g_ref, wmu_ref, wmo_ref, wif_ref, wift_ref, bifc_ref, bifr_ref, cw_ref,
                  cb_ref, wq_ref, wk_ref, wv_ref, ng_ref, o_ref, ubuf, q_s, k_s, v_s, og_s, ifc_s, ifr_s,
                  lfc_s, lfr_s, c_st, n_st, m_st, *, TS, nS):
    H, d, _ = wq_ref.shape
    L = ML_CHUNK
    pad = SUBLANES
    tail = CONV_W - 1
    g = pl.program_id(0)

    @pl.when(g == 0)
    def _():
        for r in (q_s, k_s, v_s, og_s, ifc_s, ifr_s, lfc_s, lfr_s):
            r[...] = jnp.zeros_like(r)

    @pl.when(lax.rem(g, nS) == 0)
    def _():
        ubuf[0:pad, :] = jnp.zeros((pad, ubuf.shape[1]), F32)

    @pl.when((g == 0) | (lax.rem(g + (nS - 1), nS) == 0))
    def _():
        c_st[...] = jnp.zeros_like(c_st)
        n_st[...] = jnp.zeros_like(n_st)
        m_st[...] = jnp.zeros_like(m_st)

    if_c = ifc_s[...]
    if_r = ifr_s[...]
    lf_c = lfc_s[...]
    lf_r = lfr_s[...]
    ri = lax.broadcasted_iota(jnp.int32, (L, L), 0)
    ci = lax.broadcasted_iota(jnp.int32, (L, L), 1)
    causal = ci <= ri
    tri_l = jnp.where(causal, 1.0, 0.0).astype(BF16)
    tri_u = jnp.where(ri <= ci, 1.0, 0.0).astype(BF16)
    ng = ng_ref[...]
    nck = TS // L
    bcs = [sum(_dot(tri_l, p) for p in _split3(lf_c[ck * L:(ck + 1) * L, :])) for ck in range(nck)]
    brs = [sum(_dot(p, tri_u) for p in _split3(lf_r[:, ck * L:(ck + 1) * L])) for ck in range(nck)]
    h = _rms(x_ref[...], g_ref[...]).astype(BF16)

    def wide(t):
        return jnp.concatenate([t] * (d // LANES), axis=1)

    def recurrence(hd):
        cs = slice(hd * d, (hd + 1) * d)
        for ck in range(nck):
            r0 = ck * L
            bc, br = bcs[ck], brs[ck]
            b_col = jnp.broadcast_to(bc[:, H + hd:H + hd + 1], (L, LANES))
            i_col = jnp.broadcast_to(if_c[r0:r0 + L, hd:hd + 1], (L, LANES))
            b_row = br[H + hd:H + hd + 1, :]
            i_row = if_r[hd:hd + 1, r0:r0 + L]
            m = m_st[hd][0:1, :]
            qb = q_s[r0:r0 + L, cs]
            kb = k_s[r0:r0 + L, cs]
            vb = v_s[r0:r0 + L, cs]
            gg = b_col + m
            dm = jnp.where(causal, b_col - b_row + i_row, -jnp.inf)
            yield
            m_row = jnp.maximum(gg, jnp.max(dm, axis=-1, keepdims=True))
            s_qk = _dot_nt(qb, kb)
            yield
            w = jnp.exp(dm - m_row) * s_qk
            inter = jnp.exp(gg - m_row)
            cmat = c_st[hd]
            nvec = n_st[hd]
            yield
            num = wide(inter) * _dot(qb, cmat.astype(BF16)) + _dot(w.astype(BF16), vb)
            yield
            qf, kf, vf = qb.astype(F32), kb.astype(F32), vb.astype(F32)
            den = inter * jnp.sum(qf * nvec, axis=-1, keepdims=True) + jnp.sum(w, axis=-1, keepdims=True)
            hh = num * wide(1.0 / jnp.maximum(jnp.abs(den), jnp.exp(-m_row)))
            yield
            b_last = b_col[L - 1:L, :]
            dl = b_last - b_col + i_col
            m_new = jnp.maximum(b_last + m, jnp.max(dl, axis=0, keepdims=True))
            decay = jnp.exp(b_last + m - m_new)
            wl = jnp.exp(dl - m_new)
            yield
            c_st[hd] = wide(decay) * cmat + _dot_tn(kb, (wide(wl) * vf).astype(BF16))
            n_st[hd] = wide(decay) * nvec + jnp.sum(wide(wl) * kf, axis=0, keepdims=True)
            m_st[hd] = jnp.broadcast_to(m_new, m_st.shape[1:])
            yield
            y = og_s[r0:r0 + L, cs] * hh
            y = y * lax.rsqrt(jnp.mean(y * y, axis=-1, keepdims=True) + EPS)
            o_ref[r0:r0 + L, cs] = (y * ng[:, cs]).astype(BF16)
            yield

    def projection(hd):
        cs = slice(hd * d, (hd + 1) * d)
        u = _dot(h, wmu_ref[:, cs])
        ubuf[pad:pad + TS, cs] = u
        yield
        mo = _dot(h, wmo_ref[:, cs])
        yield
        cw = cw_ref[:, cs]
        c = cb_ref[:, cs] + cw[0:1, :] * ubuf[pad - tail:pad - tail + TS, cs]
        for k in range(1, CONV_W):
            c = c + cw[k:k + 1, :] * ubuf[pad - tail + k:pad - tail + k + TS, cs]
            yield
        ubuf[0:pad, cs] = ubuf[TS:TS + pad, cs]
        cb16 = jax.nn.silu(c).astype(BF16)
        yield
        q = _dot(cb16, wq_ref[hd]).astype(BF16)
        yield
        k_ = (_dot(cb16, wk_ref[hd]) * (d ** -0.5)).astype(BF16)
        yield
        v = _dot(u.astype(BF16), wv_ref[hd]).astype(BF16)
        yield
        og = jax.nn.sigmoid(mo)
        yield
        return q, k_, v, og

    for hd in range(H):
        cs = slice(hd * d, (hd + 1) * d)
        q, k_, v, og = _alternate(recurrence(hd), projection(hd))
        q_s[:, cs] = q
        k_s[:, cs] = k_
        v_s[:, cs] = v
        og_s[:, cs] = og
    new_if_c = _dot(h, wif_ref[...]) + bifc_ref[...]
    new_if_r = _dot_nt(wift_ref[...], h) + bifr_ref[...]
    ifc_s[...] = new_if_c
    ifr_s[...] = new_if_r
    lfc_s[...] = jax.nn.log_sigmoid(new_if_c)
    lfr_s[...] = jax.nn.log_sigmoid(new_if_r)


def _mlstm_branch(x2d, B, norm_g, w_mu, w_mo, w_if, w_ift, b_if_c, b_if_r, conv_w, conv_b,
                  w_q, w_k, w_v, ml_norm_g):
    T, D = x2d.shape
    S = T // B
    C = w_mu.shape[1]
    H, d, _ = w_q.shape
    TS = min(256, S)
    assert S % TS == 0 and TS % ML_CHUNK == 0 and d % LANES == 0
    nS = S // TS
    G = B * nS
    return pl.pallas_call(
        functools.partial(_mlstm_kernel, TS=TS, nS=nS),
        out_shape=jax.ShapeDtypeStruct((T, C), BF16),
        grid=(G + 1,),
        in_specs=[pl.BlockSpec((TS, D), lambda g: (jnp.minimum(g, G - 1), 0)), _const_spec((1, D)),
                  _const_spec((D, C)), _const_spec((D, C)), _const_spec((D, LANES)),
                  _const_spec((SUBLANES, D)), _const_spec((1, LANES)), _const_spec((SUBLANES, 1)),
                  _const_spec((CONV_W, C)), _const_spec((1, C)), _const_spec(w_q.shape),
                  _const_spec(w_k.shape), _const_spec(w_v.shape), _const_spec((1, C))],
        out_specs=pl.BlockSpec((TS, C), lambda g: (jnp.maximum(g - 1, 0), 0)),
        scratch_shapes=[pltpu.VMEM((SUBLANES + TS, C), F32), pltpu.VMEM((TS, C), BF16),
                        pltpu.VMEM((TS, C), BF16), pltpu.VMEM((TS, C), BF16), pltpu.VMEM((TS, C), F32),
                        pltpu.VMEM((TS, LANES), F32), pltpu.VMEM((SUBLANES, TS), F32),
                        pltpu.VMEM((TS, LANES), F32), pltpu.VMEM((SUBLANES, TS), F32),
                        pltpu.VMEM((H, d, d), F32), pltpu.VMEM((H, 1, d), F32),
                        pltpu.VMEM((H, SUBLANES, LANES), F32)],
        compiler_params=_params(("arbitrary",), 48),
        name="mlstm_branch",
    )(x2d, norm_g, w_mu, w_mo, w_if, w_ift, b_if_c, b_if_r, conv_w, conv_b, w_q, w_k, w_v, ml_norm_g)


def _kv_kernel(mem_ref, g_ref, w_ref, o_ref):
    o_ref[...] = _dot(_rms(mem_ref[...], g_ref[...]).astype(BF16), w_ref[...]).astype(BF16)


def _mem_kv(mem, g, w_kv):
    B, M, D = mem.shape
    N = w_kv.shape[1]
    return pl.pallas_call(
        _kv_kernel,
        out_shape=jax.ShapeDtypeStruct((B, M, N), BF16),
        grid=(B,),
        in_specs=[pl.BlockSpec((None, M, D), lambda b: (b, 0, 0)), _const_spec((1, D)),
                  _const_spec((D, N))],
        out_specs=pl.BlockSpec((None, M, N), lambda b: (b, 0, 0)),
        compiler_params=_params(("arbitrary",), 32),
        name="mem_kv",
    )(mem, g, w_kv)


def _merge_kernel(x_ref, g_ref, wq_ref, wg_ref, bm_ref, yrg_ref, yml_ref, kv_ref, wrg_ref, wml_ref,
                  wxa_ref, wo_ref, o_ref):
    x = x_ref[...]
    D = x.shape[1]
    h = _rms(x, g_ref[...]).astype(BF16)
    q = _dot(h, wq_ref[...]).astype(BF16)
    dxa = q.shape[1]
    dh = dxa // XA_HEADS
    heads = []
    for hd in range(XA_HEADS):
        kh = kv_ref[:, hd * dh:(hd + 1) * dh]
        vh = kv_ref[:, dxa + hd * dh:dxa + (hd + 1) * dh]
        s = _dot_nt(q[:, hd * dh:(hd + 1) * dh], kh) * (dh ** -0.5)
        e = jnp.exp(s - jnp.max(s, axis=-1, keepdims=True))
        p = e / jnp.sum(e, axis=-1, keepdims=True)
        heads.append(_dot(p.astype(BF16), vh))
    y_xa = jnp.concatenate(heads, axis=1).astype(BF16)

    def gate(k):
        return jax.nn.sigmoid(_dot(h, wg_ref[:, k * D:(k + 1) * D]) + bm_ref[:, k * D:(k + 1) * D])

    merged = gate(0) * _dot(yrg_ref[...], wrg_ref[...])
    merged = merged + gate(1) * _dot(yml_ref[...], wml_ref[...])
    merged = merged + gate(2) * _dot(y_xa, wxa_ref[...])
    o_ref[...] = x + _dot(merged.astype(BF16), wo_ref[...])


def _merge(x2d, B, norm_g, w_q, w_g, b_merge, y_rg, y_ml, kv, w_br_rg, w_br_ml, w_br_xa, w_out):
    T, D = x2d.shape
    S = T // B
    C = y_ml.shape[1]
    M, N = kv.shape[1:]
    TM = min(512, S)
    assert S % TM == 0
    nS = S // TM
    row = lambda b, s: (b * nS + s, 0)
    one = pl.Buffered(1)
    cspec = lambda shape: pl.BlockSpec(shape, lambda *_: (0,) * len(shape), pipeline_mode=one)
    return pl.pallas_call(
        _merge_kernel,
        out_shape=jax.ShapeDtypeStruct((T, D), F32),
        grid=(B, nS),
        in_specs=[pl.BlockSpec((TM, D), row), cspec((1, D)), cspec(w_q.shape), cspec(w_g.shape),
                  cspec(b_merge.shape), pl.BlockSpec((TM, C), row),
                  pl.BlockSpec((TM, C), row), pl.BlockSpec((None, M, N), lambda b, s: (b, 0, 0)),
                  cspec(w_br_rg.shape), cspec(w_br_ml.shape), cspec(w_br_xa.shape),
                  cspec(w_out.shape)],
        out_specs=pl.BlockSpec((TM, D), row),
        compiler_params=_params(("arbitrary", "arbitrary"), 56),
        name="merge",
    )(x2d, norm_g, w_q, w_g, b_merge, y_rg, y_ml, kv, w_br_rg, w_br_ml, w_br_xa, w_out)


def _ffn_kernel(x_ref, g_ref, w1_ref, w3_ref, w2_ref, *rest, n_cast):
    cast_in, o_ref, cast_out = rest[:n_cast], rest[n_cast], rest[n_cast + 1:2 * n_cast + 1]
    hs, acc = rest[2 * n_cast + 1:]
    f = pl.program_id(1)
    for src, dst in zip(cast_in, cast_out):
        dst[...] = src[...].astype(BF16)

    @pl.when(f == 0)
    def _():
        hs[...] = _rms(x_ref[...], g_ref[...]).astype(BF16)
        acc[...] = jnp.zeros_like(acc)

    hb = hs[...]
    mid = jax.nn.silu(_dot(hb, w1_ref[...])) * _dot(hb, w3_ref[...])
    acc[...] += _dot(mid.astype(BF16), w2_ref[...])

    @pl.when(f == pl.num_programs(1) - 1)
    def _():
        o_ref[...] = x_ref[...] + acc[...]


FFN_ROW_TILE = 512


def _ffn_hidden_tile(F):
    for tf in (1792, 1024, 512, 256):
        if F % tf == 0:
            return tf
    return F


def _ffn_can_cast(x2d, w1, t):
    steps = (x2d.shape[0] // min(FFN_ROW_TILE, x2d.shape[0])) * (w1.shape[1] // _ffn_hidden_tile(w1.shape[1]))
    rows = t.size // t.shape[-1]
    return rows % steps == 0 and (rows // steps) % (2 * SUBLANES) == 0


def _ffn(x2d, norm_g, w1, w3, w2, cast=()):
    T, D = x2d.shape
    F = w1.shape[1]
    TM = min(FFN_ROW_TILE, T)
    TF = _ffn_hidden_tile(F)
    assert T % TM == 0
    nF = F // TF
    steps = (T // TM) * nF
    cast2d = [t.reshape(-1, t.shape[-1]) for t in cast]
    cast_specs = [pl.BlockSpec((t.shape[0] // steps, t.shape[1]), lambda i, f: (i * nF + f, 0))
                  for t in cast2d]
    out = pl.pallas_call(
        functools.partial(_ffn_kernel, n_cast=len(cast)),
        out_shape=(jax.ShapeDtypeStruct((T, D), F32),
                   *[jax.ShapeDtypeStruct(t.shape, BF16) for t in cast2d]),
        grid=(T // TM, nF),
        in_specs=[pl.BlockSpec((TM, D), lambda i, f: (i, 0)), _const_spec((1, D)),
                  pl.BlockSpec((D, TF), lambda i, f: (0, f)), pl.BlockSpec((D, TF), lambda i, f: (0, f)),
                  pl.BlockSpec((TF, D), lambda i, f: (f, 0)), *cast_specs],
        out_specs=(pl.BlockSpec((TM, D), lambda i, f: (i, 0)), *cast_specs),
        scratch_shapes=[pltpu.VMEM((TM, D), BF16), pltpu.VMEM((TM, D), F32)],
        compiler_params=_params(("arbitrary", "arbitrary"), 56),
        name="ffn_dense",
    )(x2d, norm_g, w1, w3, w2, *cast2d)
    return out[0], [o.reshape(t.shape) for o, t in zip(out[1:], cast)]


ROUTE_TM = 512
RUN_ALIGN = 16


def _route_kernel(x_ref, g_ref, whi_ref, wlo_ref, rb_ref, idx_ref, gate_ref, len_ref, off_ref,
                  tot_ref, run_s, *, E):
    @pl.when(pl.program_id(0) == 0)
    def _():
        run_s[...] = jnp.zeros_like(run_s)

    h2 = _rms(x_ref[...], g_ref[...])
    TM = h2.shape[0]
    hi = h2.astype(BF16)
    lo = (h2 - hi.astype(F32)).astype(BF16)
    logits = _dot(hi, whi_ref[...]) + (_dot(lo, whi_ref[...]) + _dot(hi, wlo_ref[...]))
    lt = logits.T[0:E, :] + rb_ref[...]
    ie = lax.broadcasted_iota(jnp.int32, (E, TM), 0)
    m1 = jnp.max(lt, axis=0, keepdims=True)
    i1 = jnp.min(jnp.where(lt == m1, ie, E), axis=0, keepdims=True)
    l2 = jnp.where(ie == i1, -jnp.inf, lt)
    m2 = jnp.max(l2, axis=0, keepdims=True)
    i2 = jnp.min(jnp.where(l2 == m2, ie, E), axis=0, keepdims=True)
    ex = jnp.exp(m2 - m1)
    g1 = 1.0 / (1.0 + ex)
    g2 = ex / (1.0 + ex)
    oh1 = jnp.where(ie == i1, 1.0, 0.0)
    oh2 = jnp.where(ie == i2, 1.0, 0.0)
    oh = oh1 + oh2
    ri = lax.broadcasted_iota(jnp.int32, (TM, TM), 0)
    ci = lax.broadcasted_iota(jnp.int32, (TM, TM), 1)
    upper = jnp.where(ri < ci, 1.0, 0.0).astype(BF16)
    excl = _dot(oh.astype(BF16), upper)
    cnt = jnp.sum(oh, axis=1, keepdims=True).astype(jnp.int32)
    run_len = jnp.broadcast_to(((cnt + (RUN_ALIGN - 1)) // RUN_ALIGN) * RUN_ALIGN, (E, LANES))
    iec = lax.broadcasted_iota(jnp.int32, (E, LANES), 0)
    run_start = jnp.zeros((E, LANES), jnp.int32)
    for e in range(E - 1):
        run_start = run_start + jnp.where(iec > e, run_len[e:e + 1, :], 0)
    slot = run_start[:, 0:1].astype(F32) + excl
    s1 = jnp.sum(oh1 * slot, axis=0, keepdims=True).astype(jnp.int32)
    s2 = jnp.sum(oh2 * slot, axis=0, keepdims=True).astype(jnp.int32)
    len_ref[...] = run_len
    off_ref[...] = run_s[...]
    run_s[...] = run_s[...] + run_len
    tot_ref[...] = run_s[...]
    row = lax.broadcasted_iota(jnp.int32, (SUBLANES, TM), 0)
    idx_ref[...] = jnp.where(row == 0, i1, jnp.where(row == 1, i2, jnp.where(row == 2, s1,
                             jnp.where(row == 3, s2, 0))))
    gate_ref[...] = jnp.where(row == 0, g1, jnp.where(row == 1, g2, 0.0))


def _route(x2d, norm_g, w_hi, w_lo, rb, E):
    T, D = x2d.shape
    TM = min(ROUTE_TM, T)
    nT = T // TM
    assert T % TM == 0 and E == SUBLANES
    return pl.pallas_call(
        functools.partial(_route_kernel, E=E),
        out_shape=(jax.ShapeDtypeStruct((SUBLANES, T), jnp.int32),
                   jax.ShapeDtypeStruct((SUBLANES, T), F32),
                   jax.ShapeDtypeStruct((nT * E, LANES), jnp.int32),
                   jax.ShapeDtypeStruct((nT * E, LANES), jnp.int32),
                   jax.ShapeDtypeStruct((E, LANES), jnp.int32)),
        grid=(nT,),
        in_specs=[pl.BlockSpec((TM, D), lambda i: (i, 0)), _const_spec((1, D)),
                  _const_spec((D, LANES)), _const_spec((D, LANES)), _const_spec((E, 1))],
        out_specs=(pl.BlockSpec((SUBLANES, TM), lambda i: (0, i)),
                   pl.BlockSpec((SUBLANES, TM), lambda i: (0, i)),
                   pl.BlockSpec((E, LANES), lambda i: (i, 0)),
                   pl.BlockSpec((E, LANES), lambda i: (i, 0)),
                   _const_spec((E, LANES))),
        scratch_shapes=[pltpu.VMEM((E, LANES), jnp.int32)],
        compiler_params=_params(("arbitrary",), 32),
        name="moe_route",
    )(x2d, norm_g, w_hi, w_lo, rb)


def _run_dmas(len_ref, off_ref, i, E, max_len, tile_ref, sorted_ref, sem, to_sorted, wait):
    local = 0
    for e in range(E):
        n = len_ref[i * E + e]
        base = off_ref[i * E + e]
        done = 0
        sz = max_len
        while sz >= RUN_ALIGN:
            @pl.when((n & sz) != 0)
            def _(sz=sz, local=local, base=base, done=done):
                t_rows = tile_ref.at[pl.ds(pl.multiple_of(local + done, RUN_ALIGN), sz), :]
                s_rows = sorted_ref.at[pl.ds(pl.multiple_of(base + done, RUN_ALIGN), sz), :]
                cp = (pltpu.make_async_copy(t_rows, s_rows, sem) if to_sorted
                      else pltpu.make_async_copy(s_rows, t_rows, sem))
                if wait:
                    cp.wait()
                else:
                    cp.start()
            done = done + (n & sz)
            sz //= 2
        local = local + n


def _dispatch_kernel(len_ref, off_ref, x_ref, g_ref, ls_ref, init_ref, xs_ref, xs_t, sem, *, E):
    del init_ref
    i = pl.program_id(0)
    TM = x_ref.shape[0]
    LS = xs_t.shape[1]
    slot = lax.rem(i, 2)
    h2 = _rms(x_ref[...], g_ref[...]).astype(BF16)
    ls = ls_ref[...]
    j = lax.broadcasted_iota(jnp.int32, (LS, TM), 0)
    onehot = jnp.where(j == ls[2:3, :], 1.0, jnp.where(j == ls[3:4, :], 1.0, 0.0)).astype(BF16)
    xs_t[slot] = _dot(onehot, h2).astype(BF16)

    @pl.when(i > 0)
    def _():
        _run_dmas(len_ref, off_ref, i - 1, E, TM, xs_t.at[1 - slot], xs_ref, sem.at[1 - slot], True, True)

    _run_dmas(len_ref, off_ref, i, E, TM, xs_t.at[slot], xs_ref, sem.at[slot], True, False)

    @pl.when(i == pl.num_programs(0) - 1)
    def _():
        _run_dmas(len_ref, off_ref, i, E, TM, xs_t.at[slot], xs_ref, sem.at[slot], True, True)


def _dispatch(run_len, run_off, idx, x2d, norm_g, P, E):
    T, D = x2d.shape
    TM = min(ROUTE_TM, T)
    LS = TOP_K * TM + E * RUN_ALIGN
    init = jnp.zeros((P, D), BF16)
    grid_spec = pltpu.PrefetchScalarGridSpec(
        num_scalar_prefetch=2,
        grid=(T // TM,),
        in_specs=[pl.BlockSpec((TM, D), lambda i, *_: (i, 0)),
                  pl.BlockSpec((1, D), lambda i, *_: (0, 0)),
                  pl.BlockSpec((SUBLANES, TM), lambda i, *_: (0, i)),
                  pl.BlockSpec(memory_space=pl.ANY)],
        out_specs=pl.BlockSpec(memory_space=pl.ANY),
        scratch_shapes=[pltpu.VMEM((2, LS, D), BF16), pltpu.SemaphoreType.DMA((2,))])
    return pl.pallas_call(
        functools.partial(_dispatch_kernel, E=E),
        out_shape=jax.ShapeDtypeStruct((P, D), BF16),
        grid_spec=grid_spec,
        input_output_aliases={5: 0},
        compiler_params=_params(("arbitrary",), 32),
        name="moe_dispatch",
    )(run_len, run_off, x2d, norm_g, idx, init)


def _group_ffn_kernel(te_ref, tn_ref, x_ref, w1_ref, w3_ref, w2_ref, o_ref, acc):
    i = pl.program_id(0)
    f = pl.program_id(1)

    @pl.when(f == 0)
    def _():
        acc[...] = jnp.zeros_like(acc)

    @pl.when(tn_ref[i] > 0)
    def _():
        hb = x_ref[...]
        mid = jax.nn.silu(_dot(hb, w1_ref[...])) * _dot(hb, w3_ref[...])
        acc[...] += _dot(mid.astype(BF16), w2_ref[...])

    @pl.when(f == pl.num_programs(1) - 1)
    def _():
        o_ref[...] = acc[...]


def _group_ffn(tile_e, tile_n, xs, w1, w3, w2, TMG):
    P, D = xs.shape
    F = w1.shape[2]
    TF = _ffn_hidden_tile(F)
    nF = F // TF
    fblk = lambda i, f, tn: jnp.where(tn[i] > 0, f, nF - 1)
    grid_spec = pltpu.PrefetchScalarGridSpec(
        num_scalar_prefetch=2,
        grid=(P // TMG, nF),
        in_specs=[pl.BlockSpec((TMG, D), lambda i, f, te, tn: (i, 0)),
                  pl.BlockSpec((None, D, TF), lambda i, f, te, tn: (te[i], 0, fblk(i, f, tn))),
                  pl.BlockSpec((None, D, TF), lambda i, f, te, tn: (te[i], 0, fblk(i, f, tn))),
                  pl.BlockSpec((None, TF, D), lambda i, f, te, tn: (te[i], fblk(i, f, tn), 0))],
        out_specs=pl.BlockSpec((TMG, D), lambda i, f, te, tn: (i, 0)),
        scratch_shapes=[pltpu.VMEM((TMG, D), F32)])
    return pl.pallas_call(
        _group_ffn_kernel,
        out_shape=jax.ShapeDtypeStruct((P, D), F32),
        grid_spec=grid_spec,
        compiler_params=_params(("arbitrary", "arbitrary"), 56),
        name="moe_group_ffn",
    )(tile_e, tile_n, xs, w1, w3, w2)


def _combine_kernel(len_ref, off_ref, x_ref, lsr_ref, lsc_ref, gt_ref, fg_ref, ys_ref, o_ref, yt, sem,
                    *, E, final_norm):
    i = pl.program_id(0)
    TM = x_ref.shape[0]
    _, LS, D = yt.shape
    slot = lax.rem(i, 2)

    def fetch(tile, s):
        yt[s, TOP_K * TM:LS, :] = jnp.zeros((LS - TOP_K * TM, D), F32)
        _run_dmas(len_ref, off_ref, tile, E, TM, yt.at[s], ys_ref, sem.at[s], False, False)

    @pl.when(i == 0)
    def _():
        fetch(0, 0)

    @pl.when(i + 1 < pl.num_programs(0))
    def _():
        fetch(i + 1, 1 - slot)

    lsr = lsr_ref[...]
    gt = gt_ref[...]
    js = lax.broadcasted_iota(jnp.int32, (LS, TM), 0)
    lane = lax.broadcasted_iota(jnp.int32, (TM, LANES), 1)
    gs = None
    for k in range(TOP_K):
        t1, t2, t3 = (t.astype(F32) for t in _split3(gt[:, k:k + 1]))
        terms = jnp.where(lane == 0, t1, jnp.where(lane == 1, t2, jnp.where(lane == 2, t3, 0.0)))
        pk = jnp.where(js == lsr[TOP_K + k:TOP_K + k + 1, :], 1.0, 0.0).astype(BF16)
        gk = jnp.sum(_dot(pk, terms.astype(BF16)), axis=1, keepdims=True)
        gs = gk if gs is None else gs + gk
    _run_dmas(len_ref, off_ref, i, E, TM, yt.at[slot], ys_ref, sem.at[slot], False, True)
    z = yt[slot] * gs
    zh = z.astype(BF16)
    zl = (z - zh.astype(F32)).astype(BF16)
    lsc = lsc_ref[...]
    jt = lax.broadcasted_iota(jnp.int32, (TM, LS), 1)
    pt = None
    for k in range(TOP_K):
        ok = jnp.where(jt == lsc[:, TOP_K + k:TOP_K + k + 1], 1.0, 0.0)
        pt = ok if pt is None else pt + ok
    pt = pt.astype(BF16)
    xo = x_ref[...] + (_dot(pt, zh) + _dot(pt, zl))
    o_ref[...] = _rms(xo, fg_ref[...]) if final_norm else xo


def _combine(run_len, run_off, x2d, idx, idx_t, gates_t, final_g, ys, E, final_norm):
    T, D = x2d.shape
    TM = min(ROUTE_TM, T)
    LS = TOP_K * TM + E * RUN_ALIGN
    grid_spec = pltpu.PrefetchScalarGridSpec(
        num_scalar_prefetch=2,
        grid=(T // TM,),
        in_specs=[pl.BlockSpec((TM, D), lambda i, *_: (i, 0)),
                  pl.BlockSpec((SUBLANES, TM), lambda i, *_: (0, i)),
                  pl.BlockSpec((TM, SUBLANES), lambda i, *_: (i, 0)),
                  pl.BlockSpec((TM, SUBLANES), lambda i, *_: (i, 0)),
                  pl.BlockSpec((1, D), lambda i, *_: (0, 0)),
                  pl.BlockSpec(memory_space=pl.ANY)],
        out_specs=pl.BlockSpec((TM, D), lambda i, *_: (i, 0)),
        scratch_shapes=[pltpu.VMEM((2, LS, D), F32), pltpu.SemaphoreType.DMA((2,))])
    return pl.pallas_call(
        functools.partial(_combine_kernel, E=E, final_norm=final_norm),
        out_shape=jax.ShapeDtypeStruct((T, D), F32),
        grid_spec=grid_spec,
        compiler_params=_params(("arbitrary",), 48),
        name="moe_combine",
    )(run_len, run_off, x2d, idx, idx_t, gates_t, final_g, ys)


def _moe(x2d, norm_g, router_w, router_b, w1, w3, w2, final_g, final_norm):
    T, D = x2d.shape
    E = router_w.shape[1]
    TMG = 512
    nT = T // min(ROUTE_TM, T)
    rw = jnp.pad(router_w, ((0, 0), (0, LANES - E)))
    rw_hi = rw.astype(BF16)
    rw_lo = (rw - rw_hi.astype(F32)).astype(BF16)
    idx, gates, run_len, run_off, tot = _route(x2d, norm_g, rw_hi, rw_lo, router_b.reshape(E, 1), E)
    tot = tot[:, 0]
    padded = ((tot + TMG - 1) // TMG) * TMG
    pend = jnp.cumsum(padded)
    pstart = pend - padded
    run_len = run_len[:, 0]
    run_off = (run_off[:, 0].reshape(nT, E) + pstart[None, :]).reshape(nT * E)
    P = -(-(T * TOP_K + nT * E * (RUN_ALIGN - 1) + E * (TMG - 1)) // TMG) * TMG
    tile_start = jnp.arange(P // TMG, dtype=jnp.int32) * TMG
    tile_e = jnp.minimum(jnp.sum(tile_start[:, None] >= pend[None, :], axis=1), E - 1).astype(jnp.int32)
    sel = tile_e[:, None] == jnp.arange(E, dtype=jnp.int32)[None, :]
    tile_end = jnp.sum(jnp.where(sel, (pstart + tot)[None, :], 0), axis=1)
    tile_n = jnp.clip(tile_end - tile_start, 0, TMG).astype(jnp.int32)
    xs = _dispatch(run_len, run_off, idx, x2d, norm_g, P, E)
    ys = _group_ffn(tile_e, tile_n, xs, w1, w3, w2, TMG)
    return _combine(run_len, run_off, x2d, idx, idx.T, gates.T, final_g, ys, E, final_norm)


def _final_norm_kernel(x_ref, g_ref, o_ref):
    o_ref[...] = _rms(x_ref[...], g_ref[...])


def _final_norm(x2d, g):
    T, D = x2d.shape
    TM = min(1024, T)
    return pl.pallas_call(
        _final_norm_kernel,
        out_shape=jax.ShapeDtypeStruct((T, D), F32),
        grid=(T // TM,),
        in_specs=[pl.BlockSpec((TM, D), lambda i: (i, 0)), _const_spec((1, D))],
        out_specs=pl.BlockSpec((TM, D), lambda i: (i, 0)),
        compiler_params=_params(("arbitrary",), 32),
        name="final_norm",
    )(x2d, g)


def kernel(x, mem, norm_mix_g, w_in, conv_rg_w, conv_rg_b, rg_w_a, rg_b_a, rg_w_x, rg_b_x, rg_lambda,
           conv_ml_w, conv_ml_b, ml_w_q, ml_w_k, ml_w_v, ml_b_i, ml_b_f, ml_norm_g, mem_norm_g, w_kv,
           w_br_rg, w_br_ml, w_br_xa, b_merge, w_out, norm_ffn_g, ffn_w1, ffn_w3, ffn_w2, router_w,
           router_b, moe_w1, moe_w3, moe_w2, final_norm_g):
    B, S, D = x.shape
    depth = w_in.shape[0]
    d_rg = conv_rg_w.shape[2]
    d_ml = conv_ml_w.shape[2]
    H = ml_w_q.shape[1]
    d_xa = w_kv.shape[2] // 2
    o_ax, o_ay = 0, d_rg
    o_mu, o_mo = 2 * d_rg, 2 * d_rg + d_ml
    o_mi = 2 * d_rg + 2 * d_ml
    o_mf = o_mi + H
    o_q = o_mf + H
    o_g = o_q + d_xa
    assert w_in.shape[2] == o_g + N_BRANCH * D and 2 * H <= SUBLANES

    bf = lambda a: a.astype(BF16)
    row = lambda a: a.reshape(1, -1)
    x2d = x.reshape(B * S, D)
    fg = row(final_norm_g)
    moe_bf16 = []
    for l in range(depth):
        wl = w_in[l]
        w_if = jnp.pad(wl[:, o_mi:o_mi + 2 * H], ((0, 0), (0, LANES - 2 * H)))
        w_ift = jnp.pad(wl[:, o_mi:o_mi + 2 * H].T, ((0, SUBLANES - 2 * H), (0, 0)))
        b_if = jnp.concatenate([ml_b_i[l], ml_b_f[l]])
        y_ml = _mlstm_branch(x2d, B, row(norm_mix_g[l]), bf(wl[:, o_mu:o_mu + d_ml]),
                                   bf(wl[:, o_mo:o_mo + d_ml]), bf(w_if), bf(w_ift),
                                   jnp.pad(b_if, (0, LANES - 2 * H)).reshape(1, LANES),
                                   jnp.pad(b_if, (0, SUBLANES - 2 * H)).reshape(SUBLANES, 1),
                                   conv_ml_w[l], row(conv_ml_b[l]), bf(ml_w_q[l]), bf(ml_w_k[l]),
                                   bf(ml_w_v[l]), row(ml_norm_g[l]))
        y_rg = _rg_branch(x2d.reshape(B, S, D), row(norm_mix_g[l]), bf(wl[:, o_ax:o_ax + d_rg]),
                          bf(wl[:, o_ay:o_ay + d_rg]), conv_rg_w[l], row(conv_rg_b[l]), bf(rg_w_a[l]),
                          row(rg_b_a[l]), bf(rg_w_x[l]), row(rg_b_x[l]), row(rg_lambda[l]))
        kv = _mem_kv(mem, row(mem_norm_g[l]), bf(w_kv[l]))
        x2d = _merge(x2d, B, row(norm_mix_g[l]), bf(wl[:, o_q:o_q + d_xa]), bf(wl[:, o_g:]),
                     row(b_merge[l]), y_rg.reshape(B * S, d_rg), y_ml, kv, bf(w_br_rg[l]),
                     bf(w_br_ml[l]), bf(w_br_xa[l]), bf(w_out[l]))
        j = l // 2
        if l % 2 == 0:
            w1 = bf(ffn_w1[j])
            nxt = [w[j] for w in (moe_w1, moe_w3, moe_w2)] if l + 1 < depth else []
            nxt = nxt if all(_ffn_can_cast(x2d, w1, w) for w in nxt) else []
            x2d, moe_bf16 = _ffn(x2d, row(norm_ffn_g[l]), w1, bf(ffn_w3[j]), bf(ffn_w2[j]), nxt)
        else:
            e1, e3, e2 = moe_bf16 if moe_bf16 else (bf(moe_w1[j]), bf(moe_w3[j]), bf(moe_w2[j]))
            moe_bf16 = []
            x2d = _moe(x2d, row(norm_ffn_g[l]), router_w[j], router_b[j], e1, e3, e2, fg, l == depth - 1)
    if depth % 2 == 1:
        x2d = _final_norm(x2d, fg)
    return x2d.reshape(B, S, D)
```

```python
import functools

import jax
import jax.numpy as jnp
from jax import lax
from jax.experimental import pallas as pl
from jax.experimental.pallas import tpu as pltpu

EPS = 1e-6
RG_C = 8.0
CONV_W = 4
ML_CHUNK = 128
XA_HEADS = 4
TOP_K = 2
N_BRANCH = 3

V7X_VMEM_BYTES = 64 * 1024 * 1024
LANES = 128
SUBLANES = 8

F32 = jnp.float32
BF16 = jnp.bfloat16


def _params(semantics, vmem_mib):
    assert vmem_mib * 1024 * 1024 <= V7X_VMEM_BYTES
    return pltpu.CompilerParams(dimension_semantics=semantics,
                                vmem_limit_bytes=vmem_mib * 1024 * 1024)


def _const_spec(shape):
    nd = len(shape)
    return pl.BlockSpec(shape, lambda *_: (0,) * nd)


def _rms(x, g):
    ms = jnp.mean(x * x, axis=-1, keepdims=True)
    return x * lax.rsqrt(ms + EPS) * g


def _dot(a, b):
    return jnp.dot(a, b, preferred_element_type=F32)


def _dot_nt(a, b):
    return lax.dot_general(a, b, (((1,), (1,)), ((), ())), preferred_element_type=F32)


def _dot_tn(a, b):
    return lax.dot_general(a, b, (((0,), (0,)), ((), ())), preferred_element_type=F32)


def _split3(x):
    h1 = x.astype(BF16)
    r1 = x - h1.astype(F32)
    h2 = r1.astype(BF16)
    h3 = (r1 - h2.astype(F32)).astype(BF16)
    return h1, h2, h3


def _rg_kernel(x_ref, g_ref, wax_ref, way_ref, cw_ref, cb_ref, wa_ref, ba_ref, wx_ref, bx_ref,
               lam_ref, o_ref, axbuf, a_s, b_s, h_s, carry, *, B, TT):
    R = TT * B
    halo = (CONV_W - 1) * B

    @pl.when(pl.program_id(0) == 0)
    def _():
        axbuf[0:halo, :] = jnp.zeros((halo, axbuf.shape[1]), F32)
        carry[...] = jnp.zeros_like(carry)

    x = pltpu.einshape("btd->tbd", x_ref[...]).reshape(R, x_ref.shape[2])
    h = _rms(x, g_ref[...]).astype(BF16)
    G, bi, _ = wa_ref.shape
    rate = -RG_C * jax.nn.softplus(-lam_ref[...])
    cols = [slice(g * bi, (g + 1) * bi) for g in range(G)]
    axbuf[halo:halo + R, cols[0]] = _dot(h, wax_ref[:, cols[0]])
    for g in range(G):
        cs = cols[g]
        if g + 1 < G:
            axbuf[halo:halo + R, cols[g + 1]] = _dot(h, wax_ref[:, cols[g + 1]])
        ay = _dot(h, way_ref[:, cs])
        cw = cw_ref[:, cs]
        xc = cb_ref[:, cs] + cw[0:1, :] * axbuf[0:R, cs]
        for k in range(1, CONV_W):
            xc = xc + cw[k:k + 1, :] * axbuf[k * B:k * B + R, cs]
        axbuf[0:halo, cs] = axbuf[R:R + halo, cs]
        xcb = xc.astype(BF16)
        r = jax.nn.sigmoid(_dot(xcb, wa_ref[g]) + ba_ref[:, cs])
        ig = jax.nn.sigmoid(_dot(xcb, wx_ref[g]) + bx_ref[:, cs])
        log_a = r * rate[:, cs]
        a = jnp.exp(log_a)
        a_s[:, cs] = a
        b_s[:, cs] = jnp.sqrt(-jnp.tanh(log_a) * (a * a + 1.0)) * (ig * xc)

        def step(t, hc, cs=cs):
            off = pl.multiple_of(t * B, B)
            hn = a_s[pl.ds(off, B), cs] * hc + b_s[pl.ds(off, B), cs]
            h_s[pl.ds(off, B), cs] = hn
            return hn

        carry[:, cs] = lax.fori_loop(0, TT, step, carry[:, cs], unroll=True)
        y = (h_s[:, cs] * jax.nn.gelu(ay)).reshape(TT, B, bi)
        o_ref[:, :, cs] = pltpu.einshape("tbc->btc", y).astype(BF16)


def _rg_branch(x3d, norm_g, w_ax, w_ay, conv_w, conv_b, w_a, b_a, w_x, b_x, lam):
    B, S, D = x3d.shape
    C = w_ax.shape[1]
    TT = min(128, S)
    assert S % TT == 0 and B % SUBLANES == 0
    R = TT * B
    halo = (CONV_W - 1) * B
    tile = lambda i: (0, i, 0)
    return pl.pallas_call(
        functools.partial(_rg_kernel, B=B, TT=TT),
        out_shape=jax.ShapeDtypeStruct((B, S, C), BF16),
        grid=(S // TT,),
        in_specs=[pl.BlockSpec((B, TT, D), tile), _const_spec((1, D)), _const_spec((D, C)),
                  _const_spec((D, C)), _const_spec((CONV_W, C)), _const_spec((1, C)),
                  _const_spec(w_a.shape), _const_spec((1, C)), _const_spec(w_x.shape),
                  _const_spec((1, C)), _const_spec((1, C))],
        out_specs=pl.BlockSpec((B, TT, C), tile),
        scratch_shapes=[pltpu.VMEM((halo + R, C), F32), pltpu.VMEM((R, C), F32),
                        pltpu.VMEM((R, C), F32), pltpu.VMEM((R, C), F32), pltpu.VMEM((B, C), F32)],
        compiler_params=_params(("arbitrary",), 48),
        name="rg_branch",
    )(x3d, norm_g, w_ax, w_ay, conv_w, conv_b, w_a, b_a, w_x, b_x, lam)


def _alternate(first, second):
    result = None
    live = [True, True]
    while any(live):
        for n, gen in enumerate((first, second)):
            if live[n]:
                try:
                    next(gen)
                except StopIteration as stop:
                    live[n] = False
                    if n == 1:
                        result = stop.value
    return result


def _mlstm_kernel(x_ref, g_ref, wmu_ref, wmo_ref, wif_ref, wift_ref, bifc_ref, bifr_ref, cw_ref,
                  cb_ref, wq_ref, wk_ref, wv_ref, ng_ref, o_ref, ubuf, q_s, k_s, v_s, og_s, ifc_s, ifr_s,
                  lfc_s, lfr_s, c_st, n_st, m_st, *, TS, nS):
    H, d, _ = wq_ref.shape
    L = ML_CHUNK
    pad = SUBLANES
    tail = CONV_W - 1
    g = pl.program_id(0)

    @pl.when(g == 0)
    def _():
        for r in (q_s, k_s, v_s, og_s, ifc_s, ifr_s, lfc_s, lfr_s):
            r[...] = jnp.zeros_like(r)

    @pl.when(lax.rem(g, nS) == 0)
    def _():
        ubuf[0:pad, :] = jnp.zeros((pad, ubuf.shape[1]), F32)

    @pl.when((g == 0) | (lax.rem(g + (nS - 1), nS) == 0))
    def _():
        c_st[...] = jnp.zeros_like(c_st)
        n_st[...] = jnp.zeros_like(n_st)
        m_st[...] = jnp.zeros_like(m_st)

    if_c = ifc_s[...]
    if_r = ifr_s[...]
    lf_c = lfc_s[...]
    lf_r = lfr_s[...]
    ri = lax.broadcasted_iota(jnp.int32, (L, L), 0)
    ci = lax.broadcasted_iota(jnp.int32, (L, L), 1)
    causal = ci <= ri
    tri_l = jnp.where(causal, 1.0, 0.0).astype(BF16)
    tri_u = jnp.where(ri <= ci, 1.0, 0.0).astype(BF16)
    ng = ng_ref[...]
    nck = TS // L
    bcs = [sum(_dot(tri_l, p) for p in _split3(lf_c[ck * L:(ck + 1) * L, :])) for ck in range(nck)]
    brs = [sum(_dot(p, tri_u) for p in _split3(lf_r[:, ck * L:(ck + 1) * L])) for ck in range(nck)]
    h = _rms(x_ref[...], g_ref[...]).astype(BF16)

    def wide(t):
        return jnp.concatenate([t] * (d // LANES), axis=1)

    def recurrence(hd):
        cs = slice(hd * d, (hd + 1) * d)
        for ck in range(nck):
            r0 = ck * L
            bc, br = bcs[ck], brs[ck]
            b_col = jnp.broadcast_to(bc[:, H + hd:H + hd + 1], (L, LANES))
            i_col = jnp.broadcast_to(if_c[r0:r0 + L, hd:hd + 1], (L, LANES))
            b_row = br[H + hd:H + hd + 1, :]
            i_row = if_r[hd:hd + 1, r0:r0 + L]
            m = m_st[hd][0:1, :]
            qb = q_s[r0:r0 + L, cs]
            kb = k_s[r0:r0 + L, cs]
            vb = v_s[r0:r0 + L, cs]
            gg = b_col + m
            dm = jnp.where(causal, b_col - b_row + i_row, -jnp.inf)
            yield
            m_row = jnp.maximum(gg, jnp.max(dm, axis=-1, keepdims=True))
            s_qk = _dot_nt(qb, kb)
            yield
            w = jnp.exp(dm - m_row) * s_qk
            inter = jnp.exp(gg - m_row)
            cmat = c_st[hd]
            nvec = n_st[hd]
            yield
            num = wide(inter) * _dot(qb, cmat.astype(BF16)) + _dot(w.astype(BF16), vb)
            yield
            qf, kf, vf = qb.astype(F32), kb.astype(F32), vb.astype(F32)
            den = inter * jnp.sum(qf * nvec, axis=-1, keepdims=True) + jnp.sum(w, axis=-1, keepdims=True)
            hh = num * wide(1.0 / jnp.maximum(jnp.abs(den), jnp.exp(-m_row)))
            yield
            b_last = b_col[L - 1:L, :]
            dl = b_last - b_col + i_col
            m_new = jnp.maximum(b_last + m, jnp.max(dl, axis=0, keepdims=True))
            decay = jnp.exp(b_last + m - m_new)
            wl = jnp.exp(dl - m_new)
            yield
            c_st[hd] = wide(decay) * cmat + _dot_tn(kb, (wide(wl) * vf).astype(BF16))
            n_st[hd] = wide(decay) * nvec + jnp.sum(wide(wl) * kf, axis=0, keepdims=True)
            m_st[hd] = jnp.broadcast_to(m_new, m_st.shape[1:])
            yield
            y = og_s[r0:r0 + L, cs] * hh
            y = y * lax.rsqrt(jnp.mean(y * y, axis=-1, keepdims=True) + EPS)
            o_ref[r0:r0 + L, cs] = (y * ng[:, cs]).astype(BF16)
            yield

    def projection(hd):
        cs = slice(hd * d, (hd + 1) * d)
        u = _dot(h, wmu_ref[:, cs])
        ubuf[pad:pad + TS, cs] = u
        yield
        mo = _dot(h, wmo_ref[:, cs])
        yield
        cw = cw_ref[:, cs]
        c = cb_ref[:, cs] + cw[0:1, :] * ubuf[pad - tail:pad - tail + TS, cs]
        for k in range(1, CONV_W):
            c = c + cw[k:k + 1, :] * ubuf[pad - tail + k:pad - tail + k + TS, cs]
            yield
        ubuf[0:pad, cs] = ubuf[TS:TS + pad, cs]
        cb16 = jax.nn.silu(c).astype(BF16)
        yield
        q = _dot(cb16, wq_ref[hd]).astype(BF16)
        yield
        k_ = (_dot(cb16, wk_ref[hd]) * (d ** -0.5)).astype(BF16)
        yield
        v = _dot(u.astype(BF16), wv_ref[hd]).astype(BF16)
        yield
        og = jax.nn.sigmoid(mo)
        yield
        return q, k_, v, og

    for hd in range(H):
        cs = slice(hd * d, (hd + 1) * d)
        q, k_, v, og = _alternate(recurrence(hd), projection(hd))
        q_s[:, cs] = q
        k_s[:, cs] = k_
        v_s[:, cs] = v
        og_s[:, cs] = og
    new_if_c = _dot(h, wif_ref[...]) + bifc_ref[...]
    new_if_r = _dot_nt(wift_ref[...], h) + bifr_ref[...]
    ifc_s[...] = new_if_c
    ifr_s[...] = new_if_r
    lfc_s[...] = jax.nn.log_sigmoid(new_if_c)
    lfr_s[...] = jax.nn.log_sigmoid(new_if_r)


def _mlstm_branch(x2d, B, norm_g, w_mu, w_mo, w_if, w_ift, b_if_c, b_if_r, conv_w, conv_b,
                  w_q, w_k, w_v, ml_norm_g):
    T, D = x2d.shape
    S = T // B
    C = w_mu.shape[1]
    H, d, _ = w_q.shape
    TS = min(512, S)
    assert S % TS == 0 and TS % ML_CHUNK == 0 and d % LANES == 0
    nS = S // TS
    G = B * nS
    return pl.pallas_call(
        functools.partial(_mlstm_kernel, TS=TS, nS=nS),
        out_shape=jax.ShapeDtypeStruct((T, C), BF16),
        grid=(G + 1,),
        in_specs=[pl.BlockSpec((TS, D), lambda g: (jnp.minimum(g, G - 1), 0)), _const_spec((1, D)),
                  _const_spec((D, C)), _const_spec((D, C)), _const_spec((D, LANES)),
                  _const_spec((SUBLANES, D)), _const_spec((1, LANES)), _const_spec((SUBLANES, 1)),
                  _const_spec((CONV_W, C)), _const_spec((1, C)), _const_spec(w_q.shape),
                  _const_spec(w_k.shape), _const_spec(w_v.shape), _const_spec((1, C))],
        out_specs=pl.BlockSpec((TS, C), lambda g: (jnp.maximum(g - 1, 0), 0)),
        scratch_shapes=[pltpu.VMEM((SUBLANES + TS, C), F32), pltpu.VMEM((TS, C), BF16),
                        pltpu.VMEM((TS, C), BF16), pltpu.VMEM((TS, C), BF16), pltpu.VMEM((TS, C), F32),
                        pltpu.VMEM((TS, LANES), F32), pltpu.VMEM((SUBLANES, TS), F32),
                        pltpu.VMEM((TS, LANES), F32), pltpu.VMEM((SUBLANES, TS), F32),
                        pltpu.VMEM((H, d, d), F32), pltpu.VMEM((H, 1, d), F32),
                        pltpu.VMEM((H, SUBLANES, LANES), F32)],
        compiler_params=_params(("arbitrary",), 48),
        name="mlstm_branch",
    )(x2d, norm_g, w_mu, w_mo, w_if, w_ift, b_if_c, b_if_r, conv_w, conv_b, w_q, w_k, w_v, ml_norm_g)


def _kv_kernel(mem_ref, g_ref, w_ref, o_ref):
    o_ref[...] = _dot(_rms(mem_ref[...], g_ref[...]).astype(BF16), w_ref[...]).astype(BF16)


def _mem_kv(mem, g, w_kv):
    B, M, D = mem.shape
    N = w_kv.shape[1]
    return pl.pallas_call(
        _kv_kernel,
        out_shape=jax.ShapeDtypeStruct((B, M, N), BF16),
        grid=(B,),
        in_specs=[pl.BlockSpec((None, M, D), lambda b: (b, 0, 0)), _const_spec((1, D)),
                  _const_spec((D, N))],
        out_specs=pl.BlockSpec((None, M, N), lambda b: (b, 0, 0)),
        compiler_params=_params(("arbitrary",), 32),
        name="mem_kv",
    )(mem, g, w_kv)


def _merge_kernel(x_ref, g_ref, wq_ref, wg_ref, bm_ref, yrg_ref, yml_ref, kv_ref, wrg_ref, wml_ref,
                  wxa_ref, wo_ref, o_ref):
    x = x_ref[...]
    D = x.shape[1]
    h = _rms(x, g_ref[...]).astype(BF16)
    q = _dot(h, wq_ref[...]).astype(BF16)
    dxa = q.shape[1]
    dh = dxa // XA_HEADS
    heads = []
    for hd in range(XA_HEADS):
        kh = kv_ref[:, hd * dh:(hd + 1) * dh]
        vh = kv_ref[:, dxa + hd * dh:dxa + (hd + 1) * dh]
        s = _dot_nt(q[:, hd * dh:(hd + 1) * dh], kh) * (dh ** -0.5)
        e = jnp.exp(s - jnp.max(s, axis=-1, keepdims=True))
        p = e / jnp.sum(e, axis=-1, keepdims=True)
        heads.append(_dot(p.astype(BF16), vh))
    y_xa = jnp.concatenate(heads, axis=1).astype(BF16)

    def gate(k):
        return jax.nn.sigmoid(_dot(h, wg_ref[:, k * D:(k + 1) * D]) + bm_ref[:, k * D:(k + 1) * D])

    merged = gate(0) * _dot(yrg_ref[...], wrg_ref[...])
    merged = merged + gate(1) * _dot(yml_ref[...], wml_ref[...])
    merged = merged + gate(2) * _dot(y_xa, wxa_ref[...])
    o_ref[...] = x + _dot(merged.astype(BF16), wo_ref[...])


def _merge(x2d, B, norm_g, w_q, w_g, b_merge, y_rg, y_ml, kv, w_br_rg, w_br_ml, w_br_xa, w_out):
    T, D = x2d.shape
    S = T // B
    C = y_ml.shape[1]
    M, N = kv.shape[1:]
    TM = min(1024, S)
    assert S % TM == 0
    nS = S // TM
    row = lambda b, s: (b * nS + s, 0)
    one = pl.Buffered(1)
    cspec = lambda shape: pl.BlockSpec(shape, lambda *_: (0,) * len(shape), pipeline_mode=one)
    return pl.pallas_call(
        _merge_kernel,
        out_shape=jax.ShapeDtypeStruct((T, D), F32),
        grid=(B, nS),
        in_specs=[pl.BlockSpec((TM, D), row), cspec((1, D)), cspec(w_q.shape), cspec(w_g.shape),
                  cspec(b_merge.shape), pl.BlockSpec((TM, C), row),
                  pl.BlockSpec((TM, C), row), pl.BlockSpec((None, M, N), lambda b, s: (b, 0, 0)),
                  cspec(w_br_rg.shape), cspec(w_br_ml.shape), cspec(w_br_xa.shape),
                  cspec(w_out.shape)],
        out_specs=pl.BlockSpec((TM, D), row),
        compiler_params=_params(("arbitrary", "arbitrary"), 56),
        name="merge",
    )(x2d, norm_g, w_q, w_g, b_merge, y_rg, y_ml, kv, w_br_rg, w_br_ml, w_br_xa, w_out)


def _ffn_kernel(x_ref, g_ref, w1_ref, w3_ref, w2_ref, *rest, n_cast):
    cast_in, o_ref, cast_out = rest[:n_cast], rest[n_cast], rest[n_cast + 1:2 * n_cast + 1]
    hs, acc = rest[2 * n_cast + 1:]
    f = pl.program_id(1)
    for src, dst in zip(cast_in, cast_out):
        dst[...] = src[...].astype(BF16)

    @pl.when(f == 0)
    def _():
        hs[...] = _rms(x_ref[...], g_ref[...]).astype(BF16)
        acc[...] = jnp.zeros_like(acc)

    hb = hs[...]
    mid = jax.nn.silu(_dot(hb, w1_ref[...])) * _dot(hb, w3_ref[...])
    acc[...] += _dot(mid.astype(BF16), w2_ref[...])

    @pl.when(f == pl.num_programs(1) - 1)
    def _():
        o_ref[...] = x_ref[...] + acc[...]


FFN_ROW_TILE = 512


def _ffn_hidden_tile(F):
    for tf in (1792, 1024, 512, 256):
        if F % tf == 0:
            return tf
    return F


def _ffn_can_cast(x2d, w1, t):
    steps = (x2d.shape[0] // min(FFN_ROW_TILE, x2d.shape[0])) * (w1.shape[1] // _ffn_hidden_tile(w1.shape[1]))
    rows = t.size // t.shape[-1]
    return rows % steps == 0 and (rows // steps) % (2 * SUBLANES) == 0


def _ffn(x2d, norm_g, w1, w3, w2, cast=()):
    T, D = x2d.shape
    F = w1.shape[1]
    TM = min(FFN_ROW_TILE, T)
    TF = _ffn_hidden_tile(F)
    assert T % TM == 0
    nF = F // TF
    steps = (T // TM) * nF
    cast2d = [t.reshape(-1, t.shape[-1]) for t in cast]
    cast_specs = [pl.BlockSpec((t.shape[0] // steps, t.shape[1]), lambda i, f: (i * nF + f, 0))
                  for t in cast2d]
    out = pl.pallas_call(
        functools.partial(_ffn_kernel, n_cast=len(cast)),
        out_shape=(jax.ShapeDtypeStruct((T, D), F32),
                   *[jax.ShapeDtypeStruct(t.shape, BF16) for t in cast2d]),
        grid=(T // TM, nF),
        in_specs=[pl.BlockSpec((TM, D), lambda i, f: (i, 0)), _const_spec((1, D)),
                  pl.BlockSpec((D, TF), lambda i, f: (0, f)), pl.BlockSpec((D, TF), lambda i, f: (0, f)),
                  pl.BlockSpec((TF, D), lambda i, f: (f, 0)), *cast_specs],
        out_specs=(pl.BlockSpec((TM, D), lambda i, f: (i, 0)), *cast_specs),
        scratch_shapes=[pltpu.VMEM((TM, D), BF16), pltpu.VMEM((TM, D), F32)],
        compiler_params=_params(("arbitrary", "arbitrary"), 56),
        name="ffn_dense",
    )(x2d, norm_g, w1, w3, w2, *cast2d)
    return out[0], [o.reshape(t.shape) for o, t in zip(out[1:], cast)]


ROUTE_TM = 512
RUN_ALIGN = 16


def _route_kernel(x_ref, g_ref, whi_ref, wlo_ref, rb_ref, idx_ref, gate_ref, len_ref, off_ref,
                  tot_ref, run_s, *, E):
    @pl.when(pl.program_id(0) == 0)
    def _():
        run_s[...] = jnp.zeros_like(run_s)

    h2 = _rms(x_ref[...], g_ref[...])
    TM = h2.shape[0]
    hi = h2.astype(BF16)
    lo = (h2 - hi.astype(F32)).astype(BF16)
    logits = _dot(hi, whi_ref[...]) + (_dot(lo, whi_ref[...]) + _dot(hi, wlo_ref[...]))
    lt = logits.T[0:E, :] + rb_ref[...]
    ie = lax.broadcasted_iota(jnp.int32, (E, TM), 0)
    m1 = jnp.max(lt, axis=0, keepdims=True)
    i1 = jnp.min(jnp.where(lt == m1, ie, E), axis=0, keepdims=True)
    l2 = jnp.where(ie == i1, -jnp.inf, lt)
    m2 = jnp.max(l2, axis=0, keepdims=True)
    i2 = jnp.min(jnp.where(l2 == m2, ie, E), axis=0, keepdims=True)
    ex = jnp.exp(m2 - m1)
    g1 = 1.0 / (1.0 + ex)
    g2 = ex / (1.0 + ex)
    oh1 = jnp.where(ie == i1, 1.0, 0.0)
    oh2 = jnp.where(ie == i2, 1.0, 0.0)
    oh = oh1 + oh2
    ri = lax.broadcasted_iota(jnp.int32, (TM, TM), 0)
    ci = lax.broadcasted_iota(jnp.int32, (TM, TM), 1)
    upper = jnp.where(ri < ci, 1.0, 0.0).astype(BF16)
    excl = _dot(oh.astype(BF16), upper)
    cnt = jnp.sum(oh, axis=1, keepdims=True).astype(jnp.int32)
    run_len = jnp.broadcast_to(((cnt + (RUN_ALIGN - 1)) // RUN_ALIGN) * RUN_ALIGN, (E, LANES))
    iec = lax.broadcasted_iota(jnp.int32, (E, LANES), 0)
    run_start = jnp.zeros((E, LANES), jnp.int32)
    for e in range(E - 1):
        run_start = run_start + jnp.where(iec > e, run_len[e:e + 1, :], 0)
    slot = run_start[:, 0:1].astype(F32) + excl
    s1 = jnp.sum(oh1 * slot, axis=0, keepdims=True).astype(jnp.int32)
    s2 = jnp.sum(oh2 * slot, axis=0, keepdims=True).astype(jnp.int32)
    len_ref[...] = run_len
    off_ref[...] = run_s[...]
    run_s[...] = run_s[...] + run_len
    tot_ref[...] = run_s[...]
    row = lax.broadcasted_iota(jnp.int32, (SUBLANES, TM), 0)
    idx_ref[...] = jnp.where(row == 0, i1, jnp.where(row == 1, i2, jnp.where(row == 2, s1,
                             jnp.where(row == 3, s2, 0))))
    gate_ref[...] = jnp.where(row == 0, g1, jnp.where(row == 1, g2, 0.0))


def _route(x2d, norm_g, w_hi, w_lo, rb, E):
    T, D = x2d.shape
    TM = min(ROUTE_TM, T)
    nT = T // TM
    assert T % TM == 0 and E == SUBLANES
    return pl.pallas_call(
        functools.partial(_route_kernel, E=E),
        out_shape=(jax.ShapeDtypeStruct((SUBLANES, T), jnp.int32),
                   jax.ShapeDtypeStruct((SUBLANES, T), F32),
                   jax.ShapeDtypeStruct((nT * E, LANES), jnp.int32),
                   jax.ShapeDtypeStruct((nT * E, LANES), jnp.int32),
                   jax.ShapeDtypeStruct((E, LANES), jnp.int32)),
        grid=(nT,),
        in_specs=[pl.BlockSpec((TM, D), lambda i: (i, 0)), _const_spec((1, D)),
                  _const_spec((D, LANES)), _const_spec((D, LANES)), _const_spec((E, 1))],
        out_specs=(pl.BlockSpec((SUBLANES, TM), lambda i: (0, i)),
                   pl.BlockSpec((SUBLANES, TM), lambda i: (0, i)),
                   pl.BlockSpec((E, LANES), lambda i: (i, 0)),
                   pl.BlockSpec((E, LANES), lambda i: (i, 0)),
                   _const_spec((E, LANES))),
        scratch_shapes=[pltpu.VMEM((E, LANES), jnp.int32)],
        compiler_params=_params(("arbitrary",), 32),
        name="moe_route",
    )(x2d, norm_g, w_hi, w_lo, rb)


def _run_dmas(len_ref, off_ref, i, E, max_len, tile_ref, sorted_ref, sem, to_sorted, wait, packed=True):
    local = 0
    for e in range(E):
        n = len_ref[i * E + e]
        base = off_ref[i * E + e]
        done = 0
        sz = max_len
        while sz >= RUN_ALIGN:
            @pl.when((n & sz) != 0)
            def _(sz=sz, local=local, base=base, done=done):
                t_rows = tile_ref.at[pl.ds(pl.multiple_of(local + done, RUN_ALIGN), sz), :]
                s_rows = sorted_ref.at[pl.ds(pl.multiple_of(base + done, RUN_ALIGN), sz), :]
                cp = (pltpu.make_async_copy(t_rows, s_rows, sem) if to_sorted
                      else pltpu.make_async_copy(s_rows, t_rows, sem))
                if wait:
                    cp.wait()
                else:
                    cp.start()
            done = done + (n & sz)
            sz //= 2
        if packed:
            local = local + n


def _dispatch_kernel(len_ref, off_ref, gap_len_ref, gap_off_ref, tail_ref, x_ref, g_ref, ls_ref, xs_ref, xs_t,
                     zeros, sem, *, E):
    i = pl.program_id(0)
    TM = x_ref.shape[0]
    LS = xs_t.shape[1]
    slot = lax.rem(i, 2)
    h2 = _rms(x_ref[...], g_ref[...]).astype(BF16)
    ls = ls_ref[...]
    j = lax.broadcasted_iota(jnp.int32, (LS, TM), 0)
    onehot = jnp.where(j == ls[2:3, :], 1.0, jnp.where(j == ls[3:4, :], 1.0, 0.0)).astype(BF16)
    xs_t[slot] = _dot(onehot, h2).astype(BF16)

    @pl.when(i > 0)
    def _():
        _run_dmas(len_ref, off_ref, i - 1, E, TM, xs_t.at[1 - slot], xs_ref, sem.at[1 - slot], True, True)

    _run_dmas(len_ref, off_ref, i, E, TM, xs_t.at[slot], xs_ref, sem.at[slot], True, False)

    @pl.when(i == pl.num_programs(0) - 1)
    def _():
        _run_dmas(len_ref, off_ref, i, E, TM, xs_t.at[slot], xs_ref, sem.at[slot], True, True)
        zeros[...] = jnp.zeros_like(zeros)
        tmg = zeros.shape[0]

        def tail_copy(t):
            rows = pl.ds(pl.multiple_of(t * tmg, tmg), tmg)
            return pltpu.make_async_copy(zeros, xs_ref.at[rows, :], sem.at[slot])

        def tail_start(t, c):
            tail_copy(t).start()
            return c

        def tail_wait(t, c):
            tail_copy(t).wait()
            return c

        for wait in (False, True):
            _run_dmas(gap_len_ref, gap_off_ref, 0, E, tmg // 2, zeros, xs_ref, sem.at[slot], True, wait,
                      packed=False)
            lax.fori_loop(tail_ref[0], xs_ref.shape[0] // tmg, tail_wait if wait else tail_start, 0)


def _dispatch(run_len, run_off, gap_len, gap_off, tail_tile, idx, x2d, norm_g, P, E, TMG):
    T, D = x2d.shape
    TM = min(ROUTE_TM, T)
    LS = TOP_K * TM + E * RUN_ALIGN
    grid_spec = pltpu.PrefetchScalarGridSpec(
        num_scalar_prefetch=5,
        grid=(T // TM,),
        in_specs=[pl.BlockSpec((TM, D), lambda i, *_: (i, 0)),
                  pl.BlockSpec((1, D), lambda i, *_: (0, 0)),
                  pl.BlockSpec((SUBLANES, TM), lambda i, *_: (0, i))],
        out_specs=pl.BlockSpec(memory_space=pl.ANY),
        scratch_shapes=[pltpu.VMEM((2, LS, D), BF16), pltpu.VMEM((TMG, D), BF16),
                        pltpu.SemaphoreType.DMA((2,))])
    return pl.pallas_call(
        functools.partial(_dispatch_kernel, E=E),
        out_shape=jax.ShapeDtypeStruct((P, D), BF16),
        grid_spec=grid_spec,
        compiler_params=_params(("arbitrary",), 32),
        name="moe_dispatch",
    )(run_len, run_off, gap_len, gap_off, tail_tile, x2d, norm_g, idx)


def _group_ffn_kernel(te_ref, tn_ref, x_ref, w1_ref, w3_ref, w2_ref, o_ref, acc):
    i = pl.program_id(0)
    f = pl.program_id(1)

    @pl.when(f == 0)
    def _():
        acc[...] = jnp.zeros_like(acc)

    @pl.when(tn_ref[i] > 0)
    def _():
        hb = x_ref[...]
        mid = jax.nn.silu(_dot(hb, w1_ref[...])) * _dot(hb, w3_ref[...])
        acc[...] += _dot(mid.astype(BF16), w2_ref[...])

    @pl.when(f == pl.num_programs(1) - 1)
    def _():
        o_ref[...] = acc[...]


def _group_ffn(tile_e, tile_n, xs, w1, w3, w2, TMG):
    P, D = xs.shape
    F = w1.shape[2]
    TF = _ffn_hidden_tile(F)
    nF = F // TF
    fblk = lambda i, f, tn: jnp.where(tn[i] > 0, f, nF - 1)
    grid_spec = pltpu.PrefetchScalarGridSpec(
        num_scalar_prefetch=2,
        grid=(P // TMG, nF),
        in_specs=[pl.BlockSpec((TMG, D), lambda i, f, te, tn: (jnp.where(tn[i] > 0, i, 0), 0)),
                  pl.BlockSpec((None, D, TF), lambda i, f, te, tn: (te[i], 0, fblk(i, f, tn))),
                  pl.BlockSpec((None, D, TF), lambda i, f, te, tn: (te[i], 0, fblk(i, f, tn))),
                  pl.BlockSpec((None, TF, D), lambda i, f, te, tn: (te[i], fblk(i, f, tn), 0))],
        out_specs=pl.BlockSpec((TMG, D), lambda i, f, te, tn: (i, 0)),
        scratch_shapes=[pltpu.VMEM((TMG, D), F32)])
    return pl.pallas_call(
        _group_ffn_kernel,
        out_shape=jax.ShapeDtypeStruct((P, D), F32),
        grid_spec=grid_spec,
        compiler_params=_params(("arbitrary", "arbitrary"), 56),
        name="moe_group_ffn",
    )(tile_e, tile_n, xs, w1, w3, w2)


def _combine_kernel(len_ref, off_ref, x_ref, lsr_ref, lsc_ref, gt_ref, fg_ref, ys_ref, o_ref, yt, sem,
                    *, E, final_norm):
    i = pl.program_id(0)
    TM = x_ref.shape[0]
    _, LS, D = yt.shape
    slot = lax.rem(i, 2)

    def fetch(tile, s):
        yt[s, TOP_K * TM:LS, :] = jnp.zeros((LS - TOP_K * TM, D), F32)
        _run_dmas(len_ref, off_ref, tile, E, TM, yt.at[s], ys_ref, sem.at[s], False, False)

    @pl.when(i == 0)
    def _():
        fetch(0, 0)

    @pl.when(i + 1 < pl.num_programs(0))
    def _():
        fetch(i + 1, 1 - slot)

    lsr = lsr_ref[...]
    gt = gt_ref[...]
    js = lax.broadcasted_iota(jnp.int32, (LS, TM), 0)
    lane = lax.broadcasted_iota(jnp.int32, (TM, LANES), 1)
    gs = None
    for k in range(TOP_K):
        t1, t2, t3 = (t.astype(F32) for t in _split3(gt[:, k:k + 1]))
        terms = jnp.where(lane == 0, t1, jnp.where(lane == 1, t2, jnp.where(lane == 2, t3, 0.0)))
        pk = jnp.where(js == lsr[TOP_K + k:TOP_K + k + 1, :], 1.0, 0.0).astype(BF16)
        gk = jnp.sum(_dot(pk, terms.astype(BF16)), axis=1, keepdims=True)
        gs = gk if gs is None else gs + gk
    _run_dmas(len_ref, off_ref, i, E, TM, yt.at[slot], ys_ref, sem.at[slot], False, True)
    z = yt[slot] * gs
    zh = z.astype(BF16)
    zl = (z - zh.astype(F32)).astype(BF16)
    lsc = lsc_ref[...]
    jt = lax.broadcasted_iota(jnp.int32, (TM, LS), 1)
    pt = None
    for k in range(TOP_K):
        ok = jnp.where(jt == lsc[:, TOP_K + k:TOP_K + k + 1], 1.0, 0.0)
        pt = ok if pt is None else pt + ok
    pt = pt.astype(BF16)
    xo = x_ref[...] + (_dot(pt, zh) + _dot(pt, zl))
    o_ref[...] = _rms(xo, fg_ref[...]) if final_norm else xo


def _combine(run_len, run_off, x2d, idx, idx_t, gates_t, final_g, ys, E, final_norm):
    T, D = x2d.shape
    TM = min(ROUTE_TM, T)
    LS = TOP_K * TM + E * RUN_ALIGN
    grid_spec = pltpu.PrefetchScalarGridSpec(
        num_scalar_prefetch=2,
        grid=(T // TM,),
        in_specs=[pl.BlockSpec((TM, D), lambda i, *_: (i, 0)),
                  pl.BlockSpec((SUBLANES, TM), lambda i, *_: (0, i)),
                  pl.BlockSpec((TM, SUBLANES), lambda i, *_: (i, 0)),
                  pl.BlockSpec((TM, SUBLANES), lambda i, *_: (i, 0)),
                  pl.BlockSpec((1, D), lambda i, *_: (0, 0)),
                  pl.BlockSpec(memory_space=pl.ANY)],
        out_specs=pl.BlockSpec((TM, D), lambda i, *_: (i, 0)),
        scratch_shapes=[pltpu.VMEM((2, LS, D), F32), pltpu.SemaphoreType.DMA((2,))])
    return pl.pallas_call(
        functools.partial(_combine_kernel, E=E, final_norm=final_norm),
        out_shape=jax.ShapeDtypeStruct((T, D), F32),
        grid_spec=grid_spec,
        compiler_params=_params(("arbitrary",), 48),
        name="moe_combine",
    )(run_len, run_off, x2d, idx, idx_t, gates_t, final_g, ys)


def _moe(x2d, norm_g, router_w, router_b, w1, w3, w2, final_g, final_norm):
    T, D = x2d.shape
    E = router_w.shape[1]
    TMG = 512
    nT = T // min(ROUTE_TM, T)
    rw = jnp.pad(router_w, ((0, 0), (0, LANES - E)))
    rw_hi = rw.astype(BF16)
    rw_lo = (rw - rw_hi.astype(F32)).astype(BF16)
    idx, gates, run_len, run_off, tot = _route(x2d, norm_g, rw_hi, rw_lo, router_b.reshape(E, 1), E)
    tot = tot[:, 0]
    padded = ((tot + TMG - 1) // TMG) * TMG
    pend = jnp.cumsum(padded)
    pstart = pend - padded
    run_len = run_len[:, 0]
    run_off = (run_off[:, 0].reshape(nT, E) + pstart[None, :]).reshape(nT * E)
    P = -(-(T * TOP_K + nT * E * (RUN_ALIGN - 1) + E * (TMG - 1)) // TMG) * TMG
    tile_start = jnp.arange(P // TMG, dtype=jnp.int32) * TMG
    tile_e = jnp.minimum(jnp.sum(tile_start[:, None] >= pend[None, :], axis=1), E - 1).astype(jnp.int32)
    sel = tile_e[:, None] == jnp.arange(E, dtype=jnp.int32)[None, :]
    tile_end = jnp.sum(jnp.where(sel, (pstart + tot)[None, :], 0), axis=1)
    tile_n = jnp.clip(tile_end - tile_start, 0, TMG).astype(jnp.int32)
    xs = _dispatch(run_len, run_off, padded - tot, pstart + tot, pend[E - 1:] // TMG, idx, x2d, norm_g, P,
                   E, TMG)
    ys = _group_ffn(tile_e, tile_n, xs, w1, w3, w2, TMG)
    return _combine(run_len, run_off, x2d, idx, idx.T, gates.T, final_g, ys, E, final_norm)


def _final_norm_kernel(x_ref, g_ref, o_ref):
    o_ref[...] = _rms(x_ref[...], g_ref[...])


def _final_norm(x2d, g):
    T, D = x2d.shape
    TM = min(1024, T)
    return pl.pallas_call(
        _final_norm_kernel,
        out_shape=jax.ShapeDtypeStruct((T, D), F32),
        grid=(T // TM,),
        in_specs=[pl.BlockSpec((TM, D), lambda i: (i, 0)), _const_spec((1, D))],
        out_specs=pl.BlockSpec((TM, D), lambda i: (i, 0)),
        compiler_params=_params(("arbitrary",), 32),
        name="final_norm",
    )(x2d, g)


def kernel(x, mem, norm_mix_g, w_in, conv_rg_w, conv_rg_b, rg_w_a, rg_b_a, rg_w_x, rg_b_x, rg_lambda,
           conv_ml_w, conv_ml_b, ml_w_q, ml_w_k, ml_w_v, ml_b_i, ml_b_f, ml_norm_g, mem_norm_g, w_kv,
           w_br_rg, w_br_ml, w_br_xa, b_merge, w_out, norm_ffn_g, ffn_w1, ffn_w3, ffn_w2, router_w,
           router_b, moe_w1, moe_w3, moe_w2, final_norm_g):
    B, S, D = x.shape
    depth = w_in.shape[0]
    d_rg = conv_rg_w.shape[2]
    d_ml = conv_ml_w.shape[2]
    H = ml_w_q.shape[1]
    d_xa = w_kv.shape[2] // 2
    o_ax, o_ay = 0, d_rg
    o_mu, o_mo = 2 * d_rg, 2 * d_rg + d_ml
    o_mi = 2 * d_rg + 2 * d_ml
    o_mf = o_mi + H
    o_q = o_mf + H
    o_g = o_q + d_xa
    assert w_in.shape[2] == o_g + N_BRANCH * D and 2 * H <= SUBLANES

    bf = lambda a: a.astype(BF16)
    row = lambda a: a.reshape(1, -1)
    x2d = x.reshape(B * S, D)
    fg = row(final_norm_g)
    moe_bf16 = []
    for l in range(depth):
        wl = w_in[l]
        w_if = jnp.pad(wl[:, o_mi:o_mi + 2 * H], ((0, 0), (0, LANES - 2 * H)))
        w_ift = jnp.pad(wl[:, o_mi:o_mi + 2 * H].T, ((0, SUBLANES - 2 * H), (0, 0)))
        b_if = jnp.concatenate([ml_b_i[l], ml_b_f[l]])
        y_ml = _mlstm_branch(x2d, B, row(norm_mix_g[l]), bf(wl[:, o_mu:o_mu + d_ml]),
                                   bf(wl[:, o_mo:o_mo + d_ml]), bf(w_if), bf(w_ift),
                                   jnp.pad(b_if, (0, LANES - 2 * H)).reshape(1, LANES),
                                   jnp.pad(b_if, (0, SUBLANES - 2 * H)).reshape(SUBLANES, 1),
                                   conv_ml_w[l], row(conv_ml_b[l]), bf(ml_w_q[l]), bf(ml_w_k[l]),
                                   bf(ml_w_v[l]), row(ml_norm_g[l]))
        y_rg = _rg_branch(x2d.reshape(B, S, D), row(norm_mix_g[l]), bf(wl[:, o_ax:o_ax + d_rg]),
                          bf(wl[:, o_ay:o_ay + d_rg]), conv_rg_w[l], row(conv_rg_b[l]), bf(rg_w_a[l]),
                          row(rg_b_a[l]), bf(rg_w_x[l]), row(rg_b_x[l]), row(rg_lambda[l]))
        kv = _mem_kv(mem, row(mem_norm_g[l]), bf(w_kv[l]))
        x2d = _merge(x2d, B, row(norm_mix_g[l]), bf(wl[:, o_q:o_q + d_xa]), bf(wl[:, o_g:]),
                     row(b_merge[l]), y_rg.reshape(B * S, d_rg), y_ml, kv, bf(w_br_rg[l]),
                     bf(w_br_ml[l]), bf(w_br_xa[l]), bf(w_out[l]))
        j = l // 2
        if l % 2 == 0:
            w1 = bf(ffn_w1[j])
            nxt = [w[j] for w in (moe_w1, moe_w3, moe_w2)] if l + 1 < depth else []
            nxt = nxt if all(_ffn_can_cast(x2d, w1, w) for w in nxt) else []
            x2d, moe_bf16 = _ffn(x2d, row(norm_ffn_g[l]), w1, bf(ffn_w3[j]), bf(ffn_w2[j]), nxt)
        else:
            e1, e3, e2 = moe_bf16 if moe_bf16 else (bf(moe_w1[j]), bf(moe_w3[j]), bf(moe_w2[j]))
            moe_bf16 = []
            x2d = _moe(x2d, row(norm_ffn_g[l]), router_w[j], router_b[j], e1, e3, e2, fg, l == depth - 1)
    if depth % 2 == 1:
        x2d = _final_norm(x2d, fg)
    return x2d.reshape(B, S, D)
```

```python
import functools

import jax
import jax.numpy as jnp
from jax import lax
from jax.experimental import pallas as pl
from jax.experimental.pallas import tpu as pltpu

EPS = 1e-6
RG_C = 8.0
CONV_W = 4
ML_CHUNK = 128
XA_HEADS = 4
TOP_K = 2
N_BRANCH = 3

V7X_VMEM_BYTES = 64 * 1024 * 1024
LANES = 128
SUBLANES = 8

RG_TIME_TILE = 128
ML_ROW_TILE = 512
MERGE_ROW_TILE = 1024
FFN_ROW_TILE = 512
ROUTE_TM = 512
RUN_ALIGN = 16

F32 = jnp.float32
BF16 = jnp.bfloat16


def _params(semantics, vmem_mib):
    assert vmem_mib * 1024 * 1024 <= V7X_VMEM_BYTES
    return pltpu.CompilerParams(dimension_semantics=semantics,
                                vmem_limit_bytes=vmem_mib * 1024 * 1024)


def _const_spec(shape):
    nd = len(shape)
    return pl.BlockSpec(shape, lambda *_: (0,) * nd)


def _rms(x, g):
    ms = jnp.mean(x * x, axis=-1, keepdims=True)
    return x * lax.rsqrt(ms + EPS) * g


def _dot(a, b):
    return jnp.dot(a, b, preferred_element_type=F32)


def _dot_nt(a, b):
    return lax.dot_general(a, b, (((1,), (1,)), ((), ())), preferred_element_type=F32)


def _dot_tn(a, b):
    return lax.dot_general(a, b, (((0,), (0,)), ((), ())), preferred_element_type=F32)


def _split3(x):
    h1 = x.astype(BF16)
    r1 = x - h1.astype(F32)
    h2 = r1.astype(BF16)
    h3 = (r1 - h2.astype(F32)).astype(BF16)
    return h1, h2, h3


def _rg_kernel(x_ref, g_ref, wax_ref, way_ref, cw_ref, cb_ref, wa_ref, ba_ref, wx_ref, bx_ref,
               lam_ref, o_ref, axbuf, a_s, b_s, h_s, carry, *, B, TT):
    R = TT * B
    halo = (CONV_W - 1) * B

    @pl.when(pl.program_id(0) == 0)
    def _():
        axbuf[0:halo, :] = jnp.zeros((halo, axbuf.shape[1]), F32)
        carry[...] = jnp.zeros_like(carry)

    x = pltpu.einshape("btd->tbd", x_ref[...]).reshape(R, x_ref.shape[2])
    h = _rms(x, g_ref[...]).astype(BF16)
    G, bi, _ = wa_ref.shape
    rate = -RG_C * jax.nn.softplus(-lam_ref[...])
    cols = [slice(g * bi, (g + 1) * bi) for g in range(G)]
    axbuf[halo:halo + R, cols[0]] = _dot(h, wax_ref[:, cols[0]])
    for g in range(G):
        cs = cols[g]
        if g + 1 < G:
            axbuf[halo:halo + R, cols[g + 1]] = _dot(h, wax_ref[:, cols[g + 1]])
        ay = _dot(h, way_ref[:, cs])
        cw = cw_ref[:, cs]
        xc = cb_ref[:, cs] + cw[0:1, :] * axbuf[0:R, cs]
        for k in range(1, CONV_W):
            xc = xc + cw[k:k + 1, :] * axbuf[k * B:k * B + R, cs]
        axbuf[0:halo, cs] = axbuf[R:R + halo, cs]
        xcb = xc.astype(BF16)
        r = jax.nn.sigmoid(_dot(xcb, wa_ref[g]) + ba_ref[:, cs])
        ig = jax.nn.sigmoid(_dot(xcb, wx_ref[g]) + bx_ref[:, cs])
        log_a = r * rate[:, cs]
        a = jnp.exp(log_a)
        a_s[:, cs] = a
        b_s[:, cs] = jnp.sqrt(-jnp.tanh(log_a) * (a * a + 1.0)) * (ig * xc)

        def step(t, hc, cs=cs):
            off = pl.multiple_of(t * B, B)
            hn = a_s[pl.ds(off, B), cs] * hc + b_s[pl.ds(off, B), cs]
            h_s[pl.ds(off, B), cs] = hn
            return hn

        carry[:, cs] = lax.fori_loop(0, TT, step, carry[:, cs], unroll=True)
        y = (h_s[:, cs] * jax.nn.gelu(ay)).reshape(TT, B, bi)
        o_ref[:, :, cs] = pltpu.einshape("tbc->btc", y).astype(BF16)


def _rg_branch(x3d, norm_g, w_ax, w_ay, conv_w, conv_b, w_a, b_a, w_x, b_x, lam):
    B, S, D = x3d.shape
    C = w_ax.shape[1]
    TT = min(RG_TIME_TILE, S)
    assert S % TT == 0 and B % SUBLANES == 0
    R = TT * B
    halo = (CONV_W - 1) * B
    tile = lambda i: (0, i, 0)
    return pl.pallas_call(
        functools.partial(_rg_kernel, B=B, TT=TT),
        out_shape=jax.ShapeDtypeStruct((B, S, C), BF16),
        grid=(S // TT,),
        in_specs=[pl.BlockSpec((B, TT, D), tile), _const_spec((1, D)), _const_spec((D, C)),
                  _const_spec((D, C)), _const_spec((CONV_W, C)), _const_spec((1, C)),
                  _const_spec(w_a.shape), _const_spec((1, C)), _const_spec(w_x.shape),
                  _const_spec((1, C)), _const_spec((1, C))],
        out_specs=pl.BlockSpec((B, TT, C), tile),
        scratch_shapes=[pltpu.VMEM((halo + R, C), F32), pltpu.VMEM((R, C), F32),
                        pltpu.VMEM((R, C), F32), pltpu.VMEM((R, C), F32), pltpu.VMEM((B, C), F32)],
        compiler_params=_params(("arbitrary",), 48),
        name="rg_branch",
    )(x3d, norm_g, w_ax, w_ay, conv_w, conv_b, w_a, b_a, w_x, b_x, lam)


def _alternate(first, second):
    result = None
    live = [True, True]
    while any(live):
        for n, gen in enumerate((first, second)):
            if live[n]:
                try:
                    next(gen)
                except StopIteration as stop:
                    live[n] = False
                    if n == 1:
                        result = stop.value
    return result


def _mlstm_kernel(x_ref, g_ref, wmu_ref, wmo_ref, wif_ref, wift_ref, bifc_ref, bifr_ref, cw_ref,
                  cb_ref, wq_ref, wk_ref, wv_ref, ng_ref, o_ref, ubuf, q_s, k_s, v_s, og_s, ifc_s, ifr_s,
                  lfc_s, lfr_s, c_st, n_st, m_st, *, TS, nS):
    H, d, _ = wq_ref.shape
    L = ML_CHUNK
    pad = SUBLANES
    tail = CONV_W - 1
    g = pl.program_id(0)

    @pl.when(g == 0)
    def _():
        for r in (q_s, k_s, v_s, og_s, ifc_s, ifr_s, lfc_s, lfr_s):
            r[...] = jnp.zeros_like(r)

    @pl.when(lax.rem(g, nS) == 0)
    def _():
        ubuf[...] = jnp.zeros_like(ubuf)

    @pl.when((g == 0) | (lax.rem(g + (nS - 1), nS) == 0))
    def _():
        c_st[...] = jnp.zeros_like(c_st)
        n_st[...] = jnp.zeros_like(n_st)
        m_st[...] = jnp.zeros_like(m_st)

    if_c = ifc_s[...]
    if_r = ifr_s[...]
    lf_c = lfc_s[...]
    lf_r = lfr_s[...]
    ri = lax.broadcasted_iota(jnp.int32, (L, L), 0)
    ci = lax.broadcasted_iota(jnp.int32, (L, L), 1)
    causal = ci <= ri
    tri_l = jnp.where(causal, 1.0, 0.0).astype(BF16)
    tri_u = jnp.where(ri <= ci, 1.0, 0.0).astype(BF16)
    ng = ng_ref[...]
    nck = TS // L
    h = _rms(x_ref[...], g_ref[...]).astype(BF16)
    us = [_dot(h, wmu_ref[:, 0:d])] + [None] * (H - 1)
    bcs = [sum(_dot(tri_l, p) for p in _split3(lf_c[ck * L:(ck + 1) * L, :])) for ck in range(nck)]
    brs = [sum(_dot(p, tri_u) for p in _split3(lf_r[:, ck * L:(ck + 1) * L])) for ck in range(nck)]

    def wide(t):
        return jnp.concatenate([t] * (d // LANES), axis=1)

    def recurrence(hd):
        cs = slice(hd * d, (hd + 1) * d)
        for ck in range(nck):
            r0 = ck * L
            bc, br = bcs[ck], brs[ck]
            b_col = jnp.broadcast_to(bc[:, H + hd:H + hd + 1], (L, LANES))
            i_col = jnp.broadcast_to(if_c[r0:r0 + L, hd:hd + 1], (L, LANES))
            b_row = br[H + hd:H + hd + 1, :]
            i_row = if_r[hd:hd + 1, r0:r0 + L]
            m = m_st[hd][0:1, :]
            qb = q_s[r0:r0 + L, cs]
            kb = k_s[r0:r0 + L, cs]
            vb = v_s[r0:r0 + L, cs]
            gg = b_col + m
            dm = jnp.where(causal, b_col - b_row + i_row, -jnp.inf)
            yield
            m_row = jnp.maximum(gg, jnp.max(dm, axis=-1, keepdims=True))
            s_qk = _dot_nt(qb, kb)
            yield
            w = jnp.exp(dm - m_row) * s_qk
            inter = jnp.exp(gg - m_row)
            cmat = c_st[hd]
            nvec = n_st[hd]
            yield
            num = wide(inter) * _dot(qb, cmat.astype(BF16)) + _dot(w.astype(BF16), vb)
            yield
            qf, kf, vf = qb.astype(F32), kb.astype(F32), vb.astype(F32)
            den = inter * jnp.sum(qf * nvec, axis=-1, keepdims=True) + jnp.sum(w, axis=-1, keepdims=True)
            hh = num * wide(1.0 / jnp.maximum(jnp.abs(den), jnp.exp(-m_row)))
            yield
            b_last = b_col[L - 1:L, :]
            dl = b_last - b_col + i_col
            m_new = jnp.maximum(b_last + m, jnp.max(dl, axis=0, keepdims=True))
            decay = jnp.exp(b_last + m - m_new)
            wl = jnp.exp(dl - m_new)
            yield
            c_st[hd] = wide(decay) * cmat + _dot_tn(kb, (wide(wl) * vf).astype(BF16))
            n_st[hd] = wide(decay) * nvec + jnp.sum(wide(wl) * kf, axis=0, keepdims=True)
            m_st[hd] = jnp.broadcast_to(m_new, m_st.shape[1:])
            yield
            y = og_s[r0:r0 + L, cs] * hh
            y = y * lax.rsqrt(jnp.mean(y * y, axis=-1, keepdims=True) + EPS)
            o_ref[r0:r0 + L, cs] = (y * ng[:, cs]).astype(BF16)
            yield

    def projection(hd):
        cs = slice(hd * d, (hd + 1) * d)
        u = us[hd]
        mo = _dot(h, wmo_ref[:, cs])
        yield
        if hd + 1 < H:
            us[hd + 1] = _dot(h, wmu_ref[:, (hd + 1) * d:(hd + 2) * d])
            yield
        cw = cw_ref[:, cs]
        halo = ubuf[:, cs]
        rowi = lax.broadcasted_iota(jnp.int32, (pad, d), 0)
        c = cb_ref[:, cs] + cw[tail:tail + 1, :] * u
        for j in range(1, CONV_W):
            sh = pltpu.roll(u, j, axis=0)
            head = jnp.where(rowi < j, pltpu.roll(halo, j, axis=0), sh[0:pad, :])
            sh = jnp.concatenate([head, sh[pad:, :]], axis=0)
            c = c + cw[tail - j:tail - j + 1, :] * sh
            yield
        ubuf[:, cs] = u[TS - pad:TS, :]
        cb16 = jax.nn.silu(c).astype(BF16)
        yield
        q = _dot(cb16, wq_ref[hd]).astype(BF16)
        yield
        k_ = (_dot(cb16, wk_ref[hd]) * (d ** -0.5)).astype(BF16)
        yield
        v = _dot(u.astype(BF16), wv_ref[hd]).astype(BF16)
        yield
        og = jax.nn.sigmoid(mo)
        yield
        return q, k_, v, og

    for hd in range(H):
        cs = slice(hd * d, (hd + 1) * d)
        q, k_, v, og = _alternate(recurrence(hd), projection(hd))
        q_s[:, cs] = q
        k_s[:, cs] = k_
        v_s[:, cs] = v
        og_s[:, cs] = og
    new_if_c = _dot(h, wif_ref[...]) + bifc_ref[...]
    new_if_r = _dot_nt(wift_ref[...], h) + bifr_ref[...]
    ifc_s[...] = new_if_c
    ifr_s[...] = new_if_r
    lfc_s[...] = jax.nn.log_sigmoid(new_if_c)
    lfr_s[...] = jax.nn.log_sigmoid(new_if_r)


def _mlstm_branch(x2d, B, norm_g, w_mu, w_mo, w_if, w_ift, b_if_c, b_if_r, conv_w, conv_b,
                  w_q, w_k, w_v, ml_norm_g):
    T, D = x2d.shape
    S = T // B
    C = w_mu.shape[1]
    H, d, _ = w_q.shape
    TS = min(ML_ROW_TILE, S)
    assert S % TS == 0 and TS % ML_CHUNK == 0 and d % LANES == 0
    nS = S // TS
    G = B * nS
    return pl.pallas_call(
        functools.partial(_mlstm_kernel, TS=TS, nS=nS),
        out_shape=jax.ShapeDtypeStruct((T, C), BF16),
        grid=(G + 1,),
        in_specs=[pl.BlockSpec((TS, D), lambda g: (jnp.minimum(g, G - 1), 0)), _const_spec((1, D)),
                  _const_spec((D, C)), _const_spec((D, C)), _const_spec((D, LANES)),
                  _const_spec((SUBLANES, D)), _const_spec((1, LANES)), _const_spec((SUBLANES, 1)),
                  _const_spec((CONV_W, C)), _const_spec((1, C)), _const_spec(w_q.shape),
                  _const_spec(w_k.shape), _const_spec(w_v.shape), _const_spec((1, C))],
        out_specs=pl.BlockSpec((TS, C), lambda g: (jnp.maximum(g - 1, 0), 0)),
        scratch_shapes=[pltpu.VMEM((SUBLANES, C), F32), pltpu.VMEM((TS, C), BF16),
                        pltpu.VMEM((TS, C), BF16), pltpu.VMEM((TS, C), BF16), pltpu.VMEM((TS, C), F32),
                        pltpu.VMEM((TS, LANES), F32), pltpu.VMEM((SUBLANES, TS), F32),
                        pltpu.VMEM((TS, LANES), F32), pltpu.VMEM((SUBLANES, TS), F32),
                        pltpu.VMEM((H, d, d), F32), pltpu.VMEM((H, 1, d), F32),
                        pltpu.VMEM((H, SUBLANES, LANES), F32)],
        compiler_params=_params(("arbitrary",), 48),
        name="mlstm_branch",
    )(x2d, norm_g, w_mu, w_mo, w_if, w_ift, b_if_c, b_if_r, conv_w, conv_b, w_q, w_k, w_v, ml_norm_g)


def _kv_kernel(mem_ref, g_ref, w_ref, o_ref):
    o_ref[...] = _dot(_rms(mem_ref[...], g_ref[...]).astype(BF16), w_ref[...]).astype(BF16)


def _mem_kv(mem, g, w_kv):
    B, M, D = mem.shape
    N = w_kv.shape[1]
    return pl.pallas_call(
        _kv_kernel,
        out_shape=jax.ShapeDtypeStruct((B, M, N), BF16),
        grid=(B,),
        in_specs=[pl.BlockSpec((None, M, D), lambda b: (b, 0, 0)), _const_spec((1, D)),
                  _const_spec((D, N))],
        out_specs=pl.BlockSpec((None, M, N), lambda b: (b, 0, 0)),
        compiler_params=_params(("arbitrary",), 32),
        name="mem_kv",
    )(mem, g, w_kv)


def _merge_kernel(x_ref, g_ref, wq_ref, wg_ref, bm_ref, yrg_ref, yml_ref, kv_ref, wrg_ref, wml_ref,
                  wxa_ref, wo_ref, o_ref):
    x = x_ref[...]
    D = x.shape[1]
    h = _rms(x, g_ref[...]).astype(BF16)
    q = _dot(h, wq_ref[...]).astype(BF16)
    dxa = q.shape[1]
    dh = dxa // XA_HEADS
    heads = []
    for hd in range(XA_HEADS):
        kh = kv_ref[:, hd * dh:(hd + 1) * dh]
        vh = kv_ref[:, dxa + hd * dh:dxa + (hd + 1) * dh]
        s = _dot_nt(q[:, hd * dh:(hd + 1) * dh], kh) * (dh ** -0.5)
        e = jnp.exp(s - jnp.max(s, axis=-1, keepdims=True))
        p = e / jnp.sum(e, axis=-1, keepdims=True)
        heads.append(_dot(p.astype(BF16), vh))
    y_xa = jnp.concatenate(heads, axis=1).astype(BF16)

    def gate(k):
        return jax.nn.sigmoid(_dot(h, wg_ref[:, k * D:(k + 1) * D]) + bm_ref[:, k * D:(k + 1) * D])

    merged = gate(0) * _dot(yrg_ref[...], wrg_ref[...])
    merged = merged + gate(1) * _dot(yml_ref[...], wml_ref[...])
    merged = merged + gate(2) * _dot(y_xa, wxa_ref[...])
    o_ref[...] = x + _dot(merged.astype(BF16), wo_ref[...])


def _merge(x2d, B, norm_g, w_q, w_g, b_merge, y_rg, y_ml, kv, w_br_rg, w_br_ml, w_br_xa, w_out):
    T, D = x2d.shape
    S = T // B
    C = y_ml.shape[1]
    M, N = kv.shape[1:]
    TM = min(MERGE_ROW_TILE, S)
    assert S % TM == 0
    nS = S // TM
    row = lambda b, s: (b * nS + s, 0)
    one = pl.Buffered(1)
    cspec = lambda shape: pl.BlockSpec(shape, lambda *_: (0,) * len(shape), pipeline_mode=one)
    return pl.pallas_call(
        _merge_kernel,
        out_shape=jax.ShapeDtypeStruct((T, D), F32),
        grid=(B, nS),
        in_specs=[pl.BlockSpec((TM, D), row), cspec((1, D)), cspec(w_q.shape), cspec(w_g.shape),
                  cspec(b_merge.shape), pl.BlockSpec((TM, C), row),
                  pl.BlockSpec((TM, C), row), pl.BlockSpec((None, M, N), lambda b, s: (b, 0, 0)),
                  cspec(w_br_rg.shape), cspec(w_br_ml.shape), cspec(w_br_xa.shape),
                  cspec(w_out.shape)],
        out_specs=pl.BlockSpec((TM, D), row),
        compiler_params=_params(("arbitrary", "arbitrary"), 56),
        name="merge",
    )(x2d, norm_g, w_q, w_g, b_merge, y_rg, y_ml, kv, w_br_rg, w_br_ml, w_br_xa, w_out)


def _ffn_kernel(x_ref, g_ref, w1_ref, w3_ref, w2_ref, *rest, n_cast):
    cast_in, o_ref, cast_out = rest[:n_cast], rest[n_cast], rest[n_cast + 1:2 * n_cast + 1]
    hs, acc = rest[2 * n_cast + 1:]
    f = pl.program_id(1)
    for src, dst in zip(cast_in, cast_out):
        dst[...] = src[...].astype(BF16)

    @pl.when(f == 0)
    def _():
        hs[...] = _rms(x_ref[...], g_ref[...]).astype(BF16)
        acc[...] = jnp.zeros_like(acc)

    hb = hs[...]
    mid = jax.nn.silu(_dot(hb, w1_ref[...])) * _dot(hb, w3_ref[...])
    acc[...] += _dot(mid.astype(BF16), w2_ref[...])

    @pl.when(f == pl.num_programs(1) - 1)
    def _():
        o_ref[...] = x_ref[...] + acc[...]


def _ffn_hidden_tile(F):
    for tf in (1792, 1024, 512, 256):
        if F % tf == 0:
            return tf
    return F


def _ffn_can_cast(x2d, w1, t):
    steps = (x2d.shape[0] // min(FFN_ROW_TILE, x2d.shape[0])) * (w1.shape[1] // _ffn_hidden_tile(w1.shape[1]))
    rows = t.size // t.shape[-1]
    return rows % steps == 0 and (rows // steps) % (2 * SUBLANES) == 0


def _ffn(x2d, norm_g, w1, w3, w2, cast=()):
    T, D = x2d.shape
    F = w1.shape[1]
    TM = min(FFN_ROW_TILE, T)
    TF = _ffn_hidden_tile(F)
    assert T % TM == 0
    nF = F // TF
    steps = (T // TM) * nF
    cast2d = [t.reshape(-1, t.shape[-1]) for t in cast]
    cast_specs = [pl.BlockSpec((t.shape[0] // steps, t.shape[1]), lambda i, f: (i * nF + f, 0))
                  for t in cast2d]
    out = pl.pallas_call(
        functools.partial(_ffn_kernel, n_cast=len(cast)),
        out_shape=(jax.ShapeDtypeStruct((T, D), F32),
                   *[jax.ShapeDtypeStruct(t.shape, BF16) for t in cast2d]),
        grid=(T // TM, nF),
        in_specs=[pl.BlockSpec((TM, D), lambda i, f: (i, 0)), _const_spec((1, D)),
                  pl.BlockSpec((D, TF), lambda i, f: (0, f)), pl.BlockSpec((D, TF), lambda i, f: (0, f)),
                  pl.BlockSpec((TF, D), lambda i, f: (f, 0)), *cast_specs],
        out_specs=(pl.BlockSpec((TM, D), lambda i, f: (i, 0)), *cast_specs),
        scratch_shapes=[pltpu.VMEM((TM, D), BF16), pltpu.VMEM((TM, D), F32)],
        compiler_params=_params(("arbitrary", "arbitrary"), 56),
        name="ffn_dense",
    )(x2d, norm_g, w1, w3, w2, *cast2d)
    return out[0], [o.reshape(t.shape) for o, t in zip(out[1:], cast)]


def _route_kernel(x_ref, g_ref, whi_ref, wlo_ref, rb_ref, idx_ref, gate_ref, len_ref, off_ref,
                  tot_ref, run_s, *, E):
    @pl.when(pl.program_id(0) == 0)
    def _():
        run_s[...] = jnp.zeros_like(run_s)

    h2 = _rms(x_ref[...], g_ref[...])
    TM = h2.shape[0]
    hi = h2.astype(BF16)
    lo = (h2 - hi.astype(F32)).astype(BF16)
    logits = _dot(hi, whi_ref[...]) + (_dot(lo, whi_ref[...]) + _dot(hi, wlo_ref[...]))
    lt = logits.T[0:E, :] + rb_ref[...]
    ie = lax.broadcasted_iota(jnp.int32, (E, TM), 0)
    m1 = jnp.max(lt, axis=0, keepdims=True)
    i1 = jnp.min(jnp.where(lt == m1, ie, E), axis=0, keepdims=True)
    l2 = jnp.where(ie == i1, -jnp.inf, lt)
    m2 = jnp.max(l2, axis=0, keepdims=True)
    i2 = jnp.min(jnp.where(l2 == m2, ie, E), axis=0, keepdims=True)
    ex = jnp.exp(m2 - m1)
    g1 = 1.0 / (1.0 + ex)
    g2 = ex / (1.0 + ex)
    oh1 = jnp.where(ie == i1, 1.0, 0.0)
    oh2 = jnp.where(ie == i2, 1.0, 0.0)
    oh = oh1 + oh2
    ri = lax.broadcasted_iota(jnp.int32, (TM, TM), 0)
    ci = lax.broadcasted_iota(jnp.int32, (TM, TM), 1)
    upper = jnp.where(ri < ci, 1.0, 0.0).astype(BF16)
    excl = _dot(oh.astype(BF16), upper)
    cnt = jnp.sum(oh, axis=1, keepdims=True).astype(jnp.int32)
    run_len = jnp.broadcast_to(((cnt + (RUN_ALIGN - 1)) // RUN_ALIGN) * RUN_ALIGN, (E, LANES))
    iec = lax.broadcasted_iota(jnp.int32, (E, LANES), 0)
    run_start = jnp.zeros((E, LANES), jnp.int32)
    for e in range(E - 1):
        run_start = run_start + jnp.where(iec > e, run_len[e:e + 1, :], 0)
    slot = run_start[:, 0:1].astype(F32) + excl
    s1 = jnp.sum(oh1 * slot, axis=0, keepdims=True).astype(jnp.int32)
    s2 = jnp.sum(oh2 * slot, axis=0, keepdims=True).astype(jnp.int32)
    len_ref[...] = run_len
    off_ref[...] = run_s[...]
    run_s[...] = run_s[...] + run_len
    tot_ref[...] = run_s[...]
    row = lax.broadcasted_iota(jnp.int32, (SUBLANES, TM), 0)
    idx_ref[...] = jnp.where(row == 0, i1, jnp.where(row == 1, i2, jnp.where(row == 2, s1,
                             jnp.where(row == 3, s2, 0))))
    gate_ref[...] = jnp.where(row == 0, g1, jnp.where(row == 1, g2, 0.0))


def _route(x2d, norm_g, w_hi, w_lo, rb, E):
    T, D = x2d.shape
    TM = min(ROUTE_TM, T)
    nT = T // TM
    assert T % TM == 0 and E == SUBLANES
    return pl.pallas_call(
        functools.partial(_route_kernel, E=E),
        out_shape=(jax.ShapeDtypeStruct((SUBLANES, T), jnp.int32),
                   jax.ShapeDtypeStruct((SUBLANES, T), F32),
                   jax.ShapeDtypeStruct((nT * E, LANES), jnp.int32),
                   jax.ShapeDtypeStruct((nT * E, LANES), jnp.int32),
                   jax.ShapeDtypeStruct((E, LANES), jnp.int32)),
        grid=(nT,),
        in_specs=[pl.BlockSpec((TM, D), lambda i: (i, 0)), _const_spec((1, D)),
                  _const_spec((D, LANES)), _const_spec((D, LANES)), _const_spec((E, 1))],
        out_specs=(pl.BlockSpec((SUBLANES, TM), lambda i: (0, i)),
                   pl.BlockSpec((SUBLANES, TM), lambda i: (0, i)),
                   pl.BlockSpec((E, LANES), lambda i: (i, 0)),
                   pl.BlockSpec((E, LANES), lambda i: (i, 0)),
                   _const_spec((E, LANES))),
        scratch_shapes=[pltpu.VMEM((E, LANES), jnp.int32)],
        compiler_params=_params(("arbitrary",), 32),
        name="moe_route",
    )(x2d, norm_g, w_hi, w_lo, rb)


def _run_dmas(len_ref, off_ref, i, E, max_len, tile_ref, sorted_ref, sem, to_sorted, wait, packed=True):
    local = 0
    for e in range(E):
        n = len_ref[i * E + e]
        base = off_ref[i * E + e]
        done = 0
        sz = max_len
        while sz >= RUN_ALIGN:
            @pl.when((n & sz) != 0)
            def _(sz=sz, local=local, base=base, done=done):
                t_rows = tile_ref.at[pl.ds(pl.multiple_of(local + done, RUN_ALIGN), sz), :]
                s_rows = sorted_ref.at[pl.ds(pl.multiple_of(base + done, RUN_ALIGN), sz), :]
                cp = (pltpu.make_async_copy(t_rows, s_rows, sem) if to_sorted
                      else pltpu.make_async_copy(s_rows, t_rows, sem))
                if wait:
                    cp.wait()
                else:
                    cp.start()
            done = done + (n & sz)
            sz //= 2
        if packed:
            local = local + n


def _dispatch_kernel(len_ref, off_ref, gap_len_ref, gap_off_ref, tail_ref, x_ref, g_ref, ls_ref, xs_ref, xs_t,
                     zeros, sem, *, E):
    i = pl.program_id(0)
    TM = x_ref.shape[0]
    LS = xs_t.shape[1]
    slot = lax.rem(i, 2)
    h2 = _rms(x_ref[...], g_ref[...]).astype(BF16)
    ls = ls_ref[...]
    j = lax.broadcasted_iota(jnp.int32, (LS, TM), 0)
    onehot = jnp.where(j == ls[2:3, :], 1.0, jnp.where(j == ls[3:4, :], 1.0, 0.0)).astype(BF16)
    xs_t[slot] = _dot(onehot, h2).astype(BF16)

    @pl.when(i > 0)
    def _():
        _run_dmas(len_ref, off_ref, i - 1, E, TM, xs_t.at[1 - slot], xs_ref, sem.at[1 - slot], True, True)

    _run_dmas(len_ref, off_ref, i, E, TM, xs_t.at[slot], xs_ref, sem.at[slot], True, False)

    @pl.when(i == pl.num_programs(0) - 1)
    def _():
        _run_dmas(len_ref, off_ref, i, E, TM, xs_t.at[slot], xs_ref, sem.at[slot], True, True)
        zeros[...] = jnp.zeros_like(zeros)
        tmg = zeros.shape[0]

        def tail_copy(t):
            rows = pl.ds(pl.multiple_of(t * tmg, tmg), tmg)
            return pltpu.make_async_copy(zeros, xs_ref.at[rows, :], sem.at[slot])

        def tail_start(t, c):
            tail_copy(t).start()
            return c

        def tail_wait(t, c):
            tail_copy(t).wait()
            return c

        for wait in (False, True):
            _run_dmas(gap_len_ref, gap_off_ref, 0, E, tmg // 2, zeros, xs_ref, sem.at[slot], True, wait,
                      packed=False)
            lax.fori_loop(tail_ref[0], xs_ref.shape[0] // tmg, tail_wait if wait else tail_start, 0)


def _dispatch(run_len, run_off, gap_len, gap_off, tail_tile, idx, x2d, norm_g, P, E, TMG):
    T, D = x2d.shape
    TM = min(ROUTE_TM, T)
    LS = TOP_K * TM + E * RUN_ALIGN
    grid_spec = pltpu.PrefetchScalarGridSpec(
        num_scalar_prefetch=5,
        grid=(T // TM,),
        in_specs=[pl.BlockSpec((TM, D), lambda i, *_: (i, 0)),
                  pl.BlockSpec((1, D), lambda i, *_: (0, 0)),
                  pl.BlockSpec((SUBLANES, TM), lambda i, *_: (0, i))],
        out_specs=pl.BlockSpec(memory_space=pl.ANY),
        scratch_shapes=[pltpu.VMEM((2, LS, D), BF16), pltpu.VMEM((TMG, D), BF16),
                        pltpu.SemaphoreType.DMA((2,))])
    return pl.pallas_call(
        functools.partial(_dispatch_kernel, E=E),
        out_shape=jax.ShapeDtypeStruct((P, D), BF16),
        grid_spec=grid_spec,
        compiler_params=_params(("arbitrary",), 32),
        name="moe_dispatch",
    )(run_len, run_off, gap_len, gap_off, tail_tile, x2d, norm_g, idx)


def _group_ffn_kernel(te_ref, tn_ref, x_ref, w1_ref, w3_ref, w2_ref, o_ref, acc):
    i = pl.program_id(0)
    f = pl.program_id(1)

    @pl.when(f == 0)
    def _():
        acc[...] = jnp.zeros_like(acc)

    @pl.when(tn_ref[i] > 0)
    def _():
        hb = x_ref[...]
        mid = jax.nn.silu(_dot(hb, w1_ref[...])) * _dot(hb, w3_ref[...])
        acc[...] += _dot(mid.astype(BF16), w2_ref[...])

    @pl.when(f == pl.num_programs(1) - 1)
    def _():
        o_ref[...] = acc[...]


def _group_ffn(tile_e, tile_n, xs, w1, w3, w2, TMG):
    P, D = xs.shape
    F = w1.shape[2]
    TF = _ffn_hidden_tile(F)
    nF = F // TF
    fblk = lambda i, f, tn: jnp.where(tn[i] > 0, f, nF - 1)
    grid_spec = pltpu.PrefetchScalarGridSpec(
        num_scalar_prefetch=2,
        grid=(P // TMG, nF),
        in_specs=[pl.BlockSpec((TMG, D), lambda i, f, te, tn: (jnp.where(tn[i] > 0, i, 0), 0)),
                  pl.BlockSpec((None, D, TF), lambda i, f, te, tn: (te[i], 0, fblk(i, f, tn))),
                  pl.BlockSpec((None, D, TF), lambda i, f, te, tn: (te[i], 0, fblk(i, f, tn))),
                  pl.BlockSpec((None, TF, D), lambda i, f, te, tn: (te[i], fblk(i, f, tn), 0))],
        out_specs=pl.BlockSpec((TMG, D), lambda i, f, te, tn: (i, 0)),
        scratch_shapes=[pltpu.VMEM((TMG, D), F32)])
    return pl.pallas_call(
        _group_ffn_kernel,
        out_shape=jax.ShapeDtypeStruct((P, D), F32),
        grid_spec=grid_spec,
        compiler_params=_params(("arbitrary", "arbitrary"), 56),
        name="moe_group_ffn",
    )(tile_e, tile_n, xs, w1, w3, w2)


def _combine_kernel(len_ref, off_ref, x_ref, lsr_ref, lsc_ref, gt_ref, fg_ref, ys_ref, o_ref, yt, sem,
                    *, E, final_norm):
    i = pl.program_id(0)
    TM = x_ref.shape[0]
    _, LS, D = yt.shape
    slot = lax.rem(i, 2)

    def fetch(tile, s):
        yt[s, TOP_K * TM:LS, :] = jnp.zeros((LS - TOP_K * TM, D), F32)
        _run_dmas(len_ref, off_ref, tile, E, TM, yt.at[s], ys_ref, sem.at[s], False, False)

    @pl.when(i == 0)
    def _():
        fetch(0, 0)

    @pl.when(i + 1 < pl.num_programs(0))
    def _():
        fetch(i + 1, 1 - slot)

    lsr = lsr_ref[...]
    gt = gt_ref[...]
    js = lax.broadcasted_iota(jnp.int32, (LS, TM), 0)
    lane = lax.broadcasted_iota(jnp.int32, (TM, LANES), 1)
    gs = None
    for k in range(TOP_K):
        t1, t2, t3 = (t.astype(F32) for t in _split3(gt[:, k:k + 1]))
        terms = jnp.where(lane == 0, t1, jnp.where(lane == 1, t2, jnp.where(lane == 2, t3, 0.0)))
        pk = jnp.where(js == lsr[TOP_K + k:TOP_K + k + 1, :], 1.0, 0.0).astype(BF16)
        gk = jnp.sum(_dot(pk, terms.astype(BF16)), axis=1, keepdims=True)
        gs = gk if gs is None else gs + gk
    _run_dmas(len_ref, off_ref, i, E, TM, yt.at[slot], ys_ref, sem.at[slot], False, True)
    z = yt[slot] * gs
    zh = z.astype(BF16)
    zl = (z - zh.astype(F32)).astype(BF16)
    lsc = lsc_ref[...]
    jt = lax.broadcasted_iota(jnp.int32, (TM, LS), 1)
    pt = None
    for k in range(TOP_K):
        ok = jnp.where(jt == lsc[:, TOP_K + k:TOP_K + k + 1], 1.0, 0.0)
        pt = ok if pt is None else pt + ok
    pt = pt.astype(BF16)
    xo = x_ref[...] + (_dot(pt, zh) + _dot(pt, zl))
    o_ref[...] = _rms(xo, fg_ref[...]) if final_norm else xo


def _combine(run_len, run_off, x2d, idx, idx_t, gates_t, final_g, ys, E, final_norm):
    T, D = x2d.shape
    TM = min(ROUTE_TM, T)
    LS = TOP_K * TM + E * RUN_ALIGN
    grid_spec = pltpu.PrefetchScalarGridSpec(
        num_scalar_prefetch=2,
        grid=(T // TM,),
        in_specs=[pl.BlockSpec((TM, D), lambda i, *_: (i, 0)),
                  pl.BlockSpec((SUBLANES, TM), lambda i, *_: (0, i)),
                  pl.BlockSpec((TM, SUBLANES), lambda i, *_: (i, 0)),
                  pl.BlockSpec((TM, SUBLANES), lambda i, *_: (i, 0)),
                  pl.BlockSpec((1, D), lambda i, *_: (0, 0)),
                  pl.BlockSpec(memory_space=pl.ANY)],
        out_specs=pl.BlockSpec((TM, D), lambda i, *_: (i, 0)),
        scratch_shapes=[pltpu.VMEM((2, LS, D), F32), pltpu.SemaphoreType.DMA((2,))])
    return pl.pallas_call(
        functools.partial(_combine_kernel, E=E, final_norm=final_norm),
        out_shape=jax.ShapeDtypeStruct((T, D), F32),
        grid_spec=grid_spec,
        compiler_params=_params(("arbitrary",), 48),
        name="moe_combine",
    )(run_len, run_off, x2d, idx, idx_t, gates_t, final_g, ys)


def _moe(x2d, norm_g, router_w, router_b, w1, w3, w2, final_g, final_norm):
    T, D = x2d.shape
    E = router_w.shape[1]
    TMG = FFN_ROW_TILE
    nT = T // min(ROUTE_TM, T)
    rw = jnp.pad(router_w, ((0, 0), (0, LANES - E)))
    rw_hi = rw.astype(BF16)
    rw_lo = (rw - rw_hi.astype(F32)).astype(BF16)
    idx, gates, run_len, run_off, tot = _route(x2d, norm_g, rw_hi, rw_lo, router_b.reshape(E, 1), E)
    tot = tot[:, 0]
    padded = ((tot + TMG - 1) // TMG) * TMG
    pend = jnp.cumsum(padded)
    pstart = pend - padded
    run_len = run_len[:, 0]
    run_off = (run_off[:, 0].reshape(nT, E) + pstart[None, :]).reshape(nT * E)
    P = -(-(T * TOP_K + nT * E * (RUN_ALIGN - 1) + E * (TMG - 1)) // TMG) * TMG
    tile_start = jnp.arange(P // TMG, dtype=jnp.int32) * TMG
    tile_e = jnp.minimum(jnp.sum(tile_start[:, None] >= pend[None, :], axis=1), E - 1).astype(jnp.int32)
    sel = tile_e[:, None] == jnp.arange(E, dtype=jnp.int32)[None, :]
    tile_end = jnp.sum(jnp.where(sel, (pstart + tot)[None, :], 0), axis=1)
    tile_n = jnp.clip(tile_end - tile_start, 0, TMG).astype(jnp.int32)
    xs = _dispatch(run_len, run_off, padded - tot, pstart + tot, pend[E - 1:] // TMG, idx, x2d, norm_g, P,
                   E, TMG)
    ys = _group_ffn(tile_e, tile_n, xs, w1, w3, w2, TMG)
    return _combine(run_len, run_off, x2d, idx, idx.T, gates.T, final_g, ys, E, final_norm)


def _final_norm_kernel(x_ref, g_ref, o_ref):
    o_ref[...] = _rms(x_ref[...], g_ref[...])


def _final_norm(x2d, g):
    T, D = x2d.shape
    TM = min(1024, T)
    return pl.pallas_call(
        _final_norm_kernel,
        out_shape=jax.ShapeDtypeStruct((T, D), F32),
        grid=(T // TM,),
        in_specs=[pl.BlockSpec((TM, D), lambda i: (i, 0)), _const_spec((1, D))],
        out_specs=pl.BlockSpec((TM, D), lambda i: (i, 0)),
        compiler_params=_params(("arbitrary",), 32),
        name="final_norm",
    )(x2d, g)


def kernel(x, mem, norm_mix_g, w_in, conv_rg_w, conv_rg_b, rg_w_a, rg_b_a, rg_w_x, rg_b_x, rg_lambda,
           conv_ml_w, conv_ml_b, ml_w_q, ml_w_k, ml_w_v, ml_b_i, ml_b_f, ml_norm_g, mem_norm_g, w_kv,
           w_br_rg, w_br_ml, w_br_xa, b_merge, w_out, norm_ffn_g, ffn_w1, ffn_w3, ffn_w2, router_w,
           router_b, moe_w1, moe_w3, moe_w2, final_norm_g):
    B, S, D = x.shape
    depth = w_in.shape[0]
    d_rg = conv_rg_w.shape[2]
    d_ml = conv_ml_w.shape[2]
    H = ml_w_q.shape[1]
    d_xa = w_kv.shape[2] // 2
    o_ax, o_ay = 0, d_rg
    o_mu, o_mo = 2 * d_rg, 2 * d_rg + d_ml
    o_mi = 2 * d_rg + 2 * d_ml
    o_mf = o_mi + H
    o_q = o_mf + H
    o_g = o_q + d_xa
    assert w_in.shape[2] == o_g + N_BRANCH * D and 2 * H <= SUBLANES

    bf = lambda a: a.astype(BF16)
    row = lambda a: a.reshape(1, -1)
    x2d = x.reshape(B * S, D)
    fg = row(final_norm_g)
    moe_bf16 = []
    for l in range(depth):
        wl = w_in[l]
        w_if = jnp.pad(wl[:, o_mi:o_mi + 2 * H], ((0, 0), (0, LANES - 2 * H)))
        w_ift = jnp.pad(wl[:, o_mi:o_mi + 2 * H].T, ((0, SUBLANES - 2 * H), (0, 0)))
        b_if = jnp.concatenate([ml_b_i[l], ml_b_f[l]])
        y_ml = _mlstm_branch(x2d, B, row(norm_mix_g[l]), bf(wl[:, o_mu:o_mu + d_ml]),
                                   bf(wl[:, o_mo:o_mo + d_ml]), bf(w_if), bf(w_ift),
                                   jnp.pad(b_if, (0, LANES - 2 * H)).reshape(1, LANES),
                                   jnp.pad(b_if, (0, SUBLANES - 2 * H)).reshape(SUBLANES, 1),
                                   conv_ml_w[l], row(conv_ml_b[l]), bf(ml_w_q[l]), bf(ml_w_k[l]),
                                   bf(ml_w_v[l]), row(ml_norm_g[l]))
        y_rg = _rg_branch(x2d.reshape(B, S, D), row(norm_mix_g[l]), bf(wl[:, o_ax:o_ax + d_rg]),
                          bf(wl[:, o_ay:o_ay + d_rg]), conv_rg_w[l], row(conv_rg_b[l]), bf(rg_w_a[l]),
                          row(rg_b_a[l]), bf(rg_w_x[l]), row(rg_b_x[l]), row(rg_lambda[l]))
        kv = _mem_kv(mem, row(mem_norm_g[l]), bf(w_kv[l]))
        x2d = _merge(x2d, B, row(norm_mix_g[l]), bf(wl[:, o_q:o_q + d_xa]), bf(wl[:, o_g:]),
                     row(b_merge[l]), y_rg.reshape(B * S, d_rg), y_ml, kv, bf(w_br_rg[l]),
                     bf(w_br_ml[l]), bf(w_br_xa[l]), bf(w_out[l]))
        j = l // 2
        if l % 2 == 0:
            w1 = bf(ffn_w1[j])
            nxt = [w[j] for w in (moe_w1, moe_w3, moe_w2)] if l + 1 < depth else []
            nxt = nxt if all(_ffn_can_cast(x2d, w1, w) for w in nxt) else []
            x2d, moe_bf16 = _ffn(x2d, row(norm_ffn_g[l]), w1, bf(ffn_w3[j]), bf(ffn_w2[j]), nxt)
        else:
            e1, e3, e2 = moe_bf16 if moe_bf16 else (bf(moe_w1[j]), bf(moe_w3[j]), bf(moe_w2[j]))
            moe_bf16 = []
            x2d = _moe(x2d, row(norm_ffn_g[l]), router_w[j], router_b[j], e1, e3, e2, fg, l == depth - 1)
    if depth % 2 == 1:
        x2d = _final_norm(x2d, fg)
    return x2d.reshape(B, S, D)
```

```python
import functools

import jax
import jax.numpy as jnp
from jax import lax
from jax.experimental import pallas as pl
from jax.experimental.pallas import tpu as pltpu

EPS = 1e-6
RG_C = 8.0
CONV_W = 4
ML_CHUNK = 128
XA_HEADS = 4
TOP_K = 2
N_BRANCH = 3

V7X_VMEM_BYTES = 64 * 1024 * 1024
LANES = 128
SUBLANES = 8

RG_TIME_TILE = 128
ML_ROW_TILE = 512
MERGE_ROW_TILE = 1024
FFN_ROW_TILE = 512
ROUTE_TM = 512
RUN_ALIGN = 16

F32 = jnp.float32
BF16 = jnp.bfloat16


def _params(semantics, vmem_mib):
    assert vmem_mib * 1024 * 1024 <= V7X_VMEM_BYTES
    return pltpu.CompilerParams(dimension_semantics=semantics,
                                vmem_limit_bytes=vmem_mib * 1024 * 1024)


def _const_spec(shape):
    nd = len(shape)
    return pl.BlockSpec(shape, lambda *_: (0,) * nd)


def _rms(x, g):
    ms = jnp.mean(x * x, axis=-1, keepdims=True)
    return x * lax.rsqrt(ms + EPS) * g


def _dot(a, b):
    return jnp.dot(a, b, preferred_element_type=F32)


def _dot_nt(a, b):
    return lax.dot_general(a, b, (((1,), (1,)), ((), ())), preferred_element_type=F32)


def _dot_tn(a, b):
    return lax.dot_general(a, b, (((0,), (0,)), ((), ())), preferred_element_type=F32)


def _split3(x):
    h1 = x.astype(BF16)
    r1 = x - h1.astype(F32)
    h2 = r1.astype(BF16)
    h3 = (r1 - h2.astype(F32)).astype(BF16)
    return h1, h2, h3


def _rg_kernel(x_ref, g_ref, wax_ref, way_ref, cw_ref, cb_ref, wa_ref, ba_ref, wx_ref, bx_ref,
               lam_ref, o_ref, axbuf, a_s, b_s, h_s, carry, xbuf, xsem, *, B, TT):
    R = TT * B
    halo = (CONV_W - 1) * B

    @pl.when(pl.program_id(0) == 0)
    def _():
        axbuf[0:halo, :] = jnp.zeros((halo, axbuf.shape[1]), F32)
        carry[...] = jnp.zeros_like(carry)

    i = pl.program_id(0)
    slot = lax.rem(i, 2)

    def x_copies(tile, s):
        rows = pl.ds(pl.multiple_of(tile * TT, TT), TT)
        return [pltpu.make_async_copy(x_ref.at[b, rows, :], xbuf.at[s, :, b, :], xsem.at[s])
                for b in range(B)]

    @pl.when(i == 0)
    def _():
        for cp in x_copies(0, 0):
            cp.start()

    @pl.when(i + 1 < pl.num_programs(0))
    def _():
        for cp in x_copies(i + 1, 1 - slot):
            cp.start()

    for cp in x_copies(i, slot):
        cp.wait()
    x = xbuf[slot].reshape(R, xbuf.shape[3])
    h = _rms(x, g_ref[...]).astype(BF16)
    G, bi, _ = wa_ref.shape
    rate = -RG_C * jax.nn.softplus(-lam_ref[...])
    cols = [slice(g * bi, (g + 1) * bi) for g in range(G)]
    axbuf[halo:halo + R, cols[0]] = _dot(h, wax_ref[:, cols[0]])
    for g in range(G):
        cs = cols[g]
        if g + 1 < G:
            axbuf[halo:halo + R, cols[g + 1]] = _dot(h, wax_ref[:, cols[g + 1]])
        ay = _dot(h, way_ref[:, cs])
        cw = cw_ref[:, cs]
        xc = cb_ref[:, cs] + cw[0:1, :] * axbuf[0:R, cs]
        for k in range(1, CONV_W):
            xc = xc + cw[k:k + 1, :] * axbuf[k * B:k * B + R, cs]
        axbuf[0:halo, cs] = axbuf[R:R + halo, cs]
        xcb = xc.astype(BF16)
        r = jax.nn.sigmoid(_dot(xcb, wa_ref[g]) + ba_ref[:, cs])
        ig = jax.nn.sigmoid(_dot(xcb, wx_ref[g]) + bx_ref[:, cs])
        log_a = r * rate[:, cs]
        a = jnp.exp(log_a)
        a_s[:, cs] = a
        b_s[:, cs] = jnp.sqrt(-jnp.tanh(log_a) * (a * a + 1.0)) * (ig * xc)

        def step(t, hc, cs=cs):
            off = pl.multiple_of(t * B, B)
            hn = a_s[pl.ds(off, B), cs] * hc + b_s[pl.ds(off, B), cs]
            h_s[pl.ds(off, B), cs] = hn
            return hn

        carry[:, cs] = lax.fori_loop(0, TT, step, carry[:, cs], unroll=True)
        y = (h_s[:, cs] * jax.nn.gelu(ay)).reshape(TT, B, bi)
        o_ref[:, :, cs] = pltpu.einshape("tbc->btc", y).astype(BF16)


def _rg_branch(x3d, norm_g, w_ax, w_ay, conv_w, conv_b, w_a, b_a, w_x, b_x, lam):
    B, S, D = x3d.shape
    C = w_ax.shape[1]
    TT = min(RG_TIME_TILE, S)
    assert S % TT == 0 and B % SUBLANES == 0
    R = TT * B
    halo = (CONV_W - 1) * B
    tile = lambda i: (0, i, 0)
    return pl.pallas_call(
        functools.partial(_rg_kernel, B=B, TT=TT),
        out_shape=jax.ShapeDtypeStruct((B, S, C), BF16),
        grid=(S // TT,),
        in_specs=[pl.BlockSpec(memory_space=pl.ANY), _const_spec((1, D)), _const_spec((D, C)),
                  _const_spec((D, C)), _const_spec((CONV_W, C)), _const_spec((1, C)),
                  _const_spec(w_a.shape), _const_spec((1, C)), _const_spec(w_x.shape),
                  _const_spec((1, C)), _const_spec((1, C))],
        out_specs=pl.BlockSpec((B, TT, C), tile),
        scratch_shapes=[pltpu.VMEM((halo + R, C), F32), pltpu.VMEM((R, C), F32),
                        pltpu.VMEM((R, C), F32), pltpu.VMEM((R, C), F32), pltpu.VMEM((B, C), F32),
                        pltpu.VMEM((2, TT, B, D), F32), pltpu.SemaphoreType.DMA((2,))],
        compiler_params=_params(("arbitrary",), 48),
        name="rg_branch",
    )(x3d, norm_g, w_ax, w_ay, conv_w, conv_b, w_a, b_a, w_x, b_x, lam)


def _alternate(first, second):
    result = None
    live = [True, True]
    while any(live):
        for n, gen in enumerate((first, second)):
            if live[n]:
                try:
                    next(gen)
                except StopIteration as stop:
                    live[n] = False
                    if n == 1:
                        result = stop.value
    return result


def _mlstm_kernel(x_ref, g_ref, wmu_ref, wmo_ref, wif_ref, wift_ref, bifc_ref, bifr_ref, cw_ref,
                  cb_ref, wq_ref, wk_ref, wv_ref, ng_ref, o_ref, ubuf, q_s, k_s, v_s, og_s, ifc_s, ifr_s,
                  lfc_s, lfr_s, c_st, n_st, m_st, *, TS, nS):
    H, d, _ = wq_ref.shape
    L = ML_CHUNK
    pad = SUBLANES
    tail = CONV_W - 1
    g = pl.program_id(0)

    @pl.when(g == 0)
    def _():
        for r in (q_s, k_s, v_s, og_s, ifc_s, ifr_s, lfc_s, lfr_s):
            r[...] = jnp.zeros_like(r)

    @pl.when(lax.rem(g, nS) == 0)
    def _():
        ubuf[...] = jnp.zeros_like(ubuf)

    @pl.when((g == 0) | (lax.rem(g + (nS - 1), nS) == 0))
    def _():
        c_st[...] = jnp.zeros_like(c_st)
        n_st[...] = jnp.zeros_like(n_st)
        m_st[...] = jnp.zeros_like(m_st)

    if_c = ifc_s[...]
    if_r = ifr_s[...]
    lf_c = lfc_s[...]
    lf_r = lfr_s[...]
    ri = lax.broadcasted_iota(jnp.int32, (L, L), 0)
    ci = lax.broadcasted_iota(jnp.int32, (L, L), 1)
    causal = ci <= ri
    tri_l = jnp.where(causal, 1.0, 0.0).astype(BF16)
    tri_u = jnp.where(ri <= ci, 1.0, 0.0).astype(BF16)
    ng = ng_ref[...]
    nck = TS // L
    h = _rms(x_ref[...], g_ref[...]).astype(BF16)
    us = [_dot(h, wmu_ref[:, 0:d])] + [None] * (H - 1)
    bcs = [sum(_dot(tri_l, p) for p in _split3(lf_c[ck * L:(ck + 1) * L, :])) for ck in range(nck)]
    brs = [sum(_dot(p, tri_u) for p in _split3(lf_r[:, ck * L:(ck + 1) * L])) for ck in range(nck)]

    def wide(t):
        return jnp.concatenate([t] * (d // LANES), axis=1)

    def recurrence(hd):
        cs = slice(hd * d, (hd + 1) * d)
        for ck in range(nck):
            r0 = ck * L
            bc, br = bcs[ck], brs[ck]
            b_col = jnp.broadcast_to(bc[:, H + hd:H + hd + 1], (L, LANES))
            i_col = jnp.broadcast_to(if_c[r0:r0 + L, hd:hd + 1], (L, LANES))
            b_row = br[H + hd:H + hd + 1, :]
            i_row = if_r[hd:hd + 1, r0:r0 + L]
            m = m_st[hd][0:1, :]
            qb = q_s[r0:r0 + L, cs]
            kb = k_s[r0:r0 + L, cs]
            vb = v_s[r0:r0 + L, cs]
            gg = b_col + m
            dm = jnp.where(causal, b_col - b_row + i_row, -jnp.inf)
            yield
            m_row = jnp.maximum(gg, jnp.max(dm, axis=-1, keepdims=True))
            s_qk = _dot_nt(qb, kb)
            yield
            w = jnp.exp(dm - m_row) * s_qk
            inter = jnp.exp(gg - m_row)
            cmat = c_st[hd]
            nvec = n_st[hd]
            yield
            num = wide(inter) * _dot(qb, cmat.astype(BF16)) + _dot(w.astype(BF16), vb)
            yield
            qf, kf, vf = qb.astype(F32), kb.astype(F32), vb.astype(F32)
            den = inter * jnp.sum(qf * nvec, axis=-1, keepdims=True) + jnp.sum(w, axis=-1, keepdims=True)
            hh = num * wide(1.0 / jnp.maximum(jnp.abs(den), jnp.exp(-m_row)))
            yield
            b_last = b_col[L - 1:L, :]
            dl = b_last - b_col + i_col
            m_new = jnp.maximum(b_last + m, jnp.max(dl, axis=0, keepdims=True))
            decay = jnp.exp(b_last + m - m_new)
            wl = jnp.exp(dl - m_new)
            yield
            c_st[hd] = wide(decay) * cmat + _dot_tn(kb, (wide(wl) * vf).astype(BF16))
            n_st[hd] = wide(decay) * nvec + jnp.sum(wide(wl) * kf, axis=0, keepdims=True)
            m_st[hd] = jnp.broadcast_to(m_new, m_st.shape[1:])
            yield
            y = og_s[r0:r0 + L, cs] * hh
            y = y * lax.rsqrt(jnp.mean(y * y, axis=-1, keepdims=True) + EPS)
            o_ref[r0:r0 + L, cs] = (y * ng[:, cs]).astype(BF16)
            yield

    def projection(hd):
        cs = slice(hd * d, (hd + 1) * d)
        u = us[hd]
        mo = _dot(h, wmo_ref[:, cs])
        yield
        if hd + 1 < H:
            us[hd + 1] = _dot(h, wmu_ref[:, (hd + 1) * d:(hd + 2) * d])
            yield
        cw = cw_ref[:, cs]
        halo = ubuf[:, cs]
        rowi = lax.broadcasted_iota(jnp.int32, (pad, d), 0)
        c = cb_ref[:, cs] + cw[tail:tail + 1, :] * u
        for j in range(1, CONV_W):
            sh = pltpu.roll(u, j, axis=0)
            head = jnp.where(rowi < j, pltpu.roll(halo, j, axis=0), sh[0:pad, :])
            sh = jnp.concatenate([head, sh[pad:, :]], axis=0)
            c = c + cw[tail - j:tail - j + 1, :] * sh
            yield
        ubuf[:, cs] = u[TS - pad:TS, :]
        cb16 = jax.nn.silu(c).astype(BF16)
        yield
        q = _dot(cb16, wq_ref[hd]).astype(BF16)
        yield
        k_ = (_dot(cb16, wk_ref[hd]) * (d ** -0.5)).astype(BF16)
        yield
        v = _dot(u.astype(BF16), wv_ref[hd]).astype(BF16)
        yield
        og = jax.nn.sigmoid(mo)
        yield
        return q, k_, v, og

    for hd in range(H):
        cs = slice(hd * d, (hd + 1) * d)
        q, k_, v, og = _alternate(recurrence(hd), projection(hd))
        q_s[:, cs] = q
        k_s[:, cs] = k_
        v_s[:, cs] = v
        og_s[:, cs] = og
    new_if_c = _dot(h, wif_ref[...]) + bifc_ref[...]
    new_if_r = _dot_nt(wift_ref[...], h) + bifr_ref[...]
    ifc_s[...] = new_if_c
    ifr_s[...] = new_if_r
    lfc_s[...] = jax.nn.log_sigmoid(new_if_c)
    lfr_s[...] = jax.nn.log_sigmoid(new_if_r)


def _mlstm_branch(x2d, B, norm_g, w_mu, w_mo, w_if, w_ift, b_if_c, b_if_r, conv_w, conv_b,
                  w_q, w_k, w_v, ml_norm_g):
    T, D = x2d.shape
    S = T // B
    C = w_mu.shape[1]
    H, d, _ = w_q.shape
    TS = min(ML_ROW_TILE, S)
    assert S % TS == 0 and TS % ML_CHUNK == 0 and d % LANES == 0
    nS = S // TS
    G = B * nS
    return pl.pallas_call(
        functools.partial(_mlstm_kernel, TS=TS, nS=nS),
        out_shape=jax.ShapeDtypeStruct((T, C), BF16),
        grid=(G + 1,),
        in_specs=[pl.BlockSpec((TS, D), lambda g: (jnp.minimum(g, G - 1), 0)), _const_spec((1, D)),
                  _const_spec((D, C)), _const_spec((D, C)), _const_spec((D, LANES)),
                  _const_spec((SUBLANES, D)), _const_spec((1, LANES)), _const_spec((SUBLANES, 1)),
                  _const_spec((CONV_W, C)), _const_spec((1, C)), _const_spec(w_q.shape),
                  _const_spec(w_k.shape), _const_spec(w_v.shape), _const_spec((1, C))],
        out_specs=pl.BlockSpec((TS, C), lambda g: (jnp.maximum(g - 1, 0), 0)),
        scratch_shapes=[pltpu.VMEM((SUBLANES, C), F32), pltpu.VMEM((TS, C), BF16),
                        pltpu.VMEM((TS, C), BF16), pltpu.VMEM((TS, C), BF16), pltpu.VMEM((TS, C), F32),
                        pltpu.VMEM((TS, LANES), F32), pltpu.VMEM((SUBLANES, TS), F32),
                        pltpu.VMEM((TS, LANES), F32), pltpu.VMEM((SUBLANES, TS), F32),
                        pltpu.VMEM((H, d, d), F32), pltpu.VMEM((H, 1, d), F32),
                        pltpu.VMEM((H, SUBLANES, LANES), F32)],
        compiler_params=_params(("arbitrary",), 48),
        name="mlstm_branch",
    )(x2d, norm_g, w_mu, w_mo, w_if, w_ift, b_if_c, b_if_r, conv_w, conv_b, w_q, w_k, w_v, ml_norm_g)


def _kv_kernel(mem_ref, g_ref, w_ref, o_ref):
    o_ref[...] = _dot(_rms(mem_ref[...], g_ref[...]).astype(BF16), w_ref[...]).astype(BF16)


def _mem_kv(mem, g, w_kv):
    B, M, D = mem.shape
    N = w_kv.shape[1]
    return pl.pallas_call(
        _kv_kernel,
        out_shape=jax.ShapeDtypeStruct((B, M, N), BF16),
        grid=(B,),
        in_specs=[pl.BlockSpec((None, M, D), lambda b: (b, 0, 0)), _const_spec((1, D)),
                  _const_spec((D, N))],
        out_specs=pl.BlockSpec((None, M, N), lambda b: (b, 0, 0)),
        compiler_params=_params(("arbitrary",), 32),
        name="mem_kv",
    )(mem, g, w_kv)


def _merge_kernel(x_ref, g_ref, wq_ref, wg_ref, bm_ref, yrg_ref, yml_ref, kv_ref, wrg_ref, wml_ref,
                  wxa_ref, wo_ref, o_ref):
    x = x_ref[...]
    D = x.shape[1]
    h = _rms(x, g_ref[...]).astype(BF16)
    q = _dot(h, wq_ref[...]).astype(BF16)
    dxa = q.shape[1]
    dh = dxa // XA_HEADS
    heads = []
    for hd in range(XA_HEADS):
        kh = kv_ref[:, hd * dh:(hd + 1) * dh]
        vh = kv_ref[:, dxa + hd * dh:dxa + (hd + 1) * dh]
        s = _dot_nt(q[:, hd * dh:(hd + 1) * dh], kh) * (dh ** -0.5)
        e = jnp.exp(s - jnp.max(s, axis=-1, keepdims=True))
        p = e / jnp.sum(e, axis=-1, keepdims=True)
        heads.append(_dot(p.astype(BF16), vh))
    y_xa = jnp.concatenate(heads, axis=1).astype(BF16)

    def gate(k):
        return jax.nn.sigmoid(_dot(h, wg_ref[:, k * D:(k + 1) * D]) + bm_ref[:, k * D:(k + 1) * D])

    merged = gate(0) * _dot(yrg_ref[...], wrg_ref[...])
    merged = merged + gate(1) * _dot(yml_ref[...], wml_ref[...])
    merged = merged + gate(2) * _dot(y_xa, wxa_ref[...])
    o_ref[...] = x + _dot(merged.astype(BF16), wo_ref[...])


def _merge(x2d, B, norm_g, w_q, w_g, b_merge, y_rg, y_ml, kv, w_br_rg, w_br_ml, w_br_xa, w_out):
    T, D = x2d.shape
    S = T // B
    C = y_ml.shape[1]
    M, N = kv.shape[1:]
    TM = min(MERGE_ROW_TILE, S)
    assert S % TM == 0
    nS = S // TM
    row = lambda b, s: (b * nS + s, 0)
    one = pl.Buffered(1)
    cspec = lambda shape: pl.BlockSpec(shape, lambda *_: (0,) * len(shape), pipeline_mode=one)
    return pl.pallas_call(
        _merge_kernel,
        out_shape=jax.ShapeDtypeStruct((T, D), F32),
        grid=(B, nS),
        in_specs=[pl.BlockSpec((TM, D), row), cspec((1, D)), cspec(w_q.shape), cspec(w_g.shape),
                  cspec(b_merge.shape), pl.BlockSpec((TM, C), row),
                  pl.BlockSpec((TM, C), row), pl.BlockSpec((None, M, N), lambda b, s: (b, 0, 0)),
                  cspec(w_br_rg.shape), cspec(w_br_ml.shape), cspec(w_br_xa.shape),
                  cspec(w_out.shape)],
        out_specs=pl.BlockSpec((TM, D), row),
        compiler_params=_params(("arbitrary", "arbitrary"), 56),
        name="merge",
    )(x2d, norm_g, w_q, w_g, b_merge, y_rg, y_ml, kv, w_br_rg, w_br_ml, w_br_xa, w_out)


def _ffn_kernel(x_ref, g_ref, w1_ref, w3_ref, w2_ref, *rest, n_cast):
    cast_in, o_ref, cast_out = rest[:n_cast], rest[n_cast], rest[n_cast + 1:2 * n_cast + 1]
    hs, acc = rest[2 * n_cast + 1:]
    f = pl.program_id(1)
    for src, dst in zip(cast_in, cast_out):
        dst[...] = src[...].astype(BF16)

    @pl.when(f == 0)
    def _():
        hs[...] = _rms(x_ref[...], g_ref[...]).astype(BF16)
        acc[...] = jnp.zeros_like(acc)

    hb = hs[...]
    mid = jax.nn.silu(_dot(hb, w1_ref[...])) * _dot(hb, w3_ref[...])
    acc[...] += _dot(mid.astype(BF16), w2_ref[...])

    @pl.when(f == pl.num_programs(1) - 1)
    def _():
        o_ref[...] = x_ref[...] + acc[...]


def _ffn_hidden_tile(F):
    for tf in (1792, 1024, 512, 256):
        if F % tf == 0:
            return tf
    return F


def _ffn_can_cast(x2d, w1, t):
    steps = (x2d.shape[0] // min(FFN_ROW_TILE, x2d.shape[0])) * (w1.shape[1] // _ffn_hidden_tile(w1.shape[1]))
    rows = t.size // t.shape[-1]
    return rows % steps == 0 and (rows // steps) % (2 * SUBLANES) == 0


def _ffn(x2d, norm_g, w1, w3, w2, cast=()):
    T, D = x2d.shape
    F = w1.shape[1]
    TM = min(FFN_ROW_TILE, T)
    TF = _ffn_hidden_tile(F)
    assert T % TM == 0
    nF = F // TF
    steps = (T // TM) * nF
    cast2d = [t.reshape(-1, t.shape[-1]) for t in cast]
    cast_specs = [pl.BlockSpec((t.shape[0] // steps, t.shape[1]), lambda i, f: (i * nF + f, 0))
                  for t in cast2d]
    out = pl.pallas_call(
        functools.partial(_ffn_kernel, n_cast=len(cast)),
        out_shape=(jax.ShapeDtypeStruct((T, D), F32),
                   *[jax.ShapeDtypeStruct(t.shape, BF16) for t in cast2d]),
        grid=(T // TM, nF),
        in_specs=[pl.BlockSpec((TM, D), lambda i, f: (i, 0)), _const_spec((1, D)),
                  pl.BlockSpec((D, TF), lambda i, f: (0, f)), pl.BlockSpec((D, TF), lambda i, f: (0, f)),
                  pl.BlockSpec((TF, D), lambda i, f: (f, 0)), *cast_specs],
        out_specs=(pl.BlockSpec((TM, D), lambda i, f: (i, 0)), *cast_specs),
        scratch_shapes=[pltpu.VMEM((TM, D), BF16), pltpu.VMEM((TM, D), F32)],
        compiler_params=_params(("arbitrary", "arbitrary"), 56),
        name="ffn_dense",
    )(x2d, norm_g, w1, w3, w2, *cast2d)
    return out[0], [o.reshape(t.shape) for o, t in zip(out[1:], cast)]


def _route_kernel(x_ref, g_ref, whi_ref, wlo_ref, rb_ref, idx_ref, gate_ref, len_ref, off_ref,
                  tot_ref, run_s, *, E):
    @pl.when(pl.program_id(0) == 0)
    def _():
        run_s[...] = jnp.zeros_like(run_s)

    h2 = _rms(x_ref[...], g_ref[...])
    TM = h2.shape[0]
    hi = h2.astype(BF16)
    lo = (h2 - hi.astype(F32)).astype(BF16)
    logits = _dot(hi, whi_ref[...]) + (_dot(lo, whi_ref[...]) + _dot(hi, wlo_ref[...]))
    lt = logits.T[0:E, :] + rb_ref[...]
    ie = lax.broadcasted_iota(jnp.int32, (E, TM), 0)
    m1 = jnp.max(lt, axis=0, keepdims=True)
    i1 = jnp.min(jnp.where(lt == m1, ie, E), axis=0, keepdims=True)
    l2 = jnp.where(ie == i1, -jnp.inf, lt)
    m2 = jnp.max(l2, axis=0, keepdims=True)
    i2 = jnp.min(jnp.where(l2 == m2, ie, E), axis=0, keepdims=True)
    ex = jnp.exp(m2 - m1)
    g1 = 1.0 / (1.0 + ex)
    g2 = ex / (1.0 + ex)
    oh1 = jnp.where(ie == i1, 1.0, 0.0)
    oh2 = jnp.where(ie == i2, 1.0, 0.0)
    oh = oh1 + oh2
    ri = lax.broadcasted_iota(jnp.int32, (TM, TM), 0)
    ci = lax.broadcasted_iota(jnp.int32, (TM, TM), 1)
    upper = jnp.where(ri < ci, 1.0, 0.0).astype(BF16)
    excl = _dot(oh.astype(BF16), upper)
    cnt = jnp.sum(oh, axis=1, keepdims=True).astype(jnp.int32)
    run_len = jnp.broadcast_to(((cnt + (RUN_ALIGN - 1)) // RUN_ALIGN) * RUN_ALIGN, (E, LANES))
    iec = lax.broadcasted_iota(jnp.int32, (E, LANES), 0)
    run_start = jnp.zeros((E, LANES), jnp.int32)
    for e in range(E - 1):
        run_start = run_start + jnp.where(iec > e, run_len[e:e + 1, :], 0)
    slot = run_start[:, 0:1].astype(F32) + excl
    s1 = jnp.sum(oh1 * slot, axis=0, keepdims=True).astype(jnp.int32)
    s2 = jnp.sum(oh2 * slot, axis=0, keepdims=True).astype(jnp.int32)
    len_ref[...] = run_len
    off_ref[...] = run_s[...]
    run_s[...] = run_s[...] + run_len
    tot_ref[...] = run_s[...]
    row = lax.broadcasted_iota(jnp.int32, (SUBLANES, TM), 0)
    idx_ref[...] = jnp.where(row == 0, i1, jnp.where(row == 1, i2, jnp.where(row == 2, s1,
                             jnp.where(row == 3, s2, 0))))
    gate_ref[...] = jnp.where(row == 0, g1, jnp.where(row == 1, g2, 0.0))


def _route(x2d, norm_g, w_hi, w_lo, rb, E):
    T, D = x2d.shape
    TM = min(ROUTE_TM, T)
    nT = T // TM
    assert T % TM == 0 and E == SUBLANES
    return pl.pallas_call(
        functools.partial(_route_kernel, E=E),
        out_shape=(jax.ShapeDtypeStruct((SUBLANES, T), jnp.int32),
                   jax.ShapeDtypeStruct((SUBLANES, T), F32),
                   jax.ShapeDtypeStruct((nT * E, LANES), jnp.int32),
                   jax.ShapeDtypeStruct((nT * E, LANES), jnp.int32),
                   jax.ShapeDtypeStruct((E, LANES), jnp.int32)),
        grid=(nT,),
        in_specs=[pl.BlockSpec((TM, D), lambda i: (i, 0)), _const_spec((1, D)),
                  _const_spec((D, LANES)), _const_spec((D, LANES)), _const_spec((E, 1))],
        out_specs=(pl.BlockSpec((SUBLANES, TM), lambda i: (0, i)),
                   pl.BlockSpec((SUBLANES, TM), lambda i: (0, i)),
                   pl.BlockSpec((E, LANES), lambda i: (i, 0)),
                   pl.BlockSpec((E, LANES), lambda i: (i, 0)),
                   _const_spec((E, LANES))),
        scratch_shapes=[pltpu.VMEM((E, LANES), jnp.int32)],
        compiler_params=_params(("arbitrary",), 32),
        name="moe_route",
    )(x2d, norm_g, w_hi, w_lo, rb)


def _run_dmas(len_ref, off_ref, i, E, max_len, tile_ref, sorted_ref, sem, to_sorted, wait, packed=True):
    local = 0
    for e in range(E):
        n = len_ref[i * E + e]
        base = off_ref[i * E + e]
        done = 0
        sz = max_len
        while sz >= RUN_ALIGN:
            @pl.when((n & sz) != 0)
            def _(sz=sz, local=local, base=base, done=done):
                t_rows = tile_ref.at[pl.ds(pl.multiple_of(local + done, RUN_ALIGN), sz), :]
                s_rows = sorted_ref.at[pl.ds(pl.multiple_of(base + done, RUN_ALIGN), sz), :]
                cp = (pltpu.make_async_copy(t_rows, s_rows, sem) if to_sorted
                      else pltpu.make_async_copy(s_rows, t_rows, sem))
                if wait:
                    cp.wait()
                else:
                    cp.start()
            done = done + (n & sz)
            sz //= 2
        if packed:
            local = local + n


def _dispatch_kernel(len_ref, off_ref, gap_len_ref, gap_off_ref, tail_ref, x_ref, g_ref, ls_ref, xs_ref, xs_t,
                     zeros, sem, *, E):
    i = pl.program_id(0)
    TM = x_ref.shape[0]
    LS = xs_t.shape[1]
    slot = lax.rem(i, 2)
    h2 = _rms(x_ref[...], g_ref[...]).astype(BF16)
    ls = ls_ref[...]
    j = lax.broadcasted_iota(jnp.int32, (LS, TM), 0)
    onehot = jnp.where(j == ls[2:3, :], 1.0, jnp.where(j == ls[3:4, :], 1.0, 0.0)).astype(BF16)
    xs_t[slot] = _dot(onehot, h2).astype(BF16)

    @pl.when(i > 0)
    def _():
        _run_dmas(len_ref, off_ref, i - 1, E, TM, xs_t.at[1 - slot], xs_ref, sem.at[1 - slot], True, True)

    _run_dmas(len_ref, off_ref, i, E, TM, xs_t.at[slot], xs_ref, sem.at[slot], True, False)

    @pl.when(i == pl.num_programs(0) - 1)
    def _():
        _run_dmas(len_ref, off_ref, i, E, TM, xs_t.at[slot], xs_ref, sem.at[slot], True, True)
        zeros[...] = jnp.zeros_like(zeros)
        tmg = zeros.shape[0]

        def tail_copy(t):
            rows = pl.ds(pl.multiple_of(t * tmg, tmg), tmg)
            return pltpu.make_async_copy(zeros, xs_ref.at[rows, :], sem.at[slot])

        def tail_start(t, c):
            tail_copy(t).start()
            return c

        def tail_wait(t, c):
            tail_copy(t).wait()
            return c

        for wait in (False, True):
            _run_dmas(gap_len_ref, gap_off_ref, 0, E, tmg // 2, zeros, xs_ref, sem.at[slot], True, wait,
                      packed=False)
            lax.fori_loop(tail_ref[0], xs_ref.shape[0] // tmg, tail_wait if wait else tail_start, 0)


def _dispatch(run_len, run_off, gap_len, gap_off, tail_tile, idx, x2d, norm_g, P, E, TMG):
    T, D = x2d.shape
    TM = min(ROUTE_TM, T)
    LS = TOP_K * TM + E * RUN_ALIGN
    grid_spec = pltpu.PrefetchScalarGridSpec(
        num_scalar_prefetch=5,
        grid=(T // TM,),
        in_specs=[pl.BlockSpec((TM, D), lambda i, *_: (i, 0)),
                  pl.BlockSpec((1, D), lambda i, *_: (0, 0)),
                  pl.BlockSpec((SUBLANES, TM), lambda i, *_: (0, i))],
        out_specs=pl.BlockSpec(memory_space=pl.ANY),
        scratch_shapes=[pltpu.VMEM((2, LS, D), BF16), pltpu.VMEM((TMG, D), BF16),
                        pltpu.SemaphoreType.DMA((2,))])
    return pl.pallas_call(
        functools.partial(_dispatch_kernel, E=E),
        out_shape=jax.ShapeDtypeStruct((P, D), BF16),
        grid_spec=grid_spec,
        compiler_params=_params(("arbitrary",), 32),
        name="moe_dispatch",
    )(run_len, run_off, gap_len, gap_off, tail_tile, x2d, norm_g, idx)


def _group_ffn_kernel(te_ref, tn_ref, x_ref, w1_ref, w3_ref, w2_ref, o_ref, acc):
    i = pl.program_id(0)
    f = pl.program_id(1)

    @pl.when(f == 0)
    def _():
        acc[...] = jnp.zeros_like(acc)

    @pl.when(tn_ref[i] > 0)
    def _():
        hb = x_ref[...]
        mid = jax.nn.silu(_dot(hb, w1_ref[...])) * _dot(hb, w3_ref[...])
        acc[...] += _dot(mid.astype(BF16), w2_ref[...])

    @pl.when(f == pl.num_programs(1) - 1)
    def _():
        o_ref[...] = acc[...]


def _group_ffn(tile_e, tile_n, xs, w1, w3, w2, TMG):
    P, D = xs.shape
    F = w1.shape[2]
    TF = _ffn_hidden_tile(F)
    nF = F // TF
    fblk = lambda i, f, tn: jnp.where(tn[i] > 0, f, nF - 1)
    grid_spec = pltpu.PrefetchScalarGridSpec(
        num_scalar_prefetch=2,
        grid=(P // TMG, nF),
        in_specs=[pl.BlockSpec((TMG, D), lambda i, f, te, tn: (jnp.where(tn[i] > 0, i, 0), 0)),
                  pl.BlockSpec((None, D, TF), lambda i, f, te, tn: (te[i], 0, fblk(i, f, tn))),
                  pl.BlockSpec((None, D, TF), lambda i, f, te, tn: (te[i], 0, fblk(i, f, tn))),
                  pl.BlockSpec((None, TF, D), lambda i, f, te, tn: (te[i], fblk(i, f, tn), 0))],
        out_specs=pl.BlockSpec((TMG, D), lambda i, f, te, tn: (i, 0)),
        scratch_shapes=[pltpu.VMEM((TMG, D), F32)])
    return pl.pallas_call(
        _group_ffn_kernel,
        out_shape=jax.ShapeDtypeStruct((P, D), F32),
        grid_spec=grid_spec,
        compiler_params=_params(("arbitrary", "arbitrary"), 56),
        name="moe_group_ffn",
    )(tile_e, tile_n, xs, w1, w3, w2)


def _combine_kernel(len_ref, off_ref, x_ref, lsr_ref, lsc_ref, gt_ref, fg_ref, ys_ref, o_ref, yt, sem,
                    *, E, final_norm):
    i = pl.program_id(0)
    TM = x_ref.shape[0]
    _, LS, D = yt.shape
    slot = lax.rem(i, 2)

    def fetch(tile, s):
        yt[s, TOP_K * TM:LS, :] = jnp.zeros((LS - TOP_K * TM, D), F32)
        _run_dmas(len_ref, off_ref, tile, E, TM, yt.at[s], ys_ref, sem.at[s], False, False)

    @pl.when(i == 0)
    def _():
        fetch(0, 0)

    @pl.when(i + 1 < pl.num_programs(0))
    def _():
        fetch(i + 1, 1 - slot)

    lsr = lsr_ref[...]
    gt = gt_ref[...]
    js = lax.broadcasted_iota(jnp.int32, (LS, TM), 0)
    lane = lax.broadcasted_iota(jnp.int32, (TM, LANES), 1)
    gs = None
    for k in range(TOP_K):
        t1, t2, t3 = (t.astype(F32) for t in _split3(gt[:, k:k + 1]))
        terms = jnp.where(lane == 0, t1, jnp.where(lane == 1, t2, jnp.where(lane == 2, t3, 0.0)))
        pk = jnp.where(js == lsr[TOP_K + k:TOP_K + k + 1, :], 1.0, 0.0).astype(BF16)
        gk = jnp.sum(_dot(pk, terms.astype(BF16)), axis=1, keepdims=True)
        gs = gk if gs is None else gs + gk
    _run_dmas(len_ref, off_ref, i, E, TM, yt.at[slot], ys_ref, sem.at[slot], False, True)
    z = yt[slot] * gs
    zh = z.astype(BF16)
    zl = (z - zh.astype(F32)).astype(BF16)
    lsc = lsc_ref[...]
    jt = lax.broadcasted_iota(jnp.int32, (TM, LS), 1)
    pt = None
    for k in range(TOP_K):
        ok = jnp.where(jt == lsc[:, TOP_K + k:TOP_K + k + 1], 1.0, 0.0)
        pt = ok if pt is None else pt + ok
    pt = pt.astype(BF16)
    xo = x_ref[...] + (_dot(pt, zh) + _dot(pt, zl))
    o_ref[...] = _rms(xo, fg_ref[...]) if final_norm else xo


def _combine(run_len, run_off, x2d, idx, idx_t, gates_t, final_g, ys, E, final_norm):
    T, D = x2d.shape
    TM = min(ROUTE_TM, T)
    LS = TOP_K * TM + E * RUN_ALIGN
    grid_spec = pltpu.PrefetchScalarGridSpec(
        num_scalar_prefetch=2,
        grid=(T // TM,),
        in_specs=[pl.BlockSpec((TM, D), lambda i, *_: (i, 0)),
                  pl.BlockSpec((SUBLANES, TM), lambda i, *_: (0, i)),
                  pl.BlockSpec((TM, SUBLANES), lambda i, *_: (i, 0)),
                  pl.BlockSpec((TM, SUBLANES), lambda i, *_: (i, 0)),
                  pl.BlockSpec((1, D), lambda i, *_: (0, 0)),
                  pl.BlockSpec(memory_space=pl.ANY)],
        out_specs=pl.BlockSpec((TM, D), lambda i, *_: (i, 0)),
        scratch_shapes=[pltpu.VMEM((2, LS, D), F32), pltpu.SemaphoreType.DMA((2,))])
    return pl.pallas_call(
        functools.partial(_combine_kernel, E=E, final_norm=final_norm),
        out_shape=jax.ShapeDtypeStruct((T, D), F32),
        grid_spec=grid_spec,
        compiler_params=_params(("arbitrary",), 48),
        name="moe_combine",
    )(run_len, run_off, x2d, idx, idx_t, gates_t, final_g, ys)


def _moe(x2d, norm_g, router_w, router_b, w1, w3, w2, final_g, final_norm):
    T, D = x2d.shape
    E = router_w.shape[1]
    TMG = FFN_ROW_TILE
    nT = T // min(ROUTE_TM, T)
    rw = jnp.pad(router_w, ((0, 0), (0, LANES - E)))
    rw_hi = rw.astype(BF16)
    rw_lo = (rw - rw_hi.astype(F32)).astype(BF16)
    idx, gates, run_len, run_off, tot = _route(x2d, norm_g, rw_hi, rw_lo, router_b.reshape(E, 1), E)
    tot = tot[:, 0]
    padded = ((tot + TMG - 1) // TMG) * TMG
    pend = jnp.cumsum(padded)
    pstart = pend - padded
    run_len = run_len[:, 0]
    run_off = (run_off[:, 0].reshape(nT, E) + pstart[None, :]).reshape(nT * E)
    P = -(-(T * TOP_K + nT * E * (RUN_ALIGN - 1) + E * (TMG - 1)) // TMG) * TMG
    tile_start = jnp.arange(P // TMG, dtype=jnp.int32) * TMG
    tile_e = jnp.minimum(jnp.sum(tile_start[:, None] >= pend[None, :], axis=1), E - 1).astype(jnp.int32)
    sel = tile_e[:, None] == jnp.arange(E, dtype=jnp.int32)[None, :]
    tile_end = jnp.sum(jnp.where(sel, (pstart + tot)[None, :], 0), axis=1)
    tile_n = jnp.clip(tile_end - tile_start, 0, TMG).astype(jnp.int32)
    xs = _dispatch(run_len, run_off, padded - tot, pstart + tot, pend[E - 1:] // TMG, idx, x2d, norm_g, P,
                   E, TMG)
    ys = _group_ffn(tile_e, tile_n, xs, w1, w3, w2, TMG)
    return _combine(run_len, run_off, x2d, idx, idx.T, gates.T, final_g, ys, E, final_norm)


def _final_norm_kernel(x_ref, g_ref, o_ref):
    o_ref[...] = _rms(x_ref[...], g_ref[...])


def _final_norm(x2d, g):
    T, D = x2d.shape
    TM = min(1024, T)
    return pl.pallas_call(
        _final_norm_kernel,
        out_shape=jax.ShapeDtypeStruct((T, D), F32),
        grid=(T // TM,),
        in_specs=[pl.BlockSpec((TM, D), lambda i: (i, 0)), _const_spec((1, D))],
        out_specs=pl.BlockSpec((TM, D), lambda i: (i, 0)),
        compiler_params=_params(("arbitrary",), 32),
        name="final_norm",
    )(x2d, g)


def kernel(x, mem, norm_mix_g, w_in, conv_rg_w, conv_rg_b, rg_w_a, rg_b_a, rg_w_x, rg_b_x, rg_lambda,
           conv_ml_w, conv_ml_b, ml_w_q, ml_w_k, ml_w_v, ml_b_i, ml_b_f, ml_norm_g, mem_norm_g, w_kv,
           w_br_rg, w_br_ml, w_br_xa, b_merge, w_out, norm_ffn_g, ffn_w1, ffn_w3, ffn_w2, router_w,
           router_b, moe_w1, moe_w3, moe_w2, final_norm_g):
    B, S, D = x.shape
    depth = w_in.shape[0]
    d_rg = conv_rg_w.shape[2]
    d_ml = conv_ml_w.shape[2]
    H = ml_w_q.shape[1]
    d_xa = w_kv.shape[2] // 2
    o_ax, o_ay = 0, d_rg
    o_mu, o_mo = 2 * d_rg, 2 * d_rg + d_ml
    o_mi = 2 * d_rg + 2 * d_ml
    o_mf = o_mi + H
    o_q = o_mf + H
    o_g = o_q + d_xa
    assert w_in.shape[2] == o_g + N_BRANCH * D and 2 * H <= SUBLANES

    bf = lambda a: a.astype(BF16)
    row = lambda a: a.reshape(1, -1)
    x2d = x.reshape(B * S, D)
    fg = row(final_norm_g)
    moe_bf16 = []
    for l in range(depth):
        wl = w_in[l]
        w_if = jnp.pad(wl[:, o_mi:o_mi + 2 * H], ((0, 0), (0, LANES - 2 * H)))
        w_ift = jnp.pad(wl[:, o_mi:o_mi + 2 * H].T, ((0, SUBLANES - 2 * H), (0, 0)))
        b_if = jnp.concatenate([ml_b_i[l], ml_b_f[l]])
        y_ml = _mlstm_branch(x2d, B, row(norm_mix_g[l]), bf(wl[:, o_mu:o_mu + d_ml]),
                                   bf(wl[:, o_mo:o_mo + d_ml]), bf(w_if), bf(w_ift),
                                   jnp.pad(b_if, (0, LANES - 2 * H)).reshape(1, LANES),
                                   jnp.pad(b_if, (0, SUBLANES - 2 * H)).reshape(SUBLANES, 1),
                                   conv_ml_w[l], row(conv_ml_b[l]), bf(ml_w_q[l]), bf(ml_w_k[l]),
                                   bf(ml_w_v[l]), row(ml_norm_g[l]))
        y_rg = _rg_branch(x2d.reshape(B, S, D), row(norm_mix_g[l]), bf(wl[:, o_ax:o_ax + d_rg]),
                          bf(wl[:, o_ay:o_ay + d_rg]), conv_rg_w[l], row(conv_rg_b[l]), bf(rg_w_a[l]),
                          row(rg_b_a[l]), bf(rg_w_x[l]), row(rg_b_x[l]), row(rg_lambda[l]))
        kv = _mem_kv(mem, row(mem_norm_g[l]), bf(w_kv[l]))
        x2d = _merge(x2d, B, row(norm_mix_g[l]), bf(wl[:, o_q:o_q + d_xa]), bf(wl[:, o_g:]),
                     row(b_merge[l]), y_rg.reshape(B * S, d_rg), y_ml, kv, bf(w_br_rg[l]),
                     bf(w_br_ml[l]), bf(w_br_xa[l]), bf(w_out[l]))
        j = l // 2
        if l % 2 == 0:
            w1 = bf(ffn_w1[j])
            nxt = [w[j] for w in (moe_w1, moe_w3, moe_w2)] if l + 1 < depth else []
            nxt = nxt if all(_ffn_can_cast(x2d, w1, w) for w in nxt) else []
            x2d, moe_bf16 = _ffn(x2d, row(norm_ffn_g[l]), w1, bf(ffn_w3[j]), bf(ffn_w2[j]), nxt)
        else:
            e1, e3, e2 = moe_bf16 if moe_bf16 else (bf(moe_w1[j]), bf(moe_w3[j]), bf(moe_w2[j]))
            moe_bf16 = []
            x2d = _moe(x2d, row(norm_ffn_g[l]), router_w[j], router_b[j], e1, e3, e2, fg, l == depth - 1)
    if depth % 2 == 1:
        x2d = _final_norm(x2d, fg)
    return x2d.reshape(B, S, D)
```

```python
import functools

import jax
import jax.numpy as jnp
from jax import lax
from jax.experimental import pallas as pl
from jax.experimental.pallas import tpu as pltpu

EPS = 1e-6
RG_C = 8.0
CONV_W = 4
ML_CHUNK = 128
XA_HEADS = 4
TOP_K = 2
N_BRANCH = 3

V7X_VMEM_BYTES = 64 * 1024 * 1024
LANES = 128
SUBLANES = 8

RG_TIME_TILE = 128
ML_ROW_TILE = 512
MERGE_ROW_TILE = 1024
FFN_ROW_TILE = 512
ROUTE_TM = 512
RUN_ALIGN = 16

F32 = jnp.float32
BF16 = jnp.bfloat16


def _params(semantics, vmem_mib):
    assert vmem_mib * 1024 * 1024 <= V7X_VMEM_BYTES
    return pltpu.CompilerParams(dimension_semantics=semantics,
                                vmem_limit_bytes=vmem_mib * 1024 * 1024)


def _const_spec(shape):
    nd = len(shape)
    return pl.BlockSpec(shape, lambda *_: (0,) * nd)


def _rms(x, g):
    ms = jnp.mean(x * x, axis=-1, keepdims=True)
    return x * lax.rsqrt(ms + EPS) * g


def _dot(a, b):
    return jnp.dot(a, b, preferred_element_type=F32)


def _dot_nt(a, b):
    return lax.dot_general(a, b, (((1,), (1,)), ((), ())), preferred_element_type=F32)


def _dot_tn(a, b):
    return lax.dot_general(a, b, (((0,), (0,)), ((), ())), preferred_element_type=F32)


def _split3(x):
    h1 = x.astype(BF16)
    r1 = x - h1.astype(F32)
    h2 = r1.astype(BF16)
    h3 = (r1 - h2.astype(F32)).astype(BF16)
    return h1, h2, h3


def _rg_kernel(x_ref, g_ref, wax_ref, way_ref, cw_ref, cb_ref, wa_ref, ba_ref, wx_ref, bx_ref,
               lam_ref, o_ref, axbuf, a_s, b_s, h_s, carry, xbuf, xsem, *, B, TT):
    R = TT * B
    halo = (CONV_W - 1) * B

    @pl.when(pl.program_id(0) == 0)
    def _():
        axbuf[0:halo, :] = jnp.zeros((halo, axbuf.shape[1]), F32)
        carry[...] = jnp.zeros_like(carry)

    i = pl.program_id(0)
    slot = lax.rem(i, 2)

    def x_copies(tile, s):
        rows = pl.ds(pl.multiple_of(tile * TT, TT), TT)
        return [pltpu.make_async_copy(x_ref.at[b, rows, :], xbuf.at[s, :, b, :], xsem.at[s])
                for b in range(B)]

    @pl.when(i == 0)
    def _():
        for cp in x_copies(0, 0):
            cp.start()

    @pl.when(i + 1 < pl.num_programs(0))
    def _():
        for cp in x_copies(i + 1, 1 - slot):
            cp.start()

    for cp in x_copies(i, slot):
        cp.wait()
    x = xbuf[slot].reshape(R, xbuf.shape[3])
    h = _rms(x, g_ref[...]).astype(BF16)
    G, bi, _ = wa_ref.shape
    rate = -RG_C * jax.nn.softplus(-lam_ref[...])
    cols = [slice(g * bi, (g + 1) * bi) for g in range(G)]
    axbuf[halo:halo + R, cols[0]] = _dot(h, wax_ref[:, cols[0]])
    for g in range(G):
        cs = cols[g]
        if g + 1 < G:
            axbuf[halo:halo + R, cols[g + 1]] = _dot(h, wax_ref[:, cols[g + 1]])
        ay = _dot(h, way_ref[:, cs])
        cw = cw_ref[:, cs]
        xc = cb_ref[:, cs] + cw[0:1, :] * axbuf[0:R, cs]
        for k in range(1, CONV_W):
            xc = xc + cw[k:k + 1, :] * axbuf[k * B:k * B + R, cs]
        axbuf[0:halo, cs] = axbuf[R:R + halo, cs]
        xcb = xc.astype(BF16)
        r = jax.nn.sigmoid(_dot(xcb, wa_ref[g]) + ba_ref[:, cs])
        ig = jax.nn.sigmoid(_dot(xcb, wx_ref[g]) + bx_ref[:, cs])
        log_a = r * rate[:, cs]
        a = jnp.exp(log_a)
        a_s[:, cs] = a
        b_s[:, cs] = jnp.sqrt(-jnp.tanh(log_a) * (a * a + 1.0)) * (ig * xc)

        def step(t, hc, cs=cs):
            off = pl.multiple_of(t * B, B)
            hn = a_s[pl.ds(off, B), cs] * hc + b_s[pl.ds(off, B), cs]
            h_s[pl.ds(off, B), cs] = hn
            return hn

        carry[:, cs] = lax.fori_loop(0, TT, step, carry[:, cs], unroll=True)
        y = (h_s[:, cs] * jax.nn.gelu(ay)).astype(BF16).reshape(TT, B, bi)
        o_ref[:, :, cs] = pltpu.einshape("tbc->btc", y)


def _rg_branch(x3d, norm_g, w_ax, w_ay, conv_w, conv_b, w_a, b_a, w_x, b_x, lam):
    B, S, D = x3d.shape
    C = w_ax.shape[1]
    TT = min(RG_TIME_TILE, S)
    assert S % TT == 0 and B % SUBLANES == 0
    R = TT * B
    halo = (CONV_W - 1) * B
    tile = lambda i: (0, i, 0)
    return pl.pallas_call(
        functools.partial(_rg_kernel, B=B, TT=TT),
        out_shape=jax.ShapeDtypeStruct((B, S, C), BF16),
        grid=(S // TT,),
        in_specs=[pl.BlockSpec(memory_space=pl.ANY), _const_spec((1, D)), _const_spec((D, C)),
                  _const_spec((D, C)), _const_spec((CONV_W, C)), _const_spec((1, C)),
                  _const_spec(w_a.shape), _const_spec((1, C)), _const_spec(w_x.shape),
                  _const_spec((1, C)), _const_spec((1, C))],
        out_specs=pl.BlockSpec((B, TT, C), tile),
        scratch_shapes=[pltpu.VMEM((halo + R, C), F32), pltpu.VMEM((R, C), F32),
                        pltpu.VMEM((R, C), F32), pltpu.VMEM((R, C), F32), pltpu.VMEM((B, C), F32),
                        pltpu.VMEM((2, TT, B, D), F32), pltpu.SemaphoreType.DMA((2,))],
        compiler_params=_params(("arbitrary",), 48),
        name="rg_branch",
    )(x3d, norm_g, w_ax, w_ay, conv_w, conv_b, w_a, b_a, w_x, b_x, lam)


def _alternate(first, second):
    result = None
    live = [True, True]
    while any(live):
        for n, gen in enumerate((first, second)):
            if live[n]:
                try:
                    next(gen)
                except StopIteration as stop:
                    live[n] = False
                    if n == 1:
                        result = stop.value
    return result


def _mlstm_kernel(x_ref, g_ref, wmu_ref, wmo_ref, wif_ref, wift_ref, bifc_ref, bifr_ref, cw_ref,
                  cb_ref, wq_ref, wk_ref, wv_ref, ng_ref, o_ref, ubuf, q_s, k_s, v_s, og_s, ifc_s, ifr_s,
                  lfc_s, lfr_s, c_st, n_st, m_st, *, TS, nS):
    H, d, _ = wq_ref.shape
    L = ML_CHUNK
    pad = SUBLANES
    tail = CONV_W - 1
    g = pl.program_id(0)

    @pl.when(g == 0)
    def _():
        for r in (q_s, k_s, v_s, og_s, ifc_s, ifr_s, lfc_s, lfr_s):
            r[...] = jnp.zeros_like(r)

    @pl.when(lax.rem(g, nS) == 0)
    def _():
        ubuf[...] = jnp.zeros_like(ubuf)

    @pl.when((g == 0) | (lax.rem(g + (nS - 1), nS) == 0))
    def _():
        c_st[...] = jnp.zeros_like(c_st)
        n_st[...] = jnp.zeros_like(n_st)
        m_st[...] = jnp.zeros_like(m_st)

    if_c = ifc_s[...]
    if_r = ifr_s[...]
    lf_c = lfc_s[...]
    lf_r = lfr_s[...]
    ri = lax.broadcasted_iota(jnp.int32, (L, L), 0)
    ci = lax.broadcasted_iota(jnp.int32, (L, L), 1)
    causal = ci <= ri
    tri_l = jnp.where(causal, 1.0, 0.0).astype(BF16)
    tri_u = jnp.where(ri <= ci, 1.0, 0.0).astype(BF16)
    ng = ng_ref[...]
    nck = TS // L
    h = _rms(x_ref[...], g_ref[...]).astype(BF16)
    us = [_dot(h, wmu_ref[:, 0:d])] + [None] * (H - 1)
    bcs = [sum(_dot(tri_l, p) for p in _split3(lf_c[ck * L:(ck + 1) * L, :])) for ck in range(nck)]
    brs = [sum(_dot(p, tri_u) for p in _split3(lf_r[:, ck * L:(ck + 1) * L])) for ck in range(nck)]

    def wide(t):
        return jnp.concatenate([t] * (d // LANES), axis=1)

    def recurrence(hd):
        cs = slice(hd * d, (hd + 1) * d)
        for ck in range(nck):
            r0 = ck * L
            bc, br = bcs[ck], brs[ck]
            b_col = jnp.broadcast_to(bc[:, H + hd:H + hd + 1], (L, LANES))
            i_col = jnp.broadcast_to(if_c[r0:r0 + L, hd:hd + 1], (L, LANES))
            b_row = br[H + hd:H + hd + 1, :]
            i_row = if_r[hd:hd + 1, r0:r0 + L]
            m = m_st[hd][0:1, :]
            qb = q_s[r0:r0 + L, cs]
            kb = k_s[r0:r0 + L, cs]
            vb = v_s[r0:r0 + L, cs]
            gg = b_col + m
            dm = jnp.where(causal, b_col - b_row + i_row, -jnp.inf)
            yield
            m_row = jnp.maximum(gg, jnp.max(dm, axis=-1, keepdims=True))
            s_qk = _dot_nt(qb, kb)
            yield
            w = jnp.exp(dm - m_row) * s_qk
            inter = jnp.exp(gg - m_row)
            cmat = c_st[hd]
            nvec = n_st[hd]
            yield
            num = wide(inter) * _dot(qb, cmat.astype(BF16)) + _dot(w.astype(BF16), vb)
            yield
            qf, kf, vf = qb.astype(F32), kb.astype(F32), vb.astype(F32)
            den = inter * jnp.sum(qf * nvec, axis=-1, keepdims=True) + jnp.sum(w, axis=-1, keepdims=True)
            hh = num * wide(1.0 / jnp.maximum(jnp.abs(den), jnp.exp(-m_row)))
            yield
            b_last = b_col[L - 1:L, :]
            dl = b_last - b_col + i_col
            m_new = jnp.maximum(b_last + m, jnp.max(dl, axis=0, keepdims=True))
            decay = jnp.exp(b_last + m - m_new)
            wl = jnp.exp(dl - m_new)
            yield
            c_st[hd] = wide(decay) * cmat + _dot_tn(kb, (wide(wl) * vf).astype(BF16))
            n_st[hd] = wide(decay) * nvec + jnp.sum(wide(wl) * kf, axis=0, keepdims=True)
            m_st[hd] = jnp.broadcast_to(m_new, m_st.shape[1:])
            yield
            y = og_s[r0:r0 + L, cs] * hh
            y = y * lax.rsqrt(jnp.mean(y * y, axis=-1, keepdims=True) + EPS)
            o_ref[r0:r0 + L, cs] = (y * ng[:, cs]).astype(BF16)
            yield

    def projection(hd):
        cs = slice(hd * d, (hd + 1) * d)
        u = us[hd]
        mo = _dot(h, wmo_ref[:, cs])
        yield
        if hd + 1 < H:
            us[hd + 1] = _dot(h, wmu_ref[:, (hd + 1) * d:(hd + 2) * d])
            yield
        cw = cw_ref[:, cs]
        halo = ubuf[:, cs]
        rowi = lax.broadcasted_iota(jnp.int32, (pad, d), 0)
        c = cb_ref[:, cs] + cw[tail:tail + 1, :] * u
        for j in range(1, CONV_W):
            sh = pltpu.roll(u, j, axis=0)
            head = jnp.where(rowi < j, pltpu.roll(halo, j, axis=0), sh[0:pad, :])
            sh = jnp.concatenate([head, sh[pad:, :]], axis=0)
            c = c + cw[tail - j:tail - j + 1, :] * sh
            yield
        ubuf[:, cs] = u[TS - pad:TS, :]
        cb16 = jax.nn.silu(c).astype(BF16)
        yield
        q = _dot(cb16, wq_ref[hd]).astype(BF16)
        yield
        k_ = (_dot(cb16, wk_ref[hd]) * (d ** -0.5)).astype(BF16)
        yield
        v = _dot(u.astype(BF16), wv_ref[hd]).astype(BF16)
        yield
        og = jax.nn.sigmoid(mo)
        yield
        return q, k_, v, og

    for hd in range(H):
        cs = slice(hd * d, (hd + 1) * d)
        q, k_, v, og = _alternate(recurrence(hd), projection(hd))
        q_s[:, cs] = q
        k_s[:, cs] = k_
        v_s[:, cs] = v
        og_s[:, cs] = og
    new_if_c = _dot(h, wif_ref[...]) + bifc_ref[...]
    new_if_r = _dot_nt(wift_ref[...], h) + bifr_ref[...]
    ifc_s[...] = new_if_c
    ifr_s[...] = new_if_r
    lfc_s[...] = jax.nn.log_sigmoid(new_if_c)
    lfr_s[...] = jax.nn.log_sigmoid(new_if_r)


def _mlstm_branch(x2d, B, norm_g, w_mu, w_mo, w_if, w_ift, b_if_c, b_if_r, conv_w, conv_b,
                  w_q, w_k, w_v, ml_norm_g):
    T, D = x2d.shape
    S = T // B
    C = w_mu.shape[1]
    H, d, _ = w_q.shape
    TS = min(ML_ROW_TILE, S)
    assert S % TS == 0 and TS % ML_CHUNK == 0 and d % LANES == 0
    nS = S // TS
    G = B * nS
    return pl.pallas_call(
        functools.partial(_mlstm_kernel, TS=TS, nS=nS),
        out_shape=jax.ShapeDtypeStruct((T, C), BF16),
        grid=(G + 1,),
        in_specs=[pl.BlockSpec((TS, D), lambda g: (jnp.minimum(g, G - 1), 0)), _const_spec((1, D)),
                  _const_spec((D, C)), _const_spec((D, C)), _const_spec((D, LANES)),
                  _const_spec((SUBLANES, D)), _const_spec((1, LANES)), _const_spec((SUBLANES, 1)),
                  _const_spec((CONV_W, C)), _const_spec((1, C)), _const_spec(w_q.shape),
                  _const_spec(w_k.shape), _const_spec(w_v.shape), _const_spec((1, C))],
        out_specs=pl.BlockSpec((TS, C), lambda g: (jnp.maximum(g - 1, 0), 0)),
        scratch_shapes=[pltpu.VMEM((SUBLANES, C), F32), pltpu.VMEM((TS, C), BF16),
                        pltpu.VMEM((TS, C), BF16), pltpu.VMEM((TS, C), BF16), pltpu.VMEM((TS, C), F32),
                        pltpu.VMEM((TS, LANES), F32), pltpu.VMEM((SUBLANES, TS), F32),
                        pltpu.VMEM((TS, LANES), F32), pltpu.VMEM((SUBLANES, TS), F32),
                        pltpu.VMEM((H, d, d), F32), pltpu.VMEM((H, 1, d), F32),
                        pltpu.VMEM((H, SUBLANES, LANES), F32)],
        compiler_params=_params(("arbitrary",), 48),
        name="mlstm_branch",
    )(x2d, norm_g, w_mu, w_mo, w_if, w_ift, b_if_c, b_if_r, conv_w, conv_b, w_q, w_k, w_v, ml_norm_g)


def _kv_kernel(mem_ref, g_ref, w_ref, o_ref):
    o_ref[...] = _dot(_rms(mem_ref[...], g_ref[...]).astype(BF16), w_ref[...]).astype(BF16)


def _mem_kv(mem, g, w_kv):
    B, M, D = mem.shape
    N = w_kv.shape[1]
    return pl.pallas_call(
        _kv_kernel,
        out_shape=jax.ShapeDtypeStruct((B, M, N), BF16),
        grid=(B,),
        in_specs=[pl.BlockSpec((None, M, D), lambda b: (b, 0, 0)), _const_spec((1, D)),
                  _const_spec((D, N))],
        out_specs=pl.BlockSpec((None, M, N), lambda b: (b, 0, 0)),
        compiler_params=_params(("arbitrary",), 32),
        name="mem_kv",
    )(mem, g, w_kv)


def _merge_kernel(x_ref, g_ref, wq_ref, wg_ref, bm_ref, yrg_ref, yml_ref, kv_ref, wrg_ref, wml_ref,
                  wxa_ref, wo_ref, o_ref):
    x = x_ref[...]
    D = x.shape[1]
    h = _rms(x, g_ref[...]).astype(BF16)
    q = _dot(h, wq_ref[...]).astype(BF16)
    dxa = q.shape[1]
    dh = dxa // XA_HEADS
    heads = []
    for hd in range(XA_HEADS):
        kh = kv_ref[:, hd * dh:(hd + 1) * dh]
        vh = kv_ref[:, dxa + hd * dh:dxa + (hd + 1) * dh]
        s = _dot_nt(q[:, hd * dh:(hd + 1) * dh], kh) * (dh ** -0.5)
        e = jnp.exp(s - jnp.max(s, axis=-1, keepdims=True))
        p = e / jnp.sum(e, axis=-1, keepdims=True)
        heads.append(_dot(p.astype(BF16), vh))
    y_xa = jnp.concatenate(heads, axis=1).astype(BF16)

    def gate(k):
        return jax.nn.sigmoid(_dot(h, wg_ref[:, k * D:(k + 1) * D]) + bm_ref[:, k * D:(k + 1) * D])

    merged = gate(0) * _dot(yrg_ref[...], wrg_ref[...])
    merged = merged + gate(1) * _dot(yml_ref[...], wml_ref[...])
    merged = merged + gate(2) * _dot(y_xa, wxa_ref[...])
    o_ref[...] = x + _dot(merged.astype(BF16), wo_ref[...])


def _merge(x2d, B, norm_g, w_q, w_g, b_merge, y_rg, y_ml, kv, w_br_rg, w_br_ml, w_br_xa, w_out):
    T, D = x2d.shape
    S = T // B
    C = y_ml.shape[1]
    M, N = kv.shape[1:]
    TM = min(MERGE_ROW_TILE, S)
    assert S % TM == 0
    nS = S // TM
    row = lambda b, s: (b * nS + s, 0)
    one = pl.Buffered(1)
    cspec = lambda shape: pl.BlockSpec(shape, lambda *_: (0,) * len(shape), pipeline_mode=one)
    return pl.pallas_call(
        _merge_kernel,
        out_shape=jax.ShapeDtypeStruct((T, D), F32),
        grid=(B, nS),
        in_specs=[pl.BlockSpec((TM, D), row), cspec((1, D)), cspec(w_q.shape), cspec(w_g.shape),
                  cspec(b_merge.shape), pl.BlockSpec((TM, C), row),
                  pl.BlockSpec((TM, C), row), pl.BlockSpec((None, M, N), lambda b, s: (b, 0, 0)),
                  cspec(w_br_rg.shape), cspec(w_br_ml.shape), cspec(w_br_xa.shape),
                  cspec(w_out.shape)],
        out_specs=pl.BlockSpec((TM, D), row),
        compiler_params=_params(("arbitrary", "arbitrary"), 56),
        name="merge",
    )(x2d, norm_g, w_q, w_g, b_merge, y_rg, y_ml, kv, w_br_rg, w_br_ml, w_br_xa, w_out)


def _ffn_kernel(x_ref, g_ref, w1_ref, w3_ref, w2_ref, *rest, n_cast):
    cast_in, o_ref, cast_out = rest[:n_cast], rest[n_cast], rest[n_cast + 1:2 * n_cast + 1]
    hs, acc = rest[2 * n_cast + 1:]
    f = pl.program_id(1)
    for src, dst in zip(cast_in, cast_out):
        dst[...] = src[...].astype(BF16)

    @pl.when(f == 0)
    def _():
        hs[...] = _rms(x_ref[...], g_ref[...]).astype(BF16)
        acc[...] = jnp.zeros_like(acc)

    hb = hs[...]
    mid = jax.nn.silu(_dot(hb, w1_ref[...])) * _dot(hb, w3_ref[...])
    acc[...] += _dot(mid.astype(BF16), w2_ref[...])

    @pl.when(f == pl.num_programs(1) - 1)
    def _():
        o_ref[...] = x_ref[...] + acc[...]


def _ffn_hidden_tile(F):
    for tf in (1792, 1024, 512, 256):
        if F % tf == 0:
            return tf
    return F


def _ffn_cast_block(x2d, w1, t):
    tiles = x2d.shape[0] // min(FFN_ROW_TILE, x2d.shape[0])
    nF = w1.shape[1] // _ffn_hidden_tile(w1.shape[1])
    rows = t.size // t.shape[-1]
    for steps, every_step in ((tiles * nF, True), (tiles, False)):
        if rows % steps == 0 and (rows // steps) % (2 * SUBLANES) == 0:
            return rows // steps, every_step
    return None


def _ffn(x2d, norm_g, w1, w3, w2, cast=()):
    T, D = x2d.shape
    F = w1.shape[1]
    TM = min(FFN_ROW_TILE, T)
    TF = _ffn_hidden_tile(F)
    assert T % TM == 0
    nF = F // TF
    cast2d = [t.reshape(-1, t.shape[-1]) for t in cast]
    cast_specs = []
    for t in cast2d:
        rows, every_step = _ffn_cast_block(x2d, w1, t)
        index = (lambda i, f: (i * nF + f, 0)) if every_step else (lambda i, f: (i, 0))
        cast_specs.append(pl.BlockSpec((rows, t.shape[1]), index))
    out = pl.pallas_call(
        functools.partial(_ffn_kernel, n_cast=len(cast)),
        out_shape=(jax.ShapeDtypeStruct((T, D), F32),
                   *[jax.ShapeDtypeStruct(t.shape, BF16) for t in cast2d]),
        grid=(T // TM, nF),
        in_specs=[pl.BlockSpec((TM, D), lambda i, f: (i, 0)), _const_spec((1, D)),
                  pl.BlockSpec((D, TF), lambda i, f: (0, f)), pl.BlockSpec((D, TF), lambda i, f: (0, f)),
                  pl.BlockSpec((TF, D), lambda i, f: (f, 0)), *cast_specs],
        out_specs=(pl.BlockSpec((TM, D), lambda i, f: (i, 0)), *cast_specs),
        scratch_shapes=[pltpu.VMEM((TM, D), BF16), pltpu.VMEM((TM, D), F32)],
        compiler_params=_params(("arbitrary", "arbitrary"), 56),
        name="ffn_dense",
    )(x2d, norm_g, w1, w3, w2, *cast2d)
    return out[0], [o.reshape(t.shape) for o, t in zip(out[1:], cast)]


def _route_kernel(x_ref, g_ref, wcat_ref, rb_ref, idx_ref, gate_ref, len_ref, off_ref,
                  tot_ref, run_s, *, E):
    @pl.when(pl.program_id(0) == 0)
    def _():
        run_s[...] = jnp.zeros_like(run_s)

    h2 = _rms(x_ref[...], g_ref[...])
    TM = h2.shape[0]
    hi = h2.astype(BF16)
    lo = (h2 - hi.astype(F32)).astype(BF16)
    both = _dot(hi, wcat_ref[...])
    logits = both[:, 0:LANES] + (_dot(lo, wcat_ref[:, 0:LANES]) + both[:, LANES:2 * LANES])
    lt = logits.T[0:E, :] + rb_ref[...]
    ie = lax.broadcasted_iota(jnp.int32, (E, TM), 0)
    m1 = jnp.max(lt, axis=0, keepdims=True)
    i1 = jnp.min(jnp.where(lt == m1, ie, E), axis=0, keepdims=True)
    l2 = jnp.where(ie == i1, -jnp.inf, lt)
    m2 = jnp.max(l2, axis=0, keepdims=True)
    i2 = jnp.min(jnp.where(l2 == m2, ie, E), axis=0, keepdims=True)
    ex = jnp.exp(m2 - m1)
    g1 = 1.0 / (1.0 + ex)
    g2 = ex / (1.0 + ex)
    oh1 = jnp.where(ie == i1, 1.0, 0.0)
    oh2 = jnp.where(ie == i2, 1.0, 0.0)
    oh = oh1 + oh2
    ri = lax.broadcasted_iota(jnp.int32, (TM, TM), 0)
    ci = lax.broadcasted_iota(jnp.int32, (TM, TM), 1)
    upper = jnp.where(ri < ci, 1.0, 0.0).astype(BF16)
    excl = _dot(oh.astype(BF16), upper)
    cnt = jnp.sum(oh, axis=1, keepdims=True).astype(jnp.int32)
    run_len = jnp.broadcast_to(((cnt + (RUN_ALIGN - 1)) // RUN_ALIGN) * RUN_ALIGN, (E, LANES))
    iec = lax.broadcasted_iota(jnp.int32, (E, LANES), 0)
    run_start = jnp.zeros((E, LANES), jnp.int32)
    for e in range(E - 1):
        run_start = run_start + jnp.where(iec > e, run_len[e:e + 1, :], 0)
    slot = run_start[:, 0:1].astype(F32) + excl
    s1 = jnp.sum(oh1 * slot, axis=0, keepdims=True).astype(jnp.int32)
    s2 = jnp.sum(oh2 * slot, axis=0, keepdims=True).astype(jnp.int32)
    len_ref[...] = run_len
    off_ref[...] = run_s[...]
    run_s[...] = run_s[...] + run_len
    tot_ref[...] = run_s[...]
    row = lax.broadcasted_iota(jnp.int32, (SUBLANES, TM), 0)
    idx_ref[...] = jnp.where(row == 0, i1, jnp.where(row == 1, i2, jnp.where(row == 2, s1,
                             jnp.where(row == 3, s2, 0))))
    gate_ref[...] = jnp.where(row == 0, g1, jnp.where(row == 1, g2, 0.0))


def _route(x2d, norm_g, w_cat, rb, E):
    T, D = x2d.shape
    TM = min(ROUTE_TM, T)
    nT = T // TM
    assert T % TM == 0 and E == SUBLANES
    return pl.pallas_call(
        functools.partial(_route_kernel, E=E),
        out_shape=(jax.ShapeDtypeStruct((SUBLANES, T), jnp.int32),
                   jax.ShapeDtypeStruct((SUBLANES, T), F32),
                   jax.ShapeDtypeStruct((nT * E, LANES), jnp.int32),
                   jax.ShapeDtypeStruct((nT * E, LANES), jnp.int32),
                   jax.ShapeDtypeStruct((E, LANES), jnp.int32)),
        grid=(nT,),
        in_specs=[pl.BlockSpec((TM, D), lambda i: (i, 0)), _const_spec((1, D)),
                  _const_spec((D, 2 * LANES)), _const_spec((E, 1))],
        out_specs=(pl.BlockSpec((SUBLANES, TM), lambda i: (0, i)),
                   pl.BlockSpec((SUBLANES, TM), lambda i: (0, i)),
                   pl.BlockSpec((E, LANES), lambda i: (i, 0)),
                   pl.BlockSpec((E, LANES), lambda i: (i, 0)),
                   _const_spec((E, LANES))),
        scratch_shapes=[pltpu.VMEM((E, LANES), jnp.int32)],
        compiler_params=_params(("arbitrary",), 32),
        name="moe_route",
    )(x2d, norm_g, w_cat, rb)


def _run_dmas(len_ref, off_ref, i, E, max_len, tile_ref, sorted_ref, sem, to_sorted, wait, packed=True):
    local = 0
    for e in range(E):
        n = len_ref[i * E + e]
        base = off_ref[i * E + e]
        done = 0
        sz = max_len
        while sz >= RUN_ALIGN:
            @pl.when((n & sz) != 0)
            def _(sz=sz, local=local, base=base, done=done):
                t_rows = tile_ref.at[pl.ds(pl.multiple_of(local + done, RUN_ALIGN), sz), :]
                s_rows = sorted_ref.at[pl.ds(pl.multiple_of(base + done, RUN_ALIGN), sz), :]
                cp = (pltpu.make_async_copy(t_rows, s_rows, sem) if to_sorted
                      else pltpu.make_async_copy(s_rows, t_rows, sem))
                if wait:
                    cp.wait()
                else:
                    cp.start()
            done = done + (n & sz)
            sz //= 2
        if packed:
            local = local + n


def _dispatch_kernel(len_ref, off_ref, gap_len_ref, gap_off_ref, tail_ref, x_ref, g_ref, ls_ref, xs_ref, xs_t,
                     zeros, sem, *, E):
    i = pl.program_id(0)
    TM = x_ref.shape[0]
    LS = xs_t.shape[1]
    slot = lax.rem(i, 2)
    h2 = _rms(x_ref[...], g_ref[...]).astype(BF16)
    ls = ls_ref[...]
    j = lax.broadcasted_iota(jnp.int32, (LS, TM), 0)
    onehot = jnp.where(j == ls[2:3, :], 1.0, jnp.where(j == ls[3:4, :], 1.0, 0.0)).astype(BF16)
    xs_t[slot] = _dot(onehot, h2).astype(BF16)

    @pl.when(i > 0)
    def _():
        _run_dmas(len_ref, off_ref, i - 1, E, TM, xs_t.at[1 - slot], xs_ref, sem.at[1 - slot], True, True)

    _run_dmas(len_ref, off_ref, i, E, TM, xs_t.at[slot], xs_ref, sem.at[slot], True, False)

    @pl.when(i == pl.num_programs(0) - 1)
    def _():
        _run_dmas(len_ref, off_ref, i, E, TM, xs_t.at[slot], xs_ref, sem.at[slot], True, True)
        zeros[...] = jnp.zeros_like(zeros)
        tmg = zeros.shape[0]

        def tail_copy(t):
            rows = pl.ds(pl.multiple_of(t * tmg, tmg), tmg)
            return pltpu.make_async_copy(zeros, xs_ref.at[rows, :], sem.at[slot])

        def tail_start(t, c):
            tail_copy(t).start()
            return c

        def tail_wait(t, c):
            tail_copy(t).wait()
            return c

        for wait in (False, True):
            _run_dmas(gap_len_ref, gap_off_ref, 0, E, tmg // 2, zeros, xs_ref, sem.at[slot], True, wait,
                      packed=False)
            lax.fori_loop(tail_ref[0], xs_ref.shape[0] // tmg, tail_wait if wait else tail_start, 0)


def _dispatch(run_len, run_off, gap_len, gap_off, tail_tile, idx, x2d, norm_g, P, E, TMG):
    T, D = x2d.shape
    TM = min(ROUTE_TM, T)
    LS = TOP_K * TM + E * RUN_ALIGN
    grid_spec = pltpu.PrefetchScalarGridSpec(
        num_scalar_prefetch=5,
        grid=(T // TM,),
        in_specs=[pl.BlockSpec((TM, D), lambda i, *_: (i, 0)),
                  pl.BlockSpec((1, D), lambda i, *_: (0, 0)),
                  pl.BlockSpec((SUBLANES, TM), lambda i, *_: (0, i))],
        out_specs=pl.BlockSpec(memory_space=pl.ANY),
        scratch_shapes=[pltpu.VMEM((2, LS, D), BF16), pltpu.VMEM((TMG, D), BF16),
                        pltpu.SemaphoreType.DMA((2,))])
    return pl.pallas_call(
        functools.partial(_dispatch_kernel, E=E),
        out_shape=jax.ShapeDtypeStruct((P, D), BF16),
        grid_spec=grid_spec,
        compiler_params=_params(("arbitrary",), 32),
        name="moe_dispatch",
    )(run_len, run_off, gap_len, gap_off, tail_tile, x2d, norm_g, idx)


def _group_ffn_kernel(te_ref, tn_ref, x_ref, w1_ref, w3_ref, w2_ref, o_ref, acc):
    i = pl.program_id(0)
    f = pl.program_id(1)

    @pl.when(f == 0)
    def _():
        acc[...] = jnp.zeros_like(acc)

    @pl.when(tn_ref[i] > 0)
    def _():
        hb = x_ref[...]
        mid = jax.nn.silu(_dot(hb, w1_ref[...])) * _dot(hb, w3_ref[...])
        acc[...] += _dot(mid.astype(BF16), w2_ref[...])

    @pl.when(f == pl.num_programs(1) - 1)
    def _():
        o_ref[...] = acc[...]


def _group_ffn(tile_e, tile_n, xs, w1, w3, w2, TMG):
    P, D = xs.shape
    F = w1.shape[2]
    TF = _ffn_hidden_tile(F)
    nF = F // TF
    fblk = lambda i, f, tn: jnp.where(tn[i] > 0, f, nF - 1)
    grid_spec = pltpu.PrefetchScalarGridSpec(
        num_scalar_prefetch=2,
        grid=(P // TMG, nF),
        in_specs=[pl.BlockSpec((TMG, D), lambda i, f, te, tn: (jnp.where(tn[i] > 0, i, 0), 0)),
                  pl.BlockSpec((None, D, TF), lambda i, f, te, tn: (te[i], 0, fblk(i, f, tn))),
                  pl.BlockSpec((None, D, TF), lambda i, f, te, tn: (te[i], 0, fblk(i, f, tn))),
                  pl.BlockSpec((None, TF, D), lambda i, f, te, tn: (te[i], fblk(i, f, tn), 0))],
        out_specs=pl.BlockSpec((TMG, D), lambda i, f, te, tn: (i, 0)),
        scratch_shapes=[pltpu.VMEM((TMG, D), F32)])
    return pl.pallas_call(
        _group_ffn_kernel,
        out_shape=jax.ShapeDtypeStruct((P, D), F32),
        grid_spec=grid_spec,
        compiler_params=_params(("arbitrary", "arbitrary"), 56),
        name="moe_group_ffn",
    )(tile_e, tile_n, xs, w1, w3, w2)


def _combine_kernel(len_ref, off_ref, x_ref, lsr_ref, lsc_ref, gt_ref, fg_ref, ys_ref, o_ref, yt, sem,
                    *, E, final_norm):
    i = pl.program_id(0)
    TM = x_ref.shape[0]
    _, LS, D = yt.shape
    slot = lax.rem(i, 2)

    def fetch(tile, s):
        yt[s, TOP_K * TM:LS, :] = jnp.zeros((LS - TOP_K * TM, D), F32)
        _run_dmas(len_ref, off_ref, tile, E, TM, yt.at[s], ys_ref, sem.at[s], False, False)

    @pl.when(i == 0)
    def _():
        fetch(0, 0)

    @pl.when(i + 1 < pl.num_programs(0))
    def _():
        fetch(i + 1, 1 - slot)

    lsr = lsr_ref[...]
    gt = gt_ref[...]
    js = lax.broadcasted_iota(jnp.int32, (LS, TM), 0)
    lane = lax.broadcasted_iota(jnp.int32, (TM, LANES), 1)
    gs = None
    for k in range(TOP_K):
        t1, t2, t3 = (t.astype(F32) for t in _split3(gt[:, k:k + 1]))
        terms = jnp.where(lane == 0, t1, jnp.where(lane == 1, t2, jnp.where(lane == 2, t3, 0.0)))
        pk = jnp.where(js == lsr[TOP_K + k:TOP_K + k + 1, :], 1.0, 0.0).astype(BF16)
        gk = jnp.sum(_dot(pk, terms.astype(BF16)), axis=1, keepdims=True)
        gs = gk if gs is None else gs + gk
    _run_dmas(len_ref, off_ref, i, E, TM, yt.at[slot], ys_ref, sem.at[slot], False, True)
    z = yt[slot] * gs
    zh = z.astype(BF16)
    zl = (z - zh.astype(F32)).astype(BF16)
    lsc = lsc_ref[...]
    jt = lax.broadcasted_iota(jnp.int32, (TM, LS), 1)
    pt = None
    for k in range(TOP_K):
        ok = jnp.where(jt == lsc[:, TOP_K + k:TOP_K + k + 1], 1.0, 0.0)
        pt = ok if pt is None else pt + ok
    pt = pt.astype(BF16)
    xo = x_ref[...] + (_dot(pt, zh) + _dot(pt, zl))
    o_ref[...] = _rms(xo, fg_ref[...]) if final_norm else xo


def _combine(run_len, run_off, x2d, idx, idx_t, gates_t, final_g, ys, E, final_norm):
    T, D = x2d.shape
    TM = min(ROUTE_TM, T)
    LS = TOP_K * TM + E * RUN_ALIGN
    grid_spec = pltpu.PrefetchScalarGridSpec(
        num_scalar_prefetch=2,
        grid=(T // TM,),
        in_specs=[pl.BlockSpec((TM, D), lambda i, *_: (i, 0)),
                  pl.BlockSpec((SUBLANES, TM), lambda i, *_: (0, i)),
                  pl.BlockSpec((TM, SUBLANES), lambda i, *_: (i, 0)),
                  pl.BlockSpec((TM, SUBLANES), lambda i, *_: (i, 0)),
                  pl.BlockSpec((1, D), lambda i, *_: (0, 0)),
                  pl.BlockSpec(memory_space=pl.ANY)],
        out_specs=pl.BlockSpec((TM, D), lambda i, *_: (i, 0)),
        scratch_shapes=[pltpu.VMEM((2, LS, D), F32), pltpu.SemaphoreType.DMA((2,))])
    return pl.pallas_call(
        functools.partial(_combine_kernel, E=E, final_norm=final_norm),
        out_shape=jax.ShapeDtypeStruct((T, D), F32),
        grid_spec=grid_spec,
        compiler_params=_params(("arbitrary",), 48),
        name="moe_combine",
    )(run_len, run_off, x2d, idx, idx_t, gates_t, final_g, ys)


def _moe(x2d, norm_g, router_w, router_b, w1, w3, w2, final_g, final_norm):
    T, D = x2d.shape
    E = router_w.shape[1]
    TMG = FFN_ROW_TILE
    nT = T // min(ROUTE_TM, T)
    rw = jnp.pad(router_w, ((0, 0), (0, LANES - E)))
    rw_hi = rw.astype(BF16)
    rw_lo = (rw - rw_hi.astype(F32)).astype(BF16)
    idx, gates, run_len, run_off, tot = _route(x2d, norm_g, jnp.concatenate([rw_hi, rw_lo], axis=1),
                                               router_b.reshape(E, 1), E)
    tot = tot[:, 0]
    padded = ((tot + TMG - 1) // TMG) * TMG
    pend = jnp.cumsum(padded)
    pstart = pend - padded
    run_len = run_len[:, 0]
    run_off = (run_off[:, 0].reshape(nT, E) + pstart[None, :]).reshape(nT * E)
    P = -(-(T * TOP_K + nT * E * (RUN_ALIGN - 1) + E * (TMG - 1)) // TMG) * TMG
    tile_start = jnp.arange(P // TMG, dtype=jnp.int32) * TMG
    tile_e = jnp.minimum(jnp.sum(tile_start[:, None] >= pend[None, :], axis=1), E - 1).astype(jnp.int32)
    sel = tile_e[:, None] == jnp.arange(E, dtype=jnp.int32)[None, :]
    tile_end = jnp.sum(jnp.where(sel, (pstart + tot)[None, :], 0), axis=1)
    tile_n = jnp.clip(tile_end - tile_start, 0, TMG).astype(jnp.int32)
    xs = _dispatch(run_len, run_off, padded - tot, pstart + tot, pend[E - 1:] // TMG, idx, x2d, norm_g, P,
                   E, TMG)
    ys = _group_ffn(tile_e, tile_n, xs, w1, w3, w2, TMG)
    return _combine(run_len, run_off, x2d, idx, idx.T, gates.T, final_g, ys, E, final_norm)


def _final_norm_kernel(x_ref, g_ref, o_ref):
    o_ref[...] = _rms(x_ref[...], g_ref[...])


def _final_norm(x2d, g):
    T, D = x2d.shape
    TM = min(1024, T)
    return pl.pallas_call(
        _final_norm_kernel,
        out_shape=jax.ShapeDtypeStruct((T, D), F32),
        grid=(T // TM,),
        in_specs=[pl.BlockSpec((TM, D), lambda i: (i, 0)), _const_spec((1, D))],
        out_specs=pl.BlockSpec((TM, D), lambda i: (i, 0)),
        compiler_params=_params(("arbitrary",), 32),
        name="final_norm",
    )(x2d, g)


def kernel(x, mem, norm_mix_g, w_in, conv_rg_w, conv_rg_b, rg_w_a, rg_b_a, rg_w_x, rg_b_x, rg_lambda,
           conv_ml_w, conv_ml_b, ml_w_q, ml_w_k, ml_w_v, ml_b_i, ml_b_f, ml_norm_g, mem_norm_g, w_kv,
           w_br_rg, w_br_ml, w_br_xa, b_merge, w_out, norm_ffn_g, ffn_w1, ffn_w3, ffn_w2, router_w,
           router_b, moe_w1, moe_w3, moe_w2, final_norm_g):
    B, S, D = x.shape
    depth = w_in.shape[0]
    d_rg = conv_rg_w.shape[2]
    d_ml = conv_ml_w.shape[2]
    H = ml_w_q.shape[1]
    d_xa = w_kv.shape[2] // 2
    o_ax, o_ay = 0, d_rg
    o_mu, o_mo = 2 * d_rg, 2 * d_rg + d_ml
    o_mi = 2 * d_rg + 2 * d_ml
    o_mf = o_mi + H
    o_q = o_mf + H
    o_g = o_q + d_xa
    assert w_in.shape[2] == o_g + N_BRANCH * D and 2 * H <= SUBLANES

    bf = lambda a: a.astype(BF16)
    row = lambda a: a.reshape(1, -1)
    x2d = x.reshape(B * S, D)
    fg = row(final_norm_g)
    pre = {}

    def layer_bf16(name, arr, idx):
        return pre.pop((name, idx)) if (name, idx) in pre else bf(arr[idx])

    for l in range(depth):
        w_in_bf16 = pre.pop(("w_in", l), None)
        if w_in_bf16 is None:
            cols = lambda a, b, l=l: bf(w_in[l][:, a:b])
        else:
            cols = lambda a, b, w=w_in_bf16: w[:, a:b]
        w_if = jnp.pad(cols(o_mi, o_mi + 2 * H), ((0, 0), (0, LANES - 2 * H)))
        w_ift = jnp.pad(cols(o_mi, o_mi + 2 * H).T, ((0, SUBLANES - 2 * H), (0, 0)))
        b_if = jnp.concatenate([ml_b_i[l], ml_b_f[l]])
        y_ml = _mlstm_branch(x2d, B, row(norm_mix_g[l]), cols(o_mu, o_mu + d_ml),
                                   cols(o_mo, o_mo + d_ml), w_if, w_ift,
                                   jnp.pad(b_if, (0, LANES - 2 * H)).reshape(1, LANES),
                                   jnp.pad(b_if, (0, SUBLANES - 2 * H)).reshape(SUBLANES, 1),
                                   conv_ml_w[l], row(conv_ml_b[l]), bf(ml_w_q[l]), bf(ml_w_k[l]),
                                   bf(ml_w_v[l]), row(ml_norm_g[l]))
        y_rg = _rg_branch(x2d.reshape(B, S, D), row(norm_mix_g[l]), cols(o_ax, o_ax + d_rg),
                          cols(o_ay, o_ay + d_rg), conv_rg_w[l], row(conv_rg_b[l]), bf(rg_w_a[l]),
                          row(rg_b_a[l]), bf(rg_w_x[l]), row(rg_b_x[l]), row(rg_lambda[l]))
        kv = _mem_kv(mem, row(mem_norm_g[l]), layer_bf16("w_kv", w_kv, l))
        x2d = _merge(x2d, B, row(norm_mix_g[l]), cols(o_q, o_q + d_xa), cols(o_g, o_g + N_BRANCH * D),
                     row(b_merge[l]), y_rg.reshape(B * S, d_rg), y_ml, kv,
                     layer_bf16("w_br_rg", w_br_rg, l), layer_bf16("w_br_ml", w_br_ml, l),
                     layer_bf16("w_br_xa", w_br_xa, l), layer_bf16("w_out", w_out, l))
        j = l // 2
        if l % 2 == 0:
            w1 = bf(ffn_w1[j])
            todo = {}
            if l + 1 < depth:
                todo = {(name, l + 1): arr[l + 1] for name, arr in (
                    ("w_in", w_in), ("w_kv", w_kv), ("w_br_rg", w_br_rg), ("w_br_ml", w_br_ml),
                    ("w_br_xa", w_br_xa), ("w_out", w_out))}
                if (l + 1) % 2 == 1:
                    jn = (l + 1) // 2
                    todo.update({("moe_w1", jn): moe_w1[jn], ("moe_w3", jn): moe_w3[jn],
                                 ("moe_w2", jn): moe_w2[jn]})
            todo = {key: t for key, t in todo.items() if _ffn_cast_block(x2d, w1, t) is not None}
            x2d, done = _ffn(x2d, row(norm_ffn_g[l]), w1, bf(ffn_w3[j]), bf(ffn_w2[j]), list(todo.values()))
            pre.update(zip(todo.keys(), done))
        else:
            x2d = _moe(x2d, row(norm_ffn_g[l]), router_w[j], router_b[j], layer_bf16("moe_w1", moe_w1, j),
                       layer_bf16("moe_w3", moe_w3, j), layer_bf16("moe_w2", moe_w2, j), fg, l == depth - 1)
    if depth % 2 == 1:
        x2d = _final_norm(x2d, fg)
    return x2d.reshape(B, S, D)
```

```python
import functools
import math

import jax
import jax.numpy as jnp
from jax import lax
from jax.experimental import pallas as pl
from jax.experimental.pallas import tpu as pltpu

EPS = 1e-6
RG_C = 8.0
CONV_W = 4
ML_CHUNK = 128
XA_HEADS = 4
TOP_K = 2
N_BRANCH = 3

V7X_VMEM_BYTES = 64 * 1024 * 1024
LANES = 128
SUBLANES = 8

RG_TIME_TILE = 128
ML_ROW_TILE = 512
MERGE_ROW_TILE = 1024
FFN_ROW_TILE = 512
ROUTE_TM = 512
RUN_ALIGN = 16

F32 = jnp.float32
BF16 = jnp.bfloat16


def _params(semantics, vmem_mib):
    assert vmem_mib * 1024 * 1024 <= V7X_VMEM_BYTES
    return pltpu.CompilerParams(dimension_semantics=semantics,
                                vmem_limit_bytes=vmem_mib * 1024 * 1024)


def _const_spec(shape):
    nd = len(shape)
    return pl.BlockSpec(shape, lambda *_: (0,) * nd)


def _rms(x, g):
    ms = jnp.mean(x * x, axis=-1, keepdims=True)
    return x * lax.rsqrt(ms + EPS) * g


def _dot(a, b):
    return jnp.dot(a, b, preferred_element_type=F32)


def _dot_nt(a, b):
    return lax.dot_general(a, b, (((1,), (1,)), ((), ())), preferred_element_type=F32)


def _dot_tn(a, b):
    return lax.dot_general(a, b, (((0,), (0,)), ((), ())), preferred_element_type=F32)


def _split3(x):
    h1 = x.astype(BF16)
    r1 = x - h1.astype(F32)
    h2 = r1.astype(BF16)
    h3 = (r1 - h2.astype(F32)).astype(BF16)
    return h1, h2, h3


def _rg_kernel(x_ref, g_ref, wax_ref, way_ref, cw_ref, cb_ref, wa_ref, ba_ref, wx_ref, bx_ref,
               lam_ref, o_ref, axbuf, a_s, b_s, h_s, carry, xbuf, xsem, *, B, TT):
    R = TT * B
    halo = (CONV_W - 1) * B

    @pl.when(pl.program_id(0) == 0)
    def _():
        axbuf[0:halo, :] = jnp.zeros((halo, axbuf.shape[1]), F32)
        carry[...] = jnp.zeros_like(carry)

    i = pl.program_id(0)
    slot = lax.rem(i, 2)

    def x_copies(tile, s):
        rows = pl.ds(pl.multiple_of(tile * TT, TT), TT)
        return [pltpu.make_async_copy(x_ref.at[b, rows, :], xbuf.at[s, :, b, :], xsem.at[s])
                for b in range(B)]

    @pl.when(i == 0)
    def _():
        for cp in x_copies(0, 0):
            cp.start()

    @pl.when(i + 1 < pl.num_programs(0))
    def _():
        for cp in x_copies(i + 1, 1 - slot):
            cp.start()

    for cp in x_copies(i, slot):
        cp.wait()
    x = xbuf[slot].reshape(R, xbuf.shape[3])
    h = _rms(x, g_ref[...]).astype(BF16)
    G, bi, _ = wa_ref.shape
    rate = -RG_C * jax.nn.softplus(-lam_ref[...])
    cols = [slice(g * bi, (g + 1) * bi) for g in range(G)]
    axbuf[halo:halo + R, cols[0]] = _dot(h, wax_ref[:, cols[0]])
    for g in range(G):
        cs = cols[g]
        if g + 1 < G:
            axbuf[halo:halo + R, cols[g + 1]] = _dot(h, wax_ref[:, cols[g + 1]])
        ay = _dot(h, way_ref[:, cs])
        cw = cw_ref[:, cs]
        xc = cb_ref[:, cs] + cw[0:1, :] * axbuf[0:R, cs]
        for k in range(1, CONV_W):
            xc = xc + cw[k:k + 1, :] * axbuf[k * B:k * B + R, cs]
        axbuf[0:halo, cs] = axbuf[R:R + halo, cs]
        xcb = xc.astype(BF16)
        r = jax.nn.sigmoid(_dot(xcb, wa_ref[g]) + ba_ref[:, cs])
        ig = jax.nn.sigmoid(_dot(xcb, wx_ref[g]) + bx_ref[:, cs])
        log_a = r * rate[:, cs]
        a = jnp.exp(log_a)
        a_s[:, cs] = a
        b_s[:, cs] = jnp.sqrt(-jnp.tanh(log_a) * (a * a + 1.0)) * (ig * xc)

        def step(t, hc, cs=cs):
            off = pl.multiple_of(t * B, B)
            hn = a_s[pl.ds(off, B), cs] * hc + b_s[pl.ds(off, B), cs]
            h_s[pl.ds(off, B), cs] = hn
            return hn

        carry[:, cs] = lax.fori_loop(0, TT, step, carry[:, cs], unroll=True)
        y = (h_s[:, cs] * jax.nn.gelu(ay)).astype(BF16).reshape(TT, B, bi)
        o_ref[:, :, cs] = pltpu.einshape("tbc->btc", y)


def _rg_branch(x3d, norm_g, w_ax, w_ay, conv_w, conv_b, w_a, b_a, w_x, b_x, lam):
    B, S, D = x3d.shape
    C = w_ax.shape[1]
    TT = min(RG_TIME_TILE, S)
    assert S % TT == 0 and B % SUBLANES == 0
    R = TT * B
    halo = (CONV_W - 1) * B
    tile = lambda i: (0, i, 0)
    return pl.pallas_call(
        functools.partial(_rg_kernel, B=B, TT=TT),
        out_shape=jax.ShapeDtypeStruct((B, S, C), BF16),
        grid=(S // TT,),
        in_specs=[pl.BlockSpec(memory_space=pl.ANY), _const_spec((1, D)), _const_spec((D, C)),
                  _const_spec((D, C)), _const_spec((CONV_W, C)), _const_spec((1, C)),
                  _const_spec(w_a.shape), _const_spec((1, C)), _const_spec(w_x.shape),
                  _const_spec((1, C)), _const_spec((1, C))],
        out_specs=pl.BlockSpec((B, TT, C), tile),
        scratch_shapes=[pltpu.VMEM((halo + R, C), F32), pltpu.VMEM((R, C), F32),
                        pltpu.VMEM((R, C), F32), pltpu.VMEM((R, C), F32), pltpu.VMEM((B, C), F32),
                        pltpu.VMEM((2, TT, B, D), F32), pltpu.SemaphoreType.DMA((2,))],
        compiler_params=_params(("arbitrary",), 48),
        name="rg_branch",
    )(x3d, norm_g, w_ax, w_ay, conv_w, conv_b, w_a, b_a, w_x, b_x, lam)


def _alternate(first, second):
    result = None
    live = [True, True]
    while any(live):
        for n, gen in enumerate((first, second)):
            if live[n]:
                try:
                    next(gen)
                except StopIteration as stop:
                    live[n] = False
                    if n == 1:
                        result = stop.value
    return result


def _mlstm_kernel(x_ref, g_ref, wmu_ref, wmo_ref, wif_ref, wift_ref, bifc_ref, bifr_ref, cw_ref,
                  cb_ref, wq_ref, wk_ref, wv_ref, ng_ref, o_ref, ubuf, q_s, k_s, v_s, og_s, ifc_s, ifr_s,
                  lfc_s, lfr_s, c_st, n_st, m_st, *, TS, nS):
    H, d, _ = wq_ref.shape
    L = ML_CHUNK
    pad = SUBLANES
    tail = CONV_W - 1
    g = pl.program_id(0)

    @pl.when(g == 0)
    def _():
        for r in (q_s, k_s, v_s, og_s, ifc_s, ifr_s, lfc_s, lfr_s):
            r[...] = jnp.zeros_like(r)

    @pl.when(lax.rem(g, nS) == 0)
    def _():
        ubuf[...] = jnp.zeros_like(ubuf)

    @pl.when((g == 0) | (lax.rem(g + (nS - 1), nS) == 0))
    def _():
        c_st[...] = jnp.zeros_like(c_st)
        n_st[...] = jnp.zeros_like(n_st)
        m_st[...] = jnp.zeros_like(m_st)

    if_c = ifc_s[...]
    if_r = ifr_s[...]
    lf_c = lfc_s[...]
    lf_r = lfr_s[...]
    ri = lax.broadcasted_iota(jnp.int32, (L, L), 0)
    ci = lax.broadcasted_iota(jnp.int32, (L, L), 1)
    causal = ci <= ri
    tri_l = jnp.where(causal, 1.0, 0.0).astype(BF16)
    tri_u = jnp.where(ri <= ci, 1.0, 0.0).astype(BF16)
    ng = ng_ref[...]
    nck = TS // L
    h = _rms(x_ref[...], g_ref[...]).astype(BF16)
    us = [_dot(h, wmu_ref[:, 0:d])] + [None] * (H - 1)
    bcs = [sum(_dot(tri_l, p) for p in _split3(lf_c[ck * L:(ck + 1) * L, :])) for ck in range(nck)]
    brs = [sum(_dot(p, tri_u) for p in _split3(lf_r[:, ck * L:(ck + 1) * L])) for ck in range(nck)]

    def wide(t):
        return jnp.concatenate([t] * (d // LANES), axis=1)

    def recurrence(hd):
        cs = slice(hd * d, (hd + 1) * d)
        for ck in range(nck):
            r0 = ck * L
            bc, br = bcs[ck], brs[ck]
            b_col = jnp.broadcast_to(bc[:, H + hd:H + hd + 1], (L, LANES))
            i_col = jnp.broadcast_to(if_c[r0:r0 + L, hd:hd + 1], (L, LANES))
            b_row = br[H + hd:H + hd + 1, :]
            i_row = if_r[hd:hd + 1, r0:r0 + L]
            m = m_st[hd][0:1, :]
            qb = q_s[r0:r0 + L, cs]
            kb = k_s[r0:r0 + L, cs]
            vb = v_s[r0:r0 + L, cs]
            gg = b_col + m
            dm = jnp.where(causal, b_col - b_row + i_row, -jnp.inf)
            yield
            m_row = jnp.maximum(gg, jnp.max(dm, axis=-1, keepdims=True))
            s_qk = _dot_nt(qb, kb)
            yield
            w = jnp.exp(dm - m_row) * s_qk
            inter = jnp.exp(gg - m_row)
            cmat = c_st[hd]
            nvec = n_st[hd]
            yield
            num = wide(inter) * _dot(qb, cmat.astype(BF16)) + _dot(w.astype(BF16), vb)
            yield
            qf, kf, vf = qb.astype(F32), kb.astype(F32), vb.astype(F32)
            den = inter * jnp.sum(qf * nvec, axis=-1, keepdims=True) + jnp.sum(w, axis=-1, keepdims=True)
            hh = num * wide(1.0 / jnp.maximum(jnp.abs(den), jnp.exp(-m_row)))
            yield
            b_last = b_col[L - 1:L, :]
            dl = b_last - b_col + i_col
            m_new = jnp.maximum(b_last + m, jnp.max(dl, axis=0, keepdims=True))
            decay = jnp.exp(b_last + m - m_new)
            wl = jnp.exp(dl - m_new)
            yield
            c_st[hd] = wide(decay) * cmat + _dot_tn(kb, (wide(wl) * vf).astype(BF16))
            n_st[hd] = wide(decay) * nvec + jnp.sum(wide(wl) * kf, axis=0, keepdims=True)
            m_st[hd] = jnp.broadcast_to(m_new, m_st.shape[1:])
            yield
            y = og_s[r0:r0 + L, cs] * hh
            y = y * lax.rsqrt(jnp.mean(y * y, axis=-1, keepdims=True) + EPS)
            o_ref[r0:r0 + L, cs] = (y * ng[:, cs]).astype(BF16)
            yield

    def projection(hd):
        cs = slice(hd * d, (hd + 1) * d)
        u = us[hd]
        mo = _dot(h, wmo_ref[:, cs])
        yield
        if hd + 1 < H:
            us[hd + 1] = _dot(h, wmu_ref[:, (hd + 1) * d:(hd + 2) * d])
            yield
        cw = cw_ref[:, cs]
        halo = ubuf[:, cs]
        rowi = lax.broadcasted_iota(jnp.int32, (pad, d), 0)
        c = cb_ref[:, cs] + cw[tail:tail + 1, :] * u
        for j in range(1, CONV_W):
            sh = pltpu.roll(u, j, axis=0)
            head = jnp.where(rowi < j, pltpu.roll(halo, j, axis=0), sh[0:pad, :])
            sh = jnp.concatenate([head, sh[pad:, :]], axis=0)
            c = c + cw[tail - j:tail - j + 1, :] * sh
            yield
        ubuf[:, cs] = u[TS - pad:TS, :]
        cb16 = jax.nn.silu(c).astype(BF16)
        yield
        q = _dot(cb16, wq_ref[hd]).astype(BF16)
        yield
        k_ = (_dot(cb16, wk_ref[hd]) * (d ** -0.5)).astype(BF16)
        yield
        v = _dot(u.astype(BF16), wv_ref[hd]).astype(BF16)
        yield
        og = jax.nn.sigmoid(mo)
        yield
        return q, k_, v, og

    for hd in range(H):
        cs = slice(hd * d, (hd + 1) * d)
        q, k_, v, og = _alternate(recurrence(hd), projection(hd))
        q_s[:, cs] = q
        k_s[:, cs] = k_
        v_s[:, cs] = v
        og_s[:, cs] = og
    new_if_c = _dot(h, wif_ref[...]) + bifc_ref[...]
    new_if_r = _dot_nt(wift_ref[...], h) + bifr_ref[...]
    ifc_s[...] = new_if_c
    ifr_s[...] = new_if_r
    lfc_s[...] = jax.nn.log_sigmoid(new_if_c)
    lfr_s[...] = jax.nn.log_sigmoid(new_if_r)


def _mlstm_branch(x2d, B, norm_g, w_mu, w_mo, w_if, w_ift, b_if_c, b_if_r, conv_w, conv_b,
                  w_q, w_k, w_v, ml_norm_g):
    T, D = x2d.shape
    S = T // B
    C = w_mu.shape[1]
    H, d, _ = w_q.shape
    TS = min(ML_ROW_TILE, S)
    assert S % TS == 0 and TS % ML_CHUNK == 0 and d % LANES == 0
    nS = S // TS
    G = B * nS
    return pl.pallas_call(
        functools.partial(_mlstm_kernel, TS=TS, nS=nS),
        out_shape=jax.ShapeDtypeStruct((T, C), BF16),
        grid=(G + 1,),
        in_specs=[pl.BlockSpec((TS, D), lambda g: (jnp.minimum(g, G - 1), 0)), _const_spec((1, D)),
                  _const_spec((D, C)), _const_spec((D, C)), _const_spec((D, LANES)),
                  _const_spec((SUBLANES, D)), _const_spec((1, LANES)), _const_spec((SUBLANES, 1)),
                  _const_spec((CONV_W, C)), _const_spec((1, C)), _const_spec(w_q.shape),
                  _const_spec(w_k.shape), _const_spec(w_v.shape), _const_spec((1, C))],
        out_specs=pl.BlockSpec((TS, C), lambda g: (jnp.maximum(g - 1, 0), 0)),
        scratch_shapes=[pltpu.VMEM((SUBLANES, C), F32), pltpu.VMEM((TS, C), BF16),
                        pltpu.VMEM((TS, C), BF16), pltpu.VMEM((TS, C), BF16), pltpu.VMEM((TS, C), F32),
                        pltpu.VMEM((TS, LANES), F32), pltpu.VMEM((SUBLANES, TS), F32),
                        pltpu.VMEM((TS, LANES), F32), pltpu.VMEM((SUBLANES, TS), F32),
                        pltpu.VMEM((H, d, d), F32), pltpu.VMEM((H, 1, d), F32),
                        pltpu.VMEM((H, SUBLANES, LANES), F32)],
        compiler_params=_params(("arbitrary",), 48),
        name="mlstm_branch",
    )(x2d, norm_g, w_mu, w_mo, w_if, w_ift, b_if_c, b_if_r, conv_w, conv_b, w_q, w_k, w_v, ml_norm_g)


def _kv_kernel(mem_ref, g_ref, w_ref, o_ref):
    o_ref[...] = _dot(_rms(mem_ref[...], g_ref[...]).astype(BF16), w_ref[...]).astype(BF16)


def _mem_kv(mem, g, w_kv):
    B, M, D = mem.shape
    N = w_kv.shape[1]
    return pl.pallas_call(
        _kv_kernel,
        out_shape=jax.ShapeDtypeStruct((B, M, N), BF16),
        grid=(B,),
        in_specs=[pl.BlockSpec((None, M, D), lambda b: (b, 0, 0)), _const_spec((1, D)),
                  _const_spec((D, N))],
        out_specs=pl.BlockSpec((None, M, N), lambda b: (b, 0, 0)),
        compiler_params=_params(("arbitrary",), 32),
        name="mem_kv",
    )(mem, g, w_kv)


def _merge_kernel(x_ref, g_ref, wq_ref, wg_ref, bm_ref, yrg_ref, yml_ref, kv_ref, wrg_ref, wml_ref,
                  wxa_ref, wo_ref, o_ref):
    x = x_ref[...]
    D = x.shape[1]
    h = _rms(x, g_ref[...]).astype(BF16)
    q = _dot(h, wq_ref[...]).astype(BF16)
    dxa = q.shape[1]
    dh = dxa // XA_HEADS
    heads = []
    for hd in range(XA_HEADS):
        kh = kv_ref[:, hd * dh:(hd + 1) * dh]
        vh = kv_ref[:, dxa + hd * dh:dxa + (hd + 1) * dh]
        s = _dot_nt(q[:, hd * dh:(hd + 1) * dh], kh) * (dh ** -0.5)
        e = jnp.exp(s - jnp.max(s, axis=-1, keepdims=True))
        p = e / jnp.sum(e, axis=-1, keepdims=True)
        heads.append(_dot(p.astype(BF16), vh))
    y_xa = jnp.concatenate(heads, axis=1).astype(BF16)

    def gate(k):
        return jax.nn.sigmoid(_dot(h, wg_ref[:, k * D:(k + 1) * D]) + bm_ref[:, k * D:(k + 1) * D])

    merged = gate(0) * _dot(yrg_ref[...], wrg_ref[...])
    merged = merged + gate(1) * _dot(yml_ref[...], wml_ref[...])
    merged = merged + gate(2) * _dot(y_xa, wxa_ref[...])
    o_ref[...] = x + _dot(merged.astype(BF16), wo_ref[...])


def _merge(x2d, B, norm_g, w_q, w_g, b_merge, y_rg, y_ml, kv, w_br_rg, w_br_ml, w_br_xa, w_out):
    T, D = x2d.shape
    S = T // B
    C = y_ml.shape[1]
    M, N = kv.shape[1:]
    TM = min(MERGE_ROW_TILE, S)
    assert S % TM == 0
    nS = S // TM
    row = lambda b, s: (b * nS + s, 0)
    one = pl.Buffered(1)
    cspec = lambda shape: pl.BlockSpec(shape, lambda *_: (0,) * len(shape), pipeline_mode=one)
    return pl.pallas_call(
        _merge_kernel,
        out_shape=jax.ShapeDtypeStruct((T, D), F32),
        grid=(B, nS),
        in_specs=[pl.BlockSpec((TM, D), row), cspec((1, D)), cspec(w_q.shape), cspec(w_g.shape),
                  cspec(b_merge.shape), pl.BlockSpec((TM, C), row),
                  pl.BlockSpec((TM, C), row), pl.BlockSpec((None, M, N), lambda b, s: (b, 0, 0)),
                  cspec(w_br_rg.shape), cspec(w_br_ml.shape), cspec(w_br_xa.shape),
                  cspec(w_out.shape)],
        out_specs=pl.BlockSpec((TM, D), row),
        compiler_params=_params(("arbitrary", "arbitrary"), 56),
        name="merge",
    )(x2d, norm_g, w_q, w_g, b_merge, y_rg, y_ml, kv, w_br_rg, w_br_ml, w_br_xa, w_out)


def _ffn_kernel(x_ref, g_ref, w1_ref, w3_ref, w2_ref, *rest, n_cast):
    cast_in, o_ref, cast_out = rest[:n_cast], rest[n_cast], rest[n_cast + 1:2 * n_cast + 1]
    hs, acc = rest[2 * n_cast + 1:]
    f = pl.program_id(1)
    for src, dst in zip(cast_in, cast_out):
        dst[...] = src[...].astype(BF16)

    @pl.when(f == 0)
    def _():
        hs[...] = _rms(x_ref[...], g_ref[...]).astype(BF16)
        acc[...] = jnp.zeros_like(acc)

    hb = hs[...]
    mid = jax.nn.silu(_dot(hb, w1_ref[...])) * _dot(hb, w3_ref[...])
    acc[...] += _dot(mid.astype(BF16), w2_ref[...])

    @pl.when(f == pl.num_programs(1) - 1)
    def _():
        o_ref[...] = x_ref[...] + acc[...]


def _ffn_hidden_tile(F):
    for tf in (1792, 1024, 512, 256):
        if F % tf == 0:
            return tf
    return F


def _ffn_cast_block(x2d, w1, shape):
    tiles = x2d.shape[0] // min(FFN_ROW_TILE, x2d.shape[0])
    nF = w1.shape[1] // _ffn_hidden_tile(w1.shape[1])
    rows = math.prod(shape[:-1])
    for steps, every_step in ((tiles * nF, True), (tiles, False)):
        if rows % steps == 0 and (rows // steps) % (2 * SUBLANES) == 0:
            return rows // steps, every_step
    return None


def _ffn(x2d, norm_g, w1, w3, w2, cast=()):
    T, D = x2d.shape
    F = w1.shape[1]
    TM = min(FFN_ROW_TILE, T)
    TF = _ffn_hidden_tile(F)
    assert T % TM == 0
    nF = F // TF
    cast2d, cast_in_specs, cast_out_specs, cast_shapes = [], [], [], []
    for arr, idx in cast:
        shape = arr.shape[1:]
        rows, every_step = _ffn_cast_block(x2d, w1, shape)
        layer_rows = math.prod(shape[:-1])
        first = idx * (layer_rows // rows)
        step = (lambda i, f: i * nF + f) if every_step else (lambda i, f: i)
        cast2d.append(arr.reshape(-1, shape[-1]))
        cast_in_specs.append(pl.BlockSpec((rows, shape[-1]),
                                          lambda i, f, first=first, step=step: (first + step(i, f), 0)))
        cast_out_specs.append(pl.BlockSpec((rows, shape[-1]), lambda i, f, step=step: (step(i, f), 0)))
        cast_shapes.append(jax.ShapeDtypeStruct((layer_rows, shape[-1]), BF16))
    out = pl.pallas_call(
        functools.partial(_ffn_kernel, n_cast=len(cast)),
        out_shape=(jax.ShapeDtypeStruct((T, D), F32), *cast_shapes),
        grid=(T // TM, nF),
        in_specs=[pl.BlockSpec((TM, D), lambda i, f: (i, 0)), _const_spec((1, D)),
                  pl.BlockSpec((D, TF), lambda i, f: (0, f)), pl.BlockSpec((D, TF), lambda i, f: (0, f)),
                  pl.BlockSpec((TF, D), lambda i, f: (f, 0)), *cast_in_specs],
        out_specs=(pl.BlockSpec((TM, D), lambda i, f: (i, 0)), *cast_out_specs),
        scratch_shapes=[pltpu.VMEM((TM, D), BF16), pltpu.VMEM((TM, D), F32)],
        compiler_params=_params(("arbitrary", "arbitrary"), 56),
        name="ffn_dense",
    )(x2d, norm_g, w1, w3, w2, *cast2d)
    return out[0], [o.reshape(arr.shape[1:]) for o, (arr, _) in zip(out[1:], cast)]


def _route_kernel(x_ref, g_ref, wcat_ref, rb_ref, idx_ref, gate_ref, len_ref, off_ref,
                  tot_ref, run_s, *, E):
    @pl.when(pl.program_id(0) == 0)
    def _():
        run_s[...] = jnp.zeros_like(run_s)

    h2 = _rms(x_ref[...], g_ref[...])
    TM = h2.shape[0]
    hi = h2.astype(BF16)
    lo = (h2 - hi.astype(F32)).astype(BF16)
    both = _dot(hi, wcat_ref[...])
    logits = both[:, 0:LANES] + (_dot(lo, wcat_ref[:, 0:LANES]) + both[:, LANES:2 * LANES])
    lt = logits.T[0:E, :] + rb_ref[...]
    ie = lax.broadcasted_iota(jnp.int32, (E, TM), 0)
    m1 = jnp.max(lt, axis=0, keepdims=True)
    i1 = jnp.min(jnp.where(lt == m1, ie, E), axis=0, keepdims=True)
    l2 = jnp.where(ie == i1, -jnp.inf, lt)
    m2 = jnp.max(l2, axis=0, keepdims=True)
    i2 = jnp.min(jnp.where(l2 == m2, ie, E), axis=0, keepdims=True)
    ex = jnp.exp(m2 - m1)
    g1 = 1.0 / (1.0 + ex)
    g2 = ex / (1.0 + ex)
    oh1 = jnp.where(ie == i1, 1.0, 0.0)
    oh2 = jnp.where(ie == i2, 1.0, 0.0)
    oh = oh1 + oh2
    ri = lax.broadcasted_iota(jnp.int32, (TM, TM), 0)
    ci = lax.broadcasted_iota(jnp.int32, (TM, TM), 1)
    upper = jnp.where(ri < ci, 1.0, 0.0).astype(BF16)
    excl = _dot(oh.astype(BF16), upper)
    cnt = jnp.sum(oh, axis=1, keepdims=True).astype(jnp.int32)
    run_len = jnp.broadcast_to(((cnt + (RUN_ALIGN - 1)) // RUN_ALIGN) * RUN_ALIGN, (E, LANES))
    iec = lax.broadcasted_iota(jnp.int32, (E, LANES), 0)
    run_start = jnp.zeros((E, LANES), jnp.int32)
    for e in range(E - 1):
        run_start = run_start + jnp.where(iec > e, run_len[e:e + 1, :], 0)
    slot = run_start[:, 0:1].astype(F32) + excl
    s1 = jnp.sum(oh1 * slot, axis=0, keepdims=True).astype(jnp.int32)
    s2 = jnp.sum(oh2 * slot, axis=0, keepdims=True).astype(jnp.int32)
    len_ref[...] = run_len
    off_ref[...] = run_s[...]
    run_s[...] = run_s[...] + run_len
    tot_ref[...] = run_s[...]
    row = lax.broadcasted_iota(jnp.int32, (SUBLANES, TM), 0)
    idx_ref[...] = jnp.where(row == 0, i1, jnp.where(row == 1, i2, jnp.where(row == 2, s1,
                             jnp.where(row == 3, s2, 0))))
    gate_ref[...] = jnp.where(row == 0, g1, jnp.where(row == 1, g2, 0.0))


def _route(x2d, norm_g, w_cat, rb, E):
    T, D = x2d.shape
    TM = min(ROUTE_TM, T)
    nT = T // TM
    assert T % TM == 0 and E == SUBLANES
    return pl.pallas_call(
        functools.partial(_route_kernel, E=E),
        out_shape=(jax.ShapeDtypeStruct((SUBLANES, T), jnp.int32),
                   jax.ShapeDtypeStruct((SUBLANES, T), F32),
                   jax.ShapeDtypeStruct((nT * E, LANES), jnp.int32),
                   jax.ShapeDtypeStruct((nT * E, LANES), jnp.int32),
                   jax.ShapeDtypeStruct((E, LANES), jnp.int32)),
        grid=(nT,),
        in_specs=[pl.BlockSpec((TM, D), lambda i: (i, 0)), _const_spec((1, D)),
                  _const_spec((D, 2 * LANES)), _const_spec((E, 1))],
        out_specs=(pl.BlockSpec((SUBLANES, TM), lambda i: (0, i)),
                   pl.BlockSpec((SUBLANES, TM), lambda i: (0, i)),
                   pl.BlockSpec((E, LANES), lambda i: (i, 0)),
                   pl.BlockSpec((E, LANES), lambda i: (i, 0)),
                   _const_spec((E, LANES))),
        scratch_shapes=[pltpu.VMEM((E, LANES), jnp.int32)],
        compiler_params=_params(("arbitrary",), 32),
        name="moe_route",
    )(x2d, norm_g, w_cat, rb)


def _run_dmas(len_ref, off_ref, i, E, max_len, tile_ref, sorted_ref, sem, to_sorted, wait, packed=True):
    local = 0
    for e in range(E):
        n = len_ref[i * E + e]
        base = off_ref[i * E + e]
        done = 0
        sz = max_len
        while sz >= RUN_ALIGN:
            @pl.when((n & sz) != 0)
            def _(sz=sz, local=local, base=base, done=done):
                t_rows = tile_ref.at[pl.ds(pl.multiple_of(local + done, RUN_ALIGN), sz), :]
                s_rows = sorted_ref.at[pl.ds(pl.multiple_of(base + done, RUN_ALIGN), sz), :]
                cp = (pltpu.make_async_copy(t_rows, s_rows, sem) if to_sorted
                      else pltpu.make_async_copy(s_rows, t_rows, sem))
                if wait:
                    cp.wait()
                else:
                    cp.start()
            done = done + (n & sz)
            sz //= 2
        if packed:
            local = local + n


def _dispatch_kernel(len_ref, off_ref, gap_len_ref, gap_off_ref, tail_ref, x_ref, g_ref, ls_ref, xs_ref, xs_t,
                     zeros, sem, *, E):
    i = pl.program_id(0)
    TM = x_ref.shape[0]
    LS = xs_t.shape[1]
    slot = lax.rem(i, 2)
    h2 = _rms(x_ref[...], g_ref[...]).astype(BF16)
    ls = ls_ref[...]
    j = lax.broadcasted_iota(jnp.int32, (LS, TM), 0)
    onehot = jnp.where(j == ls[2:3, :], 1.0, jnp.where(j == ls[3:4, :], 1.0, 0.0)).astype(BF16)
    xs_t[slot] = _dot(onehot, h2).astype(BF16)

    @pl.when(i > 0)
    def _():
        _run_dmas(len_ref, off_ref, i - 1, E, TM, xs_t.at[1 - slot], xs_ref, sem.at[1 - slot], True, True)

    _run_dmas(len_ref, off_ref, i, E, TM, xs_t.at[slot], xs_ref, sem.at[slot], True, False)

    @pl.when(i == pl.num_programs(0) - 1)
    def _():
        _run_dmas(len_ref, off_ref, i, E, TM, xs_t.at[slot], xs_ref, sem.at[slot], True, True)
        zeros[...] = jnp.zeros_like(zeros)
        tmg = zeros.shape[0]

        def tail_copy(t):
            rows = pl.ds(pl.multiple_of(t * tmg, tmg), tmg)
            return pltpu.make_async_copy(zeros, xs_ref.at[rows, :], sem.at[slot])

        def tail_start(t, c):
            tail_copy(t).start()
            return c

        def tail_wait(t, c):
            tail_copy(t).wait()
            return c

        for wait in (False, True):
            _run_dmas(gap_len_ref, gap_off_ref, 0, E, tmg // 2, zeros, xs_ref, sem.at[slot], True, wait,
                      packed=False)
            lax.fori_loop(tail_ref[0], xs_ref.shape[0] // tmg, tail_wait if wait else tail_start, 0)


def _dispatch(run_len, run_off, gap_len, gap_off, tail_tile, idx, x2d, norm_g, P, E, TMG):
    T, D = x2d.shape
    TM = min(ROUTE_TM, T)
    LS = TOP_K * TM + E * RUN_ALIGN
    grid_spec = pltpu.PrefetchScalarGridSpec(
        num_scalar_prefetch=5,
        grid=(T // TM,),
        in_specs=[pl.BlockSpec((TM, D), lambda i, *_: (i, 0)),
                  pl.BlockSpec((1, D), lambda i, *_: (0, 0)),
                  pl.BlockSpec((SUBLANES, TM), lambda i, *_: (0, i))],
        out_specs=pl.BlockSpec(memory_space=pl.ANY),
        scratch_shapes=[pltpu.VMEM((2, LS, D), BF16), pltpu.VMEM((TMG, D), BF16),
                        pltpu.SemaphoreType.DMA((2,))])
    return pl.pallas_call(
        functools.partial(_dispatch_kernel, E=E),
        out_shape=jax.ShapeDtypeStruct((P, D), BF16),
        grid_spec=grid_spec,
        compiler_params=_params(("arbitrary",), 32),
        name="moe_dispatch",
    )(run_len, run_off, gap_len, gap_off, tail_tile, x2d, norm_g, idx)


def _group_ffn_kernel(te_ref, tn_ref, x_ref, w1_ref, w3_ref, w2_ref, o_ref, acc):
    i = pl.program_id(0)
    f = pl.program_id(1)

    @pl.when(f == 0)
    def _():
        acc[...] = jnp.zeros_like(acc)

    @pl.when(tn_ref[i] > 0)
    def _():
        hb = x_ref[...]
        mid = jax.nn.silu(_dot(hb, w1_ref[...])) * _dot(hb, w3_ref[...])
        acc[...] += _dot(mid.astype(BF16), w2_ref[...])

    @pl.when(f == pl.num_programs(1) - 1)
    def _():
        o_ref[...] = acc[...]


def _group_ffn(tile_e, tile_n, xs, w1, w3, w2, TMG):
    P, D = xs.shape
    F = w1.shape[2]
    TF = _ffn_hidden_tile(F)
    nF = F // TF
    fblk = lambda i, f, tn: jnp.where(tn[i] > 0, f, nF - 1)
    grid_spec = pltpu.PrefetchScalarGridSpec(
        num_scalar_prefetch=2,
        grid=(P // TMG, nF),
        in_specs=[pl.BlockSpec((TMG, D), lambda i, f, te, tn: (jnp.where(tn[i] > 0, i, 0), 0)),
                  pl.BlockSpec((None, D, TF), lambda i, f, te, tn: (te[i], 0, fblk(i, f, tn))),
                  pl.BlockSpec((None, D, TF), lambda i, f, te, tn: (te[i], 0, fblk(i, f, tn))),
                  pl.BlockSpec((None, TF, D), lambda i, f, te, tn: (te[i], fblk(i, f, tn), 0))],
        out_specs=pl.BlockSpec((TMG, D), lambda i, f, te, tn: (i, 0)),
        scratch_shapes=[pltpu.VMEM((TMG, D), F32)])
    return pl.pallas_call(
        _group_ffn_kernel,
        out_shape=jax.ShapeDtypeStruct((P, D), F32),
        grid_spec=grid_spec,
        compiler_params=_params(("arbitrary", "arbitrary"), 56),
        name="moe_group_ffn",
    )(tile_e, tile_n, xs, w1, w3, w2)


def _combine_kernel(len_ref, off_ref, x_ref, lsr_ref, lsc_ref, gt_ref, fg_ref, ys_ref, o_ref, yt, sem,
                    *, E, final_norm):
    i = pl.program_id(0)
    TM = x_ref.shape[0]
    _, LS, D = yt.shape
    slot = lax.rem(i, 2)

    def fetch(tile, s):
        yt[s, TOP_K * TM:LS, :] = jnp.zeros((LS - TOP_K * TM, D), F32)
        _run_dmas(len_ref, off_ref, tile, E, TM, yt.at[s], ys_ref, sem.at[s], False, False)

    @pl.when(i == 0)
    def _():
        fetch(0, 0)

    @pl.when(i + 1 < pl.num_programs(0))
    def _():
        fetch(i + 1, 1 - slot)

    lsr = lsr_ref[...]
    gt = gt_ref[...]
    js = lax.broadcasted_iota(jnp.int32, (LS, TM), 0)
    lane = lax.broadcasted_iota(jnp.int32, (TM, LANES), 1)
    gs = None
    for k in range(TOP_K):
        t1, t2, t3 = (t.astype(F32) for t in _split3(gt[:, k:k + 1]))
        terms = jnp.where(lane == 0, t1, jnp.where(lane == 1, t2, jnp.where(lane == 2, t3, 0.0)))
        pk = jnp.where(js == lsr[TOP_K + k:TOP_K + k + 1, :], 1.0, 0.0).astype(BF16)
        gk = jnp.sum(_dot(pk, terms.astype(BF16)), axis=1, keepdims=True)
        gs = gk if gs is None else gs + gk
    _run_dmas(len_ref, off_ref, i, E, TM, yt.at[slot], ys_ref, sem.at[slot], False, True)
    z = yt[slot] * gs
    zh = z.astype(BF16)
    zl = (z - zh.astype(F32)).astype(BF16)
    lsc = lsc_ref[...]
    jt = lax.broadcasted_iota(jnp.int32, (TM, LS), 1)
    pt = None
    for k in range(TOP_K):
        ok = jnp.where(jt == lsc[:, TOP_K + k:TOP_K + k + 1], 1.0, 0.0)
        pt = ok if pt is None else pt + ok
    pt = pt.astype(BF16)
    xo = x_ref[...] + (_dot(pt, zh) + _dot(pt, zl))
    o_ref[...] = _rms(xo, fg_ref[...]) if final_norm else xo


def _combine(run_len, run_off, x2d, idx, idx_t, gates_t, final_g, ys, E, final_norm):
    T, D = x2d.shape
    TM = min(ROUTE_TM, T)
    LS = TOP_K * TM + E * RUN_ALIGN
    grid_spec = pltpu.PrefetchScalarGridSpec(
        num_scalar_prefetch=2,
        grid=(T // TM,),
        in_specs=[pl.BlockSpec((TM, D), lambda i, *_: (i, 0)),
                  pl.BlockSpec((SUBLANES, TM), lambda i, *_: (0, i)),
                  pl.BlockSpec((TM, SUBLANES), lambda i, *_: (i, 0)),
                  pl.BlockSpec((TM, SUBLANES), lambda i, *_: (i, 0)),
                  pl.BlockSpec((1, D), lambda i, *_: (0, 0)),
                  pl.BlockSpec(memory_space=pl.ANY)],
        out_specs=pl.BlockSpec((TM, D), lambda i, *_: (i, 0)),
        scratch_shapes=[pltpu.VMEM((2, LS, D), F32), pltpu.SemaphoreType.DMA((2,))])
    return pl.pallas_call(
        functools.partial(_combine_kernel, E=E, final_norm=final_norm),
        out_shape=jax.ShapeDtypeStruct((T, D), F32),
        grid_spec=grid_spec,
        compiler_params=_params(("arbitrary",), 48),
        name="moe_combine",
    )(run_len, run_off, x2d, idx, idx_t, gates_t, final_g, ys)


def _moe(x2d, norm_g, router_w, router_b, w1, w3, w2, final_g, final_norm):
    T, D = x2d.shape
    E = router_w.shape[1]
    TMG = FFN_ROW_TILE
    nT = T // min(ROUTE_TM, T)
    rw = jnp.pad(router_w, ((0, 0), (0, LANES - E)))
    rw_hi = rw.astype(BF16)
    rw_lo = (rw - rw_hi.astype(F32)).astype(BF16)
    idx, gates, run_len, run_off, tot = _route(x2d, norm_g, jnp.concatenate([rw_hi, rw_lo], axis=1),
                                               router_b.reshape(E, 1), E)
    tot = tot[:, 0]
    padded = ((tot + TMG - 1) // TMG) * TMG
    pend = jnp.cumsum(padded)
    pstart = pend - padded
    run_len = run_len[:, 0]
    run_off = (run_off[:, 0].reshape(nT, E) + pstart[None, :]).reshape(nT * E)
    P = -(-(T * TOP_K + nT * E * (RUN_ALIGN - 1) + E * (TMG - 1)) // TMG) * TMG
    tile_start = jnp.arange(P // TMG, dtype=jnp.int32) * TMG
    tile_e = jnp.minimum(jnp.sum(tile_start[:, None] >= pend[None, :], axis=1), E - 1).astype(jnp.int32)
    sel = tile_e[:, None] == jnp.arange(E, dtype=jnp.int32)[None, :]
    tile_end = jnp.sum(jnp.where(sel, (pstart + tot)[None, :], 0), axis=1)
    tile_n = jnp.clip(tile_end - tile_start, 0, TMG).astype(jnp.int32)
    xs = _dispatch(run_len, run_off, padded - tot, pstart + tot, pend[E - 1:] // TMG, idx, x2d, norm_g, P,
                   E, TMG)
    ys = _group_ffn(tile_e, tile_n, xs, w1, w3, w2, TMG)
    return _combine(run_len, run_off, x2d, idx, idx.T, gates.T, final_g, ys, E, final_norm)


def _final_norm_kernel(x_ref, g_ref, o_ref):
    o_ref[...] = _rms(x_ref[...], g_ref[...])


def _final_norm(x2d, g):
    T, D = x2d.shape
    TM = min(1024, T)
    return pl.pallas_call(
        _final_norm_kernel,
        out_shape=jax.ShapeDtypeStruct((T, D), F32),
        grid=(T // TM,),
        in_specs=[pl.BlockSpec((TM, D), lambda i: (i, 0)), _const_spec((1, D))],
        out_specs=pl.BlockSpec((TM, D), lambda i: (i, 0)),
        compiler_params=_params(("arbitrary",), 32),
        name="final_norm",
    )(x2d, g)


def kernel(x, mem, norm_mix_g, w_in, conv_rg_w, conv_rg_b, rg_w_a, rg_b_a, rg_w_x, rg_b_x, rg_lambda,
           conv_ml_w, conv_ml_b, ml_w_q, ml_w_k, ml_w_v, ml_b_i, ml_b_f, ml_norm_g, mem_norm_g, w_kv,
           w_br_rg, w_br_ml, w_br_xa, b_merge, w_out, norm_ffn_g, ffn_w1, ffn_w3, ffn_w2, router_w,
           router_b, moe_w1, moe_w3, moe_w2, final_norm_g):
    B, S, D = x.shape
    depth = w_in.shape[0]
    d_rg = conv_rg_w.shape[2]
    d_ml = conv_ml_w.shape[2]
    H = ml_w_q.shape[1]
    d_xa = w_kv.shape[2] // 2
    o_ax, o_ay = 0, d_rg
    o_mu, o_mo = 2 * d_rg, 2 * d_rg + d_ml
    o_mi = 2 * d_rg + 2 * d_ml
    o_mf = o_mi + H
    o_q = o_mf + H
    o_g = o_q + d_xa
    assert w_in.shape[2] == o_g + N_BRANCH * D and 2 * H <= SUBLANES

    bf = lambda a: a.astype(BF16)
    row = lambda a: a.reshape(1, -1)
    x2d = x.reshape(B * S, D)
    fg = row(final_norm_g)
    pre = {}

    def layer_bf16(name, arr, idx):
        return pre.pop((name, idx)) if (name, idx) in pre else bf(arr[idx])

    for l in range(depth):
        w_in_bf16 = pre.pop(("w_in", l), None)
        if w_in_bf16 is None:
            cols = lambda a, b, l=l: bf(w_in[l][:, a:b])
        else:
            cols = lambda a, b, w=w_in_bf16: w[:, a:b]
        w_if = jnp.pad(cols(o_mi, o_mi + 2 * H), ((0, 0), (0, LANES - 2 * H)))
        w_ift = jnp.pad(cols(o_mi, o_mi + 2 * H).T, ((0, SUBLANES - 2 * H), (0, 0)))
        b_if = jnp.concatenate([ml_b_i[l], ml_b_f[l]])
        y_ml = _mlstm_branch(x2d, B, row(norm_mix_g[l]), cols(o_mu, o_mu + d_ml),
                                   cols(o_mo, o_mo + d_ml), w_if, w_ift,
                                   jnp.pad(b_if, (0, LANES - 2 * H)).reshape(1, LANES),
                                   jnp.pad(b_if, (0, SUBLANES - 2 * H)).reshape(SUBLANES, 1),
                                   conv_ml_w[l], row(conv_ml_b[l]), bf(ml_w_q[l]), bf(ml_w_k[l]),
                                   bf(ml_w_v[l]), row(ml_norm_g[l]))
        y_rg = _rg_branch(x2d.reshape(B, S, D), row(norm_mix_g[l]), cols(o_ax, o_ax + d_rg),
                          cols(o_ay, o_ay + d_rg), conv_rg_w[l], row(conv_rg_b[l]), bf(rg_w_a[l]),
                          row(rg_b_a[l]), bf(rg_w_x[l]), row(rg_b_x[l]), row(rg_lambda[l]))
        kv = _mem_kv(mem, row(mem_norm_g[l]), layer_bf16("w_kv", w_kv, l))
        x2d = _merge(x2d, B, row(norm_mix_g[l]), cols(o_q, o_q + d_xa), cols(o_g, o_g + N_BRANCH * D),
                     row(b_merge[l]), y_rg.reshape(B * S, d_rg), y_ml, kv,
                     layer_bf16("w_br_rg", w_br_rg, l), layer_bf16("w_br_ml", w_br_ml, l),
                     layer_bf16("w_br_xa", w_br_xa, l), layer_bf16("w_out", w_out, l))
        j = l // 2
        if l % 2 == 0:
            w1 = bf(ffn_w1[j])
            todo = {}
            if l + 1 < depth:
                todo = {(name, l + 1): (arr, l + 1) for name, arr in (
                    ("w_in", w_in), ("w_kv", w_kv), ("w_br_rg", w_br_rg), ("w_br_ml", w_br_ml),
                    ("w_br_xa", w_br_xa), ("w_out", w_out))}
                if (l + 1) % 2 == 1:
                    jn = (l + 1) // 2
                    todo.update({("moe_w1", jn): (moe_w1, jn), ("moe_w3", jn): (moe_w3, jn),
                                 ("moe_w2", jn): (moe_w2, jn)})
            todo = {key: t for key, t in todo.items()
                    if _ffn_cast_block(x2d, w1, t[0].shape[1:]) is not None}
            x2d, done = _ffn(x2d, row(norm_ffn_g[l]), w1, bf(ffn_w3[j]), bf(ffn_w2[j]), list(todo.values()))
            pre.update(zip(todo.keys(), done))
        else:
            x2d = _moe(x2d, row(norm_ffn_g[l]), router_w[j], router_b[j], layer_bf16("moe_w1", moe_w1, j),
                       layer_bf16("moe_w3", moe_w3, j), layer_bf16("moe_w2", moe_w2, j), fg, l == depth - 1)
    if depth % 2 == 1:
        x2d = _final_norm(x2d, fg)
    return x2d.reshape(B, S, D)
```

```python
import functools
import math

import jax
import jax.numpy as jnp
from jax import lax
from jax.experimental import pallas as pl
from jax.experimental.pallas import tpu as pltpu

EPS = 1e-6
RG_C = 8.0
CONV_W = 4
ML_CHUNK = 128
XA_HEADS = 4
TOP_K = 2
N_BRANCH = 3

V7X_VMEM_BYTES = 64 * 1024 * 1024
LANES = 128
SUBLANES = 8

RG_TIME_TILE = 128
ML_ROW_TILE = 512
MERGE_ROW_TILE = 1024
FFN_ROW_TILE = 512
ROUTE_TM = 512
RUN_ALIGN = 16

F32 = jnp.float32
BF16 = jnp.bfloat16


def _params(semantics, vmem_mib):
    assert vmem_mib * 1024 * 1024 <= V7X_VMEM_BYTES
    return pltpu.CompilerParams(dimension_semantics=semantics,
                                vmem_limit_bytes=vmem_mib * 1024 * 1024)


def _const_spec(shape):
    nd = len(shape)
    return pl.BlockSpec(shape, lambda *_: (0,) * nd)


def _rms(x, g):
    ms = jnp.mean(x * x, axis=-1, keepdims=True)
    return x * lax.rsqrt(ms + EPS) * g


def _dot(a, b):
    return jnp.dot(a, b, preferred_element_type=F32)


def _dot_nt(a, b):
    return lax.dot_general(a, b, (((1,), (1,)), ((), ())), preferred_element_type=F32)


def _dot_tn(a, b):
    return lax.dot_general(a, b, (((0,), (0,)), ((), ())), preferred_element_type=F32)


def _split3(x):
    h1 = x.astype(BF16)
    r1 = x - h1.astype(F32)
    h2 = r1.astype(BF16)
    h3 = (r1 - h2.astype(F32)).astype(BF16)
    return h1, h2, h3


def _rg_kernel(x_ref, g_ref, wax_ref, way_ref, cw_ref, cb_ref, wa_ref, ba_ref, wx_ref, bx_ref,
               lam_ref, o_ref, axbuf, a_s, b_s, h_s, carry, xbuf, xsem, *, B, TT):
    R = TT * B
    halo = (CONV_W - 1) * B

    @pl.when(pl.program_id(0) == 0)
    def _():
        axbuf[0:halo, :] = jnp.zeros((halo, axbuf.shape[1]), F32)
        carry[...] = jnp.zeros_like(carry)

    i = pl.program_id(0)
    slot = lax.rem(i, 2)

    def x_copies(tile, s):
        rows = pl.ds(pl.multiple_of(tile * TT, TT), TT)
        return [pltpu.make_async_copy(x_ref.at[b, rows, :], xbuf.at[s, :, b, :], xsem.at[s])
                for b in range(B)]

    @pl.when(i == 0)
    def _():
        for cp in x_copies(0, 0):
            cp.start()

    @pl.when(i + 1 < pl.num_programs(0))
    def _():
        for cp in x_copies(i + 1, 1 - slot):
            cp.start()

    for cp in x_copies(i, slot):
        cp.wait()
    x = xbuf[slot].reshape(R, xbuf.shape[3])
    h = _rms(x, g_ref[...]).astype(BF16)
    G, bi, _ = wa_ref.shape
    rate = -RG_C * jax.nn.softplus(-lam_ref[...])
    cols = [slice(g * bi, (g + 1) * bi) for g in range(G)]
    axbuf[halo:halo + R, cols[0]] = _dot(h, wax_ref[:, cols[0]])
    for g in range(G):
        cs = cols[g]
        if g + 1 < G:
            axbuf[halo:halo + R, cols[g + 1]] = _dot(h, wax_ref[:, cols[g + 1]])
        ay = _dot(h, way_ref[:, cs])
        cw = cw_ref[:, cs]
        xc = cb_ref[:, cs] + cw[0:1, :] * axbuf[0:R, cs]
        for k in range(1, CONV_W):
            xc = xc + cw[k:k + 1, :] * axbuf[k * B:k * B + R, cs]
        axbuf[0:halo, cs] = axbuf[R:R + halo, cs]
        xcb = xc.astype(BF16)
        r = jax.nn.sigmoid(_dot(xcb, wa_ref[g]) + ba_ref[:, cs])
        ig = jax.nn.sigmoid(_dot(xcb, wx_ref[g]) + bx_ref[:, cs])
        log_a = r * rate[:, cs]
        a = jnp.exp(log_a)
        a_s[:, cs] = a
        b_s[:, cs] = jnp.sqrt(-jnp.tanh(log_a) * (a * a + 1.0)) * (ig * xc)

        def step(t, hc, cs=cs):
            off = pl.multiple_of(t * B, B)
            hn = a_s[pl.ds(off, B), cs] * hc + b_s[pl.ds(off, B), cs]
            h_s[pl.ds(off, B), cs] = hn
            return hn

        carry[:, cs] = lax.fori_loop(0, TT, step, carry[:, cs], unroll=True)
        y = (h_s[:, cs] * jax.nn.gelu(ay)).astype(BF16).reshape(TT, B, bi)
        o_ref[:, :, cs] = pltpu.einshape("tbc->btc", y)


def _rg_branch(x3d, norm_g, w_ax, w_ay, conv_w, conv_b, w_a, b_a, w_x, b_x, lam):
    B, S, D = x3d.shape
    C = w_ax.shape[1]
    TT = min(RG_TIME_TILE, S)
    assert S % TT == 0 and B % SUBLANES == 0
    R = TT * B
    halo = (CONV_W - 1) * B
    tile = lambda i: (0, i, 0)
    return pl.pallas_call(
        functools.partial(_rg_kernel, B=B, TT=TT),
        out_shape=jax.ShapeDtypeStruct((B, S, C), BF16),
        grid=(S // TT,),
        in_specs=[pl.BlockSpec(memory_space=pl.ANY), _const_spec((1, D)), _const_spec((D, C)),
                  _const_spec((D, C)), _const_spec((CONV_W, C)), _const_spec((1, C)),
                  _const_spec(w_a.shape), _const_spec((1, C)), _const_spec(w_x.shape),
                  _const_spec((1, C)), _const_spec((1, C))],
        out_specs=pl.BlockSpec((B, TT, C), tile),
        scratch_shapes=[pltpu.VMEM((halo + R, C), F32), pltpu.VMEM((R, C), F32),
                        pltpu.VMEM((R, C), F32), pltpu.VMEM((R, C), F32), pltpu.VMEM((B, C), F32),
                        pltpu.VMEM((2, TT, B, D), F32), pltpu.SemaphoreType.DMA((2,))],
        compiler_params=_params(("arbitrary",), 48),
        name="rg_branch",
    )(x3d, norm_g, w_ax, w_ay, conv_w, conv_b, w_a, b_a, w_x, b_x, lam)


def _alternate(first, second):
    result = None
    live = [True, True]
    while any(live):
        for n, gen in enumerate((first, second)):
            if live[n]:
                try:
                    next(gen)
                except StopIteration as stop:
                    live[n] = False
                    if n == 1:
                        result = stop.value
    return result


def _mlstm_kernel(x_ref, g_ref, wmu_ref, wmo_ref, wif_ref, wift_ref, bifc_ref, bifr_ref, cw_ref,
                  cb_ref, wq_ref, wk_ref, wv_ref, ng_ref, o_ref, ubuf, q_s, k_s, v_s, og_s, ifc_s, ifr_s,
                  lfc_s, lfr_s, c_st, n_st, m_st, *, TS, nS):
    H, d, _ = wq_ref.shape
    L = ML_CHUNK
    pad = SUBLANES
    tail = CONV_W - 1
    g = pl.program_id(0)

    @pl.when(g == 0)
    def _():
        for r in (q_s, k_s, v_s, og_s, ifc_s, ifr_s, lfc_s, lfr_s):
            r[...] = jnp.zeros_like(r)

    @pl.when(lax.rem(g, nS) == 0)
    def _():
        ubuf[...] = jnp.zeros_like(ubuf)

    @pl.when((g == 0) | (lax.rem(g + (nS - 1), nS) == 0))
    def _():
        c_st[...] = jnp.zeros_like(c_st)
        n_st[...] = jnp.zeros_like(n_st)
        m_st[...] = jnp.zeros_like(m_st)

    if_c = ifc_s[...]
    if_r = ifr_s[...]
    lf_c = lfc_s[...]
    lf_r = lfr_s[...]
    ri = lax.broadcasted_iota(jnp.int32, (L, L), 0)
    ci = lax.broadcasted_iota(jnp.int32, (L, L), 1)
    causal = ci <= ri
    tri_l = jnp.where(causal, 1.0, 0.0).astype(BF16)
    tri_u = jnp.where(ri <= ci, 1.0, 0.0).astype(BF16)
    ng = ng_ref[...]
    nck = TS // L
    h = _rms(x_ref[...], g_ref[...]).astype(BF16)
    us = [_dot(h, wmu_ref[:, 0:d])] + [None] * (H - 1)
    bcs = [sum(_dot(tri_l, p) for p in _split3(lf_c[ck * L:(ck + 1) * L, :])) for ck in range(nck)]
    brs = [sum(_dot(p, tri_u) for p in _split3(lf_r[:, ck * L:(ck + 1) * L])) for ck in range(nck)]

    def wide(t):
        return jnp.concatenate([t] * (d // LANES), axis=1)

    def recurrence(hd):
        cs = slice(hd * d, (hd + 1) * d)
        for ck in range(nck):
            r0 = ck * L
            bc, br = bcs[ck], brs[ck]
            b_col = jnp.broadcast_to(bc[:, H + hd:H + hd + 1], (L, LANES))
            i_col = jnp.broadcast_to(if_c[r0:r0 + L, hd:hd + 1], (L, LANES))
            b_row = br[H + hd:H + hd + 1, :]
            i_row = if_r[hd:hd + 1, r0:r0 + L]
            m = m_st[hd][0:1, :]
            qb = q_s[r0:r0 + L, cs]
            kb = k_s[r0:r0 + L, cs]
            vb = v_s[r0:r0 + L, cs]
            gg = b_col + m
            dm = jnp.where(causal, b_col - b_row + i_row, -jnp.inf)
            yield
            m_row = jnp.maximum(gg, jnp.max(dm, axis=-1, keepdims=True))
            s_qk = _dot_nt(qb, kb)
            yield
            w = jnp.exp(dm - m_row) * s_qk
            inter = jnp.exp(gg - m_row)
            cmat = c_st[hd]
            nvec = n_st[hd]
            yield
            num = wide(inter) * _dot(qb, cmat.astype(BF16)) + _dot(w.astype(BF16), vb)
            yield
            qf, kf, vf = qb.astype(F32), kb.astype(F32), vb.astype(F32)
            den = inter * jnp.sum(qf * nvec, axis=-1, keepdims=True) + jnp.sum(w, axis=-1, keepdims=True)
            hh = num * wide(1.0 / jnp.maximum(jnp.abs(den), jnp.exp(-m_row)))
            yield
            b_last = b_col[L - 1:L, :]
            dl = b_last - b_col + i_col
            m_new = jnp.maximum(b_last + m, jnp.max(dl, axis=0, keepdims=True))
            decay = jnp.exp(b_last + m - m_new)
            wl = jnp.exp(dl - m_new)
            yield
            c_st[hd] = wide(decay) * cmat + _dot_tn(kb, (wide(wl) * vf).astype(BF16))
            n_st[hd] = wide(decay) * nvec + jnp.sum(wide(wl) * kf, axis=0, keepdims=True)
            m_st[hd] = jnp.broadcast_to(m_new, m_st.shape[1:])
            yield
            y = og_s[r0:r0 + L, cs] * hh
            y = y * lax.rsqrt(jnp.mean(y * y, axis=-1, keepdims=True) + EPS)
            o_ref[r0:r0 + L, cs] = (y * ng[:, cs]).astype(BF16)
            yield

    def projection(hd):
        cs = slice(hd * d, (hd + 1) * d)
        u = us[hd]
        mo = _dot(h, wmo_ref[:, cs])
        yield
        if hd + 1 < H:
            us[hd + 1] = _dot(h, wmu_ref[:, (hd + 1) * d:(hd + 2) * d])
            yield
        cw = cw_ref[:, cs]
        halo = ubuf[:, cs]
        rowi = lax.broadcasted_iota(jnp.int32, (pad, d), 0)
        c = cb_ref[:, cs] + cw[tail:tail + 1, :] * u
        for j in range(1, CONV_W):
            sh = pltpu.roll(u, j, axis=0)
            head = jnp.where(rowi < j, pltpu.roll(halo, j, axis=0), sh[0:pad, :])
            sh = jnp.concatenate([head, sh[pad:, :]], axis=0)
            c = c + cw[tail - j:tail - j + 1, :] * sh
            yield
        ubuf[:, cs] = u[TS - pad:TS, :]
        cb16 = jax.nn.silu(c).astype(BF16)
        yield
        q = _dot(cb16, wq_ref[hd]).astype(BF16)
        yield
        k_ = (_dot(cb16, wk_ref[hd]) * (d ** -0.5)).astype(BF16)
        yield
        v = _dot(u.astype(BF16), wv_ref[hd]).astype(BF16)
        yield
        og = jax.nn.sigmoid(mo)
        yield
        return q, k_, v, og

    for hd in range(H):
        cs = slice(hd * d, (hd + 1) * d)
        q, k_, v, og = _alternate(recurrence(hd), projection(hd))
        q_s[:, cs] = q
        k_s[:, cs] = k_
        v_s[:, cs] = v
        og_s[:, cs] = og
    new_if_c = _dot(h, wif_ref[...]) + bifc_ref[...]
    new_if_r = _dot_nt(wift_ref[...], h) + bifr_ref[...]
    ifc_s[...] = new_if_c
    ifr_s[...] = new_if_r
    lfc_s[...] = jax.nn.log_sigmoid(new_if_c)
    lfr_s[...] = jax.nn.log_sigmoid(new_if_r)


def _mlstm_branch(x2d, B, norm_g, w_mu, w_mo, w_if, w_ift, b_if_c, b_if_r, conv_w, conv_b,
                  w_q, w_k, w_v, ml_norm_g):
    T, D = x2d.shape
    S = T // B
    C = w_mu.shape[1]
    H, d, _ = w_q.shape
    TS = min(ML_ROW_TILE, S)
    assert S % TS == 0 and TS % ML_CHUNK == 0 and d % LANES == 0
    nS = S // TS
    G = B * nS
    return pl.pallas_call(
        functools.partial(_mlstm_kernel, TS=TS, nS=nS),
        out_shape=jax.ShapeDtypeStruct((T, C), BF16),
        grid=(G + 1,),
        in_specs=[pl.BlockSpec((TS, D), lambda g: (jnp.minimum(g, G - 1), 0)), _const_spec((1, D)),
                  _const_spec((D, C)), _const_spec((D, C)), _const_spec((D, LANES)),
                  _const_spec((SUBLANES, D)), _const_spec((1, LANES)), _const_spec((SUBLANES, 1)),
                  _const_spec((CONV_W, C)), _const_spec((1, C)), _const_spec(w_q.shape),
                  _const_spec(w_k.shape), _const_spec(w_v.shape), _const_spec((1, C))],
        out_specs=pl.BlockSpec((TS, C), lambda g: (jnp.maximum(g - 1, 0), 0)),
        scratch_shapes=[pltpu.VMEM((SUBLANES, C), F32), pltpu.VMEM((TS, C), BF16),
                        pltpu.VMEM((TS, C), BF16), pltpu.VMEM((TS, C), BF16), pltpu.VMEM((TS, C), F32),
                        pltpu.VMEM((TS, LANES), F32), pltpu.VMEM((SUBLANES, TS), F32),
                        pltpu.VMEM((TS, LANES), F32), pltpu.VMEM((SUBLANES, TS), F32),
                        pltpu.VMEM((H, d, d), F32), pltpu.VMEM((H, 1, d), F32),
                        pltpu.VMEM((H, SUBLANES, LANES), F32)],
        compiler_params=_params(("arbitrary",), 48),
        name="mlstm_branch",
    )(x2d, norm_g, w_mu, w_mo, w_if, w_ift, b_if_c, b_if_r, conv_w, conv_b, w_q, w_k, w_v, ml_norm_g)


def _kv_kernel(mem_ref, g_ref, w_ref, o_ref):
    o_ref[...] = _dot(_rms(mem_ref[...], g_ref[...]).astype(BF16), w_ref[...]).astype(BF16)


def _mem_kv(mem, g, w_kv):
    B, M, D = mem.shape
    N = w_kv.shape[1]
    return pl.pallas_call(
        _kv_kernel,
        out_shape=jax.ShapeDtypeStruct((B, M, N), BF16),
        grid=(B,),
        in_specs=[pl.BlockSpec((None, M, D), lambda b: (b, 0, 0)), _const_spec((1, D)),
                  _const_spec((D, N))],
        out_specs=pl.BlockSpec((None, M, N), lambda b: (b, 0, 0)),
        compiler_params=_params(("arbitrary",), 32),
        name="mem_kv",
    )(mem, g, w_kv)


def _merge_kernel(x_ref, g_ref, wq_ref, wg_ref, bm_ref, yrg_ref, yml_ref, kv_ref, wrg_ref, wml_ref,
                  wxa_ref, wo_ref, o_ref):
    x = x_ref[...]
    D = x.shape[1]
    h = _rms(x, g_ref[...]).astype(BF16)
    q = _dot(h, wq_ref[...]).astype(BF16)
    dxa = q.shape[1]
    dh = dxa // XA_HEADS
    heads = []
    for hd in range(XA_HEADS):
        kh = kv_ref[:, hd * dh:(hd + 1) * dh]
        vh = kv_ref[:, dxa + hd * dh:dxa + (hd + 1) * dh]
        s = _dot_nt(q[:, hd * dh:(hd + 1) * dh], kh) * (dh ** -0.5)
        e = jnp.exp(s - jnp.max(s, axis=-1, keepdims=True))
        p = e / jnp.sum(e, axis=-1, keepdims=True)
        heads.append(_dot(p.astype(BF16), vh))
    y_xa = jnp.concatenate(heads, axis=1).astype(BF16)

    def gate(k):
        return jax.nn.sigmoid(_dot(h, wg_ref[:, k * D:(k + 1) * D]) + bm_ref[:, k * D:(k + 1) * D])

    merged = gate(0) * _dot(yrg_ref[...], wrg_ref[...])
    merged = merged + gate(1) * _dot(yml_ref[...], wml_ref[...])
    merged = merged + gate(2) * _dot(y_xa, wxa_ref[...])
    o_ref[...] = x + _dot(merged.astype(BF16), wo_ref[...])


def _merge(x2d, B, norm_g, w_q, w_g, b_merge, y_rg, y_ml, kv, w_br_rg, w_br_ml, w_br_xa, w_out):
    T, D = x2d.shape
    S = T // B
    C = y_ml.shape[1]
    M, N = kv.shape[1:]
    TM = min(MERGE_ROW_TILE, S)
    assert S % TM == 0
    nS = S // TM
    row = lambda b, s: (b * nS + s, 0)
    one = pl.Buffered(1)
    cspec = lambda shape: pl.BlockSpec(shape, lambda *_: (0,) * len(shape), pipeline_mode=one)
    return pl.pallas_call(
        _merge_kernel,
        out_shape=jax.ShapeDtypeStruct((T, D), F32),
        grid=(B, nS),
        in_specs=[pl.BlockSpec((TM, D), row), cspec((1, D)), cspec(w_q.shape), cspec(w_g.shape),
                  cspec(b_merge.shape), pl.BlockSpec((TM, C), row),
                  pl.BlockSpec((TM, C), row), pl.BlockSpec((None, M, N), lambda b, s: (b, 0, 0)),
                  cspec(w_br_rg.shape), cspec(w_br_ml.shape), cspec(w_br_xa.shape),
                  cspec(w_out.shape)],
        out_specs=pl.BlockSpec((TM, D), row),
        compiler_params=_params(("arbitrary", "arbitrary"), 56),
        name="merge",
    )(x2d, norm_g, w_q, w_g, b_merge, y_rg, y_ml, kv, w_br_rg, w_br_ml, w_br_xa, w_out)


def _ffn_kernel(x_ref, g_ref, w1_ref, w3_ref, w2_ref, *rest, n_cast):
    cast_in, o_ref, cast_out = rest[:n_cast], rest[n_cast], rest[n_cast + 1:2 * n_cast + 1]
    hs, acc = rest[2 * n_cast + 1:]
    f = pl.program_id(1)
    for src, dst in zip(cast_in, cast_out):
        dst[...] = src[...].astype(BF16)

    @pl.when(f == 0)
    def _():
        hs[...] = _rms(x_ref[...], g_ref[...]).astype(BF16)
        acc[...] = jnp.zeros_like(acc)

    hb = hs[...]
    mid = jax.nn.silu(_dot(hb, w1_ref[...])) * _dot(hb, w3_ref[...])
    acc[...] += _dot(mid.astype(BF16), w2_ref[...])

    @pl.when(f == pl.num_programs(1) - 1)
    def _():
        o_ref[...] = x_ref[...] + acc[...]


def _ffn_hidden_tile(F):
    for tf in (1792, 1024, 512, 256):
        if F % tf == 0:
            return tf
    return F


def _ffn_cast_block(x2d, w1, shape):
    tiles = x2d.shape[0] // min(FFN_ROW_TILE, x2d.shape[0])
    nF = w1.shape[1] // _ffn_hidden_tile(w1.shape[1])
    rows = math.prod(shape[:-1])
    for steps, every_step in ((tiles * nF, True), (tiles, False)):
        if rows % steps == 0 and (rows // steps) % (2 * SUBLANES) == 0:
            return rows // steps, every_step
    return None


def _ffn(x2d, norm_g, w1, w3, w2, cast=()):
    T, D = x2d.shape
    F = w1.shape[1]
    TM = min(FFN_ROW_TILE, T)
    TF = _ffn_hidden_tile(F)
    assert T % TM == 0
    nF = F // TF
    cast2d, cast_in_specs, cast_out_specs, cast_shapes = [], [], [], []
    for arr, idx in cast:
        shape = arr.shape[1:]
        rows, every_step = _ffn_cast_block(x2d, w1, shape)
        layer_rows = math.prod(shape[:-1])
        first = idx * (layer_rows // rows)
        step = (lambda i, f: i * nF + f) if every_step else (lambda i, f: i)
        cast2d.append(arr.reshape(-1, shape[-1]))
        cast_in_specs.append(pl.BlockSpec((rows, shape[-1]),
                                          lambda i, f, first=first, step=step: (first + step(i, f), 0)))
        cast_out_specs.append(pl.BlockSpec((rows, shape[-1]), lambda i, f, step=step: (step(i, f), 0)))
        cast_shapes.append(jax.ShapeDtypeStruct((layer_rows, shape[-1]), BF16))
    out = pl.pallas_call(
        functools.partial(_ffn_kernel, n_cast=len(cast)),
        out_shape=(jax.ShapeDtypeStruct((T, D), F32), *cast_shapes),
        grid=(T // TM, nF),
        in_specs=[pl.BlockSpec((TM, D), lambda i, f: (i, 0)), _const_spec((1, D)),
                  pl.BlockSpec((D, TF), lambda i, f: (0, f)), pl.BlockSpec((D, TF), lambda i, f: (0, f)),
                  pl.BlockSpec((TF, D), lambda i, f: (f, 0)), *cast_in_specs],
        out_specs=(pl.BlockSpec((TM, D), lambda i, f: (i, 0)), *cast_out_specs),
        scratch_shapes=[pltpu.VMEM((TM, D), BF16), pltpu.VMEM((TM, D), F32)],
        compiler_params=_params(("arbitrary", "arbitrary"), 56),
        name="ffn_dense",
    )(x2d, norm_g, w1, w3, w2, *cast2d)
    return out[0], [o.reshape(arr.shape[1:]) for o, (arr, _) in zip(out[1:], cast)]


def _route_kernel(x_ref, g_ref, wcat_ref, rb_ref, idx_ref, gate_ref, len_ref, off_ref,
                  tot_ref, run_s, *, E):
    @pl.when(pl.program_id(0) == 0)
    def _():
        run_s[...] = jnp.zeros_like(run_s)

    h2 = _rms(x_ref[...], g_ref[...])
    TM = h2.shape[0]
    hi = h2.astype(BF16)
    lo = (h2 - hi.astype(F32)).astype(BF16)
    both = _dot(hi, wcat_ref[...])
    logits = both[:, 0:LANES] + (_dot(lo, wcat_ref[:, 0:LANES]) + both[:, LANES:2 * LANES])
    lt = logits.T[0:E, :] + rb_ref[...]
    ie = lax.broadcasted_iota(jnp.int32, (E, TM), 0)
    m1 = jnp.max(lt, axis=0, keepdims=True)
    i1 = jnp.min(jnp.where(lt == m1, ie, E), axis=0, keepdims=True)
    l2 = jnp.where(ie == i1, -jnp.inf, lt)
    m2 = jnp.max(l2, axis=0, keepdims=True)
    i2 = jnp.min(jnp.where(l2 == m2, ie, E), axis=0, keepdims=True)
    ex = jnp.exp(m2 - m1)
    g1 = 1.0 / (1.0 + ex)
    g2 = ex / (1.0 + ex)
    oh1 = jnp.where(ie == i1, 1.0, 0.0)
    oh2 = jnp.where(ie == i2, 1.0, 0.0)
    oh = oh1 + oh2
    ri = lax.broadcasted_iota(jnp.int32, (TM, TM), 0)
    ci = lax.broadcasted_iota(jnp.int32, (TM, TM), 1)
    upper = jnp.where(ri < ci, 1.0, 0.0).astype(BF16)
    excl = _dot(oh.astype(BF16), upper)
    cnt = jnp.sum(oh, axis=1, keepdims=True).astype(jnp.int32)
    run_len = jnp.broadcast_to(((cnt + (RUN_ALIGN - 1)) // RUN_ALIGN) * RUN_ALIGN, (E, LANES))
    iec = lax.broadcasted_iota(jnp.int32, (E, LANES), 0)
    run_start = jnp.zeros((E, LANES), jnp.int32)
    for e in range(E - 1):
        run_start = run_start + jnp.where(iec > e, run_len[e:e + 1, :], 0)
    slot = run_start[:, 0:1].astype(F32) + excl
    s1 = jnp.sum(oh1 * slot, axis=0, keepdims=True).astype(jnp.int32)
    s2 = jnp.sum(oh2 * slot, axis=0, keepdims=True).astype(jnp.int32)
    len_ref[...] = run_len
    off_ref[...] = run_s[...]
    run_s[...] = run_s[...] + run_len
    tot_ref[...] = run_s[...]
    row = lax.broadcasted_iota(jnp.int32, (SUBLANES, TM), 0)
    idx_ref[...] = jnp.where(row == 0, i1, jnp.where(row == 1, i2, jnp.where(row == 2, s1,
                             jnp.where(row == 3, s2, 0))))
    gate_ref[...] = jnp.where(row == 0, g1, jnp.where(row == 1, g2, 0.0))


def _route(x2d, norm_g, w_cat, rb, E):
    T, D = x2d.shape
    TM = min(ROUTE_TM, T)
    nT = T // TM
    assert T % TM == 0 and E == SUBLANES
    return pl.pallas_call(
        functools.partial(_route_kernel, E=E),
        out_shape=(jax.ShapeDtypeStruct((SUBLANES, T), jnp.int32),
                   jax.ShapeDtypeStruct((SUBLANES, T), F32),
                   jax.ShapeDtypeStruct((nT * E, LANES), jnp.int32),
                   jax.ShapeDtypeStruct((nT * E, LANES), jnp.int32),
                   jax.ShapeDtypeStruct((E, LANES), jnp.int32)),
        grid=(nT,),
        in_specs=[pl.BlockSpec((TM, D), lambda i: (i, 0)), _const_spec((1, D)),
                  _const_spec((D, 2 * LANES)), _const_spec((E, 1))],
        out_specs=(pl.BlockSpec((SUBLANES, TM), lambda i: (0, i)),
                   pl.BlockSpec((SUBLANES, TM), lambda i: (0, i)),
                   pl.BlockSpec((E, LANES), lambda i: (i, 0)),
                   pl.BlockSpec((E, LANES), lambda i: (i, 0)),
                   _const_spec((E, LANES))),
        scratch_shapes=[pltpu.VMEM((E, LANES), jnp.int32)],
        compiler_params=_params(("arbitrary",), 32),
        name="moe_route",
    )(x2d, norm_g, w_cat, rb)


def _run_dmas(len_ref, off_ref, i, E, max_len, tile_ref, sorted_ref, sem, to_sorted, wait, packed=True):
    local = 0
    for e in range(E):
        n = len_ref[i * E + e]
        base = off_ref[i * E + e]
        done = 0
        sz = max_len
        while sz >= RUN_ALIGN:
            @pl.when((n & sz) != 0)
            def _(sz=sz, local=local, base=base, done=done):
                t_rows = tile_ref.at[pl.ds(pl.multiple_of(local + done, RUN_ALIGN), sz), :]
                s_rows = sorted_ref.at[pl.ds(pl.multiple_of(base + done, RUN_ALIGN), sz), :]
                cp = (pltpu.make_async_copy(t_rows, s_rows, sem) if to_sorted
                      else pltpu.make_async_copy(s_rows, t_rows, sem))
                if wait:
                    cp.wait()
                else:
                    cp.start()
            done = done + (n & sz)
            sz //= 2
        if packed:
            local = local + n


def _dispatch_kernel(len_ref, off_ref, gap_len_ref, gap_off_ref, tail_ref, x_ref, g_ref, ls_ref, xs_ref, xs_t,
                     zeros, sem, *, E):
    i = pl.program_id(0)
    TM = x_ref.shape[0]
    LS = xs_t.shape[1]
    slot = lax.rem(i, 2)
    h2 = _rms(x_ref[...], g_ref[...]).astype(BF16)
    ls = ls_ref[...]
    j = lax.broadcasted_iota(jnp.int32, (LS, TM), 0)
    onehot = jnp.where(j == ls[2:3, :], 1.0, jnp.where(j == ls[3:4, :], 1.0, 0.0)).astype(BF16)
    xs_t[slot] = _dot(onehot, h2).astype(BF16)

    @pl.when(i > 0)
    def _():
        _run_dmas(len_ref, off_ref, i - 1, E, TM, xs_t.at[1 - slot], xs_ref, sem.at[1 - slot], True, True)

    _run_dmas(len_ref, off_ref, i, E, TM, xs_t.at[slot], xs_ref, sem.at[slot], True, False)

    @pl.when(i == pl.num_programs(0) - 1)
    def _():
        _run_dmas(len_ref, off_ref, i, E, TM, xs_t.at[slot], xs_ref, sem.at[slot], True, True)
        zeros[...] = jnp.zeros_like(zeros)
        tmg = zeros.shape[0]

        def tail_copy(t):
            rows = pl.ds(pl.multiple_of(t * tmg, tmg), tmg)
            return pltpu.make_async_copy(zeros, xs_ref.at[rows, :], sem.at[slot])

        def tail_start(t, c):
            tail_copy(t).start()
            return c

        def tail_wait(t, c):
            tail_copy(t).wait()
            return c

        for wait in (False, True):
            _run_dmas(gap_len_ref, gap_off_ref, 0, E, tmg // 2, zeros, xs_ref, sem.at[slot], True, wait,
                      packed=False)
            lax.fori_loop(tail_ref[0], xs_ref.shape[0] // tmg, tail_wait if wait else tail_start, 0)


def _dispatch(run_len, run_off, gap_len, gap_off, tail_tile, idx, x2d, norm_g, P, E, TMG):
    T, D = x2d.shape
    TM = min(ROUTE_TM, T)
    LS = TOP_K * TM + E * RUN_ALIGN
    grid_spec = pltpu.PrefetchScalarGridSpec(
        num_scalar_prefetch=5,
        grid=(T // TM,),
        in_specs=[pl.BlockSpec((TM, D), lambda i, *_: (i, 0)),
                  pl.BlockSpec((1, D), lambda i, *_: (0, 0)),
                  pl.BlockSpec((SUBLANES, TM), lambda i, *_: (0, i))],
        out_specs=pl.BlockSpec(memory_space=pl.ANY),
        scratch_shapes=[pltpu.VMEM((2, LS, D), BF16), pltpu.VMEM((TMG, D), BF16),
                        pltpu.SemaphoreType.DMA((2,))])
    return pl.pallas_call(
        functools.partial(_dispatch_kernel, E=E),
        out_shape=jax.ShapeDtypeStruct((P, D), BF16),
        grid_spec=grid_spec,
        compiler_params=_params(("arbitrary",), 32),
        name="moe_dispatch",
    )(run_len, run_off, gap_len, gap_off, tail_tile, x2d, norm_g, idx)


def _group_ffn_kernel(te_ref, tn_ref, x_ref, w1_ref, w3_ref, w2_ref, o_ref, acc):
    i = pl.program_id(0)
    f = pl.program_id(1)

    @pl.when(f == 0)
    def _():
        acc[...] = jnp.zeros_like(acc)

    @pl.when(tn_ref[i] > 0)
    def _():
        hb = x_ref[...]
        mid = jax.nn.silu(_dot(hb, w1_ref[...])) * _dot(hb, w3_ref[...])
        acc[...] += _dot(mid.astype(BF16), w2_ref[...])

    @pl.when(f == pl.num_programs(1) - 1)
    def _():
        o_ref[...] = acc[...]


def _group_ffn(tile_e, tile_n, xs, w1, w3, w2, TMG):
    P, D = xs.shape
    F = w1.shape[2]
    TF = _ffn_hidden_tile(F)
    nF = F // TF
    fblk = lambda i, f, tn: jnp.where(tn[i] > 0, f, nF - 1)
    grid_spec = pltpu.PrefetchScalarGridSpec(
        num_scalar_prefetch=2,
        grid=(P // TMG, nF),
        in_specs=[pl.BlockSpec((TMG, D), lambda i, f, te, tn: (jnp.where(tn[i] > 0, i, 0), 0)),
                  pl.BlockSpec((None, D, TF), lambda i, f, te, tn: (te[i], 0, fblk(i, f, tn))),
                  pl.BlockSpec((None, D, TF), lambda i, f, te, tn: (te[i], 0, fblk(i, f, tn))),
                  pl.BlockSpec((None, TF, D), lambda i, f, te, tn: (te[i], fblk(i, f, tn), 0))],
        out_specs=pl.BlockSpec((TMG, D), lambda i, f, te, tn: (i, 0)),
        scratch_shapes=[pltpu.VMEM((TMG, D), F32)])
    return pl.pallas_call(
        _group_ffn_kernel,
        out_shape=jax.ShapeDtypeStruct((P, D), F32),
        grid_spec=grid_spec,
        compiler_params=_params(("arbitrary", "arbitrary"), 56),
        name="moe_group_ffn",
    )(tile_e, tile_n, xs, w1, w3, w2)


def _combine_kernel(len_ref, off_ref, x_ref, lsr_ref, lsc_ref, gt_ref, fg_ref, ys_ref, o_ref, yt, sem,
                    *, E, final_norm):
    i = pl.program_id(0)
    TM = x_ref.shape[0]
    _, LS, D = yt.shape
    slot = lax.rem(i, 2)

    def fetch(tile, s):
        yt[s, TOP_K * TM:LS, :] = jnp.zeros((LS - TOP_K * TM, D), F32)
        _run_dmas(len_ref, off_ref, tile, E, TM, yt.at[s], ys_ref, sem.at[s], False, False)

    @pl.when(i == 0)
    def _():
        fetch(0, 0)

    @pl.when(i + 1 < pl.num_programs(0))
    def _():
        fetch(i + 1, 1 - slot)

    lsr = lsr_ref[...]
    gt = gt_ref[...]
    js = lax.broadcasted_iota(jnp.int32, (LS, TM), 0)
    lane = lax.broadcasted_iota(jnp.int32, (TM, LANES), 1)
    gs = None
    for k in range(TOP_K):
        t1, t2, t3 = (t.astype(F32) for t in _split3(gt[:, k:k + 1]))
        terms = jnp.where(lane == 0, t1, jnp.where(lane == 1, t2, jnp.where(lane == 2, t3, 0.0)))
        pk = jnp.where(js == lsr[TOP_K + k:TOP_K + k + 1, :], 1.0, 0.0).astype(BF16)
        gk = jnp.sum(_dot(pk, terms.astype(BF16)), axis=1, keepdims=True)
        gs = gk if gs is None else gs + gk
    _run_dmas(len_ref, off_ref, i, E, TM, yt.at[slot], ys_ref, sem.at[slot], False, True)
    z = yt[slot] * gs
    zh = z.astype(BF16)
    zl = (z - zh.astype(F32)).astype(BF16)
    lsc = lsc_ref[...]
    jt = lax.broadcasted_iota(jnp.int32, (TM, LS), 1)
    pt = None
    for k in range(TOP_K):
        ok = jnp.where(jt == lsc[:, TOP_K + k:TOP_K + k + 1], 1.0, 0.0)
        pt = ok if pt is None else pt + ok
    pt = pt.astype(BF16)
    xo = x_ref[...] + (_dot(pt, zh) + _dot(pt, zl))
    o_ref[...] = _rms(xo, fg_ref[...]) if final_norm else xo


def _combine(run_len, run_off, x2d, idx, idx_t, gates_t, final_g, ys, E, final_norm):
    T, D = x2d.shape
    TM = min(ROUTE_TM, T)
    LS = TOP_K * TM + E * RUN_ALIGN
    grid_spec = pltpu.PrefetchScalarGridSpec(
        num_scalar_prefetch=2,
        grid=(T // TM,),
        in_specs=[pl.BlockSpec((TM, D), lambda i, *_: (i, 0)),
                  pl.BlockSpec((SUBLANES, TM), lambda i, *_: (0, i)),
                  pl.BlockSpec((TM, SUBLANES), lambda i, *_: (i, 0)),
                  pl.BlockSpec((TM, SUBLANES), lambda i, *_: (i, 0)),
                  pl.BlockSpec((1, D), lambda i, *_: (0, 0)),
                  pl.BlockSpec(memory_space=pl.ANY)],
        out_specs=pl.BlockSpec((TM, D), lambda i, *_: (i, 0)),
        scratch_shapes=[pltpu.VMEM((2, LS, D), F32), pltpu.SemaphoreType.DMA((2,))])
    return pl.pallas_call(
        functools.partial(_combine_kernel, E=E, final_norm=final_norm),
        out_shape=jax.ShapeDtypeStruct((T, D), F32),
        grid_spec=grid_spec,
        compiler_params=_params(("arbitrary",), 48),
        name="moe_combine",
    )(run_len, run_off, x2d, idx, idx_t, gates_t, final_g, ys)


def _moe(x2d, norm_g, router_w, router_b, w1, w3, w2, final_g, final_norm):
    T, D = x2d.shape
    E = router_w.shape[1]
    TMG = FFN_ROW_TILE
    nT = T // min(ROUTE_TM, T)
    rw = jnp.pad(router_w, ((0, 0), (0, LANES - E)))
    rw_hi = rw.astype(BF16)
    rw_lo = (rw - rw_hi.astype(F32)).astype(BF16)
    idx, gates, run_len, run_off, tot = _route(x2d, norm_g, jnp.concatenate([rw_hi, rw_lo], axis=1),
                                               router_b.reshape(E, 1), E)
    tot = tot[:, 0]
    padded = ((tot + TMG - 1) // TMG) * TMG
    pend = jnp.cumsum(padded)
    pstart = pend - padded
    run_len = run_len[:, 0]
    run_off = (run_off[:, 0].reshape(nT, E) + pstart[None, :]).reshape(nT * E)
    P = -(-(T * TOP_K + nT * E * (RUN_ALIGN - 1) + E * (TMG - 1)) // TMG) * TMG
    tile_start = jnp.arange(P // TMG, dtype=jnp.int32) * TMG
    tile_e = jnp.minimum(jnp.sum(tile_start[:, None] >= pend[None, :], axis=1), E - 1).astype(jnp.int32)
    sel = tile_e[:, None] == jnp.arange(E, dtype=jnp.int32)[None, :]
    tile_end = jnp.sum(jnp.where(sel, (pstart + tot)[None, :], 0), axis=1)
    tile_n = jnp.clip(tile_end - tile_start, 0, TMG).astype(jnp.int32)
    xs = _dispatch(run_len, run_off, padded - tot, pstart + tot, pend[E - 1:] // TMG, idx, x2d, norm_g, P,
                   E, TMG)
    ys = _group_ffn(tile_e, tile_n, xs, w1, w3, w2, TMG)
    return _combine(run_len, run_off, x2d, idx, idx.T, gates.T, final_g, ys, E, final_norm)


def _final_norm_kernel(x_ref, g_ref, o_ref):
    o_ref[...] = _rms(x_ref[...], g_ref[...])


def _final_norm(x2d, g):
    T, D = x2d.shape
    TM = min(1024, T)
    return pl.pallas_call(
        _final_norm_kernel,
        out_shape=jax.ShapeDtypeStruct((T, D), F32),
        grid=(T // TM,),
        in_specs=[pl.BlockSpec((TM, D), lambda i: (i, 0)), _const_spec((1, D))],
        out_specs=pl.BlockSpec((TM, D), lambda i: (i, 0)),
        compiler_params=_params(("arbitrary",), 32),
        name="final_norm",
    )(x2d, g)


def kernel(x, mem, norm_mix_g, w_in, conv_rg_w, conv_rg_b, rg_w_a, rg_b_a, rg_w_x, rg_b_x, rg_lambda,
           conv_ml_w, conv_ml_b, ml_w_q, ml_w_k, ml_w_v, ml_b_i, ml_b_f, ml_norm_g, mem_norm_g, w_kv,
           w_br_rg, w_br_ml, w_br_xa, b_merge, w_out, norm_ffn_g, ffn_w1, ffn_w3, ffn_w2, router_w,
           router_b, moe_w1, moe_w3, moe_w2, final_norm_g):
    B, S, D = x.shape
    depth = w_in.shape[0]
    d_rg = conv_rg_w.shape[2]
    d_ml = conv_ml_w.shape[2]
    H = ml_w_q.shape[1]
    d_xa = w_kv.shape[2] // 2
    o_ax, o_ay = 0, d_rg
    o_mu, o_mo = 2 * d_rg, 2 * d_rg + d_ml
    o_mi = 2 * d_rg + 2 * d_ml
    o_mf = o_mi + H
    o_q = o_mf + H
    o_g = o_q + d_xa
    assert w_in.shape[2] == o_g + N_BRANCH * D and 2 * H <= SUBLANES

    bf = lambda a: a.astype(BF16)
    row = lambda a: a.reshape(1, -1)
    x2d = x.reshape(B * S, D)
    fg = row(final_norm_g)
    pre = {}

    def layer_bf16(name, arr, idx):
        return pre.pop((name, idx)) if (name, idx) in pre else bf(arr[idx])

    for l in range(depth):
        cols = lambda a, b, l=l: bf(w_in[l][:, a:b])
        w_if = jnp.pad(cols(o_mi, o_mi + 2 * H), ((0, 0), (0, LANES - 2 * H)))
        w_ift = jnp.pad(cols(o_mi, o_mi + 2 * H).T, ((0, SUBLANES - 2 * H), (0, 0)))
        b_if = jnp.concatenate([ml_b_i[l], ml_b_f[l]])
        y_ml = _mlstm_branch(x2d, B, row(norm_mix_g[l]), cols(o_mu, o_mu + d_ml),
                                   cols(o_mo, o_mo + d_ml), w_if, w_ift,
                                   jnp.pad(b_if, (0, LANES - 2 * H)).reshape(1, LANES),
                                   jnp.pad(b_if, (0, SUBLANES - 2 * H)).reshape(SUBLANES, 1),
                                   conv_ml_w[l], row(conv_ml_b[l]), bf(ml_w_q[l]), bf(ml_w_k[l]),
                                   bf(ml_w_v[l]), row(ml_norm_g[l]))
        y_rg = _rg_branch(x2d.reshape(B, S, D), row(norm_mix_g[l]), cols(o_ax, o_ax + d_rg),
                          cols(o_ay, o_ay + d_rg), conv_rg_w[l], row(conv_rg_b[l]), bf(rg_w_a[l]),
                          row(rg_b_a[l]), bf(rg_w_x[l]), row(rg_b_x[l]), row(rg_lambda[l]))
        kv = _mem_kv(mem, row(mem_norm_g[l]), layer_bf16("w_kv", w_kv, l))
        x2d = _merge(x2d, B, row(norm_mix_g[l]), cols(o_q, o_q + d_xa), cols(o_g, o_g + N_BRANCH * D),
                     row(b_merge[l]), y_rg.reshape(B * S, d_rg), y_ml, kv,
                     layer_bf16("w_br_rg", w_br_rg, l), layer_bf16("w_br_ml", w_br_ml, l),
                     layer_bf16("w_br_xa", w_br_xa, l), layer_bf16("w_out", w_out, l))
        j = l // 2
        if l % 2 == 0:
            w1 = bf(ffn_w1[j])
            todo = {}
            if l + 1 < depth:
                todo = {(name, l + 1): (arr, l + 1) for name, arr in (
                    ("w_kv", w_kv), ("w_br_rg", w_br_rg), ("w_br_ml", w_br_ml), ("w_br_xa", w_br_xa),
                    ("w_out", w_out))}
                if (l + 1) % 2 == 1:
                    jn = (l + 1) // 2
                    todo.update({("moe_w1", jn): (moe_w1, jn), ("moe_w3", jn): (moe_w3, jn),
                                 ("moe_w2", jn): (moe_w2, jn)})
            todo = {key: t for key, t in todo.items()
                    if _ffn_cast_block(x2d, w1, t[0].shape[1:]) is not None}
            x2d, done = _ffn(x2d, row(norm_ffn_g[l]), w1, bf(ffn_w3[j]), bf(ffn_w2[j]), list(todo.values()))
            pre.update(zip(todo.keys(), done))
        else:
            x2d = _moe(x2d, row(norm_ffn_g[l]), router_w[j], router_b[j], layer_bf16("moe_w1", moe_w1, j),
                       layer_bf16("moe_w3", moe_w3, j), layer_bf16("moe_w2", moe_w2, j), fg, l == depth - 1)
    if depth % 2 == 1:
        x2d = _final_norm(x2d, fg)
    return x2d.reshape(B, S, D)
```

```python
import functools
import math

import jax
import jax.numpy as jnp
from jax import lax
from jax.experimental import pallas as pl
from jax.experimental.pallas import tpu as pltpu

EPS = 1e-6
RG_C = 8.0
CONV_W = 4
ML_CHUNK = 128
XA_HEADS = 4
TOP_K = 2
N_BRANCH = 3

V7X_VMEM_BYTES = 64 * 1024 * 1024
LANES = 128
SUBLANES = 8

RG_TIME_TILE = 128
ML_ROW_TILE = 512
MERGE_ROW_TILE = 1024
FFN_ROW_TILE = 512
ROUTE_TM = 512
RUN_ALIGN = 16

F32 = jnp.float32
BF16 = jnp.bfloat16


def _params(semantics, vmem_mib):
    assert vmem_mib * 1024 * 1024 <= V7X_VMEM_BYTES
    return pltpu.CompilerParams(dimension_semantics=semantics,
                                vmem_limit_bytes=vmem_mib * 1024 * 1024)


def _const_spec(shape):
    nd = len(shape)
    return pl.BlockSpec(shape, lambda *_: (0,) * nd)


def _rms(x, g):
    ms = jnp.mean(x * x, axis=-1, keepdims=True)
    return x * lax.rsqrt(ms + EPS) * g


def _dot(a, b):
    return jnp.dot(a, b, preferred_element_type=F32)


def _dot_nt(a, b):
    return lax.dot_general(a, b, (((1,), (1,)), ((), ())), preferred_element_type=F32)


def _dot_tn(a, b):
    return lax.dot_general(a, b, (((0,), (0,)), ((), ())), preferred_element_type=F32)


def _split3(x):
    h1 = x.astype(BF16)
    r1 = x - h1.astype(F32)
    h2 = r1.astype(BF16)
    h3 = (r1 - h2.astype(F32)).astype(BF16)
    return h1, h2, h3


def _rg_kernel(x_ref, g_ref, wax_ref, way_ref, cw_ref, cb_ref, wa_ref, ba_ref, wx_ref, bx_ref,
               lam_ref, o_ref, axbuf, a_s, b_s, h_s, carry, xbuf, xsem, *, B, TT):
    R = TT * B
    halo = (CONV_W - 1) * B

    @pl.when(pl.program_id(0) == 0)
    def _():
        axbuf[0:halo, :] = jnp.zeros((halo, axbuf.shape[1]), F32)
        carry[...] = jnp.zeros_like(carry)

    i = pl.program_id(0)
    slot = lax.rem(i, 2)

    def x_copies(tile, s):
        rows = pl.ds(pl.multiple_of(tile * TT, TT), TT)
        return [pltpu.make_async_copy(x_ref.at[b, rows, :], xbuf.at[s, :, b, :], xsem.at[s])
                for b in range(B)]

    @pl.when(i == 0)
    def _():
        for cp in x_copies(0, 0):
            cp.start()

    @pl.when(i + 1 < pl.num_programs(0))
    def _():
        for cp in x_copies(i + 1, 1 - slot):
            cp.start()

    for cp in x_copies(i, slot):
        cp.wait()
    x = xbuf[slot].reshape(R, xbuf.shape[3])
    h = _rms(x, g_ref[...]).astype(BF16)
    G, bi, _ = wa_ref.shape
    rate = -RG_C * jax.nn.softplus(-lam_ref[...])
    cols = [slice(g * bi, (g + 1) * bi) for g in range(G)]
    axbuf[halo:halo + R, cols[0]] = _dot(h, wax_ref[:, cols[0]])
    for g in range(G):
        cs = cols[g]
        if g + 1 < G:
            axbuf[halo:halo + R, cols[g + 1]] = _dot(h, wax_ref[:, cols[g + 1]])
        ay = _dot(h, way_ref[:, cs])
        cw = cw_ref[:, cs]
        xc = cb_ref[:, cs] + cw[0:1, :] * axbuf[0:R, cs]
        for k in range(1, CONV_W):
            xc = xc + cw[k:k + 1, :] * axbuf[k * B:k * B + R, cs]
        axbuf[0:halo, cs] = axbuf[R:R + halo, cs]
        xcb = xc.astype(BF16)
        r = jax.nn.sigmoid(_dot(xcb, wa_ref[g]) + ba_ref[:, cs])
        ig = jax.nn.sigmoid(_dot(xcb, wx_ref[g]) + bx_ref[:, cs])
        log_a = r * rate[:, cs]
        a = jnp.exp(log_a)
        a_s[:, cs] = a
        b_s[:, cs] = jnp.sqrt(-jnp.tanh(log_a) * (a * a + 1.0)) * (ig * xc)

        def step(t, hc, cs=cs):
            off = pl.multiple_of(t * B, B)
            hn = a_s[pl.ds(off, B), cs] * hc + b_s[pl.ds(off, B), cs]
            h_s[pl.ds(off, B), cs] = hn
            return hn

        carry[:, cs] = lax.fori_loop(0, TT, step, carry[:, cs], unroll=True)
        y = (h_s[:, cs] * jax.nn.gelu(ay)).astype(BF16).reshape(TT, B, bi)
        o_ref[:, :, cs] = pltpu.einshape("tbc->btc", y)


def _rg_branch(x3d, norm_g, w_ax, w_ay, conv_w, conv_b, w_a, b_a, w_x, b_x, lam):
    B, S, D = x3d.shape
    C = w_ax.shape[1]
    TT = min(RG_TIME_TILE, S)
    assert S % TT == 0 and B % SUBLANES == 0
    R = TT * B
    halo = (CONV_W - 1) * B
    tile = lambda i: (0, i, 0)
    return pl.pallas_call(
        functools.partial(_rg_kernel, B=B, TT=TT),
        out_shape=jax.ShapeDtypeStruct((B, S, C), BF16),
        grid=(S // TT,),
        in_specs=[pl.BlockSpec(memory_space=pl.ANY), _const_spec((1, D)), _const_spec((D, C)),
                  _const_spec((D, C)), _const_spec((CONV_W, C)), _const_spec((1, C)),
                  _const_spec(w_a.shape), _const_spec((1, C)), _const_spec(w_x.shape),
                  _const_spec((1, C)), _const_spec((1, C))],
        out_specs=pl.BlockSpec((B, TT, C), tile),
        scratch_shapes=[pltpu.VMEM((halo + R, C), F32), pltpu.VMEM((R, C), F32),
                        pltpu.VMEM((R, C), F32), pltpu.VMEM((R, C), F32), pltpu.VMEM((B, C), F32),
                        pltpu.VMEM((2, TT, B, D), F32), pltpu.SemaphoreType.DMA((2,))],
        compiler_params=_params(("arbitrary",), 48),
        name="rg_branch",
    )(x3d, norm_g, w_ax, w_ay, conv_w, conv_b, w_a, b_a, w_x, b_x, lam)


def _alternate(first, second):
    result = None
    live = [True, True]
    while any(live):
        for n, gen in enumerate((first, second)):
            if live[n]:
                try:
                    next(gen)
                except StopIteration as stop:
                    live[n] = False
                    if n == 1:
                        result = stop.value
    return result


def _mlstm_kernel(x_ref, g_ref, wmu_ref, wmo_ref, wif_ref, wift_ref, bifc_ref, bifr_ref, cw_ref,
                  cb_ref, wq_ref, wk_ref, wv_ref, ng_ref, o_ref, ubuf, q_s, k_s, v_s, og_s, ifc_s, ifr_s,
                  lfc_s, lfr_s, c_st, n_st, m_st, *, TS, nS):
    H, d, _ = wq_ref.shape
    L = ML_CHUNK
    pad = SUBLANES
    tail = CONV_W - 1
    g = pl.program_id(0)

    @pl.when(g == 0)
    def _():
        for r in (q_s, k_s, v_s, og_s, ifc_s, ifr_s, lfc_s, lfr_s):
            r[...] = jnp.zeros_like(r)

    @pl.when(lax.rem(g, nS) == 0)
    def _():
        ubuf[...] = jnp.zeros_like(ubuf)

    @pl.when((g == 0) | (lax.rem(g + (nS - 1), nS) == 0))
    def _():
        c_st[...] = jnp.zeros_like(c_st)
        n_st[...] = jnp.zeros_like(n_st)
        m_st[...] = jnp.zeros_like(m_st)

    if_c = ifc_s[...]
    if_r = ifr_s[...]
    lf_c = lfc_s[...]
    lf_r = lfr_s[...]
    ri = lax.broadcasted_iota(jnp.int32, (L, L), 0)
    ci = lax.broadcasted_iota(jnp.int32, (L, L), 1)
    causal = ci <= ri
    tri_l = jnp.where(causal, 1.0, 0.0).astype(BF16)
    tri_u = jnp.where(ri <= ci, 1.0, 0.0).astype(BF16)
    ng = ng_ref[...]
    nck = TS // L
    h = _rms(x_ref[...], g_ref[...]).astype(BF16)
    us = [_dot(h, wmu_ref[:, 0:d])] + [None] * (H - 1)
    bcs = [sum(_dot(tri_l, p) for p in _split3(lf_c[ck * L:(ck + 1) * L, :])) for ck in range(nck)]
    brs = [sum(_dot(p, tri_u) for p in _split3(lf_r[:, ck * L:(ck + 1) * L])) for ck in range(nck)]

    def wide(t):
        return jnp.concatenate([t] * (d // LANES), axis=1)

    def recurrence(hd):
        cs = slice(hd * d, (hd + 1) * d)
        for ck in range(nck):
            r0 = ck * L
            bc, br = bcs[ck], brs[ck]
            b_col = jnp.broadcast_to(bc[:, H + hd:H + hd + 1], (L, LANES))
            i_col = jnp.broadcast_to(if_c[r0:r0 + L, hd:hd + 1], (L, LANES))
            b_row = br[H + hd:H + hd + 1, :]
            i_row = if_r[hd:hd + 1, r0:r0 + L]
            m = m_st[hd][0:1, :]
            qb = q_s[r0:r0 + L, cs]
            kb = k_s[r0:r0 + L, cs]
            vb = v_s[r0:r0 + L, cs]
            gg = b_col + m
            dm = jnp.where(causal, b_col - b_row + i_row, -jnp.inf)
            yield
            m_row = jnp.maximum(gg, jnp.max(dm, axis=-1, keepdims=True))
            s_qk = _dot_nt(qb, kb)
            yield
            w = jnp.exp(dm - m_row) * s_qk
            inter = jnp.exp(gg - m_row)
            cmat = c_st[hd]
            nvec = n_st[hd]
            yield
            num = wide(inter) * _dot(qb, cmat.astype(BF16)) + _dot(w.astype(BF16), vb)
            yield
            qf, kf, vf = qb.astype(F32), kb.astype(F32), vb.astype(F32)
            den = inter * jnp.sum(qf * nvec, axis=-1, keepdims=True) + jnp.sum(w, axis=-1, keepdims=True)
            hh = num * wide(1.0 / jnp.maximum(jnp.abs(den), jnp.exp(-m_row)))
            yield
            b_last = b_col[L - 1:L, :]
            dl = b_last - b_col + i_col
            m_new = jnp.maximum(b_last + m, jnp.max(dl, axis=0, keepdims=True))
            decay = jnp.exp(b_last + m - m_new)
            wl = jnp.exp(dl - m_new)
            yield
            c_st[hd] = wide(decay) * cmat + _dot_tn(kb, (wide(wl) * vf).astype(BF16))
            n_st[hd] = wide(decay) * nvec + jnp.sum(wide(wl) * kf, axis=0, keepdims=True)
            m_st[hd] = jnp.broadcast_to(m_new, m_st.shape[1:])
            yield
            y = og_s[r0:r0 + L, cs] * hh
            y = y * lax.rsqrt(jnp.mean(y * y, axis=-1, keepdims=True) + EPS)
            o_ref[r0:r0 + L, cs] = (y * ng[:, cs]).astype(BF16)
            yield

    def projection(hd):
        cs = slice(hd * d, (hd + 1) * d)
        u = us[hd]
        mo = _dot(h, wmo_ref[:, cs])
        yield
        if hd + 1 < H:
            us[hd + 1] = _dot(h, wmu_ref[:, (hd + 1) * d:(hd + 2) * d])
            yield
        cw = cw_ref[:, cs]
        halo = ubuf[:, cs]
        rowi = lax.broadcasted_iota(jnp.int32, (pad, d), 0)
        c = cb_ref[:, cs] + cw[tail:tail + 1, :] * u
        for j in range(1, CONV_W):
            sh = pltpu.roll(u, j, axis=0)
            head = jnp.where(rowi < j, pltpu.roll(halo, j, axis=0), sh[0:pad, :])
            sh = jnp.concatenate([head, sh[pad:, :]], axis=0)
            c = c + cw[tail - j:tail - j + 1, :] * sh
            yield
        ubuf[:, cs] = u[TS - pad:TS, :]
        cb16 = jax.nn.silu(c).astype(BF16)
        yield
        q = _dot(cb16, wq_ref[hd]).astype(BF16)
        yield
        k_ = (_dot(cb16, wk_ref[hd]) * (d ** -0.5)).astype(BF16)
        yield
        v = _dot(u.astype(BF16), wv_ref[hd]).astype(BF16)
        yield
        og = jax.nn.sigmoid(mo)
        yield
        return q, k_, v, og

    for hd in range(H):
        cs = slice(hd * d, (hd + 1) * d)
        q, k_, v, og = _alternate(recurrence(hd), projection(hd))
        q_s[:, cs] = q
        k_s[:, cs] = k_
        v_s[:, cs] = v
        og_s[:, cs] = og
    new_if_c = _dot(h, wif_ref[...]) + bifc_ref[...]
    new_if_r = _dot_nt(wift_ref[...], h) + bifr_ref[...]
    ifc_s[...] = new_if_c
    ifr_s[...] = new_if_r
    lfc_s[...] = jax.nn.log_sigmoid(new_if_c)
    lfr_s[...] = jax.nn.log_sigmoid(new_if_r)


def _mlstm_branch(x2d, B, norm_g, w_mu, w_mo, w_if, w_ift, b_if_c, b_if_r, conv_w, conv_b,
                  w_q, w_k, w_v, ml_norm_g):
    T, D = x2d.shape
    S = T // B
    C = w_mu.shape[1]
    H, d, _ = w_q.shape
    TS = min(ML_ROW_TILE, S)
    assert S % TS == 0 and TS % ML_CHUNK == 0 and d % LANES == 0
    nS = S // TS
    G = B * nS
    return pl.pallas_call(
        functools.partial(_mlstm_kernel, TS=TS, nS=nS),
        out_shape=jax.ShapeDtypeStruct((T, C), BF16),
        grid=(G + 1,),
        in_specs=[pl.BlockSpec((TS, D), lambda g: (jnp.minimum(g, G - 1), 0)), _const_spec((1, D)),
                  _const_spec((D, C)), _const_spec((D, C)), _const_spec((D, LANES)),
                  _const_spec((SUBLANES, D)), _const_spec((1, LANES)), _const_spec((SUBLANES, 1)),
                  _const_spec((CONV_W, C)), _const_spec((1, C)), _const_spec(w_q.shape),
                  _const_spec(w_k.shape), _const_spec(w_v.shape), _const_spec((1, C))],
        out_specs=pl.BlockSpec((TS, C), lambda g: (jnp.maximum(g - 1, 0), 0)),
        scratch_shapes=[pltpu.VMEM((SUBLANES, C), F32), pltpu.VMEM((TS, C), BF16),
                        pltpu.VMEM((TS, C), BF16), pltpu.VMEM((TS, C), BF16), pltpu.VMEM((TS, C), F32),
                        pltpu.VMEM((TS, LANES), F32), pltpu.VMEM((SUBLANES, TS), F32),
                        pltpu.VMEM((TS, LANES), F32), pltpu.VMEM((SUBLANES, TS), F32),
                        pltpu.VMEM((H, d, d), F32), pltpu.VMEM((H, 1, d), F32),
                        pltpu.VMEM((H, SUBLANES, LANES), F32)],
        compiler_params=_params(("arbitrary",), 48),
        name="mlstm_branch",
    )(x2d, norm_g, w_mu, w_mo, w_if, w_ift, b_if_c, b_if_r, conv_w, conv_b, w_q, w_k, w_v, ml_norm_g)


def _kv_kernel(mem_ref, g_ref, w_ref, o_ref):
    o_ref[...] = _dot(_rms(mem_ref[...], g_ref[...]).astype(BF16), w_ref[...]).astype(BF16)


def _mem_kv(mem, g, w_kv):
    B, M, D = mem.shape
    N = w_kv.shape[1]
    return pl.pallas_call(
        _kv_kernel,
        out_shape=jax.ShapeDtypeStruct((B, M, N), BF16),
        grid=(B,),
        in_specs=[pl.BlockSpec((None, M, D), lambda b: (b, 0, 0)), _const_spec((1, D)),
                  _const_spec((D, N))],
        out_specs=pl.BlockSpec((None, M, N), lambda b: (b, 0, 0)),
        compiler_params=_params(("arbitrary",), 32),
        name="mem_kv",
    )(mem, g, w_kv)


def _merge_kernel(x_ref, g_ref, wq_ref, wg_ref, bm_ref, yrg_ref, yml_ref, kv_ref, wrg_ref, wml_ref,
                  wxa_ref, wo_ref, o_ref):
    x = x_ref[...]
    D = x.shape[1]
    h = _rms(x, g_ref[...]).astype(BF16)
    q = _dot(h, wq_ref[...]).astype(BF16)
    dxa = q.shape[1]
    dh = dxa // XA_HEADS
    heads = []
    for hd in range(XA_HEADS):
        kh = kv_ref[:, hd * dh:(hd + 1) * dh]
        vh = kv_ref[:, dxa + hd * dh:dxa + (hd + 1) * dh]
        s = _dot_nt(q[:, hd * dh:(hd + 1) * dh], kh) * (dh ** -0.5)
        e = jnp.exp(s - jnp.max(s, axis=-1, keepdims=True))
        p = e / jnp.sum(e, axis=-1, keepdims=True)
        heads.append(_dot(p.astype(BF16), vh))
    y_xa = jnp.concatenate(heads, axis=1).astype(BF16)

    def gate(k):
        return jax.nn.sigmoid(_dot(h, wg_ref[:, k * D:(k + 1) * D]) + bm_ref[:, k * D:(k + 1) * D])

    merged = gate(0) * _dot(yrg_ref[...], wrg_ref[...])
    merged = merged + gate(1) * _dot(yml_ref[...], wml_ref[...])
    merged = merged + gate(2) * _dot(y_xa, wxa_ref[...])
    o_ref[...] = x + _dot(merged.astype(BF16), wo_ref[...])


def _merge(x2d, B, norm_g, w_q, w_g, b_merge, y_rg, y_ml, kv, w_br_rg, w_br_ml, w_br_xa, w_out):
    T, D = x2d.shape
    S = T // B
    C = y_ml.shape[1]
    M, N = kv.shape[1:]
    TM = min(MERGE_ROW_TILE, S)
    assert S % TM == 0
    nS = S // TM
    row = lambda b, s: (b * nS + s, 0)
    one = pl.Buffered(1)
    cspec = lambda shape: pl.BlockSpec(shape, lambda *_: (0,) * len(shape), pipeline_mode=one)
    return pl.pallas_call(
        _merge_kernel,
        out_shape=jax.ShapeDtypeStruct((T, D), F32),
        grid=(B, nS),
        in_specs=[pl.BlockSpec((TM, D), row), cspec((1, D)), cspec(w_q.shape), cspec(w_g.shape),
                  cspec(b_merge.shape), pl.BlockSpec((TM, C), row),
                  pl.BlockSpec((TM, C), row), pl.BlockSpec((None, M, N), lambda b, s: (b, 0, 0)),
                  cspec(w_br_rg.shape), cspec(w_br_ml.shape), cspec(w_br_xa.shape),
                  cspec(w_out.shape)],
        out_specs=pl.BlockSpec((TM, D), row),
        compiler_params=_params(("arbitrary", "arbitrary"), 56),
        name="merge",
    )(x2d, norm_g, w_q, w_g, b_merge, y_rg, y_ml, kv, w_br_rg, w_br_ml, w_br_xa, w_out)


def _ffn_kernel(x_ref, g_ref, w1_ref, w3_ref, w2_ref, *rest, n_cast):
    cast_in, o_ref, cast_out = rest[:n_cast], rest[n_cast], rest[n_cast + 1:2 * n_cast + 1]
    hs, acc = rest[2 * n_cast + 1:]
    f = pl.program_id(1)
    for src, dst in zip(cast_in, cast_out):
        dst[...] = src[...].astype(BF16)

    @pl.when(f == 0)
    def _():
        hs[...] = _rms(x_ref[...], g_ref[...]).astype(BF16)
        acc[...] = jnp.zeros_like(acc)

    hb = hs[...]
    mid = jax.nn.silu(_dot(hb, w1_ref[...])) * _dot(hb, w3_ref[...])
    acc[...] += _dot(mid.astype(BF16), w2_ref[...])

    @pl.when(f == pl.num_programs(1) - 1)
    def _():
        o_ref[...] = x_ref[...] + acc[...]


def _ffn_hidden_tile(F):
    for tf in (1792, 1024, 512, 256):
        if F % tf == 0:
            return tf
    return F


def _ffn_cast_block(x2d, w1, shape):
    tiles = x2d.shape[0] // min(FFN_ROW_TILE, x2d.shape[0])
    nF = w1.shape[1] // _ffn_hidden_tile(w1.shape[1])
    rows = math.prod(shape[:-1])
    for steps, every_step in ((tiles * nF, True), (tiles, False)):
        if rows % steps == 0 and (rows // steps) % (2 * SUBLANES) == 0:
            return rows // steps, every_step
    return None


def _ffn(x2d, norm_g, w1, w3, w2, cast=()):
    T, D = x2d.shape
    F = w1.shape[1]
    TM = min(FFN_ROW_TILE, T)
    TF = _ffn_hidden_tile(F)
    assert T % TM == 0
    nF = F // TF
    cast2d, cast_in_specs, cast_out_specs, cast_shapes = [], [], [], []
    for arr, idx in cast:
        shape = arr.shape[1:]
        rows, every_step = _ffn_cast_block(x2d, w1, shape)
        layer_rows = math.prod(shape[:-1])
        first = idx * (layer_rows // rows)
        step = (lambda i, f: i * nF + f) if every_step else (lambda i, f: i)
        cast2d.append(arr.reshape(-1, shape[-1]))
        cast_in_specs.append(pl.BlockSpec((rows, shape[-1]),
                                          lambda i, f, first=first, step=step: (first + step(i, f), 0)))
        cast_out_specs.append(pl.BlockSpec((rows, shape[-1]), lambda i, f, step=step: (step(i, f), 0)))
        cast_shapes.append(jax.ShapeDtypeStruct((layer_rows, shape[-1]), BF16))
    out = pl.pallas_call(
        functools.partial(_ffn_kernel, n_cast=len(cast)),
        out_shape=(jax.ShapeDtypeStruct((T, D), F32), *cast_shapes),
        grid=(T // TM, nF),
        in_specs=[pl.BlockSpec((TM, D), lambda i, f: (i, 0)), _const_spec((1, D)),
                  pl.BlockSpec((D, TF), lambda i, f: (0, f)), pl.BlockSpec((D, TF), lambda i, f: (0, f)),
                  pl.BlockSpec((TF, D), lambda i, f: (f, 0)), *cast_in_specs],
        out_specs=(pl.BlockSpec((TM, D), lambda i, f: (i, 0)), *cast_out_specs),
        scratch_shapes=[pltpu.VMEM((TM, D), BF16), pltpu.VMEM((TM, D), F32)],
        compiler_params=_params(("arbitrary", "arbitrary"), 56),
        name="ffn_dense",
    )(x2d, norm_g, w1, w3, w2, *cast2d)
    return out[0], [o.reshape(arr.shape[1:]) for o, (arr, _) in zip(out[1:], cast)]


def _route_kernel(x_ref, g_ref, wcat_ref, rb_ref, idx_ref, gate_ref, len_ref, off_ref,
                  tot_ref, h2_ref, run_s, *, E):
    @pl.when(pl.program_id(0) == 0)
    def _():
        run_s[...] = jnp.zeros_like(run_s)

    h2 = _rms(x_ref[...], g_ref[...])
    TM = h2.shape[0]
    hi = h2.astype(BF16)
    h2_ref[...] = hi
    lo = (h2 - hi.astype(F32)).astype(BF16)
    both = _dot(hi, wcat_ref[...])
    logits = both[:, 0:LANES] + (_dot(lo, wcat_ref[:, 0:LANES]) + both[:, LANES:2 * LANES])
    lt = logits.T[0:E, :] + rb_ref[...]
    ie = lax.broadcasted_iota(jnp.int32, (E, TM), 0)
    m1 = jnp.max(lt, axis=0, keepdims=True)
    i1 = jnp.min(jnp.where(lt == m1, ie, E), axis=0, keepdims=True)
    l2 = jnp.where(ie == i1, -jnp.inf, lt)
    m2 = jnp.max(l2, axis=0, keepdims=True)
    i2 = jnp.min(jnp.where(l2 == m2, ie, E), axis=0, keepdims=True)
    ex = jnp.exp(m2 - m1)
    g1 = 1.0 / (1.0 + ex)
    g2 = ex / (1.0 + ex)
    oh1 = jnp.where(ie == i1, 1.0, 0.0)
    oh2 = jnp.where(ie == i2, 1.0, 0.0)
    oh = oh1 + oh2
    ri = lax.broadcasted_iota(jnp.int32, (TM, TM), 0)
    ci = lax.broadcasted_iota(jnp.int32, (TM, TM), 1)
    upper = jnp.where(ri < ci, 1.0, 0.0).astype(BF16)
    excl = _dot(oh.astype(BF16), upper)
    cnt = jnp.sum(oh, axis=1, keepdims=True).astype(jnp.int32)
    run_len = jnp.broadcast_to(((cnt + (RUN_ALIGN - 1)) // RUN_ALIGN) * RUN_ALIGN, (E, LANES))
    iec = lax.broadcasted_iota(jnp.int32, (E, LANES), 0)
    run_start = jnp.zeros((E, LANES), jnp.int32)
    for e in range(E - 1):
        run_start = run_start + jnp.where(iec > e, run_len[e:e + 1, :], 0)
    slot = run_start[:, 0:1].astype(F32) + excl
    s1 = jnp.sum(oh1 * slot, axis=0, keepdims=True).astype(jnp.int32)
    s2 = jnp.sum(oh2 * slot, axis=0, keepdims=True).astype(jnp.int32)
    len_ref[...] = run_len
    off_ref[...] = run_s[...]
    run_s[...] = run_s[...] + run_len
    tot_ref[...] = run_s[...]
    row = lax.broadcasted_iota(jnp.int32, (SUBLANES, TM), 0)
    idx_ref[...] = jnp.where(row == 0, i1, jnp.where(row == 1, i2, jnp.where(row == 2, s1,
                             jnp.where(row == 3, s2, 0))))
    gate_ref[...] = jnp.where(row == 0, g1, jnp.where(row == 1, g2, 0.0))


def _route(x2d, norm_g, w_cat, rb, E):
    T, D = x2d.shape
    TM = min(ROUTE_TM, T)
    nT = T // TM
    assert T % TM == 0 and E == SUBLANES
    return pl.pallas_call(
        functools.partial(_route_kernel, E=E),
        out_shape=(jax.ShapeDtypeStruct((SUBLANES, T), jnp.int32),
                   jax.ShapeDtypeStruct((SUBLANES, T), F32),
                   jax.ShapeDtypeStruct((nT * E, LANES), jnp.int32),
                   jax.ShapeDtypeStruct((nT * E, LANES), jnp.int32),
                   jax.ShapeDtypeStruct((E, LANES), jnp.int32),
                   jax.ShapeDtypeStruct((T, D), BF16)),
        grid=(nT,),
        in_specs=[pl.BlockSpec((TM, D), lambda i: (i, 0)), _const_spec((1, D)),
                  _const_spec((D, 2 * LANES)), _const_spec((E, 1))],
        out_specs=(pl.BlockSpec((SUBLANES, TM), lambda i: (0, i)),
                   pl.BlockSpec((SUBLANES, TM), lambda i: (0, i)),
                   pl.BlockSpec((E, LANES), lambda i: (i, 0)),
                   pl.BlockSpec((E, LANES), lambda i: (i, 0)),
                   _const_spec((E, LANES)),
                   pl.BlockSpec((TM, D), lambda i: (i, 0))),
        scratch_shapes=[pltpu.VMEM((E, LANES), jnp.int32)],
        compiler_params=_params(("arbitrary",), 32),
        name="moe_route",
    )(x2d, norm_g, w_cat, rb)


def _run_dmas(len_ref, off_ref, i, E, max_len, tile_ref, sorted_ref, sem, to_sorted, wait, packed=True):
    local = 0
    for e in range(E):
        n = len_ref[i * E + e]
        base = off_ref[i * E + e]
        done = 0
        sz = max_len
        while sz >= RUN_ALIGN:
            @pl.when((n & sz) != 0)
            def _(sz=sz, local=local, base=base, done=done):
                t_rows = tile_ref.at[pl.ds(pl.multiple_of(local + done, RUN_ALIGN), sz), :]
                s_rows = sorted_ref.at[pl.ds(pl.multiple_of(base + done, RUN_ALIGN), sz), :]
                cp = (pltpu.make_async_copy(t_rows, s_rows, sem) if to_sorted
                      else pltpu.make_async_copy(s_rows, t_rows, sem))
                if wait:
                    cp.wait()
                else:
                    cp.start()
            done = done + (n & sz)
            sz //= 2
        if packed:
            local = local + n


def _dispatch_kernel(len_ref, off_ref, gap_len_ref, gap_off_ref, tail_ref, h2_ref, ls_ref, xs_ref, xs_t,
                     zeros, sem, *, E):
    i = pl.program_id(0)
    TM = h2_ref.shape[0]
    LS = xs_t.shape[1]
    slot = lax.rem(i, 2)
    h2 = h2_ref[...]
    ls = ls_ref[...]
    j = lax.broadcasted_iota(jnp.int32, (LS, TM), 0)
    onehot = jnp.where(j == ls[2:3, :], 1.0, jnp.where(j == ls[3:4, :], 1.0, 0.0)).astype(BF16)
    xs_t[slot] = _dot(onehot, h2).astype(BF16)

    @pl.when(i > 0)
    def _():
        _run_dmas(len_ref, off_ref, i - 1, E, TM, xs_t.at[1 - slot], xs_ref, sem.at[1 - slot], True, True)

    _run_dmas(len_ref, off_ref, i, E, TM, xs_t.at[slot], xs_ref, sem.at[slot], True, False)

    @pl.when(i == pl.num_programs(0) - 1)
    def _():
        _run_dmas(len_ref, off_ref, i, E, TM, xs_t.at[slot], xs_ref, sem.at[slot], True, True)
        zeros[...] = jnp.zeros_like(zeros)
        tmg = zeros.shape[0]

        def tail_copy(t):
            rows = pl.ds(pl.multiple_of(t * tmg, tmg), tmg)
            return pltpu.make_async_copy(zeros, xs_ref.at[rows, :], sem.at[slot])

        def tail_start(t, c):
            tail_copy(t).start()
            return c

        def tail_wait(t, c):
            tail_copy(t).wait()
            return c

        for wait in (False, True):
            _run_dmas(gap_len_ref, gap_off_ref, 0, E, tmg // 2, zeros, xs_ref, sem.at[slot], True, wait,
                      packed=False)
            lax.fori_loop(tail_ref[0], xs_ref.shape[0] // tmg, tail_wait if wait else tail_start, 0)


def _dispatch(run_len, run_off, gap_len, gap_off, tail_tile, idx, h2, P, E, TMG):
    T, D = h2.shape
    TM = min(ROUTE_TM, T)
    LS = TOP_K * TM + E * RUN_ALIGN
    grid_spec = pltpu.PrefetchScalarGridSpec(
        num_scalar_prefetch=5,
        grid=(T // TM,),
        in_specs=[pl.BlockSpec((TM, D), lambda i, *_: (i, 0)),
                  pl.BlockSpec((SUBLANES, TM), lambda i, *_: (0, i))],
        out_specs=pl.BlockSpec(memory_space=pl.ANY),
        scratch_shapes=[pltpu.VMEM((2, LS, D), BF16), pltpu.VMEM((TMG, D), BF16),
                        pltpu.SemaphoreType.DMA((2,))])
    return pl.pallas_call(
        functools.partial(_dispatch_kernel, E=E),
        out_shape=jax.ShapeDtypeStruct((P, D), BF16),
        grid_spec=grid_spec,
        compiler_params=_params(("arbitrary",), 32),
        name="moe_dispatch",
    )(run_len, run_off, gap_len, gap_off, tail_tile, h2, idx)


def _group_ffn_kernel(te_ref, tn_ref, x_ref, w1_ref, w3_ref, w2_ref, o_ref, acc):
    i = pl.program_id(0)
    f = pl.program_id(1)

    @pl.when(f == 0)
    def _():
        acc[...] = jnp.zeros_like(acc)

    @pl.when(tn_ref[i] > 0)
    def _():
        hb = x_ref[...]
        mid = jax.nn.silu(_dot(hb, w1_ref[...])) * _dot(hb, w3_ref[...])
        acc[...] += _dot(mid.astype(BF16), w2_ref[...])

    @pl.when(f == pl.num_programs(1) - 1)
    def _():
        o_ref[...] = acc[...]


def _group_ffn(tile_e, tile_n, xs, w1, w3, w2, TMG):
    P, D = xs.shape
    F = w1.shape[2]
    TF = _ffn_hidden_tile(F)
    nF = F // TF
    fblk = lambda i, f, tn: jnp.where(tn[i] > 0, f, nF - 1)
    grid_spec = pltpu.PrefetchScalarGridSpec(
        num_scalar_prefetch=2,
        grid=(P // TMG, nF),
        in_specs=[pl.BlockSpec((TMG, D), lambda i, f, te, tn: (jnp.where(tn[i] > 0, i, 0), 0)),
                  pl.BlockSpec((None, D, TF), lambda i, f, te, tn: (te[i], 0, fblk(i, f, tn))),
                  pl.BlockSpec((None, D, TF), lambda i, f, te, tn: (te[i], 0, fblk(i, f, tn))),
                  pl.BlockSpec((None, TF, D), lambda i, f, te, tn: (te[i], fblk(i, f, tn), 0))],
        out_specs=pl.BlockSpec((TMG, D), lambda i, f, te, tn: (i, 0)),
        scratch_shapes=[pltpu.VMEM((TMG, D), F32)])
    return pl.pallas_call(
        _group_ffn_kernel,
        out_shape=jax.ShapeDtypeStruct((P, D), F32),
        grid_spec=grid_spec,
        compiler_params=_params(("arbitrary", "arbitrary"), 56),
        name="moe_group_ffn",
    )(tile_e, tile_n, xs, w1, w3, w2)


def _combine_kernel(len_ref, off_ref, x_ref, lsr_ref, lsc_ref, gt_ref, fg_ref, ys_ref, o_ref, yt, sem,
                    *, E, final_norm):
    i = pl.program_id(0)
    TM = x_ref.shape[0]
    _, LS, D = yt.shape
    slot = lax.rem(i, 2)

    def fetch(tile, s):
        yt[s, TOP_K * TM:LS, :] = jnp.zeros((LS - TOP_K * TM, D), F32)
        _run_dmas(len_ref, off_ref, tile, E, TM, yt.at[s], ys_ref, sem.at[s], False, False)

    @pl.when(i == 0)
    def _():
        fetch(0, 0)

    @pl.when(i + 1 < pl.num_programs(0))
    def _():
        fetch(i + 1, 1 - slot)

    lsr = lsr_ref[...]
    gt = gt_ref[...]
    js = lax.broadcasted_iota(jnp.int32, (LS, TM), 0)
    lane = lax.broadcasted_iota(jnp.int32, (TM, LANES), 1)
    gs = None
    for k in range(TOP_K):
        t1, t2, t3 = (t.astype(F32) for t in _split3(gt[:, k:k + 1]))
        terms = jnp.where(lane == 0, t1, jnp.where(lane == 1, t2, jnp.where(lane == 2, t3, 0.0)))
        pk = jnp.where(js == lsr[TOP_K + k:TOP_K + k + 1, :], 1.0, 0.0).astype(BF16)
        gk = jnp.sum(_dot(pk, terms.astype(BF16)), axis=1, keepdims=True)
        gs = gk if gs is None else gs + gk
    _run_dmas(len_ref, off_ref, i, E, TM, yt.at[slot], ys_ref, sem.at[slot], False, True)
    z = yt[slot] * gs
    zh = z.astype(BF16)
    zl = (z - zh.astype(F32)).astype(BF16)
    lsc = lsc_ref[...]
    jt = lax.broadcasted_iota(jnp.int32, (TM, LS), 1)
    pt = None
    for k in range(TOP_K):
        ok = jnp.where(jt == lsc[:, TOP_K + k:TOP_K + k + 1], 1.0, 0.0)
        pt = ok if pt is None else pt + ok
    pt = pt.astype(BF16)
    xo = x_ref[...] + (_dot(pt, zh) + _dot(pt, zl))
    o_ref[...] = _rms(xo, fg_ref[...]) if final_norm else xo


def _combine(run_len, run_off, x2d, idx, idx_t, gates_t, final_g, ys, E, final_norm):
    T, D = x2d.shape
    TM = min(ROUTE_TM, T)
    LS = TOP_K * TM + E * RUN_ALIGN
    grid_spec = pltpu.PrefetchScalarGridSpec(
        num_scalar_prefetch=2,
        grid=(T // TM,),
        in_specs=[pl.BlockSpec((TM, D), lambda i, *_: (i, 0)),
                  pl.BlockSpec((SUBLANES, TM), lambda i, *_: (0, i)),
                  pl.BlockSpec((TM, SUBLANES), lambda i, *_: (i, 0)),
                  pl.BlockSpec((TM, SUBLANES), lambda i, *_: (i, 0)),
                  pl.BlockSpec((1, D), lambda i, *_: (0, 0)),
                  pl.BlockSpec(memory_space=pl.ANY)],
        out_specs=pl.BlockSpec((TM, D), lambda i, *_: (i, 0)),
        scratch_shapes=[pltpu.VMEM((2, LS, D), F32), pltpu.SemaphoreType.DMA((2,))])
    return pl.pallas_call(
        functools.partial(_combine_kernel, E=E, final_norm=final_norm),
        out_shape=jax.ShapeDtypeStruct((T, D), F32),
        grid_spec=grid_spec,
        compiler_params=_params(("arbitrary",), 48),
        name="moe_combine",
    )(run_len, run_off, x2d, idx, idx_t, gates_t, final_g, ys)


def _moe(x2d, norm_g, router_w, router_b, w1, w3, w2, final_g, final_norm):
    T, D = x2d.shape
    E = router_w.shape[1]
    TMG = FFN_ROW_TILE
    nT = T // min(ROUTE_TM, T)
    rw = jnp.pad(router_w, ((0, 0), (0, LANES - E)))
    rw_hi = rw.astype(BF16)
    rw_lo = (rw - rw_hi.astype(F32)).astype(BF16)
    idx, gates, run_len, run_off, tot, h2 = _route(x2d, norm_g, jnp.concatenate([rw_hi, rw_lo], axis=1),
                                                   router_b.reshape(E, 1), E)
    tot = tot[:, 0]
    padded = ((tot + TMG - 1) // TMG) * TMG
    pend = jnp.cumsum(padded)
    pstart = pend - padded
    run_len = run_len[:, 0]
    run_off = (run_off[:, 0].reshape(nT, E) + pstart[None, :]).reshape(nT * E)
    P = -(-(T * TOP_K + nT * E * (RUN_ALIGN - 1) + E * (TMG - 1)) // TMG) * TMG
    tile_start = jnp.arange(P // TMG, dtype=jnp.int32) * TMG
    tile_e = jnp.minimum(jnp.sum(tile_start[:, None] >= pend[None, :], axis=1), E - 1).astype(jnp.int32)
    sel = tile_e[:, None] == jnp.arange(E, dtype=jnp.int32)[None, :]
    tile_end = jnp.sum(jnp.where(sel, (pstart + tot)[None, :], 0), axis=1)
    tile_n = jnp.clip(tile_end - tile_start, 0, TMG).astype(jnp.int32)
    xs = _dispatch(run_len, run_off, padded - tot, pstart + tot, pend[E - 1:] // TMG, idx, h2, P, E, TMG)
    ys = _group_ffn(tile_e, tile_n, xs, w1, w3, w2, TMG)
    return _combine(run_len, run_off, x2d, idx, idx.T, gates.T, final_g, ys, E, final_norm)


def _final_norm_kernel(x_ref, g_ref, o_ref):
    o_ref[...] = _rms(x_ref[...], g_ref[...])


def _final_norm(x2d, g):
    T, D = x2d.shape
    TM = min(1024, T)
    return pl.pallas_call(
        _final_norm_kernel,
        out_shape=jax.ShapeDtypeStruct((T, D), F32),
        grid=(T // TM,),
        in_specs=[pl.BlockSpec((TM, D), lambda i: (i, 0)), _const_spec((1, D))],
        out_specs=pl.BlockSpec((TM, D), lambda i: (i, 0)),
        compiler_params=_params(("arbitrary",), 32),
        name="final_norm",
    )(x2d, g)


def kernel(x, mem, norm_mix_g, w_in, conv_rg_w, conv_rg_b, rg_w_a, rg_b_a, rg_w_x, rg_b_x, rg_lambda,
           conv_ml_w, conv_ml_b, ml_w_q, ml_w_k, ml_w_v, ml_b_i, ml_b_f, ml_norm_g, mem_norm_g, w_kv,
           w_br_rg, w_br_ml, w_br_xa, b_merge, w_out, norm_ffn_g, ffn_w1, ffn_w3, ffn_w2, router_w,
           router_b, moe_w1, moe_w3, moe_w2, final_norm_g):
    B, S, D = x.shape
    depth = w_in.shape[0]
    d_rg = conv_rg_w.shape[2]
    d_ml = conv_ml_w.shape[2]
    H = ml_w_q.shape[1]
    d_xa = w_kv.shape[2] // 2
    o_ax, o_ay = 0, d_rg
    o_mu, o_mo = 2 * d_rg, 2 * d_rg + d_ml
    o_mi = 2 * d_rg + 2 * d_ml
    o_mf = o_mi + H
    o_q = o_mf + H
    o_g = o_q + d_xa
    assert w_in.shape[2] == o_g + N_BRANCH * D and 2 * H <= SUBLANES

    bf = lambda a: a.astype(BF16)
    row = lambda a: a.reshape(1, -1)
    x2d = x.reshape(B * S, D)
    fg = row(final_norm_g)
    pre = {}

    def layer_bf16(name, arr, idx):
        return pre.pop((name, idx)) if (name, idx) in pre else bf(arr[idx])

    for l in range(depth):
        cols = lambda a, b, l=l: bf(w_in[l][:, a:b])
        w_if = jnp.pad(cols(o_mi, o_mi + 2 * H), ((0, 0), (0, LANES - 2 * H)))
        w_ift = jnp.pad(cols(o_mi, o_mi + 2 * H).T, ((0, SUBLANES - 2 * H), (0, 0)))
        b_if = jnp.concatenate([ml_b_i[l], ml_b_f[l]])
        y_ml = _mlstm_branch(x2d, B, row(norm_mix_g[l]), cols(o_mu, o_mu + d_ml),
                                   cols(o_mo, o_mo + d_ml), w_if, w_ift,
                                   jnp.pad(b_if, (0, LANES - 2 * H)).reshape(1, LANES),
                                   jnp.pad(b_if, (0, SUBLANES - 2 * H)).reshape(SUBLANES, 1),
                                   conv_ml_w[l], row(conv_ml_b[l]), bf(ml_w_q[l]), bf(ml_w_k[l]),
                                   bf(ml_w_v[l]), row(ml_norm_g[l]))
        y_rg = _rg_branch(x2d.reshape(B, S, D), row(norm_mix_g[l]), cols(o_ax, o_ax + d_rg),
                          cols(o_ay, o_ay + d_rg), conv_rg_w[l], row(conv_rg_b[l]), bf(rg_w_a[l]),
                          row(rg_b_a[l]), bf(rg_w_x[l]), row(rg_b_x[l]), row(rg_lambda[l]))
        kv = _mem_kv(mem, row(mem_norm_g[l]), layer_bf16("w_kv", w_kv, l))
        x2d = _merge(x2d, B, row(norm_mix_g[l]), cols(o_q, o_q + d_xa), cols(o_g, o_g + N_BRANCH * D),
                     row(b_merge[l]), y_rg.reshape(B * S, d_rg), y_ml, kv,
                     layer_bf16("w_br_rg", w_br_rg, l), layer_bf16("w_br_ml", w_br_ml, l),
                     layer_bf16("w_br_xa", w_br_xa, l), layer_bf16("w_out", w_out, l))
        j = l // 2
        if l % 2 == 0:
            w1 = bf(ffn_w1[j])
            todo = {}
            if l + 1 < depth:
                todo = {(name, l + 1): (arr, l + 1) for name, arr in (
                    ("w_kv", w_kv), ("w_br_rg", w_br_rg), ("w_br_ml", w_br_ml), ("w_br_xa", w_br_xa),
                    ("w_out", w_out))}
                if (l + 1) % 2 == 1:
                    jn = (l + 1) // 2
                    todo.update({("moe_w1", jn): (moe_w1, jn), ("moe_w3", jn): (moe_w3, jn),
                                 ("moe_w2", jn): (moe_w2, jn)})
            todo = {key: t for key, t in todo.items()
                    if _ffn_cast_block(x2d, w1, t[0].shape[1:]) is not None}
            x2d, done = _ffn(x2d, row(norm_ffn_g[l]), w1, bf(ffn_w3[j]), bf(ffn_w2[j]), list(todo.values()))
            pre.update(zip(todo.keys(), done))
        else:
            x2d = _moe(x2d, row(norm_ffn_g[l]), router_w[j], router_b[j], layer_bf16("moe_w1", moe_w1, j),
                       layer_bf16("moe_w3", moe_w3, j), layer_bf16("moe_w2", moe_w2, j), fg, l == depth - 1)
    if depth % 2 == 1:
        x2d = _final_norm(x2d, fg)
    return x2d.reshape(B, S, D)
```

```python
import functools
import math

import jax
import jax.numpy as jnp
from jax import lax
from jax.experimental import pallas as pl
from jax.experimental.pallas import tpu as pltpu

EPS = 1e-6
RG_C = 8.0
CONV_W = 4
ML_CHUNK = 128
XA_HEADS = 4
TOP_K = 2
N_BRANCH = 3

V7X_VMEM_BYTES = 64 * 1024 * 1024
LANES = 128
SUBLANES = 8

RG_TIME_TILE = 128
ML_ROW_TILE = 512
MERGE_ROW_TILE = 1024
FFN_ROW_TILE = 512
ROUTE_TM = 512
RUN_ALIGN = 16

F32 = jnp.float32
BF16 = jnp.bfloat16


def _params(semantics, vmem_mib):
    assert vmem_mib * 1024 * 1024 <= V7X_VMEM_BYTES
    return pltpu.CompilerParams(dimension_semantics=semantics,
                                vmem_limit_bytes=vmem_mib * 1024 * 1024)


def _const_spec(shape):
    nd = len(shape)
    return pl.BlockSpec(shape, lambda *_: (0,) * nd)


def _rms(x, g):
    ms = jnp.mean(x * x, axis=-1, keepdims=True)
    return x * lax.rsqrt(ms + EPS) * g


def _dot(a, b):
    return jnp.dot(a, b, preferred_element_type=F32)


def _dot_nt(a, b):
    return lax.dot_general(a, b, (((1,), (1,)), ((), ())), preferred_element_type=F32)


def _dot_tn(a, b):
    return lax.dot_general(a, b, (((0,), (0,)), ((), ())), preferred_element_type=F32)


def _split3(x):
    h1 = x.astype(BF16)
    r1 = x - h1.astype(F32)
    h2 = r1.astype(BF16)
    h3 = (r1 - h2.astype(F32)).astype(BF16)
    return h1, h2, h3


def _rg_kernel(x_ref, g_ref, wax_ref, way_ref, cw_ref, cb_ref, wa_ref, ba_ref, wx_ref, bx_ref,
               lam_ref, o_ref, axbuf, a_s, b_s, h_s, carry, xbuf, xsem, *, B, TT):
    R = TT * B
    halo = (CONV_W - 1) * B

    @pl.when(pl.program_id(0) == 0)
    def _():
        axbuf[0:halo, :] = jnp.zeros((halo, axbuf.shape[1]), F32)
        carry[...] = jnp.zeros_like(carry)

    i = pl.program_id(0)
    slot = lax.rem(i, 2)

    def x_copies(tile, s):
        rows = pl.ds(pl.multiple_of(tile * TT, TT), TT)
        return [pltpu.make_async_copy(x_ref.at[b, rows, :], xbuf.at[s, :, b, :], xsem.at[s])
                for b in range(B)]

    @pl.when(i == 0)
    def _():
        for b, cp in enumerate(x_copies(0, 0)):
            cp.start(priority=b % 2)

    @pl.when(i + 1 < pl.num_programs(0))
    def _():
        for b, cp in enumerate(x_copies(i + 1, 1 - slot)):
            cp.start(priority=b % 2)

    for cp in x_copies(i, slot):
        cp.wait()
    x = xbuf[slot].reshape(R, xbuf.shape[3])
    h = _rms(x, g_ref[...]).astype(BF16)
    G, bi, _ = wa_ref.shape
    rate = -RG_C * jax.nn.softplus(-lam_ref[...])
    cols = [slice(g * bi, (g + 1) * bi) for g in range(G)]
    axbuf[halo:halo + R, cols[0]] = _dot(h, wax_ref[:, cols[0]])
    for g in range(G):
        cs = cols[g]
        if g + 1 < G:
            axbuf[halo:halo + R, cols[g + 1]] = _dot(h, wax_ref[:, cols[g + 1]])
        ay = _dot(h, way_ref[:, cs])
        cw = cw_ref[:, cs]
        xc = cb_ref[:, cs] + cw[0:1, :] * axbuf[0:R, cs]
        for k in range(1, CONV_W):
            xc = xc + cw[k:k + 1, :] * axbuf[k * B:k * B + R, cs]
        axbuf[0:halo, cs] = axbuf[R:R + halo, cs]
        xcb = xc.astype(BF16)
        r = jax.nn.sigmoid(_dot(xcb, wa_ref[g]) + ba_ref[:, cs])
        ig = jax.nn.sigmoid(_dot(xcb, wx_ref[g]) + bx_ref[:, cs])
        log_a = r * rate[:, cs]
        a = jnp.exp(log_a)
        a_s[:, cs] = a
        b_s[:, cs] = jnp.sqrt(-jnp.tanh(log_a) * (a * a + 1.0)) * (ig * xc)

        def step(t, hc, cs=cs):
            off = pl.multiple_of(t * B, B)
            hn = a_s[pl.ds(off, B), cs] * hc + b_s[pl.ds(off, B), cs]
            h_s[pl.ds(off, B), cs] = hn
            return hn

        carry[:, cs] = lax.fori_loop(0, TT, step, carry[:, cs], unroll=True)
        y = (h_s[:, cs] * jax.nn.gelu(ay)).astype(BF16).reshape(TT, B, bi)
        o_ref[:, :, cs] = pltpu.einshape("tbc->btc", y)


def _rg_branch(x3d, norm_g, w_ax, w_ay, conv_w, conv_b, w_a, b_a, w_x, b_x, lam):
    B, S, D = x3d.shape
    C = w_ax.shape[1]
    TT = min(RG_TIME_TILE, S)
    assert S % TT == 0 and B % SUBLANES == 0
    R = TT * B
    halo = (CONV_W - 1) * B
    tile = lambda i: (0, i, 0)
    return pl.pallas_call(
        functools.partial(_rg_kernel, B=B, TT=TT),
        out_shape=jax.ShapeDtypeStruct((B, S, C), BF16),
        grid=(S // TT,),
        in_specs=[pl.BlockSpec(memory_space=pl.ANY), _const_spec((1, D)), _const_spec((D, C)),
                  _const_spec((D, C)), _const_spec((CONV_W, C)), _const_spec((1, C)),
                  _const_spec(w_a.shape), _const_spec((1, C)), _const_spec(w_x.shape),
                  _const_spec((1, C)), _const_spec((1, C))],
        out_specs=pl.BlockSpec((B, TT, C), tile),
        scratch_shapes=[pltpu.VMEM((halo + R, C), F32), pltpu.VMEM((R, C), F32),
                        pltpu.VMEM((R, C), F32), pltpu.VMEM((R, C), F32), pltpu.VMEM((B, C), F32),
                        pltpu.VMEM((2, TT, B, D), F32), pltpu.SemaphoreType.DMA((2,))],
        compiler_params=_params(("arbitrary",), 48),
        name="rg_branch",
    )(x3d, norm_g, w_ax, w_ay, conv_w, conv_b, w_a, b_a, w_x, b_x, lam)


def _alternate(first, second):
    result = None
    live = [True, True]
    while any(live):
        for n, gen in enumerate((first, second)):
            if live[n]:
                try:
                    next(gen)
                except StopIteration as stop:
                    live[n] = False
                    if n == 1:
                        result = stop.value
    return result


def _mlstm_kernel(x_ref, g_ref, wmu_ref, wmo_ref, wif_ref, wift_ref, bifc_ref, bifr_ref, cw_ref,
                  cb_ref, wq_ref, wk_ref, wv_ref, ng_ref, o_ref, ubuf, q_s, k_s, v_s, og_s, ifc_s, ifr_s,
                  lfc_s, lfr_s, c_st, n_st, m_st, *, TS, nS):
    H, d, _ = wq_ref.shape
    L = ML_CHUNK
    pad = SUBLANES
    tail = CONV_W - 1
    g = pl.program_id(0)

    @pl.when(g == 0)
    def _():
        for r in (q_s, k_s, v_s, og_s, ifc_s, ifr_s, lfc_s, lfr_s):
            r[...] = jnp.zeros_like(r)

    @pl.when(lax.rem(g, nS) == 0)
    def _():
        ubuf[...] = jnp.zeros_like(ubuf)

    @pl.when((g == 0) | (lax.rem(g + (nS - 1), nS) == 0))
    def _():
        c_st[...] = jnp.zeros_like(c_st)
        n_st[...] = jnp.zeros_like(n_st)
        m_st[...] = jnp.zeros_like(m_st)

    if_c = ifc_s[...]
    if_r = ifr_s[...]
    lf_c = lfc_s[...]
    lf_r = lfr_s[...]
    ri = lax.broadcasted_iota(jnp.int32, (L, L), 0)
    ci = lax.broadcasted_iota(jnp.int32, (L, L), 1)
    causal = ci <= ri
    tri_l = jnp.where(causal, 1.0, 0.0).astype(BF16)
    tri_u = jnp.where(ri <= ci, 1.0, 0.0).astype(BF16)
    ng = ng_ref[...]
    nck = TS // L
    h = _rms(x_ref[...], g_ref[...]).astype(BF16)
    us = [_dot(h, wmu_ref[:, 0:d])] + [None] * (H - 1)
    bcs = [sum(_dot(tri_l, p) for p in _split3(lf_c[ck * L:(ck + 1) * L, :])) for ck in range(nck)]
    brs = [sum(_dot(p, tri_u) for p in _split3(lf_r[:, ck * L:(ck + 1) * L])) for ck in range(nck)]

    def wide(t):
        return jnp.concatenate([t] * (d // LANES), axis=1)

    def recurrence(hd):
        cs = slice(hd * d, (hd + 1) * d)
        for ck in range(nck):
            r0 = ck * L
            bc, br = bcs[ck], brs[ck]
            b_col = jnp.broadcast_to(bc[:, H + hd:H + hd + 1], (L, LANES))
            i_col = jnp.broadcast_to(if_c[r0:r0 + L, hd:hd + 1], (L, LANES))
            b_row = br[H + hd:H + hd + 1, :]
            i_row = if_r[hd:hd + 1, r0:r0 + L]
            m = m_st[hd][0:1, :]
            qb = q_s[r0:r0 + L, cs]
            kb = k_s[r0:r0 + L, cs]
            vb = v_s[r0:r0 + L, cs]
            gg = b_col + m
            dm = jnp.where(causal, b_col - b_row + i_row, -jnp.inf)
            yield
            m_row = jnp.maximum(gg, jnp.max(dm, axis=-1, keepdims=True))
            s_qk = _dot_nt(qb, kb)
            yield
            w = jnp.exp(dm - m_row) * s_qk
            inter = jnp.exp(gg - m_row)
            cmat = c_st[hd]
            nvec = n_st[hd]
            yield
            num = wide(inter) * _dot(qb, cmat.astype(BF16)) + _dot(w.astype(BF16), vb)
            yield
            qf, kf, vf = qb.astype(F32), kb.astype(F32), vb.astype(F32)
            den = inter * jnp.sum(qf * nvec, axis=-1, keepdims=True) + jnp.sum(w, axis=-1, keepdims=True)
            hh = num * wide(1.0 / jnp.maximum(jnp.abs(den), jnp.exp(-m_row)))
            yield
            b_last = b_col[L - 1:L, :]
            dl = b_last - b_col + i_col
            m_new = jnp.maximum(b_last + m, jnp.max(dl, axis=0, keepdims=True))
            decay = jnp.exp(b_last + m - m_new)
            wl = jnp.exp(dl - m_new)
            yield
            c_st[hd] = wide(decay) * cmat + _dot_tn(kb, (wide(wl) * vf).astype(BF16))
            n_st[hd] = wide(decay) * nvec + jnp.sum(wide(wl) * kf, axis=0, keepdims=True)
            m_st[hd] = jnp.broadcast_to(m_new, m_st.shape[1:])
            yield
            y = og_s[r0:r0 + L, cs] * hh
            y = y * lax.rsqrt(jnp.mean(y * y, axis=-1, keepdims=True) + EPS)
            o_ref[r0:r0 + L, cs] = (y * ng[:, cs]).astype(BF16)
            yield

    def projection(hd):
        cs = slice(hd * d, (hd + 1) * d)
        u = us[hd]
        mo = _dot(h, wmo_ref[:, cs])
        yield
        if hd + 1 < H:
            us[hd + 1] = _dot(h, wmu_ref[:, (hd + 1) * d:(hd + 2) * d])
            yield
        cw = cw_ref[:, cs]
        halo = ubuf[:, cs]
        rowi = lax.broadcasted_iota(jnp.int32, (pad, d), 0)
        c = cb_ref[:, cs] + cw[tail:tail + 1, :] * u
        for j in range(1, CONV_W):
            sh = pltpu.roll(u, j, axis=0)
            head = jnp.where(rowi < j, pltpu.roll(halo, j, axis=0), sh[0:pad, :])
            sh = jnp.concatenate([head, sh[pad:, :]], axis=0)
            c = c + cw[tail - j:tail - j + 1, :] * sh
            yield
        ubuf[:, cs] = u[TS - pad:TS, :]
        cb16 = jax.nn.silu(c).astype(BF16)
        yield
        q = _dot(cb16, wq_ref[hd]).astype(BF16)
        yield
        k_ = (_dot(cb16, wk_ref[hd]) * (d ** -0.5)).astype(BF16)
        yield
        v = _dot(u.astype(BF16), wv_ref[hd]).astype(BF16)
        yield
        og = jax.nn.sigmoid(mo)
        yield
        return q, k_, v, og

    for hd in range(H):
        cs = slice(hd * d, (hd + 1) * d)
        q, k_, v, og = _alternate(recurrence(hd), projection(hd))
        q_s[:, cs] = q
        k_s[:, cs] = k_
        v_s[:, cs] = v
        og_s[:, cs] = og
    new_if_c = _dot(h, wif_ref[...]) + bifc_ref[...]
    new_if_r = _dot_nt(wift_ref[...], h) + bifr_ref[...]
    ifc_s[...] = new_if_c
    ifr_s[...] = new_if_r
    lfc_s[...] = jax.nn.log_sigmoid(new_if_c)
    lfr_s[...] = jax.nn.log_sigmoid(new_if_r)


def _mlstm_branch(x2d, B, norm_g, w_mu, w_mo, w_if, w_ift, b_if_c, b_if_r, conv_w, conv_b,
                  w_q, w_k, w_v, ml_norm_g):
    T, D = x2d.shape
    S = T // B
    C = w_mu.shape[1]
    H, d, _ = w_q.shape
    TS = min(ML_ROW_TILE, S)
    assert S % TS == 0 and TS % ML_CHUNK == 0 and d % LANES == 0
    nS = S // TS
    G = B * nS
    return pl.pallas_call(
        functools.partial(_mlstm_kernel, TS=TS, nS=nS),
        out_shape=jax.ShapeDtypeStruct((T, C), BF16),
        grid=(G + 1,),
        in_specs=[pl.BlockSpec((TS, D), lambda g: (jnp.minimum(g, G - 1), 0)), _const_spec((1, D)),
                  _const_spec((D, C)), _const_spec((D, C)), _const_spec((D, LANES)),
                  _const_spec((SUBLANES, D)), _const_spec((1, LANES)), _const_spec((SUBLANES, 1)),
                  _const_spec((CONV_W, C)), _const_spec((1, C)), _const_spec(w_q.shape),
                  _const_spec(w_k.shape), _const_spec(w_v.shape), _const_spec((1, C))],
        out_specs=pl.BlockSpec((TS, C), lambda g: (jnp.maximum(g - 1, 0), 0)),
        scratch_shapes=[pltpu.VMEM((SUBLANES, C), F32), pltpu.VMEM((TS, C), BF16),
                        pltpu.VMEM((TS, C), BF16), pltpu.VMEM((TS, C), BF16), pltpu.VMEM((TS, C), F32),
                        pltpu.VMEM((TS, LANES), F32), pltpu.VMEM((SUBLANES, TS), F32),
                        pltpu.VMEM((TS, LANES), F32), pltpu.VMEM((SUBLANES, TS), F32),
                        pltpu.VMEM((H, d, d), F32), pltpu.VMEM((H, 1, d), F32),
                        pltpu.VMEM((H, SUBLANES, LANES), F32)],
        compiler_params=_params(("arbitrary",), 48),
        name="mlstm_branch",
    )(x2d, norm_g, w_mu, w_mo, w_if, w_ift, b_if_c, b_if_r, conv_w, conv_b, w_q, w_k, w_v, ml_norm_g)


def _kv_kernel(mem_ref, g_ref, w_ref, o_ref):
    o_ref[...] = _dot(_rms(mem_ref[...], g_ref[...]).astype(BF16), w_ref[...]).astype(BF16)


def _mem_kv(mem, g, w_kv):
    B, M, D = mem.shape
    N = w_kv.shape[1]
    return pl.pallas_call(
        _kv_kernel,
        out_shape=jax.ShapeDtypeStruct((B, M, N), BF16),
        grid=(B,),
        in_specs=[pl.BlockSpec((None, M, D), lambda b: (b, 0, 0)), _const_spec((1, D)),
                  _const_spec((D, N))],
        out_specs=pl.BlockSpec((None, M, N), lambda b: (b, 0, 0)),
        compiler_params=_params(("arbitrary",), 32),
        name="mem_kv",
    )(mem, g, w_kv)


def _merge_kernel(x_ref, g_ref, wq_ref, wg_ref, bm_ref, yrg_ref, yml_ref, kv_ref, wrg_ref, wml_ref,
                  wxa_ref, wo_ref, o_ref):
    x = x_ref[...]
    D = x.shape[1]
    h = _rms(x, g_ref[...]).astype(BF16)
    q = _dot(h, wq_ref[...]).astype(BF16)
    dxa = q.shape[1]
    dh = dxa // XA_HEADS
    heads = []
    for hd in range(XA_HEADS):
        kh = kv_ref[:, hd * dh:(hd + 1) * dh]
        vh = kv_ref[:, dxa + hd * dh:dxa + (hd + 1) * dh]
        s = _dot_nt(q[:, hd * dh:(hd + 1) * dh], kh) * (dh ** -0.5)
        e = jnp.exp(s - jnp.max(s, axis=-1, keepdims=True))
        p = e / jnp.sum(e, axis=-1, keepdims=True)
        heads.append(_dot(p.astype(BF16), vh))
    y_xa = jnp.concatenate(heads, axis=1).astype(BF16)

    def gate(k):
        return jax.nn.sigmoid(_dot(h, wg_ref[:, k * D:(k + 1) * D]) + bm_ref[:, k * D:(k + 1) * D])

    merged = gate(0) * _dot(yrg_ref[...], wrg_ref[...])
    merged = merged + gate(1) * _dot(yml_ref[...], wml_ref[...])
    merged = merged + gate(2) * _dot(y_xa, wxa_ref[...])
    o_ref[...] = x + _dot(merged.astype(BF16), wo_ref[...])


def _merge(x2d, B, norm_g, w_q, w_g, b_merge, y_rg, y_ml, kv, w_br_rg, w_br_ml, w_br_xa, w_out):
    T, D = x2d.shape
    S = T // B
    C = y_ml.shape[1]
    M, N = kv.shape[1:]
    TM = min(MERGE_ROW_TILE, S)
    assert S % TM == 0
    nS = S // TM
    row = lambda b, s: (b * nS + s, 0)
    one = pl.Buffered(1)
    cspec = lambda shape: pl.BlockSpec(shape, lambda *_: (0,) * len(shape), pipeline_mode=one)
    return pl.pallas_call(
        _merge_kernel,
        out_shape=jax.ShapeDtypeStruct((T, D), F32),
        grid=(B, nS),
        in_specs=[pl.BlockSpec((TM, D), row), cspec((1, D)), cspec(w_q.shape), cspec(w_g.shape),
                  cspec(b_merge.shape), pl.BlockSpec((TM, C), row),
                  pl.BlockSpec((TM, C), row), pl.BlockSpec((None, M, N), lambda b, s: (b, 0, 0)),
                  cspec(w_br_rg.shape), cspec(w_br_ml.shape), cspec(w_br_xa.shape),
                  cspec(w_out.shape)],
        out_specs=pl.BlockSpec((TM, D), row),
        compiler_params=_params(("arbitrary", "arbitrary"), 56),
        name="merge",
    )(x2d, norm_g, w_q, w_g, b_merge, y_rg, y_ml, kv, w_br_rg, w_br_ml, w_br_xa, w_out)


def _ffn_kernel(x_ref, g_ref, w1_ref, w3_ref, w2_ref, *rest, n_cast):
    cast_in, o_ref, cast_out = rest[:n_cast], rest[n_cast], rest[n_cast + 1:2 * n_cast + 1]
    hs, acc = rest[2 * n_cast + 1:]
    f = pl.program_id(1)
    for src, dst in zip(cast_in, cast_out):
        dst[...] = src[...].astype(BF16)

    @pl.when(f == 0)
    def _():
        hs[...] = _rms(x_ref[...], g_ref[...]).astype(BF16)
        acc[...] = jnp.zeros_like(acc)

    hb = hs[...]
    mid = jax.nn.silu(_dot(hb, w1_ref[...])) * _dot(hb, w3_ref[...])
    acc[...] += _dot(mid.astype(BF16), w2_ref[...])

    @pl.when(f == pl.num_programs(1) - 1)
    def _():
        o_ref[...] = x_ref[...] + acc[...]


def _ffn_hidden_tile(F):
    for tf in (1792, 1024, 512, 256):
        if F % tf == 0:
            return tf
    return F


def _ffn_cast_block(x2d, w1, shape):
    tiles = x2d.shape[0] // min(FFN_ROW_TILE, x2d.shape[0])
    nF = w1.shape[1] // _ffn_hidden_tile(w1.shape[1])
    rows = math.prod(shape[:-1])
    for steps, every_step in ((tiles * nF, True), (tiles, False)):
        if rows % steps == 0 and (rows // steps) % (2 * SUBLANES) == 0:
            return rows // steps, every_step
    return None


def _ffn(x2d, norm_g, w1, w3, w2, cast=()):
    T, D = x2d.shape
    F = w1.shape[1]
    TM = min(FFN_ROW_TILE, T)
    TF = _ffn_hidden_tile(F)
    assert T % TM == 0
    nF = F // TF
    cast2d, cast_in_specs, cast_out_specs, cast_shapes = [], [], [], []
    for arr, idx in cast:
        shape = arr.shape[1:]
        rows, every_step = _ffn_cast_block(x2d, w1, shape)
        layer_rows = math.prod(shape[:-1])
        first = idx * (layer_rows // rows)
        step = (lambda i, f: i * nF + f) if every_step else (lambda i, f: i)
        cast2d.append(arr.reshape(-1, shape[-1]))
        cast_in_specs.append(pl.BlockSpec((rows, shape[-1]),
                                          lambda i, f, first=first, step=step: (first + step(i, f), 0)))
        cast_out_specs.append(pl.BlockSpec((rows, shape[-1]), lambda i, f, step=step: (step(i, f), 0)))
        cast_shapes.append(jax.ShapeDtypeStruct((layer_rows, shape[-1]), BF16))
    out = pl.pallas_call(
        functools.partial(_ffn_kernel, n_cast=len(cast)),
        out_shape=(jax.ShapeDtypeStruct((T, D), F32), *cast_shapes),
        grid=(T // TM, nF),
        in_specs=[pl.BlockSpec((TM, D), lambda i, f: (i, 0)), _const_spec((1, D)),
                  pl.BlockSpec((D, TF), lambda i, f: (0, f)), pl.BlockSpec((D, TF), lambda i, f: (0, f)),
                  pl.BlockSpec((TF, D), lambda i, f: (f, 0)), *cast_in_specs],
        out_specs=(pl.BlockSpec((TM, D), lambda i, f: (i, 0)), *cast_out_specs),
        scratch_shapes=[pltpu.VMEM((TM, D), BF16), pltpu.VMEM((TM, D), F32)],
        compiler_params=_params(("arbitrary", "arbitrary"), 56),
        name="ffn_dense",
    )(x2d, norm_g, w1, w3, w2, *cast2d)
    return out[0], [o.reshape(arr.shape[1:]) for o, (arr, _) in zip(out[1:], cast)]


def _route_kernel(x_ref, g_ref, wcat_ref, rb_ref, idx_ref, gate_ref, len_ref, off_ref,
                  tot_ref, h2_ref, run_s, *, E):
    @pl.when(pl.program_id(0) == 0)
    def _():
        run_s[...] = jnp.zeros_like(run_s)

    h2 = _rms(x_ref[...], g_ref[...])
    TM = h2.shape[0]
    hi = h2.astype(BF16)
    h2_ref[...] = hi
    lo = (h2 - hi.astype(F32)).astype(BF16)
    both = _dot(hi, wcat_ref[...])
    logits = both[:, 0:LANES] + (_dot(lo, wcat_ref[:, 0:LANES]) + both[:, LANES:2 * LANES])
    lt = logits.T[0:E, :] + rb_ref[...]
    ie = lax.broadcasted_iota(jnp.int32, (E, TM), 0)
    m1 = jnp.max(lt, axis=0, keepdims=True)
    i1 = jnp.min(jnp.where(lt == m1, ie, E), axis=0, keepdims=True)
    l2 = jnp.where(ie == i1, -jnp.inf, lt)
    m2 = jnp.max(l2, axis=0, keepdims=True)
    i2 = jnp.min(jnp.where(l2 == m2, ie, E), axis=0, keepdims=True)
    ex = jnp.exp(m2 - m1)
    g1 = 1.0 / (1.0 + ex)
    g2 = ex / (1.0 + ex)
    oh1 = jnp.where(ie == i1, 1.0, 0.0)
    oh2 = jnp.where(ie == i2, 1.0, 0.0)
    oh = oh1 + oh2
    ri = lax.broadcasted_iota(jnp.int32, (TM, TM), 0)
    ci = lax.broadcasted_iota(jnp.int32, (TM, TM), 1)
    upper = jnp.where(ri < ci, 1.0, 0.0).astype(BF16)
    excl = _dot(oh.astype(BF16), upper)
    cnt = jnp.sum(oh, axis=1, keepdims=True).astype(jnp.int32)
    run_len = jnp.broadcast_to(((cnt + (RUN_ALIGN - 1)) // RUN_ALIGN) * RUN_ALIGN, (E, LANES))
    iec = lax.broadcasted_iota(jnp.int32, (E, LANES), 0)
    run_start = jnp.zeros((E, LANES), jnp.int32)
    for e in range(E - 1):
        run_start = run_start + jnp.where(iec > e, run_len[e:e + 1, :], 0)
    slot = run_start[:, 0:1].astype(F32) + excl
    s1 = jnp.sum(oh1 * slot, axis=0, keepdims=True).astype(jnp.int32)
    s2 = jnp.sum(oh2 * slot, axis=0, keepdims=True).astype(jnp.int32)
    len_ref[...] = run_len
    off_ref[...] = run_s[...]
    run_s[...] = run_s[...] + run_len
    tot_ref[...] = run_s[...]
    row = lax.broadcasted_iota(jnp.int32, (SUBLANES, TM), 0)
    idx_ref[...] = jnp.where(row == 0, i1, jnp.where(row == 1, i2, jnp.where(row == 2, s1,
                             jnp.where(row == 3, s2, 0))))
    gate_ref[...] = jnp.where(row == 0, g1, jnp.where(row == 1, g2, 0.0))


def _route(x2d, norm_g, w_cat, rb, E):
    T, D = x2d.shape
    TM = min(ROUTE_TM, T)
    nT = T // TM
    assert T % TM == 0 and E == SUBLANES
    return pl.pallas_call(
        functools.partial(_route_kernel, E=E),
        out_shape=(jax.ShapeDtypeStruct((SUBLANES, T), jnp.int32),
                   jax.ShapeDtypeStruct((SUBLANES, T), F32),
                   jax.ShapeDtypeStruct((nT * E, LANES), jnp.int32),
                   jax.ShapeDtypeStruct((nT * E, LANES), jnp.int32),
                   jax.ShapeDtypeStruct((E, LANES), jnp.int32),
                   jax.ShapeDtypeStruct((T, D), BF16)),
        grid=(nT,),
        in_specs=[pl.BlockSpec((TM, D), lambda i: (i, 0)), _const_spec((1, D)),
                  _const_spec((D, 2 * LANES)), _const_spec((E, 1))],
        out_specs=(pl.BlockSpec((SUBLANES, TM), lambda i: (0, i)),
                   pl.BlockSpec((SUBLANES, TM), lambda i: (0, i)),
                   pl.BlockSpec((E, LANES), lambda i: (i, 0)),
                   pl.BlockSpec((E, LANES), lambda i: (i, 0)),
                   _const_spec((E, LANES)),
                   pl.BlockSpec((TM, D), lambda i: (i, 0))),
        scratch_shapes=[pltpu.VMEM((E, LANES), jnp.int32)],
        compiler_params=_params(("arbitrary",), 32),
        name="moe_route",
    )(x2d, norm_g, w_cat, rb)


def _run_dmas(len_ref, off_ref, i, E, max_len, tile_ref, sorted_ref, sem, to_sorted, wait, packed=True):
    local = 0
    for e in range(E):
        n = len_ref[i * E + e]
        base = off_ref[i * E + e]
        done = 0
        sz = max_len
        while sz >= RUN_ALIGN:
            @pl.when((n & sz) != 0)
            def _(sz=sz, local=local, base=base, done=done):
                t_rows = tile_ref.at[pl.ds(pl.multiple_of(local + done, RUN_ALIGN), sz), :]
                s_rows = sorted_ref.at[pl.ds(pl.multiple_of(base + done, RUN_ALIGN), sz), :]
                cp = (pltpu.make_async_copy(t_rows, s_rows, sem) if to_sorted
                      else pltpu.make_async_copy(s_rows, t_rows, sem))
                if wait:
                    cp.wait()
                else:
                    cp.start(priority=e % 2)
            done = done + (n & sz)
            sz //= 2
        if packed:
            local = local + n


def _dispatch_kernel(len_ref, off_ref, gap_len_ref, gap_off_ref, tail_ref, h2_ref, ls_ref, xs_ref, xs_t,
                     zeros, sem, *, E):
    i = pl.program_id(0)
    TM = h2_ref.shape[0]
    LS = xs_t.shape[1]
    slot = lax.rem(i, 2)
    h2 = h2_ref[...]
    ls = ls_ref[...]
    j = lax.broadcasted_iota(jnp.int32, (LS, TM), 0)
    onehot = jnp.where(j == ls[2:3, :], 1.0, jnp.where(j == ls[3:4, :], 1.0, 0.0)).astype(BF16)
    xs_t[slot] = _dot(onehot, h2).astype(BF16)

    @pl.when(i > 0)
    def _():
        _run_dmas(len_ref, off_ref, i - 1, E, TM, xs_t.at[1 - slot], xs_ref, sem.at[1 - slot], True, True)

    _run_dmas(len_ref, off_ref, i, E, TM, xs_t.at[slot], xs_ref, sem.at[slot], True, False)

    @pl.when(i == pl.num_programs(0) - 1)
    def _():
        _run_dmas(len_ref, off_ref, i, E, TM, xs_t.at[slot], xs_ref, sem.at[slot], True, True)
        zeros[...] = jnp.zeros_like(zeros)
        tmg = zeros.shape[0]

        def tail_copy(t):
            rows = pl.ds(pl.multiple_of(t * tmg, tmg), tmg)
            return pltpu.make_async_copy(zeros, xs_ref.at[rows, :], sem.at[slot])

        def tail_start(t, c):
            tail_copy(t).start()
            return c

        def tail_wait(t, c):
            tail_copy(t).wait()
            return c

        for wait in (False, True):
            _run_dmas(gap_len_ref, gap_off_ref, 0, E, tmg // 2, zeros, xs_ref, sem.at[slot], True, wait,
                      packed=False)
            lax.fori_loop(tail_ref[0], xs_ref.shape[0] // tmg, tail_wait if wait else tail_start, 0)


def _dispatch(run_len, run_off, gap_len, gap_off, tail_tile, idx, h2, P, E, TMG):
    T, D = h2.shape
    TM = min(ROUTE_TM, T)
    LS = TOP_K * TM + E * RUN_ALIGN
    grid_spec = pltpu.PrefetchScalarGridSpec(
        num_scalar_prefetch=5,
        grid=(T // TM,),
        in_specs=[pl.BlockSpec((TM, D), lambda i, *_: (i, 0)),
                  pl.BlockSpec((SUBLANES, TM), lambda i, *_: (0, i))],
        out_specs=pl.BlockSpec(memory_space=pl.ANY),
        scratch_shapes=[pltpu.VMEM((2, LS, D), BF16), pltpu.VMEM((TMG, D), BF16),
                        pltpu.SemaphoreType.DMA((2,))])
    return pl.pallas_call(
        functools.partial(_dispatch_kernel, E=E),
        out_shape=jax.ShapeDtypeStruct((P, D), BF16),
        grid_spec=grid_spec,
        compiler_params=_params(("arbitrary",), 32),
        name="moe_dispatch",
    )(run_len, run_off, gap_len, gap_off, tail_tile, h2, idx)


def _group_ffn_kernel(te_ref, tn_ref, x_ref, w1_ref, w3_ref, w2_ref, o_ref, acc):
    i = pl.program_id(0)
    f = pl.program_id(1)

    @pl.when(f == 0)
    def _():
        acc[...] = jnp.zeros_like(acc)

    @pl.when(tn_ref[i] > 0)
    def _():
        hb = x_ref[...]
        mid = jax.nn.silu(_dot(hb, w1_ref[...])) * _dot(hb, w3_ref[...])
        acc[...] += _dot(mid.astype(BF16), w2_ref[...])

    @pl.when(f == pl.num_programs(1) - 1)
    def _():
        o_ref[...] = acc[...]


def _group_ffn(tile_e, tile_n, xs, w1, w3, w2, TMG):
    P, D = xs.shape
    F = w1.shape[2]
    TF = _ffn_hidden_tile(F)
    nF = F // TF
    fblk = lambda i, f, tn: jnp.where(tn[i] > 0, f, nF - 1)
    grid_spec = pltpu.PrefetchScalarGridSpec(
        num_scalar_prefetch=2,
        grid=(P // TMG, nF),
        in_specs=[pl.BlockSpec((TMG, D), lambda i, f, te, tn: (jnp.where(tn[i] > 0, i, 0), 0)),
                  pl.BlockSpec((None, D, TF), lambda i, f, te, tn: (te[i], 0, fblk(i, f, tn))),
                  pl.BlockSpec((None, D, TF), lambda i, f, te, tn: (te[i], 0, fblk(i, f, tn))),
                  pl.BlockSpec((None, TF, D), lambda i, f, te, tn: (te[i], fblk(i, f, tn), 0))],
        out_specs=pl.BlockSpec((TMG, D), lambda i, f, te, tn: (i, 0)),
        scratch_shapes=[pltpu.VMEM((TMG, D), F32)])
    return pl.pallas_call(
        _group_ffn_kernel,
        out_shape=jax.ShapeDtypeStruct((P, D), F32),
        grid_spec=grid_spec,
        compiler_params=_params(("arbitrary", "arbitrary"), 56),
        name="moe_group_ffn",
    )(tile_e, tile_n, xs, w1, w3, w2)


def _combine_kernel(len_ref, off_ref, x_ref, lsr_ref, lsc_ref, gt_ref, fg_ref, ys_ref, o_ref, yt, sem,
                    *, E, final_norm):
    i = pl.program_id(0)
    TM = x_ref.shape[0]
    _, LS, D = yt.shape
    slot = lax.rem(i, 2)

    def fetch(tile, s):
        yt[s, TOP_K * TM:LS, :] = jnp.zeros((LS - TOP_K * TM, D), F32)
        _run_dmas(len_ref, off_ref, tile, E, TM, yt.at[s], ys_ref, sem.at[s], False, False)

    @pl.when(i == 0)
    def _():
        fetch(0, 0)

    @pl.when(i + 1 < pl.num_programs(0))
    def _():
        fetch(i + 1, 1 - slot)

    lsr = lsr_ref[...]
    gt = gt_ref[...]
    js = lax.broadcasted_iota(jnp.int32, (LS, TM), 0)
    lane = lax.broadcasted_iota(jnp.int32, (TM, LANES), 1)
    gs = None
    for k in range(TOP_K):
        t1, t2, t3 = (t.astype(F32) for t in _split3(gt[:, k:k + 1]))
        terms = jnp.where(lane == 0, t1, jnp.where(lane == 1, t2, jnp.where(lane == 2, t3, 0.0)))
        pk = jnp.where(js == lsr[TOP_K + k:TOP_K + k + 1, :], 1.0, 0.0).astype(BF16)
        gk = jnp.sum(_dot(pk, terms.astype(BF16)), axis=1, keepdims=True)
        gs = gk if gs is None else gs + gk
    _run_dmas(len_ref, off_ref, i, E, TM, yt.at[slot], ys_ref, sem.at[slot], False, True)
    z = yt[slot] * gs
    zh = z.astype(BF16)
    zl = (z - zh.astype(F32)).astype(BF16)
    lsc = lsc_ref[...]
    jt = lax.broadcasted_iota(jnp.int32, (TM, LS), 1)
    pt = None
    for k in range(TOP_K):
        ok = jnp.where(jt == lsc[:, TOP_K + k:TOP_K + k + 1], 1.0, 0.0)
        pt = ok if pt is None else pt + ok
    pt = pt.astype(BF16)
    xo = x_ref[...] + (_dot(pt, zh) + _dot(pt, zl))
    o_ref[...] = _rms(xo, fg_ref[...]) if final_norm else xo


def _combine(run_len, run_off, x2d, idx, idx_t, gates_t, final_g, ys, E, final_norm):
    T, D = x2d.shape
    TM = min(ROUTE_TM, T)
    LS = TOP_K * TM + E * RUN_ALIGN
    grid_spec = pltpu.PrefetchScalarGridSpec(
        num_scalar_prefetch=2,
        grid=(T // TM,),
        in_specs=[pl.BlockSpec((TM, D), lambda i, *_: (i, 0)),
                  pl.BlockSpec((SUBLANES, TM), lambda i, *_: (0, i)),
                  pl.BlockSpec((TM, SUBLANES), lambda i, *_: (i, 0)),
                  pl.BlockSpec((TM, SUBLANES), lambda i, *_: (i, 0)),
                  pl.BlockSpec((1, D), lambda i, *_: (0, 0)),
                  pl.BlockSpec(memory_space=pl.ANY)],
        out_specs=pl.BlockSpec((TM, D), lambda i, *_: (i, 0)),
        scratch_shapes=[pltpu.VMEM((2, LS, D), F32), pltpu.SemaphoreType.DMA((2,))])
    return pl.pallas_call(
        functools.partial(_combine_kernel, E=E, final_norm=final_norm),
        out_shape=jax.ShapeDtypeStruct((T, D), F32),
        grid_spec=grid_spec,
        compiler_params=_params(("arbitrary",), 48),
        name="moe_combine",
    )(run_len, run_off, x2d, idx, idx_t, gates_t, final_g, ys)


def _moe(x2d, norm_g, router_w, router_b, w1, w3, w2, final_g, final_norm):
    T, D = x2d.shape
    E = router_w.shape[1]
    TMG = FFN_ROW_TILE
    nT = T // min(ROUTE_TM, T)
    rw = jnp.pad(router_w, ((0, 0), (0, LANES - E)))
    rw_hi = rw.astype(BF16)
    rw_lo = (rw - rw_hi.astype(F32)).astype(BF16)
    idx, gates, run_len, run_off, tot, h2 = _route(x2d, norm_g, jnp.concatenate([rw_hi, rw_lo], axis=1),
                                                   router_b.reshape(E, 1), E)
    tot = tot[:, 0]
    padded = ((tot + TMG - 1) // TMG) * TMG
    pend = jnp.cumsum(padded)
    pstart = pend - padded
    run_len = run_len[:, 0]
    run_off = (run_off[:, 0].reshape(nT, E) + pstart[None, :]).reshape(nT * E)
    P = -(-(T * TOP_K + nT * E * (RUN_ALIGN - 1) + E * (TMG - 1)) // TMG) * TMG
    tile_start = jnp.arange(P // TMG, dtype=jnp.int32) * TMG
    tile_e = jnp.minimum(jnp.sum(tile_start[:, None] >= pend[None, :], axis=1), E - 1).astype(jnp.int32)
    sel = tile_e[:, None] == jnp.arange(E, dtype=jnp.int32)[None, :]
    tile_end = jnp.sum(jnp.where(sel, (pstart + tot)[None, :], 0), axis=1)
    tile_n = jnp.clip(tile_end - tile_start, 0, TMG).astype(jnp.int32)
    xs = _dispatch(run_len, run_off, padded - tot, pstart + tot, pend[E - 1:] // TMG, idx, h2, P, E, TMG)
    ys = _group_ffn(tile_e, tile_n, xs, w1, w3, w2, TMG)
    return _combine(run_len, run_off, x2d, idx, idx.T, gates.T, final_g, ys, E, final_norm)


def _final_norm_kernel(x_ref, g_ref, o_ref):
    o_ref[...] = _rms(x_ref[...], g_ref[...])


def _final_norm(x2d, g):
    T, D = x2d.shape
    TM = min(1024, T)
    return pl.pallas_call(
        _final_norm_kernel,
        out_shape=jax.ShapeDtypeStruct((T, D), F32),
        grid=(T // TM,),
        in_specs=[pl.BlockSpec((TM, D), lambda i: (i, 0)), _const_spec((1, D))],
        out_specs=pl.BlockSpec((TM, D), lambda i: (i, 0)),
        compiler_params=_params(("arbitrary",), 32),
        name="final_norm",
    )(x2d, g)


def kernel(x, mem, norm_mix_g, w_in, conv_rg_w, conv_rg_b, rg_w_a, rg_b_a, rg_w_x, rg_b_x, rg_lambda,
           conv_ml_w, conv_ml_b, ml_w_q, ml_w_k, ml_w_v, ml_b_i, ml_b_f, ml_norm_g, mem_norm_g, w_kv,
           w_br_rg, w_br_ml, w_br_xa, b_merge, w_out, norm_ffn_g, ffn_w1, ffn_w3, ffn_w2, router_w,
           router_b, moe_w1, moe_w3, moe_w2, final_norm_g):
    B, S, D = x.shape
    depth = w_in.shape[0]
    d_rg = conv_rg_w.shape[2]
    d_ml = conv_ml_w.shape[2]
    H = ml_w_q.shape[1]
    d_xa = w_kv.shape[2] // 2
    o_ax, o_ay = 0, d_rg
    o_mu, o_mo = 2 * d_rg, 2 * d_rg + d_ml
    o_mi = 2 * d_rg + 2 * d_ml
    o_mf = o_mi + H
    o_q = o_mf + H
    o_g = o_q + d_xa
    assert w_in.shape[2] == o_g + N_BRANCH * D and 2 * H <= SUBLANES

    bf = lambda a: a.astype(BF16)
    row = lambda a: a.reshape(1, -1)
    x2d = x.reshape(B * S, D)
    fg = row(final_norm_g)
    pre = {}

    def layer_bf16(name, arr, idx):
        return pre.pop((name, idx)) if (name, idx) in pre else bf(arr[idx])

    for l in range(depth):
        cols = lambda a, b, l=l: bf(w_in[l][:, a:b])
        w_if = jnp.pad(cols(o_mi, o_mi + 2 * H), ((0, 0), (0, LANES - 2 * H)))
        w_ift = jnp.pad(cols(o_mi, o_mi + 2 * H).T, ((0, SUBLANES - 2 * H), (0, 0)))
        b_if = jnp.concatenate([ml_b_i[l], ml_b_f[l]])
        y_ml = _mlstm_branch(x2d, B, row(norm_mix_g[l]), cols(o_mu, o_mu + d_ml),
                                   cols(o_mo, o_mo + d_ml), w_if, w_ift,
                                   jnp.pad(b_if, (0, LANES - 2 * H)).reshape(1, LANES),
                                   jnp.pad(b_if, (0, SUBLANES - 2 * H)).reshape(SUBLANES, 1),
                                   conv_ml_w[l], row(conv_ml_b[l]), bf(ml_w_q[l]), bf(ml_w_k[l]),
                                   bf(ml_w_v[l]), row(ml_norm_g[l]))
        y_rg = _rg_branch(x2d.reshape(B, S, D), row(norm_mix_g[l]), cols(o_ax, o_ax + d_rg),
                          cols(o_ay, o_ay + d_rg), conv_rg_w[l], row(conv_rg_b[l]), bf(rg_w_a[l]),
                          row(rg_b_a[l]), bf(rg_w_x[l]), row(rg_b_x[l]), row(rg_lambda[l]))
        kv = _mem_kv(mem, row(mem_norm_g[l]), layer_bf16("w_kv", w_kv, l))
        x2d = _merge(x2d, B, row(norm_mix_g[l]), cols(o_q, o_q + d_xa), cols(o_g, o_g + N_BRANCH * D),
                     row(b_merge[l]), y_rg.reshape(B * S, d_rg), y_ml, kv,
                     layer_bf16("w_br_rg", w_br_rg, l), layer_bf16("w_br_ml", w_br_ml, l),
                     layer_bf16("w_br_xa", w_br_xa, l), layer_bf16("w_out", w_out, l))
        j = l // 2
        if l % 2 == 0:
            w1 = bf(ffn_w1[j])
            todo = {}
            if l + 1 < depth:
                todo = {(name, l + 1): (arr, l + 1) for name, arr in (
                    ("w_kv", w_kv), ("w_br_rg", w_br_rg), ("w_br_ml", w_br_ml), ("w_br_xa", w_br_xa),
                    ("w_out", w_out))}
                if (l + 1) % 2 == 1:
                    jn = (l + 1) // 2
                    todo.update({("moe_w1", jn): (moe_w1, jn), ("moe_w3", jn): (moe_w3, jn),
                                 ("moe_w2", jn): (moe_w2, jn)})
            todo = {key: t for key, t in todo.items()
                    if _ffn_cast_block(x2d, w1, t[0].shape[1:]) is not None}
            x2d, done = _ffn(x2d, row(norm_ffn_g[l]), w1, bf(ffn_w3[j]), bf(ffn_w2[j]), list(todo.values()))
            pre.update(zip(todo.keys(), done))
        else:
            x2d = _moe(x2d, row(norm_ffn_g[l]), router_w[j], router_b[j], layer_bf16("moe_w1", moe_w1, j),
                       layer_bf16("moe_w3", moe_w3, j), layer_bf16("moe_w2", moe_w2, j), fg, l == depth - 1)
    if depth % 2 == 1:
        x2d = _final_norm(x2d, fg)
    return x2d.reshape(B, S, D)
```

```python
import functools
import math

import jax
import jax.numpy as jnp
from jax import lax
from jax.experimental import pallas as pl
from jax.experimental.pallas import tpu as pltpu

EPS = 1e-6
RG_C = 8.0
CONV_W = 4
ML_CHUNK = 128
XA_HEADS = 4
TOP_K = 2
N_BRANCH = 3

V7X_VMEM_BYTES = 64 * 1024 * 1024
LANES = 128
SUBLANES = 8

RG_TIME_TILE = 128
ML_ROW_TILE = 512
MERGE_ROW_TILE = 1024
FFN_ROW_TILE = 512
ROUTE_TM = 512
RUN_ALIGN = 16

F32 = jnp.float32
BF16 = jnp.bfloat16


def _params(semantics, vmem_mib):
    assert vmem_mib * 1024 * 1024 <= V7X_VMEM_BYTES
    return pltpu.CompilerParams(dimension_semantics=semantics,
                                vmem_limit_bytes=vmem_mib * 1024 * 1024)


def _const_spec(shape):
    nd = len(shape)
    return pl.BlockSpec(shape, lambda *_: (0,) * nd)


def _rms(x, g):
    ms = jnp.mean(x * x, axis=-1, keepdims=True)
    return x * lax.rsqrt(ms + EPS) * g


def _dot(a, b):
    return jnp.dot(a, b, preferred_element_type=F32)


def _dot_nt(a, b):
    return lax.dot_general(a, b, (((1,), (1,)), ((), ())), preferred_element_type=F32)


def _dot_tn(a, b):
    return lax.dot_general(a, b, (((0,), (0,)), ((), ())), preferred_element_type=F32)


def _split3(x):
    h1 = x.astype(BF16)
    r1 = x - h1.astype(F32)
    h2 = r1.astype(BF16)
    h3 = (r1 - h2.astype(F32)).astype(BF16)
    return h1, h2, h3


def _rg_kernel(x_ref, g_ref, wax_ref, way_ref, cw_ref, cb_ref, wa_ref, ba_ref, wx_ref, bx_ref,
               lam_ref, o_ref, axbuf, a_s, b_s, h_s, carry, xbuf, xsem, *, B, TT):
    R = TT * B
    halo = (CONV_W - 1) * B

    @pl.when(pl.program_id(0) == 0)
    def _():
        axbuf[0:halo, :] = jnp.zeros((halo, axbuf.shape[1]), F32)
        carry[...] = jnp.zeros_like(carry)

    i = pl.program_id(0)
    slot = lax.rem(i, 2)

    def x_copies(tile, s):
        rows = pl.ds(pl.multiple_of(tile * TT, TT), TT)
        return [pltpu.make_async_copy(x_ref.at[b, rows, :], xbuf.at[s, :, b, :], xsem.at[s])
                for b in range(B)]

    @pl.when(i == 0)
    def _():
        for cp in x_copies(0, 0):
            cp.start()

    @pl.when(i + 1 < pl.num_programs(0))
    def _():
        for cp in x_copies(i + 1, 1 - slot):
            cp.start()

    for cp in x_copies(i, slot):
        cp.wait()
    x = xbuf[slot].reshape(R, xbuf.shape[3])
    h = _rms(x, g_ref[...]).astype(BF16)
    G, bi, _ = wa_ref.shape
    rate = -RG_C * jax.nn.softplus(-lam_ref[...])
    cols = [slice(g * bi, (g + 1) * bi) for g in range(G)]
    axbuf[halo:halo + R, cols[0]] = _dot(h, wax_ref[:, cols[0]])
    for g in range(G):
        cs = cols[g]
        if g + 1 < G:
            axbuf[halo:halo + R, cols[g + 1]] = _dot(h, wax_ref[:, cols[g + 1]])
        ay = _dot(h, way_ref[:, cs])
        cw = cw_ref[:, cs]
        xc = cb_ref[:, cs] + cw[0:1, :] * axbuf[0:R, cs]
        for k in range(1, CONV_W):
            xc = xc + cw[k:k + 1, :] * axbuf[k * B:k * B + R, cs]
        axbuf[0:halo, cs] = axbuf[R:R + halo, cs]
        xcb = xc.astype(BF16)
        r = jax.nn.sigmoid(_dot(xcb, wa_ref[g]) + ba_ref[:, cs])
        ig = jax.nn.sigmoid(_dot(xcb, wx_ref[g]) + bx_ref[:, cs])
        log_a = r * rate[:, cs]
        a = jnp.exp(log_a)
        a_s[:, cs] = a
        b_s[:, cs] = jnp.sqrt(-jnp.tanh(log_a) * (a * a + 1.0)) * (ig * xc)

        def step(t, hc, cs=cs):
            off = pl.multiple_of(t * B, B)
            hn = a_s[pl.ds(off, B), cs] * hc + b_s[pl.ds(off, B), cs]
            h_s[pl.ds(off, B), cs] = hn
            return hn

        carry[:, cs] = lax.fori_loop(0, TT, step, carry[:, cs], unroll=True)
        y = (h_s[:, cs] * jax.nn.gelu(ay)).astype(BF16).reshape(TT, B, bi)
        o_ref[:, :, cs] = pltpu.einshape("tbc->btc", y)


def _rg_branch(x3d, norm_g, w_ax, w_ay, conv_w, conv_b, w_a, b_a, w_x, b_x, lam):
    B, S, D = x3d.shape
    C = w_ax.shape[1]
    TT = min(RG_TIME_TILE, S)
    assert S % TT == 0 and B % SUBLANES == 0
    R = TT * B
    halo = (CONV_W - 1) * B
    tile = lambda i: (0, i, 0)
    return pl.pallas_call(
        functools.partial(_rg_kernel, B=B, TT=TT),
        out_shape=jax.ShapeDtypeStruct((B, S, C), BF16),
        grid=(S // TT,),
        in_specs=[pl.BlockSpec(memory_space=pl.ANY), _const_spec((1, D)), _const_spec((D, C)),
                  _const_spec((D, C)), _const_spec((CONV_W, C)), _const_spec((1, C)),
                  _const_spec(w_a.shape), _const_spec((1, C)), _const_spec(w_x.shape),
                  _const_spec((1, C)), _const_spec((1, C))],
        out_specs=pl.BlockSpec((B, TT, C), tile),
        scratch_shapes=[pltpu.VMEM((halo + R, C), F32), pltpu.VMEM((R, C), F32),
                        pltpu.VMEM((R, C), F32), pltpu.VMEM((R, C), F32), pltpu.VMEM((B, C), F32),
                        pltpu.VMEM((2, TT, B, D), F32), pltpu.SemaphoreType.DMA((2,))],
        compiler_params=_params(("arbitrary",), 48),
        name="rg_branch",
    )(x3d, norm_g, w_ax, w_ay, conv_w, conv_b, w_a, b_a, w_x, b_x, lam)


def _alternate(first, second):
    result = None
    live = [True, True]
    while any(live):
        for n, gen in enumerate((first, second)):
            if live[n]:
                try:
                    next(gen)
                except StopIteration as stop:
                    live[n] = False
                    if n == 1:
                        result = stop.value
    return result


def _mlstm_kernel(x_ref, g_ref, wmu_ref, wmo_ref, wif_ref, wift_ref, bifc_ref, bifr_ref, cw_ref,
                  cb_ref, wq_ref, wk_ref, wv_ref, ng_ref, o_ref, ubuf, q_s, k_s, v_s, og_s, ifc_s, ifr_s,
                  lfc_s, lfr_s, c_st, n_st, m_st, *, TS, nS):
    H, d, _ = wq_ref.shape
    L = ML_CHUNK
    pad = SUBLANES
    tail = CONV_W - 1
    g = pl.program_id(0)

    @pl.when(g == 0)
    def _():
        for r in (q_s, k_s, v_s, og_s, ifc_s, ifr_s, lfc_s, lfr_s):
            r[...] = jnp.zeros_like(r)

    @pl.when(lax.rem(g, nS) == 0)
    def _():
        ubuf[...] = jnp.zeros_like(ubuf)

    @pl.when((g == 0) | (lax.rem(g + (nS - 1), nS) == 0))
    def _():
        c_st[...] = jnp.zeros_like(c_st)
        n_st[...] = jnp.zeros_like(n_st)
        m_st[...] = jnp.zeros_like(m_st)

    if_c = ifc_s[...]
    if_r = ifr_s[...]
    lf_c = lfc_s[...]
    lf_r = lfr_s[...]
    ri = lax.broadcasted_iota(jnp.int32, (L, L), 0)
    ci = lax.broadcasted_iota(jnp.int32, (L, L), 1)
    causal = ci <= ri
    tri_l = jnp.where(causal, 1.0, 0.0).astype(BF16)
    tri_u = jnp.where(ri <= ci, 1.0, 0.0).astype(BF16)
    ng = ng_ref[...]
    nck = TS // L
    h = _rms(x_ref[...], g_ref[...]).astype(BF16)
    us = [_dot(h, wmu_ref[:, 0:d])] + [None] * (H - 1)
    bcs = [sum(_dot(tri_l, p) for p in _split3(lf_c[ck * L:(ck + 1) * L, :])) for ck in range(nck)]
    brs = [sum(_dot(p, tri_u) for p in _split3(lf_r[:, ck * L:(ck + 1) * L])) for ck in range(nck)]

    def wide(t):
        return jnp.concatenate([t] * (d // LANES), axis=1)

    def recurrence(hd):
        cs = slice(hd * d, (hd + 1) * d)
        for ck in range(nck):
            r0 = ck * L
            bc, br = bcs[ck], brs[ck]
            b_col = jnp.broadcast_to(bc[:, H + hd:H + hd + 1], (L, LANES))
            i_col = jnp.broadcast_to(if_c[r0:r0 + L, hd:hd + 1], (L, LANES))
            b_row = br[H + hd:H + hd + 1, :]
            i_row = if_r[hd:hd + 1, r0:r0 + L]
            m = m_st[hd][0:1, :]
            qb = q_s[r0:r0 + L, cs]
            kb = k_s[r0:r0 + L, cs]
            vb = v_s[r0:r0 + L, cs]
            gg = b_col + m
            dm = jnp.where(causal, b_col - b_row + i_row, -jnp.inf)
            yield
            m_row = jnp.maximum(gg, jnp.max(dm, axis=-1, keepdims=True))
            s_qk = _dot_nt(qb, kb)
            yield
            w = jnp.exp(dm - m_row) * s_qk
            inter = jnp.exp(gg - m_row)
            cmat = c_st[hd]
            nvec = n_st[hd]
            yield
            num = wide(inter) * _dot(qb, cmat.astype(BF16)) + _dot(w.astype(BF16), vb)
            yield
            qf, kf, vf = qb.astype(F32), kb.astype(F32), vb.astype(F32)
            den = inter * jnp.sum(qf * nvec, axis=-1, keepdims=True) + jnp.sum(w, axis=-1, keepdims=True)
            hh = num * wide(1.0 / jnp.maximum(jnp.abs(den), jnp.exp(-m_row)))
            yield
            b_last = b_col[L - 1:L, :]
            dl = b_last - b_col + i_col
            m_new = jnp.maximum(b_last + m, jnp.max(dl, axis=0, keepdims=True))
            decay = jnp.exp(b_last + m - m_new)
            wl = jnp.exp(dl - m_new)
            yield
            c_st[hd] = wide(decay) * cmat + _dot_tn(kb, (wide(wl) * vf).astype(BF16))
            n_st[hd] = wide(decay) * nvec + jnp.sum(wide(wl) * kf, axis=0, keepdims=True)
            m_st[hd] = jnp.broadcast_to(m_new, m_st.shape[1:])
            yield
            y = og_s[r0:r0 + L, cs] * hh
            y = y * lax.rsqrt(jnp.mean(y * y, axis=-1, keepdims=True) + EPS)
            o_ref[r0:r0 + L, cs] = (y * ng[:, cs]).astype(BF16)
            yield

    def projection(hd):
        cs = slice(hd * d, (hd + 1) * d)
        u = us[hd]
        mo = _dot(h, wmo_ref[:, cs])
        yield
        if hd + 1 < H:
            us[hd + 1] = _dot(h, wmu_ref[:, (hd + 1) * d:(hd + 2) * d])
            yield
        cw = cw_ref[:, cs]
        halo = ubuf[:, cs]
        rowi = lax.broadcasted_iota(jnp.int32, (pad, d), 0)
        c = cb_ref[:, cs] + cw[tail:tail + 1, :] * u
        for j in range(1, CONV_W):
            sh = pltpu.roll(u, j, axis=0)
            head = jnp.where(rowi < j, pltpu.roll(halo, j, axis=0), sh[0:pad, :])
            sh = jnp.concatenate([head, sh[pad:, :]], axis=0)
            c = c + cw[tail - j:tail - j + 1, :] * sh
            yield
        ubuf[:, cs] = u[TS - pad:TS, :]
        cb16 = jax.nn.silu(c).astype(BF16)
        yield
        q = _dot(cb16, wq_ref[hd]).astype(BF16)
        yield
        k_ = (_dot(cb16, wk_ref[hd]) * (d ** -0.5)).astype(BF16)
        yield
        v = _dot(u.astype(BF16), wv_ref[hd]).astype(BF16)
        yield
        og = jax.nn.sigmoid(mo)
        yield
        return q, k_, v, og

    for hd in range(H):
        cs = slice(hd * d, (hd + 1) * d)
        q, k_, v, og = _alternate(recurrence(hd), projection(hd))
        q_s[:, cs] = q
        k_s[:, cs] = k_
        v_s[:, cs] = v
        og_s[:, cs] = og
    new_if_c = _dot(h, wif_ref[...]) + bifc_ref[...]
    new_if_r = _dot_nt(wift_ref[...], h) + bifr_ref[...]
    ifc_s[...] = new_if_c
    ifr_s[...] = new_if_r
    lfc_s[...] = jax.nn.log_sigmoid(new_if_c)
    lfr_s[...] = jax.nn.log_sigmoid(new_if_r)


def _mlstm_branch(x2d, B, norm_g, w_mu, w_mo, w_if, w_ift, b_if_c, b_if_r, conv_w, conv_b,
                  w_q, w_k, w_v, ml_norm_g):
    T, D = x2d.shape
    S = T // B
    C = w_mu.shape[1]
    H, d, _ = w_q.shape
    TS = min(ML_ROW_TILE, S)
    assert S % TS == 0 and TS % ML_CHUNK == 0 and d % LANES == 0
    nS = S // TS
    G = B * nS
    return pl.pallas_call(
        functools.partial(_mlstm_kernel, TS=TS, nS=nS),
        out_shape=jax.ShapeDtypeStruct((T, C), BF16),
        grid=(G + 1,),
        in_specs=[pl.BlockSpec((TS, D), lambda g: (jnp.minimum(g, G - 1), 0)), _const_spec((1, D)),
                  _const_spec((D, C)), _const_spec((D, C)), _const_spec((D, LANES)),
                  _const_spec((SUBLANES, D)), _const_spec((1, LANES)), _const_spec((SUBLANES, 1)),
                  _const_spec((CONV_W, C)), _const_spec((1, C)), _const_spec(w_q.shape),
                  _const_spec(w_k.shape), _const_spec(w_v.shape), _const_spec((1, C))],
        out_specs=pl.BlockSpec((TS, C), lambda g: (jnp.maximum(g - 1, 0), 0)),
        scratch_shapes=[pltpu.VMEM((SUBLANES, C), F32), pltpu.VMEM((TS, C), BF16),
                        pltpu.VMEM((TS, C), BF16), pltpu.VMEM((TS, C), BF16), pltpu.VMEM((TS, C), F32),
                        pltpu.VMEM((TS, LANES), F32), pltpu.VMEM((SUBLANES, TS), F32),
                        pltpu.VMEM((TS, LANES), F32), pltpu.VMEM((SUBLANES, TS), F32),
                        pltpu.VMEM((H, d, d), F32), pltpu.VMEM((H, 1, d), F32),
                        pltpu.VMEM((H, SUBLANES, LANES), F32)],
        compiler_params=_params(("arbitrary",), 48),
        name="mlstm_branch",
    )(x2d, norm_g, w_mu, w_mo, w_if, w_ift, b_if_c, b_if_r, conv_w, conv_b, w_q, w_k, w_v, ml_norm_g)


def _kv_kernel(mem_ref, g_ref, w_ref, o_ref):
    o_ref[...] = _dot(_rms(mem_ref[...], g_ref[...]).astype(BF16), w_ref[...]).astype(BF16)


def _mem_kv(mem, g, w_kv):
    B, M, D = mem.shape
    N = w_kv.shape[1]
    return pl.pallas_call(
        _kv_kernel,
        out_shape=jax.ShapeDtypeStruct((B, M, N), BF16),
        grid=(B,),
        in_specs=[pl.BlockSpec((None, M, D), lambda b: (b, 0, 0)), _const_spec((1, D)),
                  _const_spec((D, N))],
        out_specs=pl.BlockSpec((None, M, N), lambda b: (b, 0, 0)),
        compiler_params=_params(("arbitrary",), 32),
        name="mem_kv",
    )(mem, g, w_kv)


def _merge_kernel(x_ref, g_ref, wq_ref, wg_ref, bm_ref, yrg_ref, yml_ref, kv_ref, wrg_ref, wml_ref,
                  wxa_ref, wo_ref, o_ref):
    x = x_ref[...]
    D = x.shape[1]
    h = _rms(x, g_ref[...]).astype(BF16)
    q = _dot(h, wq_ref[...]).astype(BF16)
    dxa = q.shape[1]
    dh = dxa // XA_HEADS
    heads = []
    for hd in range(XA_HEADS):
        kh = kv_ref[:, hd * dh:(hd + 1) * dh]
        vh = kv_ref[:, dxa + hd * dh:dxa + (hd + 1) * dh]
        s = _dot_nt(q[:, hd * dh:(hd + 1) * dh], kh) * (dh ** -0.5)
        e = jnp.exp(s - jnp.max(s, axis=-1, keepdims=True))
        p = e / jnp.sum(e, axis=-1, keepdims=True)
        heads.append(_dot(p.astype(BF16), vh))
    y_xa = jnp.concatenate(heads, axis=1).astype(BF16)

    ys = (yrg_ref[...], yml_ref[...], y_xa)
    ws = (wrg_ref, wml_ref, wxa_ref)
    halves = []
    for c in range(2):
        lo_c, hi_c = c * (D // 2), (c + 1) * (D // 2)
        m = None
        for k in range(N_BRANCH):
            gk = jax.nn.sigmoid(_dot(h, wg_ref[:, k * D + lo_c:k * D + hi_c]) + bm_ref[:, k * D + lo_c:k * D + hi_c])
            t = gk * _dot(ys[k], ws[k][:, lo_c:hi_c])
            m = t if m is None else m + t
        halves.append(m.astype(BF16))
    o_ref[...] = x + _dot(jnp.concatenate(halves, axis=1), wo_ref[...])


def _merge(x2d, B, norm_g, w_q, w_g, b_merge, y_rg, y_ml, kv, w_br_rg, w_br_ml, w_br_xa, w_out):
    T, D = x2d.shape
    S = T // B
    C = y_ml.shape[1]
    M, N = kv.shape[1:]
    TM = min(MERGE_ROW_TILE, S)
    assert S % TM == 0
    nS = S // TM
    row = lambda b, s: (b * nS + s, 0)
    one = pl.Buffered(1)
    cspec = lambda shape: pl.BlockSpec(shape, lambda *_: (0,) * len(shape), pipeline_mode=one)
    return pl.pallas_call(
        _merge_kernel,
        out_shape=jax.ShapeDtypeStruct((T, D), F32),
        grid=(B, nS),
        in_specs=[pl.BlockSpec((TM, D), row), cspec((1, D)), cspec(w_q.shape), cspec(w_g.shape),
                  cspec(b_merge.shape), pl.BlockSpec((TM, C), row),
                  pl.BlockSpec((TM, C), row), pl.BlockSpec((None, M, N), lambda b, s: (b, 0, 0)),
                  cspec(w_br_rg.shape), cspec(w_br_ml.shape), cspec(w_br_xa.shape),
                  cspec(w_out.shape)],
        out_specs=pl.BlockSpec((TM, D), row),
        compiler_params=_params(("arbitrary", "arbitrary"), 56),
        name="merge",
    )(x2d, norm_g, w_q, w_g, b_merge, y_rg, y_ml, kv, w_br_rg, w_br_ml, w_br_xa, w_out)


def _ffn_kernel(x_ref, g_ref, w1_ref, w3_ref, w2_ref, *rest, n_cast):
    cast_in, o_ref, cast_out = rest[:n_cast], rest[n_cast], rest[n_cast + 1:2 * n_cast + 1]
    hs, acc = rest[2 * n_cast + 1:]
    f = pl.program_id(1)
    for src, dst in zip(cast_in, cast_out):
        dst[...] = src[...].astype(BF16)

    @pl.when(f == 0)
    def _():
        hs[...] = _rms(x_ref[...], g_ref[...]).astype(BF16)
        acc[...] = jnp.zeros_like(acc)

    hb = hs[...]
    mid = jax.nn.silu(_dot(hb, w1_ref[...])) * _dot(hb, w3_ref[...])
    acc[...] += _dot(mid.astype(BF16), w2_ref[...])

    @pl.when(f == pl.num_programs(1) - 1)
    def _():
        o_ref[...] = x_ref[...] + acc[...]


def _ffn_hidden_tile(F):
    for tf in (1792, 1024, 512, 256):
        if F % tf == 0:
            return tf
    return F


def _ffn_cast_block(x2d, w1, shape):
    tiles = x2d.shape[0] // min(FFN_ROW_TILE, x2d.shape[0])
    nF = w1.shape[1] // _ffn_hidden_tile(w1.shape[1])
    rows = math.prod(shape[:-1])
    for steps, every_step in ((tiles * nF, True), (tiles, False)):
        if rows % steps == 0 and (rows // steps) % (2 * SUBLANES) == 0:
            return rows // steps, every_step
    return None


def _ffn(x2d, norm_g, w1, w3, w2, cast=()):
    T, D = x2d.shape
    F = w1.shape[1]
    TM = min(FFN_ROW_TILE, T)
    TF = _ffn_hidden_tile(F)
    assert T % TM == 0
    nF = F // TF
    cast2d, cast_in_specs, cast_out_specs, cast_shapes = [], [], [], []
    for arr, idx in cast:
        shape = arr.shape[1:]
        rows, every_step = _ffn_cast_block(x2d, w1, shape)
        layer_rows = math.prod(shape[:-1])
        first = idx * (layer_rows // rows)
        step = (lambda i, f: i * nF + f) if every_step else (lambda i, f: i)
        cast2d.append(arr.reshape(-1, shape[-1]))
        cast_in_specs.append(pl.BlockSpec((rows, shape[-1]),
                                          lambda i, f, first=first, step=step: (first + step(i, f), 0)))
        cast_out_specs.append(pl.BlockSpec((rows, shape[-1]), lambda i, f, step=step: (step(i, f), 0)))
        cast_shapes.append(jax.ShapeDtypeStruct((layer_rows, shape[-1]), BF16))
    out = pl.pallas_call(
        functools.partial(_ffn_kernel, n_cast=len(cast)),
        out_shape=(jax.ShapeDtypeStruct((T, D), F32), *cast_shapes),
        grid=(T // TM, nF),
        in_specs=[pl.BlockSpec((TM, D), lambda i, f: (i, 0)), _const_spec((1, D)),
                  pl.BlockSpec((D, TF), lambda i, f: (0, f)), pl.BlockSpec((D, TF), lambda i, f: (0, f)),
                  pl.BlockSpec((TF, D), lambda i, f: (f, 0)), *cast_in_specs],
        out_specs=(pl.BlockSpec((TM, D), lambda i, f: (i, 0)), *cast_out_specs),
        scratch_shapes=[pltpu.VMEM((TM, D), BF16), pltpu.VMEM((TM, D), F32)],
        compiler_params=_params(("arbitrary", "arbitrary"), 56),
        name="ffn_dense",
    )(x2d, norm_g, w1, w3, w2, *cast2d)
    return out[0], [o.reshape(arr.shape[1:]) for o, (arr, _) in zip(out[1:], cast)]


def _route_kernel(x_ref, g_ref, wcat_ref, rb_ref, idx_ref, gate_ref, len_ref, off_ref,
                  tot_ref, h2_ref, run_s, *, E):
    @pl.when(pl.program_id(0) == 0)
    def _():
        run_s[...] = jnp.zeros_like(run_s)

    h2 = _rms(x_ref[...], g_ref[...])
    TM = h2.shape[0]
    hi = h2.astype(BF16)
    h2_ref[...] = hi
    lo = (h2 - hi.astype(F32)).astype(BF16)
    both = _dot(hi, wcat_ref[...])
    logits = both[:, 0:LANES] + (_dot(lo, wcat_ref[:, 0:LANES]) + both[:, LANES:2 * LANES])
    lt = logits.T[0:E, :] + rb_ref[...]
    ie = lax.broadcasted_iota(jnp.int32, (E, TM), 0)
    m1 = jnp.max(lt, axis=0, keepdims=True)
    i1 = jnp.min(jnp.where(lt == m1, ie, E), axis=0, keepdims=True)
    l2 = jnp.where(ie == i1, -jnp.inf, lt)
    m2 = jnp.max(l2, axis=0, keepdims=True)
    i2 = jnp.min(jnp.where(l2 == m2, ie, E), axis=0, keepdims=True)
    ex = jnp.exp(m2 - m1)
    g1 = 1.0 / (1.0 + ex)
    g2 = ex / (1.0 + ex)
    oh1 = jnp.where(ie == i1, 1.0, 0.0)
    oh2 = jnp.where(ie == i2, 1.0, 0.0)
    oh = oh1 + oh2
    ri = lax.broadcasted_iota(jnp.int32, (TM, TM), 0)
    ci = lax.broadcasted_iota(jnp.int32, (TM, TM), 1)
    upper = jnp.where(ri < ci, 1.0, 0.0).astype(BF16)
    excl = _dot(oh.astype(BF16), upper)
    cnt = jnp.sum(oh, axis=1, keepdims=True).astype(jnp.int32)
    run_len = jnp.broadcast_to(((cnt + (RUN_ALIGN - 1)) // RUN_ALIGN) * RUN_ALIGN, (E, LANES))
    iec = lax.broadcasted_iota(jnp.int32, (E, LANES), 0)
    run_start = jnp.zeros((E, LANES), jnp.int32)
    for e in range(E - 1):
        run_start = run_start + jnp.where(iec > e, run_len[e:e + 1, :], 0)
    slot = run_start[:, 0:1].astype(F32) + excl
    s1 = jnp.sum(oh1 * slot, axis=0, keepdims=True).astype(jnp.int32)
    s2 = jnp.sum(oh2 * slot, axis=0, keepdims=True).astype(jnp.int32)
    len_ref[...] = run_len
    off_ref[...] = run_s[...]
    run_s[...] = run_s[...] + run_len
    tot_ref[...] = run_s[...]
    row = lax.broadcasted_iota(jnp.int32, (SUBLANES, TM), 0)
    idx_ref[...] = jnp.where(row == 0, i1, jnp.where(row == 1, i2, jnp.where(row == 2, s1,
                             jnp.where(row == 3, s2, 0))))
    gate_ref[...] = jnp.where(row == 0, g1, jnp.where(row == 1, g2, 0.0))


def _route(x2d, norm_g, w_cat, rb, E):
    T, D = x2d.shape
    TM = min(ROUTE_TM, T)
    nT = T // TM
    assert T % TM == 0 and E == SUBLANES
    return pl.pallas_call(
        functools.partial(_route_kernel, E=E),
        out_shape=(jax.ShapeDtypeStruct((SUBLANES, T), jnp.int32),
                   jax.ShapeDtypeStruct((SUBLANES, T), F32),
                   jax.ShapeDtypeStruct((nT * E, LANES), jnp.int32),
                   jax.ShapeDtypeStruct((nT * E, LANES), jnp.int32),
                   jax.ShapeDtypeStruct((E, LANES), jnp.int32),
                   jax.ShapeDtypeStruct((T, D), BF16)),
        grid=(nT,),
        in_specs=[pl.BlockSpec((TM, D), lambda i: (i, 0)), _const_spec((1, D)),
                  _const_spec((D, 2 * LANES)), _const_spec((E, 1))],
        out_specs=(pl.BlockSpec((SUBLANES, TM), lambda i: (0, i)),
                   pl.BlockSpec((SUBLANES, TM), lambda i: (0, i)),
                   pl.BlockSpec((E, LANES), lambda i: (i, 0)),
                   pl.BlockSpec((E, LANES), lambda i: (i, 0)),
                   _const_spec((E, LANES)),
                   pl.BlockSpec((TM, D), lambda i: (i, 0))),
        scratch_shapes=[pltpu.VMEM((E, LANES), jnp.int32)],
        compiler_params=_params(("arbitrary",), 32),
        name="moe_route",
    )(x2d, norm_g, w_cat, rb)


def _run_dmas(len_ref, off_ref, i, E, max_len, tile_ref, sorted_ref, sem, to_sorted, wait, packed=True):
    local = 0
    for e in range(E):
        n = len_ref[i * E + e]
        base = off_ref[i * E + e]
        done = 0
        sz = max_len
        while sz >= RUN_ALIGN:
            @pl.when((n & sz) != 0)
            def _(sz=sz, local=local, base=base, done=done):
                t_rows = tile_ref.at[pl.ds(pl.multiple_of(local + done, RUN_ALIGN), sz), :]
                s_rows = sorted_ref.at[pl.ds(pl.multiple_of(base + done, RUN_ALIGN), sz), :]
                cp = (pltpu.make_async_copy(t_rows, s_rows, sem) if to_sorted
                      else pltpu.make_async_copy(s_rows, t_rows, sem))
                if wait:
                    cp.wait()
                else:
                    cp.start()
            done = done + (n & sz)
            sz //= 2
        if packed:
            local = local + n


def _dispatch_kernel(len_ref, off_ref, gap_len_ref, gap_off_ref, tail_ref, h2_ref, ls_ref, xs_ref, xs_t,
                     zeros, sem, *, E):
    i = pl.program_id(0)
    TM = h2_ref.shape[0]
    LS = xs_t.shape[1]
    slot = lax.rem(i, 2)
    h2 = h2_ref[...]
    ls = ls_ref[...]
    j = lax.broadcasted_iota(jnp.int32, (LS, TM), 0)
    onehot = jnp.where(j == ls[2:3, :], 1.0, jnp.where(j == ls[3:4, :], 1.0, 0.0)).astype(BF16)
    xs_t[slot] = _dot(onehot, h2).astype(BF16)

    @pl.when(i > 0)
    def _():
        _run_dmas(len_ref, off_ref, i - 1, E, TM, xs_t.at[1 - slot], xs_ref, sem.at[1 - slot], True, True)

    _run_dmas(len_ref, off_ref, i, E, TM, xs_t.at[slot], xs_ref, sem.at[slot], True, False)

    @pl.when(i == pl.num_programs(0) - 1)
    def _():
        _run_dmas(len_ref, off_ref, i, E, TM, xs_t.at[slot], xs_ref, sem.at[slot], True, True)
        zeros[...] = jnp.zeros_like(zeros)
        tmg = zeros.shape[0]

        def tail_copy(t):
            rows = pl.ds(pl.multiple_of(t * tmg, tmg), tmg)
            return pltpu.make_async_copy(zeros, xs_ref.at[rows, :], sem.at[slot])

        def tail_start(t, c):
            tail_copy(t).start()
            return c

        def tail_wait(t, c):
            tail_copy(t).wait()
            return c

        for wait in (False, True):
            _run_dmas(gap_len_ref, gap_off_ref, 0, E, tmg // 2, zeros, xs_ref, sem.at[slot], True, wait,
                      packed=False)
            lax.fori_loop(tail_ref[0], xs_ref.shape[0] // tmg, tail_wait if wait else tail_start, 0)


def _dispatch(run_len, run_off, gap_len, gap_off, tail_tile, idx, h2, P, E, TMG):
    T, D = h2.shape
    TM = min(ROUTE_TM, T)
    LS = TOP_K * TM + E * RUN_ALIGN
    grid_spec = pltpu.PrefetchScalarGridSpec(
        num_scalar_prefetch=5,
        grid=(T // TM,),
        in_specs=[pl.BlockSpec((TM, D), lambda i, *_: (i, 0)),
                  pl.BlockSpec((SUBLANES, TM), lambda i, *_: (0, i))],
        out_specs=pl.BlockSpec(memory_space=pl.ANY),
        scratch_shapes=[pltpu.VMEM((2, LS, D), BF16), pltpu.VMEM((TMG, D), BF16),
                        pltpu.SemaphoreType.DMA((2,))])
    return pl.pallas_call(
        functools.partial(_dispatch_kernel, E=E),
        out_shape=jax.ShapeDtypeStruct((P, D), BF16),
        grid_spec=grid_spec,
        compiler_params=_params(("arbitrary",), 32),
        name="moe_dispatch",
    )(run_len, run_off, gap_len, gap_off, tail_tile, h2, idx)


def _group_ffn_kernel(te_ref, tn_ref, x_ref, w1_ref, w3_ref, w2_ref, o_ref, acc):
    i = pl.program_id(0)
    f = pl.program_id(1)

    @pl.when(f == 0)
    def _():
        acc[...] = jnp.zeros_like(acc)

    @pl.when(tn_ref[i] > 0)
    def _():
        hb = x_ref[...]
        mid = jax.nn.silu(_dot(hb, w1_ref[...])) * _dot(hb, w3_ref[...])
        acc[...] += _dot(mid.astype(BF16), w2_ref[...])

    @pl.when(f == pl.num_programs(1) - 1)
    def _():
        o_ref[...] = acc[...]


def _group_ffn(tile_e, tile_n, xs, w1, w3, w2, TMG):
    P, D = xs.shape
    F = w1.shape[2]
    TF = _ffn_hidden_tile(F)
    nF = F // TF
    fblk = lambda i, f, tn: jnp.where(tn[i] > 0, f, nF - 1)
    grid_spec = pltpu.PrefetchScalarGridSpec(
        num_scalar_prefetch=2,
        grid=(P // TMG, nF),
        in_specs=[pl.BlockSpec((TMG, D), lambda i, f, te, tn: (jnp.where(tn[i] > 0, i, 0), 0)),
                  pl.BlockSpec((None, D, TF), lambda i, f, te, tn: (te[i], 0, fblk(i, f, tn))),
                  pl.BlockSpec((None, D, TF), lambda i, f, te, tn: (te[i], 0, fblk(i, f, tn))),
                  pl.BlockSpec((None, TF, D), lambda i, f, te, tn: (te[i], fblk(i, f, tn), 0))],
        out_specs=pl.BlockSpec((TMG, D), lambda i, f, te, tn: (i, 0)),
        scratch_shapes=[pltpu.VMEM((TMG, D), F32)])
    return pl.pallas_call(
        _group_ffn_kernel,
        out_shape=jax.ShapeDtypeStruct((P, D), F32),
        grid_spec=grid_spec,
        compiler_params=_params(("arbitrary", "arbitrary"), 56),
        name="moe_group_ffn",
    )(tile_e, tile_n, xs, w1, w3, w2)


def _combine_kernel(len_ref, off_ref, x_ref, lsr_ref, lsc_ref, gt_ref, fg_ref, ys_ref, o_ref, yt, sem,
                    *, E, final_norm):
    i = pl.program_id(0)
    TM = x_ref.shape[0]
    _, LS, D = yt.shape
    slot = lax.rem(i, 2)

    def fetch(tile, s):
        yt[s, TOP_K * TM:LS, :] = jnp.zeros((LS - TOP_K * TM, D), F32)
        _run_dmas(len_ref, off_ref, tile, E, TM, yt.at[s], ys_ref, sem.at[s], False, False)

    @pl.when(i == 0)
    def _():
        fetch(0, 0)

    @pl.when(i + 1 < pl.num_programs(0))
    def _():
        fetch(i + 1, 1 - slot)

    lsr = lsr_ref[...]
    gt = gt_ref[...]
    js = lax.broadcasted_iota(jnp.int32, (LS, TM), 0)
    lane = lax.broadcasted_iota(jnp.int32, (TM, LANES), 1)
    gs = None
    for k in range(TOP_K):
        t1, t2, t3 = (t.astype(F32) for t in _split3(gt[:, k:k + 1]))
        terms = jnp.where(lane == 0, t1, jnp.where(lane == 1, t2, jnp.where(lane == 2, t3, 0.0)))
        pk = jnp.where(js == lsr[TOP_K + k:TOP_K + k + 1, :], 1.0, 0.0).astype(BF16)
        gk = jnp.sum(_dot(pk, terms.astype(BF16)), axis=1, keepdims=True)
        gs = gk if gs is None else gs + gk
    _run_dmas(len_ref, off_ref, i, E, TM, yt.at[slot], ys_ref, sem.at[slot], False, True)
    z = yt[slot] * gs
    zh = z.astype(BF16)
    zl = (z - zh.astype(F32)).astype(BF16)
    lsc = lsc_ref[...]
    jt = lax.broadcasted_iota(jnp.int32, (TM, LS), 1)
    pt = None
    for k in range(TOP_K):
        ok = jnp.where(jt == lsc[:, TOP_K + k:TOP_K + k + 1], 1.0, 0.0)
        pt = ok if pt is None else pt + ok
    pt = pt.astype(BF16)
    xo = x_ref[...] + (_dot(pt, zh) + _dot(pt, zl))
    o_ref[...] = _rms(xo, fg_ref[...]) if final_norm else xo


def _combine(run_len, run_off, x2d, idx, idx_t, gates_t, final_g, ys, E, final_norm):
    T, D = x2d.shape
    TM = min(ROUTE_TM, T)
    LS = TOP_K * TM + E * RUN_ALIGN
    grid_spec = pltpu.PrefetchScalarGridSpec(
        num_scalar_prefetch=2,
        grid=(T // TM,),
        in_specs=[pl.BlockSpec((TM, D), lambda i, *_: (i, 0)),
                  pl.BlockSpec((SUBLANES, TM), lambda i, *_: (0, i)),
                  pl.BlockSpec((TM, SUBLANES), lambda i, *_: (i, 0)),
                  pl.BlockSpec((TM, SUBLANES), lambda i, *_: (i, 0)),
                  pl.BlockSpec((1, D), lambda i, *_: (0, 0)),
                  pl.BlockSpec(memory_space=pl.ANY)],
        out_specs=pl.BlockSpec((TM, D), lambda i, *_: (i, 0)),
        scratch_shapes=[pltpu.VMEM((2, LS, D), F32), pltpu.SemaphoreType.DMA((2,))])
    return pl.pallas_call(
        functools.partial(_combine_kernel, E=E, final_norm=final_norm),
        out_shape=jax.ShapeDtypeStruct((T, D), F32),
        grid_spec=grid_spec,
        compiler_params=_params(("arbitrary",), 48),
        name="moe_combine",
    )(run_len, run_off, x2d, idx, idx_t, gates_t, final_g, ys)


def _moe(x2d, norm_g, router_w, router_b, w1, w3, w2, final_g, final_norm):
    T, D = x2d.shape
    E = router_w.shape[1]
    TMG = FFN_ROW_TILE
    nT = T // min(ROUTE_TM, T)
    rw = jnp.pad(router_w, ((0, 0), (0, LANES - E)))
    rw_hi = rw.astype(BF16)
    rw_lo = (rw - rw_hi.astype(F32)).astype(BF16)
    idx, gates, run_len, run_off, tot, h2 = _route(x2d, norm_g, jnp.concatenate([rw_hi, rw_lo], axis=1),
                                                   router_b.reshape(E, 1), E)
    tot = tot[:, 0]
    padded = ((tot + TMG - 1) // TMG) * TMG
    pend = jnp.cumsum(padded)
    pstart = pend - padded
    run_len = run_len[:, 0]
    run_off = (run_off[:, 0].reshape(nT, E) + pstart[None, :]).reshape(nT * E)
    P = -(-(T * TOP_K + nT * E * (RUN_ALIGN - 1) + E * (TMG - 1)) // TMG) * TMG
    tile_start = jnp.arange(P // TMG, dtype=jnp.int32) * TMG
    tile_e = jnp.minimum(jnp.sum(tile_start[:, None] >= pend[None, :], axis=1), E - 1).astype(jnp.int32)
    sel = tile_e[:, None] == jnp.arange(E, dtype=jnp.int32)[None, :]
    tile_end = jnp.sum(jnp.where(sel, (pstart + tot)[None, :], 0), axis=1)
    tile_n = jnp.clip(tile_end - tile_start, 0, TMG).astype(jnp.int32)
    xs = _dispatch(run_len, run_off, padded - tot, pstart + tot, pend[E - 1:] // TMG, idx, h2, P, E, TMG)
    ys = _group_ffn(tile_e, tile_n, xs, w1, w3, w2, TMG)
    return _combine(run_len, run_off, x2d, idx, idx.T, gates.T, final_g, ys, E, final_norm)


def _final_norm_kernel(x_ref, g_ref, o_ref):
    o_ref[...] = _rms(x_ref[...], g_ref[...])


def _final_norm(x2d, g):
    T, D = x2d.shape
    TM = min(1024, T)
    return pl.pallas_call(
        _final_norm_kernel,
        out_shape=jax.ShapeDtypeStruct((T, D), F32),
        grid=(T // TM,),
        in_specs=[pl.BlockSpec((TM, D), lambda i: (i, 0)), _const_spec((1, D))],
        out_specs=pl.BlockSpec((TM, D), lambda i: (i, 0)),
        compiler_params=_params(("arbitrary",), 32),
        name="final_norm",
    )(x2d, g)


def kernel(x, mem, norm_mix_g, w_in, conv_rg_w, conv_rg_b, rg_w_a, rg_b_a, rg_w_x, rg_b_x, rg_lambda,
           conv_ml_w, conv_ml_b, ml_w_q, ml_w_k, ml_w_v, ml_b_i, ml_b_f, ml_norm_g, mem_norm_g, w_kv,
           w_br_rg, w_br_ml, w_br_xa, b_merge, w_out, norm_ffn_g, ffn_w1, ffn_w3, ffn_w2, router_w,
           router_b, moe_w1, moe_w3, moe_w2, final_norm_g):
    B, S, D = x.shape
    depth = w_in.shape[0]
    d_rg = conv_rg_w.shape[2]
    d_ml = conv_ml_w.shape[2]
    H = ml_w_q.shape[1]
    d_xa = w_kv.shape[2] // 2
    o_ax, o_ay = 0, d_rg
    o_mu, o_mo = 2 * d_rg, 2 * d_rg + d_ml
    o_mi = 2 * d_rg + 2 * d_ml
    o_mf = o_mi + H
    o_q = o_mf + H
    o_g = o_q + d_xa
    assert w_in.shape[2] == o_g + N_BRANCH * D and 2 * H <= SUBLANES

    bf = lambda a: a.astype(BF16)
    row = lambda a: a.reshape(1, -1)
    x2d = x.reshape(B * S, D)
    fg = row(final_norm_g)
    pre = {}

    def layer_bf16(name, arr, idx):
        return pre.pop((name, idx)) if (name, idx) in pre else bf(arr[idx])

    for l in range(depth):
        cols = lambda a, b, l=l: bf(w_in[l][:, a:b])
        w_if = jnp.pad(cols(o_mi, o_mi + 2 * H), ((0, 0), (0, LANES - 2 * H)))
        w_ift = jnp.pad(cols(o_mi, o_mi + 2 * H).T, ((0, SUBLANES - 2 * H), (0, 0)))
        b_if = jnp.concatenate([ml_b_i[l], ml_b_f[l]])
        y_ml = _mlstm_branch(x2d, B, row(norm_mix_g[l]), cols(o_mu, o_mu + d_ml),
                                   cols(o_mo, o_mo + d_ml), w_if, w_ift,
                                   jnp.pad(b_if, (0, LANES - 2 * H)).reshape(1, LANES),
                                   jnp.pad(b_if, (0, SUBLANES - 2 * H)).reshape(SUBLANES, 1),
                                   conv_ml_w[l], row(conv_ml_b[l]), bf(ml_w_q[l]), bf(ml_w_k[l]),
                                   bf(ml_w_v[l]), row(ml_norm_g[l]))
        y_rg = _rg_branch(x2d.reshape(B, S, D), row(norm_mix_g[l]), cols(o_ax, o_ax + d_rg),
                          cols(o_ay, o_ay + d_rg), conv_rg_w[l], row(conv_rg_b[l]), bf(rg_w_a[l]),
                          row(rg_b_a[l]), bf(rg_w_x[l]), row(rg_b_x[l]), row(rg_lambda[l]))
        kv = _mem_kv(mem, row(mem_norm_g[l]), layer_bf16("w_kv", w_kv, l))
        x2d = _merge(x2d, B, row(norm_mix_g[l]), cols(o_q, o_q + d_xa), cols(o_g, o_g + N_BRANCH * D),
                     row(b_merge[l]), y_rg.reshape(B * S, d_rg), y_ml, kv,
                     layer_bf16("w_br_rg", w_br_rg, l), layer_bf16("w_br_ml", w_br_ml, l),
                     layer_bf16("w_br_xa", w_br_xa, l), layer_bf16("w_out", w_out, l))
        j = l // 2
        if l % 2 == 0:
            w1 = bf(ffn_w1[j])
            todo = {}
            if l + 1 < depth:
                todo = {(name, l + 1): (arr, l + 1) for name, arr in (
                    ("w_kv", w_kv), ("w_br_rg", w_br_rg), ("w_br_ml", w_br_ml), ("w_br_xa", w_br_xa),
                    ("w_out", w_out))}
                if (l + 1) % 2 == 1:
                    jn = (l + 1) // 2
                    todo.update({("moe_w1", jn): (moe_w1, jn), ("moe_w3", jn): (moe_w3, jn),
                                 ("moe_w2", jn): (moe_w2, jn)})
            todo = {key: t for key, t in todo.items()
                    if _ffn_cast_block(x2d, w1, t[0].shape[1:]) is not None}
            x2d, done = _ffn(x2d, row(norm_ffn_g[l]), w1, bf(ffn_w3[j]), bf(ffn_w2[j]), list(todo.values()))
            pre.update(zip(todo.keys(), done))
        else:
            x2d = _moe(x2d, row(norm_ffn_g[l]), router_w[j], router_b[j], layer_bf16("moe_w1", moe_w1, j),
                       layer_bf16("moe_w3", moe_w3, j), layer_bf16("moe_w2", moe_w2, j), fg, l == depth - 1)
    if depth % 2 == 1:
        x2d = _final_norm(x2d, fg)
    return x2d.reshape(B, S, D)
```
